```python
import math
import jax, jax.numpy as jnp
from jax import lax
import numpy as np

D_MODEL = 1024
BATCH = 16
SEQ = 4096
DEPTH = 1
DEC_BATCH = 8
DEC_SEQ = 32
PAST_LEN = 1024

CHUNK = 64
MIX_W = D_MODEL
ATTN_W = MIX_W // 2
CONV_CH = MIX_W - ATTN_W
N_HEADS = 4
V_DIM = ATTN_W // N_HEADS
HEAD_DIM = V_DIM // 2
QK_W = N_HEADS * 2 * HEAD_DIM
CONV_W = 31
CONV_GROUPS = 8
IN_W = 2 * QK_W + ATTN_W + 2 * CONV_CH
N_MOD = 6
N_EXPERTS = 32
TOP_K = 4
D_FF = D_MODEL
SWIGLU_ALPHA = 1.702
SWIGLU_LIMIT = 7.0
MOE_BLOCK = 256
Q_BLOCK = 128
EPS = 1e-6

kernel_name = "hymba_diffattn_conformer_moe_stream"


def rmsnorm(x, g):
    xf = x.astype(jnp.float32)
    y = xf * lax.rsqrt(jnp.mean(xf * xf, axis=-1, keepdims=True) + EPS)
    return (y * g.astype(jnp.float32)).astype(x.dtype)


def ada_modulation(c, w_ada, b_ada):
    mod = jnp.einsum('bd,de->be', jax.nn.silu(c), w_ada) + b_ada
    mod = mod.reshape(c.shape[0], N_MOD, D_MODEL)[:, :, None, :]
    return [mod[:, i] for i in range(N_MOD)]


def alibi_slopes():
    return jnp.asarray([2.0 ** (-8.0 * (h + 1) / N_HEADS) for h in range(N_HEADS)], jnp.float32)


def in_projection(h, w_in):
    B, L, _ = h.shape
    z = jnp.einsum('bld,de->ble', h, w_in)
    q = z[..., :QK_W].reshape(B, L, N_HEADS, 2, HEAD_DIM)
    k = z[..., QK_W:2 * QK_W].reshape(B, L, N_HEADS, 2 * HEAD_DIM)
    v = z[..., 2 * QK_W:2 * QK_W + ATTN_W].reshape(B, L, N_HEADS, V_DIM)
    u = z[..., 2 * QK_W + ATTN_W:]
    return q, k, v, u


def diff_lambda(lq1, lk1, lq2, lk2, lam_init):
    f = jnp.float32
    return (jnp.exp(jnp.sum(lq1.astype(f) * lk1.astype(f)))
            - jnp.exp(jnp.sum(lq2.astype(f) * lk2.astype(f))) + lam_init)


def diff_attend(q, k, v, q_pos, k_pos, lam):
    f32 = jnp.float32
    B, K = k.shape[:2]
    kk = k.reshape(B, K, N_HEADS, 2, HEAD_DIM)
    s = jnp.einsum('bqhmd,bkhmd->mbhqk', q.astype(f32), kk.astype(f32)) * (HEAD_DIM ** -0.5)
    dist = jnp.abs(q_pos[:, None] - k_pos[None, :]).astype(f32)
    bias = -alibi_slopes()[:, None, None] * dist
    visible = (k_pos[None, :] // CHUNK) <= (q_pos[:, None] // CHUNK)
    s = jnp.where(visible, s + bias, -jnp.inf)
    p = jax.nn.softmax(s, axis=-1)
    a = p[0] - lam * p[1]
    return jnp.einsum('bhqk,bkhd->bqhd', a, v.astype(f32))


def prompt_attention(q, k, v, lam):
    B, S = q.shape[:2]
    n_blk = S // Q_BLOCK
    k_pos = jnp.arange(S, dtype=jnp.int32)
    q_blocks = q.reshape(B, n_blk, Q_BLOCK, N_HEADS, 2, HEAD_DIM).transpose(1, 0, 2, 3, 4, 5)

    def one_block(args):
        qb, start = args
        q_pos = start + jnp.arange(Q_BLOCK, dtype=jnp.int32)
        return diff_attend(qb, k, v, q_pos, k_pos, lam)

    o = lax.map(one_block, (q_blocks, jnp.arange(n_blk, dtype=jnp.int32) * Q_BLOCK))
    return o.transpose(1, 0, 2, 3, 4).reshape(B, S, N_HEADS, V_DIM)


def sample_attention(q, k, v, past_k, past_v, lam):
    P = past_k.shape[1]
    L = q.shape[1]
    keys = jnp.concatenate([past_k.astype(k.dtype), k], axis=1)
    vals = jnp.concatenate([past_v.astype(v.dtype), v], axis=1)
    k_pos = jnp.arange(P + L, dtype=jnp.int32)
    q_pos = P + jnp.arange(L, dtype=jnp.int32)
    return diff_attend(q, keys, vals, q_pos, k_pos, lam)


def diff_head_post(o, g_subln, lam_init, dtype):
    B, L = o.shape[:2]
    y = o * lax.rsqrt(jnp.mean(o * o, axis=-1, keepdims=True) + EPS) * g_subln.astype(jnp.float32)
    return (y * (1.0 - lam_init)).reshape(B, L, ATTN_W).astype(dtype)


def conv_module(u, buf, w_dw, b_dw, g_cn, b_cn):
    val, gate = u[..., :CONV_CH], u[..., CONV_CH:]
    glu = val * jax.nn.sigmoid(gate)
    ext = jnp.concatenate([buf.astype(glu.dtype), glu], axis=1)
    y = lax.conv_general_dilated(ext, w_dw.astype(ext.dtype)[:, None, :], window_strides=(1,),
                                 padding='VALID', dimension_numbers=('NWC', 'WIO', 'NWC'),
                                 feature_group_count=CONV_CH) + b_dw
    B, L, _ = y.shape
    yf = y.astype(jnp.float32).reshape(B, L, CONV_GROUPS, CONV_CH // CONV_GROUPS)
    mu = jnp.mean(yf, axis=-1, keepdims=True)
    var = jnp.mean(jnp.square(yf - mu), axis=-1, keepdims=True)
    yn = ((yf - mu) * lax.rsqrt(var + EPS)).reshape(B, L, CONV_CH)
    yn = yn * g_cn.astype(jnp.float32) + b_cn.astype(jnp.float32)
    return jax.nn.silu(yn).astype(u.dtype), ext[:, -(CONV_W - 1):]


def moe(h, w_router, b_router, w_gu, b_gu, w_down, b_down):
    T, D = h.shape
    f32 = jnp.float32
    logits = jnp.einsum('td,de->te', h.astype(f32), w_router.astype(f32)) + b_router.astype(f32)
    top_vals, top_idx = lax.top_k(logits, TOP_K)
    weights = jax.nn.softmax(top_vals, axis=-1).astype(h.dtype)
    TK = T * TOP_K
    flat_e = top_idx.reshape(TK).astype(jnp.int32)
    flat_t = jnp.arange(TK, dtype=jnp.int32) // TOP_K
    order = jnp.argsort(flat_e)
    sorted_e = flat_e[order]
    counts = jnp.bincount(flat_e, length=N_EXPERTS).astype(jnp.int32)
    padded = (counts + MOE_BLOCK - 1) // MOE_BLOCK * MOE_BLOCK
    pad_end = jnp.cumsum(padded)
    pad_start = pad_end - padded
    start = jnp.cumsum(counts) - counts
    rank = jnp.arange(TK, dtype=jnp.int32) - start[sorted_e]
    dest_sorted = (pad_start[sorted_e] + rank).astype(jnp.int32)
    dest = jnp.zeros((TK,), jnp.int32).at[order].set(dest_sorted)
    n_blocks = -(-TK // MOE_BLOCK) + N_EXPERTS
    n_pad = n_blocks * MOE_BLOCK
    src_tok = jnp.zeros((n_pad,), jnp.int32).at[dest].set(flat_t)
    valid = jnp.zeros((n_pad,), jnp.bool_).at[dest].set(True)
    block_e = jnp.minimum(jnp.searchsorted(pad_end, jnp.arange(n_blocks, dtype=jnp.int32) * MOE_BLOCK,
                                           side='right'), N_EXPERTS - 1)
    x_pad = jnp.where(valid[:, None], h[src_tok], jnp.zeros((), h.dtype)).reshape(n_blocks, MOE_BLOCK, D)

    def expert_block(args):
        xb, e = args
        gu = xb @ w_gu[e] + b_gu[e]
        g = jnp.minimum(gu[:, :D_FF], SWIGLU_LIMIT)
        lin = jnp.clip(gu[:, D_FF:], -SWIGLU_LIMIT, SWIGLU_LIMIT)
        act = g * jax.nn.sigmoid(SWIGLU_ALPHA * g) * (lin + 1.0)
        return act @ w_down[e] + b_down[e]

    y_pad = lax.map(expert_block, (x_pad, block_e)).reshape(n_pad, D)
    y_assign = y_pad[dest].reshape(T, TOP_K, D)
    return jnp.einsum('tk,tkd->td', weights, y_assign)


def trunk_layer(x, c, conv_buf, past_k, past_v, p, lam_init):
    (w_ada, b_ada, g_pre_mix, g_post_mix, w_in, lq1, lk1, lq2, lk2, g_subln, w_dw, b_dw,
     g_cn, b_cn, w_out, g_pre_ffn, g_post_ffn, w_router, b_router, w_gu, b_gu, w_down, b_down) = p
    B, L, _ = x.shape
    shift_a, scale_a, gate_a, shift_f, scale_f, gate_f = ada_modulation(c, w_ada, b_ada)
    h = rmsnorm(x, g_pre_mix) * (1.0 + scale_a) + shift_a
    q, k, v, u = in_projection(h, w_in)
    lam = diff_lambda(lq1, lk1, lq2, lk2, lam_init)
    if past_k is None:
        o = prompt_attention(q, k, v, lam)
    else:
        o = sample_attention(q, k, v, past_k, past_v, lam)
    attn_out = diff_head_post(o, g_subln, lam_init, x.dtype)
    conv_out, new_buf = conv_module(u, conv_buf, w_dw, b_dw, g_cn, b_cn)
    mix = jnp.einsum('ble,ed->bld', jnp.concatenate([attn_out, conv_out], axis=-1), w_out)
    x = x + gate_a * rmsnorm(mix, g_post_mix)
    h = rmsnorm(x, g_pre_ffn) * (1.0 + scale_f) + shift_f
    f = moe(h.reshape(B * L, D_MODEL), w_router, b_router, w_gu, b_gu, w_down, b_down).reshape(B, L, D_MODEL)
    x = x + gate_f * rmsnorm(f, g_post_ffn)
    return x, k, v, new_buf


def setup_inputs(seed: int = 0) -> dict:
    key = jax.random.key(seed)
    ks = jax.random.split(key, 32)
    f32 = jnp.float32

    def nrm(k, shape, scale=1.0):
        return jax.random.normal(k, shape, f32) * scale

    def gain(k, shape):
        return 1.0 + 0.1 * jax.random.normal(k, shape, f32)

    return {
        "x_prompt": nrm(ks[0], (BATCH, SEQ, D_MODEL)),
        "x_sample": nrm(ks[1], (DEC_BATCH, DEC_SEQ, D_MODEL)),
        "cache_k": nrm(ks[2], (DEPTH, DEC_BATCH, PAST_LEN, N_HEADS, 2 * HEAD_DIM)),
        "cache_v": nrm(ks[3], (DEPTH, DEC_BATCH, PAST_LEN, N_HEADS, V_DIM)),
        "state_conv": nrm(ks[4], (DEPTH, DEC_BATCH, CONV_W - 1, CONV_CH), 0.5),
        "c_prompt": nrm(ks[5], (BATCH, D_MODEL)),
        "c_sample": nrm(ks[6], (DEC_BATCH, D_MODEL)),
        "w_ada": nrm(ks[7], (DEPTH, D_MODEL, N_MOD * D_MODEL), 0.5 * D_MODEL ** -0.5),
        "b_ada": nrm(ks[8], (DEPTH, N_MOD * D_MODEL), 0.02),
        "g_pre_mix": gain(ks[9], (DEPTH, D_MODEL)),
        "g_post_mix": gain(ks[10], (DEPTH, D_MODEL)),
        "w_in": nrm(ks[11], (DEPTH, D_MODEL, IN_W), D_MODEL ** -0.5),
        "lambda_q1": nrm(ks[12], (DEPTH, HEAD_DIM), 0.1),
        "lambda_k1": nrm(ks[13], (DEPTH, HEAD_DIM), 0.1),
        "lambda_q2": nrm(ks[14], (DEPTH, HEAD_DIM), 0.1),
        "lambda_k2": nrm(ks[15], (DEPTH, HEAD_DIM), 0.1),
        "g_subln": gain(ks[16], (DEPTH, V_DIM)),
        "w_dw": nrm(ks[17], (DEPTH, CONV_W, CONV_CH), CONV_W ** -0.5),
        "b_dw": nrm(ks[18], (DEPTH, CONV_CH), 0.02),
        "g_cnorm": gain(ks[19], (DEPTH, CONV_CH)),
        "b_cnorm": nrm(ks[20], (DEPTH, CONV_CH), 0.02),
        "w_out": nrm(ks[21], (DEPTH, MIX_W, D_MODEL), MIX_W ** -0.5),
        "g_pre_ffn": gain(ks[22], (DEPTH, D_MODEL)),
        "g_post_ffn": gain(ks[23], (DEPTH, D_MODEL)),
        "w_router": nrm(ks[24], (DEPTH, D_MODEL, N_EXPERTS), D_MODEL ** -0.5),
        "b_router": nrm(ks[25], (DEPTH, N_EXPERTS), 0.01),
        "w_gu": nrm(ks[26], (DEPTH, N_EXPERTS, D_MODEL, 2 * D_FF), D_MODEL ** -0.5),
        "b_gu": nrm(ks[27], (DEPTH, N_EXPERTS, 2 * D_FF), 0.02),
        "w_down": nrm(ks[28], (DEPTH, N_EXPERTS, D_FF, D_MODEL), D_FF ** -0.5),
        "b_down": nrm(ks[29], (DEPTH, N_EXPERTS, D_MODEL), 0.02),
    }


def reference(x_prompt, x_sample, cache_k, cache_v, state_conv, c_prompt, c_sample,
              w_ada, b_ada, g_pre_mix, g_post_mix, w_in, lambda_q1, lambda_k1, lambda_q2, lambda_k2,
              g_subln, w_dw, b_dw, g_cnorm, b_cnorm, w_out, g_pre_ffn, g_post_ffn,
              w_router, b_router, w_gu, b_gu, w_down, b_down):
    yp, ys = x_prompt, x_sample
    kp_l, vp_l, cp_l, ks_l, vs_l, cs_l = [], [], [], [], [], []
    for l in range(DEPTH):
        p = (w_ada[l], b_ada[l], g_pre_mix[l], g_post_mix[l], w_in[l], lambda_q1[l], lambda_k1[l],
             lambda_q2[l], lambda_k2[l], g_subln[l], w_dw[l], b_dw[l], g_cnorm[l], b_cnorm[l], w_out[l],
             g_pre_ffn[l], g_post_ffn[l], w_router[l], b_router[l], w_gu[l], b_gu[l], w_down[l], b_down[l])
        lam_init = 0.8 - 0.6 * math.exp(-0.3 * l)
        buf0 = jnp.zeros((yp.shape[0], CONV_W - 1, CONV_CH), yp.dtype)
        yp, kp, vp, cp = trunk_layer(yp, c_prompt, buf0, None, None, p, lam_init)
        ys, ksn, vsn, csn = trunk_layer(ys, c_sample, state_conv[l], cache_k[l], cache_v[l], p, lam_init)
        kp_l.append(kp); vp_l.append(vp); cp_l.append(cp)
        ks_l.append(ksn); vs_l.append(vsn); cs_l.append(csn)
    return (yp, ys, jnp.stack(kp_l), jnp.stack(vp_l), jnp.stack(cp_l),
            jnp.stack(ks_l), jnp.stack(vs_l), jnp.stack(cs_l))
```

```python
import functools
import math

import jax
import jax.numpy as jnp
from jax import lax
from jax.experimental import pallas as pl
from jax.experimental.pallas import tpu as pltpu

F32 = jnp.float32
BF16 = jnp.bfloat16
I32 = jnp.int32

D_MODEL = 1024
CHUNK = 64
CHUNK_SHIFT = 6
N_HEADS = 4
V_DIM = 128
HEAD_DIM = 64
QK_W = 512
ATTN_W = 512
CONV_CH = 512
CONV_W = 31
CONV_GROUPS = 8
GROUP_CH = CONV_CH // CONV_GROUPS
IN_W = 2 * QK_W + ATTN_W + 2 * CONV_CH
N_MOD = 6
N_EXPERTS = 32
TOP_K = 4
D_FF = 1024
SWIGLU_ALPHA = 1.702
SWIGLU_LIMIT = 7.0
EPS = 1e-6

LANES = 128
CONV_HALO = 32
CONV_ROWS = 32
NEG = -1e30
VMEM_LIMIT = 48 * 1024 * 1024


def _sigmoid(x):
    return 1.0 / (1.0 + jnp.exp(-x))


def _split_bf16(x):
    hi = x.astype(BF16)
    lo = (x - hi.astype(F32)).astype(BF16)
    return hi, lo


def _dot(a, b):
    return jnp.dot(a, b, preferred_element_type=F32)


def _dot3(a, b):
    ah, al = _split_bf16(a)
    bh, bl = _split_bf16(b)
    return _dot(ah, bh) + _dot(ah, bl) + _dot(al, bh)


def _rms(x):
    return x * lax.rsqrt(jnp.mean(x * x, axis=-1, keepdims=True) + EPS)


def _ada_kernel(c_ref, w_ref, b_ref, o_ref):
    c = c_ref[...]
    o_ref[...] = _dot3(c * _sigmoid(c), w_ref[...]) + b_ref[...]


def _ada(c, w_ada, b_ada):
    n = c.shape[0]
    return pl.pallas_call(
        _ada_kernel,
        grid=(N_MOD,),
        in_specs=[pl.BlockSpec((n, D_MODEL), lambda j: (0, 0)),
                  pl.BlockSpec((D_MODEL, D_MODEL), lambda j: (0, j)),
                  pl.BlockSpec((1, D_MODEL), lambda j: (0, j))],
        out_specs=pl.BlockSpec((n, D_MODEL), lambda j: (0, j)),
        out_shape=jax.ShapeDtypeStruct((n, N_MOD * D_MODEL), F32),
        compiler_params=pltpu.CompilerParams(dimension_semantics=("arbitrary",), vmem_limit_bytes=VMEM_LIMIT),
        name="ada",
    )(c, w_ada, b_ada.reshape(1, N_MOD * D_MODEL))


def _premix_kernel(x_ref, mod_ref, g_ref, w_ref, buf_ref, wdw_ref, bdw_ref, gmat_ref, gcn_ref, bcn_ref,
                   q_ref, kb_ref, vb_ref, kf_ref, vf_ref, co_ref, st_ref, ext_ref, y_ref, *, tm):
    s = pl.program_id(1)
    x = x_ref[0]
    shift = mod_ref[0, 0:1, :]
    scale = mod_ref[0, 1:2, :]
    h = _rms(x) * g_ref[...] * (1.0 + scale) + shift
    hb = h.astype(BF16)

    zq = _dot(hb, w_ref[:, 0:QK_W])
    q_ref[0] = (zq * (HEAD_DIM ** -0.5)).astype(BF16)
    zk = _dot(hb, w_ref[:, QK_W:2 * QK_W])
    kf_ref[0] = zk
    kb_ref[0] = zk.astype(BF16)
    zv = _dot(hb, w_ref[:, 2 * QK_W:2 * QK_W + ATTN_W])
    vf_ref[0] = zv
    vb_ref[0] = zv.astype(BF16)
    u0 = 2 * QK_W + ATTN_W
    val = _dot(hb, w_ref[:, u0:u0 + CONV_CH])
    gate = _dot(hb, w_ref[:, u0 + CONV_CH:u0 + 2 * CONV_CH])
    glu = val * _sigmoid(gate)

    @pl.when(s == 0)
    def _():
        ext_ref[0:CONV_HALO, :] = buf_ref[0]

    ext_ref[CONV_HALO:CONV_HALO + tm, :] = glu

    off = CONV_HALO - (CONV_W - 1)
    for c in range(tm // CONV_ROWS):
        r0 = c * CONV_ROWS
        acc = jnp.zeros((CONV_ROWS, CONV_CH), F32)
        for j in range(CONV_W):
            acc = acc + wdw_ref[j:j + 1, :] * ext_ref[r0 + j + off:r0 + j + off + CONV_ROWS, :]
        y_ref[r0:r0 + CONV_ROWS, :] = acc + bdw_ref[...]

    y = y_ref[...]
    gm = gmat_ref[...]
    yh, yl = _split_bf16(y)
    mu = (_dot(yh, gm) + _dot(yl, gm)) * (1.0 / GROUP_CH)
    d = y - mu
    dh, dl = _split_bf16(d * d)
    var = (_dot(dh, gm) + _dot(dl, gm)) * (1.0 / GROUP_CH)
    yn = d * lax.rsqrt(var + EPS) * gcn_ref[...] + bcn_ref[...]
    co_ref[0] = (yn * _sigmoid(yn)).astype(BF16)

    tail = ext_ref[tm:tm + CONV_HALO, :]
    st_ref[0] = tail
    ext_ref[0:CONV_HALO, :] = tail


def _premix(x, mod, g_pre_mix, w_in_bf, buf, w_dw, b_dw, gmat, g_cn, b_cn, tm):
    B, L, _ = x.shape
    assert L % tm == 0 and tm % CONV_ROWS == 0
    row = lambda b, s: (b, s, 0)
    const2 = lambda b, s: (0, 0)
    bf_tile = jax.ShapeDtypeStruct((B, L, QK_W), BF16)
    f_tile = jax.ShapeDtypeStruct((B, L, QK_W), F32)
    return pl.pallas_call(
        functools.partial(_premix_kernel, tm=tm),
        grid=(B, L // tm),
        in_specs=[pl.BlockSpec((1, tm, D_MODEL), row),
                  pl.BlockSpec((1, N_MOD, D_MODEL), lambda b, s: (b, 0, 0)),
                  pl.BlockSpec((1, D_MODEL), const2),
                  pl.BlockSpec((D_MODEL, IN_W), const2),
                  pl.BlockSpec((1, CONV_HALO, CONV_CH), lambda b, s: (b, 0, 0)),
                  pl.BlockSpec((CONV_HALO, CONV_CH), const2),
                  pl.BlockSpec((1, CONV_CH), const2),
                  pl.BlockSpec((CONV_CH, CONV_CH), const2),
                  pl.BlockSpec((1, CONV_CH), const2),
                  pl.BlockSpec((1, CONV_CH), const2)],
        out_specs=[pl.BlockSpec((1, tm, QK_W), row)] * 6
                  + [pl.BlockSpec((1, CONV_HALO, CONV_CH), lambda b, s: (b, 0, 0))],
        out_shape=[bf_tile, bf_tile, bf_tile, f_tile, f_tile, bf_tile,
                   jax.ShapeDtypeStruct((B, CONV_HALO, CONV_CH), F32)],
        scratch_shapes=[pltpu.VMEM((CONV_HALO + tm, CONV_CH), F32),
                        pltpu.VMEM((tm, CONV_CH), F32)],
        compiler_params=pltpu.CompilerParams(dimension_semantics=("arbitrary", "arbitrary"),
                                             vmem_limit_bytes=VMEM_LIMIT),
        name="premix",
    )(x, mod, g_pre_mix.reshape(1, D_MODEL), w_in_bf, buf, w_dw, b_dw.reshape(1, CONV_CH), gmat,
      g_cn.reshape(1, CONV_CH), b_cn.reshape(1, CONV_CH))


def _attn_kernel(slopes_ref, q_ref, k_ref, v_ref, lam_ref, gs_ref, o_ref,
                 m1_ref, l1_ref, a1_ref, m2_ref, l2_ref, a2_ref, *, tq, tk, q_offset, causal, lam_init):
    h = pl.program_id(1)
    qi = pl.program_id(2)
    slope = slopes_ref[h]
    qbase = q_offset + qi * tq

    q = q_ref[0]
    lane = lax.broadcasted_iota(I32, (1, 2 * HEAD_DIM), 1)
    zero = jnp.zeros((), BF16)
    q1 = jnp.where(lane < HEAD_DIM, q, zero)
    q2 = jnp.where(lane >= HEAD_DIM, q, zero)

    for m_ref, l_ref, a_ref in ((m1_ref, l1_ref, a1_ref), (m2_ref, l2_ref, a2_ref)):
        m_ref[...] = jnp.full(m_ref.shape, NEG, F32)
        l_ref[...] = jnp.zeros(l_ref.shape, F32)
        a_ref[...] = jnp.zeros(a_ref.shape, F32)

    nt = (((1,), (1,)), ((), ()))

    def tile(k0, masked):
        kt = k_ref[0, pl.ds(k0, tk), :]
        vt = v_ref[0, pl.ds(k0, tk), :]
        kpos = k0 + lax.broadcasted_iota(I32, (1, tk), 1)
        if masked:
            qpos = qbase + lax.broadcasted_iota(I32, (tq, 1), 0)
            bias = slope * ((qpos - qbase) - jnp.abs(qpos - kpos)).astype(F32)
            visible = jnp.right_shift(kpos, CHUNK_SHIFT) <= jnp.right_shift(qpos, CHUNK_SHIFT)
        else:
            bias = slope * (kpos - qbase).astype(F32)
        for qm, m_ref, l_ref, a_ref in ((q1, m1_ref, l1_ref, a1_ref), (q2, m2_ref, l2_ref, a2_ref)):
            s = lax.dot_general(qm, kt, nt, preferred_element_type=F32) + bias
            if masked:
                s = jnp.where(visible, s, NEG)
            m_old = m_ref[...]
            m_new = jnp.maximum(m_old, jnp.max(s, axis=-1, keepdims=True))
            p = jnp.exp(s - m_new)
            alpha = jnp.exp(m_old - m_new)
            l_ref[...] = alpha * l_ref[...] + jnp.sum(p, axis=-1, keepdims=True)
            a_ref[...] = alpha * a_ref[...] + _dot(p.astype(BF16), vt)
            m_ref[...] = m_new

    if causal:
        def body(ki, carry):
            tile(pl.multiple_of(ki * tk, tk), False)
            return carry
        lax.fori_loop(0, qi, body, 0)
        tile(pl.multiple_of(qi * tk, tk), True)
    else:
        tile(0, True)

    lv = lam_ref[...]
    lam = (jnp.exp(jnp.sum(lv[0:1, :] * lv[1:2, :], axis=-1, keepdims=True))
           - jnp.exp(jnp.sum(lv[2:3, :] * lv[3:4, :], axis=-1, keepdims=True)) + lam_init)
    o = a1_ref[...] / l1_ref[...] - lam * (a2_ref[...] / l2_ref[...])
    o_ref[0] = (_rms(o) * gs_ref[...] * (1.0 - lam_init)).astype(BF16)


def _attention(q, k, v, lam_rows, g_subln, lam_init, tq, tk, q_offset, causal):
    B, Lq, _ = q.shape
    Lk = k.shape[1]
    assert Lq % tq == 0 and Lk % tk == 0 and (not causal or (tq == tk and tq % CHUNK == 0 and q_offset == 0))
    slopes = jnp.asarray([2.0 ** (-8.0 * (h + 1) / N_HEADS) for h in range(N_HEADS)], F32)
    stat = pltpu.VMEM((tq, 1), F32)
    acc = pltpu.VMEM((tq, V_DIM), F32)
    return pl.pallas_call(
        functools.partial(_attn_kernel, tq=tq, tk=tk, q_offset=q_offset, causal=causal, lam_init=lam_init),
        grid_spec=pltpu.PrefetchScalarGridSpec(
            num_scalar_prefetch=1,
            grid=(B, N_HEADS, Lq // tq),
            in_specs=[pl.BlockSpec((1, tq, V_DIM), lambda b, h, i, sl: (b, i, h)),
                      pl.BlockSpec((1, Lk, V_DIM), lambda b, h, i, sl: (b, 0, h)),
                      pl.BlockSpec((1, Lk, V_DIM), lambda b, h, i, sl: (b, 0, h)),
                      pl.BlockSpec((8, LANES), lambda b, h, i, sl: (0, 0)),
                      pl.BlockSpec((1, V_DIM), lambda b, h, i, sl: (0, 0))],
            out_specs=pl.BlockSpec((1, tq, V_DIM), lambda b, h, i, sl: (b, i, h)),
            scratch_shapes=[stat, stat, acc, stat, stat, acc]),
        out_shape=jax.ShapeDtypeStruct((B, Lq, ATTN_W), BF16),
        compiler_params=pltpu.CompilerParams(dimension_semantics=("arbitrary",) * 3,
                                             vmem_limit_bytes=VMEM_LIMIT),
        name="attn",
    )(slopes, q, k, v, lam_rows, g_subln.reshape(1, V_DIM))


def _postmix_kernel(a_ref, c_ref, x_ref, mod_ref, wo_ref, gpm_ref, gpf_ref, wr_ref, br_ref,
                    x1_ref, h2_ref, rt_ref, cnt_ref, carry_ref, *, tm):
    first = jnp.logical_and(pl.program_id(0) == 0, pl.program_id(1) == 0)

    @pl.when(first)
    def _():
        carry_ref[...] = jnp.zeros(carry_ref.shape, F32)

    mix = _dot(a_ref[0], wo_ref[0:ATTN_W, :]) + _dot(c_ref[0], wo_ref[ATTN_W:ATTN_W + CONV_CH, :])
    gate_a = mod_ref[0, 2:3, :]
    shift_f = mod_ref[0, 3:4, :]
    scale_f = mod_ref[0, 4:5, :]
    x1 = x_ref[0] + gate_a * (_rms(mix) * gpm_ref[...])
    x1_ref[0] = x1
    h2 = _rms(x1) * gpf_ref[...] * (1.0 + scale_f) + shift_f
    h2_ref[0] = h2

    logits = _dot3(h2, wr_ref[...]) + br_ref[...]
    lane = lax.broadcasted_iota(I32, (tm, LANES), 1).astype(F32)
    vals, idxs = [], []
    for _ in range(TOP_K):
        m = jnp.max(logits, axis=-1, keepdims=True)
        idx = jnp.min(jnp.where(logits == m, lane, float(LANES)), axis=-1, keepdims=True)
        vals.append(m)
        idxs.append(idx)
        logits = jnp.where(lane == idx, 2.0 * NEG, logits)
    es = [jnp.exp(v - vals[0]) for v in vals]
    denom = es[0] + es[1] + es[2] + es[3]

    onehot = jnp.zeros((tm, LANES), F32)
    for idx in idxs:
        onehot = jnp.where(lane == idx, 1.0, onehot)
    r_i = lax.broadcasted_iota(I32, (tm, tm), 0)
    c_i = lax.broadcasted_iota(I32, (tm, tm), 1)
    tri = jnp.where(c_i < r_i, 1.0, 0.0).astype(BF16)
    before = _dot(tri, onehot.astype(BF16)) + carry_ref[0:1, :]

    rt = jnp.zeros((tm, LANES), F32)
    for k in range(TOP_K):
        rank = jnp.sum(jnp.where(lane == idxs[k], before, 0.0), axis=-1, keepdims=True)
        rt = jnp.where(lane == k, idxs[k], rt)
        rt = jnp.where(lane == TOP_K + k, es[k] / denom, rt)
        rt = jnp.where(lane == 2 * TOP_K + k, rank, rt)
    rt_ref[0] = rt

    carry_ref[...] = carry_ref[...] + jnp.sum(onehot, axis=0, keepdims=True)
    cnt_ref[...] = carry_ref[...]


def _postmix(attn, conv, x, mod, w_out_bf, g_post_mix, g_pre_ffn, w_router_pad, b_router_pad, tm):
    B, L, _ = x.shape
    row = lambda b, s: (b, s, 0)
    const2 = lambda b, s: (0, 0)
    return pl.pallas_call(
        functools.partial(_postmix_kernel, tm=tm),
        grid=(B, L // tm),
        in_specs=[pl.BlockSpec((1, tm, ATTN_W), row),
                  pl.BlockSpec((1, tm, CONV_CH), row),
                  pl.BlockSpec((1, tm, D_MODEL), row),
                  pl.BlockSpec((1, N_MOD, D_MODEL), lambda b, s: (b, 0, 0)),
                  pl.BlockSpec((D_MODEL, D_MODEL), const2),
                  pl.BlockSpec((1, D_MODEL), const2),
                  pl.BlockSpec((1, D_MODEL), const2),
                  pl.BlockSpec((D_MODEL, LANES), const2),
                  pl.BlockSpec((1, LANES), const2)],
        out_specs=[pl.BlockSpec((1, tm, D_MODEL), row),
                   pl.BlockSpec((1, tm, D_MODEL), row),
                   pl.BlockSpec((1, tm, LANES), row),
                   pl.BlockSpec((8, LANES), const2)],
        out_shape=[jax.ShapeDtypeStruct((B, L, D_MODEL), F32),
                   jax.ShapeDtypeStruct((B, L, D_MODEL), F32),
                   jax.ShapeDtypeStruct((B, L, LANES), F32),
                   jax.ShapeDtypeStruct((8, LANES), F32)],
        scratch_shapes=[pltpu.VMEM((8, LANES), F32)],
        compiler_params=pltpu.CompilerParams(dimension_semantics=("arbitrary", "arbitrary"),
                                             vmem_limit_bytes=VMEM_LIMIT),
        name="postmix",
    )(attn, conv, x, mod, w_out_bf, g_post_mix.reshape(1, D_MODEL), g_pre_ffn.reshape(1, D_MODEL),
      w_router_pad, b_router_pad)


def _expert_kernel(be_ref, nv_ref, src_ref, h_hbm, wgu_ref, bgu_ref, wd_ref, bd_ref, y_hbm,
                   xbuf, ybuf, gsem, ssem, *, blk):
    i = pl.program_id(0)
    nv = nv_ref[i]

    @pl.when(i == 0)
    def _():
        xbuf[...] = jnp.zeros(xbuf.shape, F32)

    def gather_copy(r):
        tok = jnp.right_shift(src_ref[0, 0, r], 2)
        return pltpu.make_async_copy(h_hbm.at[pl.ds(tok, 1), :], xbuf.at[pl.ds(r, 1), :], gsem)

    def scatter_copy(r):
        return pltpu.make_async_copy(ybuf.at[pl.ds(r, 1), :], y_hbm.at[pl.ds(src_ref[0, 0, r], 1), :], ssem)

    def start_g(r, c):
        gather_copy(r).start()
        return c

    def wait_g(r, c):
        gather_copy(r).wait()
        return c

    def start_s(r, c):
        scatter_copy(r).start()
        return c

    def wait_s(r, c):
        scatter_copy(r).wait()
        return c

    @pl.when(nv > 0)
    def _():
        lax.fori_loop(0, nv, start_g, 0)
        lax.fori_loop(0, nv, wait_g, 0)
        xb = xbuf[...].astype(BF16)
        gu = _dot(xb, wgu_ref[0]) + bgu_ref[0]
        g = jnp.minimum(gu[:, :D_FF], SWIGLU_LIMIT)
        lin = jnp.clip(gu[:, D_FF:], -SWIGLU_LIMIT, SWIGLU_LIMIT)
        act = g * _sigmoid(SWIGLU_ALPHA * g) * (lin + 1.0)
        ybuf[...] = _dot(act.astype(BF16), wd_ref[0]) + bd_ref[0]
        lax.fori_loop(0, nv, start_s, 0)
        lax.fori_loop(0, nv, wait_s, 0)


def _experts(h2, block_e, n_valid, slot_src, w_gu_bf, b_gu, w_down_bf, b_down, blk):
    T = h2.shape[0]
    n_blocks = block_e.shape[0]
    return pl.pallas_call(
        functools.partial(_expert_kernel, blk=blk),
        grid_spec=pltpu.PrefetchScalarGridSpec(
            num_scalar_prefetch=2,
            grid=(n_blocks,),
            in_specs=[pl.BlockSpec((1, 1, blk), lambda i, be, nv: (i, 0, 0), memory_space=pltpu.SMEM),
                      pl.BlockSpec(memory_space=pl.ANY),
                      pl.BlockSpec((1, D_MODEL, 2 * D_FF), lambda i, be, nv: (be[i], 0, 0)),
                      pl.BlockSpec((1, 1, 2 * D_FF), lambda i, be, nv: (be[i], 0, 0)),
                      pl.BlockSpec((1, D_FF, D_MODEL), lambda i, be, nv: (be[i], 0, 0)),
                      pl.BlockSpec((1, 1, D_MODEL), lambda i, be, nv: (be[i], 0, 0))],
            out_specs=pl.BlockSpec(memory_space=pl.ANY),
            scratch_shapes=[pltpu.VMEM((blk, D_MODEL), F32),
                            pltpu.VMEM((blk, D_MODEL), F32),
                            pltpu.SemaphoreType.DMA(()),
                            pltpu.SemaphoreType.DMA(())]),
        out_shape=jax.ShapeDtypeStruct((T * TOP_K, D_MODEL), F32),
        compiler_params=pltpu.CompilerParams(dimension_semantics=("arbitrary",), vmem_limit_bytes=VMEM_LIMIT),
        name="experts",
    )(block_e, n_valid, slot_src.reshape(n_blocks, 1, blk), h2, w_gu_bf,
      b_gu.reshape(N_EXPERTS, 1, 2 * D_FF), w_down_bf, b_down.reshape(N_EXPERTS, 1, D_MODEL))


def _combine_kernel(y_ref, rt_ref, x1_ref, mod_ref, g_ref, o_ref):
    rt = rt_ref[0]
    f = rt[:, TOP_K:TOP_K + 1] * y_ref[0, :, 0:D_MODEL]
    for k in range(1, TOP_K):
        f = f + rt[:, TOP_K + k:TOP_K + k + 1] * y_ref[0, :, k * D_MODEL:(k + 1) * D_MODEL]
    gate_f = mod_ref[0, 5:6, :]
    o_ref[0] = x1_ref[0] + gate_f * (_rms(f) * g_ref[...])


def _combine(y, route, x1, mod, g_post_ffn, tm):
    B, L, _ = x1.shape
    row = lambda b, s: (b, s, 0)
    return pl.pallas_call(
        _combine_kernel,
        grid=(B, L // tm),
        in_specs=[pl.BlockSpec((1, tm, TOP_K * D_MODEL), row),
                  pl.BlockSpec((1, tm, LANES), row),
                  pl.BlockSpec((1, tm, D_MODEL), row),
                  pl.BlockSpec((1, N_MOD, D_MODEL), lambda b, s: (b, 0, 0)),
                  pl.BlockSpec((1, D_MODEL), lambda b, s: (0, 0))],
        out_specs=pl.BlockSpec((1, tm, D_MODEL), row),
        out_shape=jax.ShapeDtypeStruct((B, L, D_MODEL), F32),
        compiler_params=pltpu.CompilerParams(dimension_semantics=("arbitrary", "arbitrary"),
                                             vmem_limit_bytes=VMEM_LIMIT),
        name="combine",
    )(y.reshape(B, L, TOP_K * D_MODEL), route, x1, mod, g_post_ffn.reshape(1, D_MODEL))


def _routing_tables(route, counts, blk):
    T = route.shape[0]
    TK = T * TOP_K
    idx = route[:, 0:TOP_K].astype(I32)
    rank = route[:, 2 * TOP_K:3 * TOP_K].astype(I32)
    padded = (counts + blk - 1) // blk * blk
    pad_end = jnp.cumsum(padded)
    pad_start = pad_end - padded
    dest = (pad_start[idx] + rank).reshape(TK)
    n_blocks = -(-TK // blk) + N_EXPERTS
    slot_src = jnp.zeros((n_blocks * blk,), I32).at[dest].set(jnp.arange(TK, dtype=I32))
    blk_start = jnp.arange(n_blocks, dtype=I32) * blk
    block_e = jnp.minimum(jnp.searchsorted(pad_end, blk_start, side='right'), N_EXPERTS - 1).astype(I32)
    n_valid = jnp.clip(pad_start[block_e] + counts[block_e] - blk_start, 0, blk).astype(I32)
    return block_e, n_valid, slot_src


def _layer(x, mod, buf, past_k, past_v, p, lam_init, tm, tq, blk):
    B, L, _ = x.shape
    q, kb, vb, kf, vf, conv, state = _premix(x, mod, p["g_pre_mix"], p["w_in"], buf, p["w_dw"], p["b_dw"],
                                             p["gmat"], p["g_cn"], p["b_cn"], tm)
    if past_k is None:
        attn = _attention(q, kb, vb, p["lam_rows"], p["g_subln"], lam_init, tq, tq, 0, True)
    else:
        P = past_k.shape[1]
        keys = jnp.concatenate([past_k.reshape(B, P, QK_W).astype(BF16), kb], axis=1)
        vals = jnp.concatenate([past_v.reshape(B, P, ATTN_W).astype(BF16), vb], axis=1)
        attn = _attention(q, keys, vals, p["lam_rows"], p["g_subln"], lam_init, L, P + L, P, False)
    x1, h2, route, cnt = _postmix(attn, conv, x, mod, p["w_out"], p["g_post_mix"], p["g_pre_ffn"],
                                  p["w_router"], p["b_router"], tm)
    T = B * L
    route2 = route.reshape(T, LANES)
    counts = cnt[0, :N_EXPERTS].astype(I32)
    block_e, n_valid, slot_src = _routing_tables(route2, counts, blk)
    y = _experts(h2.reshape(T, D_MODEL), block_e, n_valid, slot_src, p["w_gu"], p["b_gu"], p["w_down"],
                 p["b_down"], blk)
    out = _combine(y, route, x1, mod, p["g_post_ffn"], min(tm, 256))
    k_new = kf.reshape(B, L, N_HEADS, 2 * HEAD_DIM)
    v_new = vf.reshape(B, L, N_HEADS, V_DIM)
    return out, k_new, v_new, state[:, CONV_HALO - (CONV_W - 1):, :]


def _prepare_params(l, w_ada, b_ada, g_pre_mix, g_post_mix, w_in, lambda_q1, lambda_k1, lambda_q2, lambda_k2,
                    g_subln, w_dw, b_dw, g_cnorm, b_cnorm, w_out, g_pre_ffn, g_post_ffn,
                    w_router, b_router, w_gu, b_gu, w_down, b_down):
    lam_rows = jnp.zeros((8, LANES), F32)
    for r, vec in enumerate((lambda_q1[l], lambda_k1[l], lambda_q2[l], lambda_k2[l])):
        lam_rows = lam_rows.at[r, :HEAD_DIM].set(vec)
    ch = jnp.arange(CONV_CH, dtype=I32) // GROUP_CH
    gmat = (ch[:, None] == ch[None, :]).astype(BF16)
    w_dw_pad = jnp.zeros((CONV_HALO, CONV_CH), F32).at[:CONV_W].set(w_dw[l])
    w_router_pad = jnp.zeros((D_MODEL, LANES), F32).at[:, :N_EXPERTS].set(w_router[l])
    b_router_pad = jnp.full((1, LANES), NEG, F32).at[0, :N_EXPERTS].set(b_router[l])
    return dict(w_ada=w_ada[l], b_ada=b_ada[l], g_pre_mix=g_pre_mix[l], g_post_mix=g_post_mix[l],
                w_in=w_in[l].astype(BF16), lam_rows=lam_rows, g_subln=g_subln[l], w_dw=w_dw_pad, b_dw=b_dw[l],
                gmat=gmat, g_cn=g_cnorm[l], b_cn=b_cnorm[l], w_out=w_out[l].astype(BF16),
                g_pre_ffn=g_pre_ffn[l], g_post_ffn=g_post_ffn[l], w_router=w_router_pad, b_router=b_router_pad,
                w_gu=w_gu[l].astype(BF16), b_gu=b_gu[l], w_down=w_down[l].astype(BF16), b_down=b_down[l])


def kernel(x_prompt, x_sample, cache_k, cache_v, state_conv, c_prompt, c_sample, w_ada, b_ada, g_pre_mix, g_post_mix, w_in, lambda_q1, lambda_k1, lambda_q2, lambda_k2, g_subln, w_dw, b_dw, g_cnorm, b_cnorm, w_out, g_pre_ffn, g_post_ffn, w_router, b_router, w_gu, b_gu, w_down, b_down):
    depth = w_ada.shape[0]
    Bp, Lp, _ = x_prompt.shape
    Bs, Ls, _ = x_sample.shape
    yp, ys = x_prompt, x_sample
    outs = [[] for _ in range(6)]
    for l in range(depth):
        p = _prepare_params(l, w_ada, b_ada, g_pre_mix, g_post_mix, w_in, lambda_q1, lambda_k1, lambda_q2,
                            lambda_k2, g_subln, w_dw, b_dw, g_cnorm, b_cnorm, w_out, g_pre_ffn, g_post_ffn,
                            w_router, b_router, w_gu, b_gu, w_down, b_down)
        lam_init = 0.8 - 0.6 * math.exp(-0.3 * l)
        mod = _ada(jnp.concatenate([c_prompt, c_sample], axis=0), p["w_ada"], p["b_ada"])
        mod = mod.reshape(Bp + Bs, N_MOD, D_MODEL)
        buf_p = jnp.zeros((Bp, CONV_HALO, CONV_CH), F32)
        buf_s = jnp.pad(state_conv[l], ((0, 0), (CONV_HALO - (CONV_W - 1), 0), (0, 0)))
        yp, kp, vp, cp = _layer(yp, mod[:Bp], buf_p, None, None, p, lam_init,
                                tm=min(Lp, 512), tq=min(Lp, 512), blk=256)
        ys, ks, vs, cs = _layer(ys, mod[Bp:], buf_s, cache_k[l], cache_v[l], p, lam_init,
                                tm=Ls, tq=Ls, blk=256)
        for lst, val in zip(outs, (kp, vp, cp, ks, vs, cs)):
            lst.append(val)
    return (yp, ys) + tuple(jnp.stack(o) for o in outs)
```

```python
import functools
import math

import jax
import jax.numpy as jnp
from jax import lax
from jax.experimental import pallas as pl
from jax.experimental.pallas import tpu as pltpu

F32 = jnp.float32
BF16 = jnp.bfloat16
I32 = jnp.int32

D_MODEL = 1024
CHUNK = 64
CHUNK_SHIFT = 6
N_HEADS = 4
V_DIM = 128
HEAD_DIM = 64
QK_W = 512
ATTN_W = 512
CONV_CH = 512
CONV_W = 31
CONV_GROUPS = 8
GROUP_CH = CONV_CH // CONV_GROUPS
IN_W = 2 * QK_W + ATTN_W + 2 * CONV_CH
N_MOD = 6
N_EXPERTS = 32
TOP_K = 4
D_FF = 1024
SWIGLU_ALPHA = 1.702
SWIGLU_LIMIT = 7.0
EPS = 1e-6

LANES = 128
CONV_HALO = 32
CONV_ROWS = 32
NEG = -1e30
LOG2E = math.log2(math.e)
ATTN_ROWS = 128
EXPERT_ROWS = 256
VMEM_LIMIT = 48 * 1024 * 1024


def _sigmoid(x):
    return 1.0 / (1.0 + jnp.exp(-x))


def _split_bf16(x):
    hi = x.astype(BF16)
    lo = (x - hi.astype(F32)).astype(BF16)
    return hi, lo


def _dot(a, b):
    return jnp.dot(a, b, preferred_element_type=F32)


def _dot3(a, b):
    ah, al = _split_bf16(a)
    bh, bl = _split_bf16(b)
    return _dot(ah, bh) + _dot(ah, bl) + _dot(al, bh)


def _rms(x):
    return x * lax.rsqrt(jnp.mean(x * x, axis=-1, keepdims=True) + EPS)


def _ada_kernel(c_ref, w_ref, b_ref, o_ref):
    c = c_ref[...]
    o_ref[...] = _dot3(c * _sigmoid(c), w_ref[...]) + b_ref[...]


def _ada(c, w_ada, b_ada):
    n = c.shape[0]
    return pl.pallas_call(
        _ada_kernel,
        grid=(N_MOD,),
        in_specs=[pl.BlockSpec((n, D_MODEL), lambda j: (0, 0)),
                  pl.BlockSpec((D_MODEL, D_MODEL), lambda j: (0, j)),
                  pl.BlockSpec((1, D_MODEL), lambda j: (0, j))],
        out_specs=pl.BlockSpec((n, D_MODEL), lambda j: (0, j)),
        out_shape=jax.ShapeDtypeStruct((n, N_MOD * D_MODEL), F32),
        compiler_params=pltpu.CompilerParams(dimension_semantics=("arbitrary",), vmem_limit_bytes=VMEM_LIMIT),
        name="ada",
    )(c, w_ada, b_ada.reshape(1, N_MOD * D_MODEL))


def _premix_kernel(x_ref, mod_ref, g_ref, w_ref, buf_ref, wdw_ref, bdw_ref, gmat_ref, gcn_ref, bcn_ref,
                   q_ref, kb_ref, vb_ref, kf_ref, vf_ref, co_ref, st_ref, ext_ref, y_ref, *, tm):
    s = pl.program_id(1)
    x = x_ref[0]
    shift = mod_ref[0, 0:1, :]
    scale = mod_ref[0, 1:2, :]
    h = _rms(x) * g_ref[...] * (1.0 + scale) + shift
    hb = h.astype(BF16)

    zq = _dot(hb, w_ref[:, 0:QK_W])
    q_ref[0] = (zq * (HEAD_DIM ** -0.5 * LOG2E)).astype(BF16)
    zk = _dot(hb, w_ref[:, QK_W:2 * QK_W])
    kf_ref[0] = zk
    kb_ref[0] = zk.astype(BF16)
    zv = _dot(hb, w_ref[:, 2 * QK_W:2 * QK_W + ATTN_W])
    vf_ref[0] = zv
    vb_ref[0] = zv.astype(BF16)
    u0 = 2 * QK_W + ATTN_W
    val = _dot(hb, w_ref[:, u0:u0 + CONV_CH])
    gate = _dot(hb, w_ref[:, u0 + CONV_CH:u0 + 2 * CONV_CH])
    glu = val * _sigmoid(gate)

    @pl.when(s == 0)
    def _():
        ext_ref[0:CONV_HALO, :] = buf_ref[0]

    ext_ref[CONV_HALO:CONV_HALO + tm, :] = glu

    off = CONV_HALO - (CONV_W - 1)
    for c in range(tm // CONV_ROWS):
        r0 = c * CONV_ROWS
        acc = jnp.zeros((CONV_ROWS, CONV_CH), F32)
        for j in range(CONV_W):
            acc = acc + wdw_ref[j:j + 1, :] * ext_ref[r0 + j + off:r0 + j + off + CONV_ROWS, :]
        y_ref[r0:r0 + CONV_ROWS, :] = acc + bdw_ref[...]

    y = y_ref[...]
    gm = gmat_ref[...]
    yh, yl = _split_bf16(y)
    mu = (_dot(yh, gm) + _dot(yl, gm)) * (1.0 / GROUP_CH)
    d = y - mu
    dh, dl = _split_bf16(d * d)
    var = (_dot(dh, gm) + _dot(dl, gm)) * (1.0 / GROUP_CH)
    yn = d * lax.rsqrt(var + EPS) * gcn_ref[...] + bcn_ref[...]
    co_ref[0] = (yn * _sigmoid(yn)).astype(BF16)

    tail = ext_ref[tm:tm + CONV_HALO, :]
    st_ref[0] = tail
    ext_ref[0:CONV_HALO, :] = tail


def _premix(x, mod, g_pre_mix, w_in_bf, buf, w_dw, b_dw, gmat, g_cn, b_cn, tm):
    B, L, _ = x.shape
    assert L % tm == 0 and tm % CONV_ROWS == 0
    row = lambda b, s: (b, s, 0)
    const2 = lambda b, s: (0, 0)
    bf_tile = jax.ShapeDtypeStruct((B, L, QK_W), BF16)
    f_tile = jax.ShapeDtypeStruct((B, L, QK_W), F32)
    return pl.pallas_call(
        functools.partial(_premix_kernel, tm=tm),
        grid=(B, L // tm),
        in_specs=[pl.BlockSpec((1, tm, D_MODEL), row),
                  pl.BlockSpec((1, N_MOD, D_MODEL), lambda b, s: (b, 0, 0)),
                  pl.BlockSpec((1, D_MODEL), const2),
                  pl.BlockSpec((D_MODEL, IN_W), const2),
                  pl.BlockSpec((1, CONV_HALO, CONV_CH), lambda b, s: (b, 0, 0)),
                  pl.BlockSpec((CONV_HALO, CONV_CH), const2),
                  pl.BlockSpec((1, CONV_CH), const2),
                  pl.BlockSpec((CONV_CH, CONV_CH), const2),
                  pl.BlockSpec((1, CONV_CH), const2),
                  pl.BlockSpec((1, CONV_CH), const2)],
        out_specs=[pl.BlockSpec((1, tm, QK_W), row)] * 6
                  + [pl.BlockSpec((1, CONV_HALO, CONV_CH), lambda b, s: (b, 0, 0))],
        out_shape=[bf_tile, bf_tile, bf_tile, f_tile, f_tile, bf_tile,
                   jax.ShapeDtypeStruct((B, CONV_HALO, CONV_CH), F32)],
        scratch_shapes=[pltpu.VMEM((CONV_HALO + tm, CONV_CH), F32),
                        pltpu.VMEM((tm, CONV_CH), F32)],
        compiler_params=pltpu.CompilerParams(dimension_semantics=("arbitrary", "arbitrary"),
                                             vmem_limit_bytes=VMEM_LIMIT),
        name="premix",
    )(x, mod, g_pre_mix.reshape(1, D_MODEL), w_in_bf, buf, w_dw, b_dw.reshape(1, CONV_CH), gmat,
      g_cn.reshape(1, CONV_CH), b_cn.reshape(1, CONV_CH))


def _attn_kernel(slopes_ref, q_ref, k_ref, v_ref, kb_ref, mt_ref, lam_ref, gs_ref, o_ref,
                 ke1_ref, ke2_ref, ve_ref, m1_ref, a1_ref, m2_ref, a2_ref,
                 *, tq, tk, rows, q_offset, causal, lam_init):
    slope = slopes_ref[pl.program_id(1)] * LOG2E
    lq = q_ref.shape[1]
    lk = k_ref.shape[1]
    nq = lq // tq
    lane = lax.broadcasted_iota(I32, (1, V_DIM), 1)

    low = jnp.broadcast_to(lane < HEAD_DIM, (lk, V_DIM))
    k = k_ref[0]
    zero = jnp.zeros((lk, V_DIM), BF16)
    ke1_ref[...] = jnp.where(low, k, zero)
    ke2_ref[...] = jnp.where(low, zero, k)
    ve_ref[:, 0:V_DIM] = v_ref[0]
    ve_ref[:, V_DIM:2 * V_DIM] = jnp.broadcast_to(jnp.where(lane == 0, 1.0, 0.0), (lk, V_DIM)).astype(BF16)

    nt = (((1,), (1,)), ((), ()))
    maps = ((ke1_ref, m1_ref, a1_ref), (ke2_ref, m2_ref, a2_ref))

    def lanes(x, n):
        if n % LANES == 0:
            return jnp.concatenate([x] * (n // LANES), axis=1)
        return jnp.broadcast_to(x[:, 0:1], (rows, n))

    def tile(q0, k0, masked):
        ve = ve_ref[pl.ds(k0, tk), :]
        q_first = (lax.broadcasted_iota(I32, (1, 1), 0) + (q0 + q_offset)).astype(F32)
        bias = kb_ref[0, :, pl.ds(k0, tk)] - slope * q_first
        for ke_ref, m_ref, a_ref in maps:
            ke = ke_ref[pl.ds(k0, tk), :]
            for r0 in range(0, tq, rows):
                s = lax.dot_general(q_ref[0, pl.ds(q0 + r0, rows), :], ke, nt, preferred_element_type=F32)
                s = s + bias
                if masked:
                    s = s + mt_ref[0, r0:r0 + rows, :]
                m_old = m_ref[r0:r0 + rows, :]
                m_new = jnp.maximum(m_old, jnp.max(s, axis=-1, keepdims=True))
                p = jnp.exp2(s - lanes(m_new, tk))
                alpha = jnp.exp2(m_old - m_new)
                a_ref[r0:r0 + rows, :] = (lanes(alpha, 2 * V_DIM) * a_ref[r0:r0 + rows, :]
                                          + _dot(p.astype(BF16), ve))
                m_ref[r0:r0 + rows, :] = m_new

    def reset():
        for _, m_ref, a_ref in maps:
            m_ref[...] = jnp.full(m_ref.shape, NEG, F32)
            a_ref[...] = jnp.zeros(a_ref.shape, F32)

    lv = lam_ref[...]
    lam = (jnp.exp(jnp.sum(lv[0:1, :] * lv[1:2, :], axis=-1, keepdims=True))
           - jnp.exp(jnp.sum(lv[2:3, :] * lv[3:4, :], axis=-1, keepdims=True)) + lam_init)

    def finish(q0):
        a1 = a1_ref[...]
        a2 = a2_ref[...]
        o = a1[:, 0:V_DIM] / a1[:, V_DIM:V_DIM + 1] - lam * (a2[:, 0:V_DIM] / a2[:, V_DIM:V_DIM + 1])
        o_ref[0, pl.ds(q0, tq), :] = (_rms(o) * gs_ref[...] * (1.0 - lam_init)).astype(BF16)

    if not causal:
        reset()
        tile(0, 0, True)
        finish(0)
        return

    def q_tile(qi, carry):
        q0 = pl.multiple_of(qi * tq, tq)
        reset()

        def pair(i, c):
            tile(q0, pl.multiple_of(2 * i * tk, tk), False)
            tile(q0, pl.multiple_of((2 * i + 1) * tk, tk), False)
            return c

        lax.fori_loop(0, qi // 2, pair, 0)
        odd = qi % 2 == 1

        @pl.when(odd)
        def _():
            tile(q0, pl.multiple_of((qi - 1) * tk, tk), False)
            tile(q0, pl.multiple_of(qi * tk, tk), True)

        @pl.when(jnp.logical_not(odd))
        def _():
            tile(q0, pl.multiple_of(qi * tk, tk), True)

        finish(q0)
        return carry

    lax.fori_loop(0, nq, q_tile, 0)


def _mask_table(slopes, tq, tk, q_offset):
    qpos = q_offset + jnp.arange(tq, dtype=I32)[:, None]
    kpos = jnp.arange(tk, dtype=I32)[None, :]
    visible = jnp.right_shift(kpos, CHUNK_SHIFT) <= jnp.right_shift(qpos, CHUNK_SHIFT)
    fix = jnp.where(kpos > qpos, 2 * (qpos - kpos), 0).astype(F32)
    return jnp.where(visible[None], (slopes * LOG2E)[:, None, None] * fix[None], NEG)


def _attention(q, k, v, lam_rows, g_subln, lam_init, tq, tk, q_offset, causal):
    B, Lq, _ = q.shape
    Lk = k.shape[1]
    assert Lq % tq == 0 and Lk % tk == 0 and (not causal or (tq == tk and tq % CHUNK == 0 and q_offset == 0))
    assert causal or Lk == tk
    rows = min(tq, ATTN_ROWS)
    slopes = jnp.asarray([2.0 ** (-8.0 * (h + 1) / N_HEADS) for h in range(N_HEADS)], F32)
    table = _mask_table(slopes, tq, tk, q_offset)
    key_bias = ((slopes * LOG2E)[:, None] * jnp.arange(Lk, dtype=F32)[None, :]).reshape(N_HEADS, 1, Lk)
    key_ext = pltpu.VMEM((Lk, V_DIM), BF16)
    stat = pltpu.VMEM((tq, LANES), F32)
    acc = pltpu.VMEM((tq, 2 * V_DIM), F32)
    head = lambda b, h, sl: (b, 0, h)
    return pl.pallas_call(
        functools.partial(_attn_kernel, tq=tq, tk=tk, rows=rows, q_offset=q_offset, causal=causal,
                          lam_init=lam_init),
        grid_spec=pltpu.PrefetchScalarGridSpec(
            num_scalar_prefetch=1,
            grid=(B, N_HEADS),
            in_specs=[pl.BlockSpec((1, Lq, V_DIM), head),
                      pl.BlockSpec((1, Lk, V_DIM), head),
                      pl.BlockSpec((1, Lk, V_DIM), head),
                      pl.BlockSpec((1, 1, Lk), lambda b, h, sl: (h, 0, 0)),
                      pl.BlockSpec((1, tq, tk), lambda b, h, sl: (h, 0, 0)),
                      pl.BlockSpec((8, LANES), lambda b, h, sl: (0, 0)),
                      pl.BlockSpec((1, V_DIM), lambda b, h, sl: (0, 0))],
            out_specs=pl.BlockSpec((1, Lq, V_DIM), head),
            scratch_shapes=[key_ext, key_ext, pltpu.VMEM((Lk, 2 * V_DIM), BF16), stat, acc, stat, acc]),
        out_shape=jax.ShapeDtypeStruct((B, Lq, ATTN_W), BF16),
        compiler_params=pltpu.CompilerParams(dimension_semantics=("arbitrary",) * 2,
                                             vmem_limit_bytes=VMEM_LIMIT),
        name="attn",
    )(slopes, q, k, v, key_bias, table, lam_rows, g_subln.reshape(1, V_DIM))


def _postmix_kernel(a_ref, c_ref, x_ref, mod_ref, wo_ref, gpm_ref, gpf_ref, wr_ref, br_ref,
                    x1_ref, h2_ref, rt_ref, cnt_ref, carry_ref, *, tm):
    first = jnp.logical_and(pl.program_id(0) == 0, pl.program_id(1) == 0)

    @pl.when(first)
    def _():
        carry_ref[...] = jnp.zeros(carry_ref.shape, F32)

    mix = _dot(a_ref[0], wo_ref[0:ATTN_W, :]) + _dot(c_ref[0], wo_ref[ATTN_W:ATTN_W + CONV_CH, :])
    gate_a = mod_ref[0, 2:3, :]
    shift_f = mod_ref[0, 3:4, :]
    scale_f = mod_ref[0, 4:5, :]
    x1 = x_ref[0] + gate_a * (_rms(mix) * gpm_ref[...])
    x1_ref[0] = x1
    h2 = _rms(x1) * gpf_ref[...] * (1.0 + scale_f) + shift_f
    h2_ref[0] = h2

    logits = _dot3(h2, wr_ref[...]) + br_ref[...]
    lane = lax.broadcasted_iota(I32, (tm, LANES), 1).astype(F32)
    vals, idxs = [], []
    for _ in range(TOP_K):
        m = jnp.max(logits, axis=-1, keepdims=True)
        idx = jnp.min(jnp.where(logits == m, lane, float(LANES)), axis=-1, keepdims=True)
        vals.append(m)
        idxs.append(idx)
        logits = jnp.where(lane == idx, 2.0 * NEG, logits)
    es = [jnp.exp(v - vals[0]) for v in vals]
    denom = es[0] + es[1] + es[2] + es[3]

    onehot = jnp.zeros((tm, LANES), F32)
    for idx in idxs:
        onehot = jnp.where(lane == idx, 1.0, onehot)
    r_i = lax.broadcasted_iota(I32, (tm, tm), 0)
    c_i = lax.broadcasted_iota(I32, (tm, tm), 1)
    tri = jnp.where(c_i < r_i, 1.0, 0.0).astype(BF16)
    before = _dot(tri, onehot.astype(BF16)) + carry_ref[0:1, :]

    rt = jnp.zeros((tm, LANES), F32)
    for k in range(TOP_K):
        rank = jnp.sum(jnp.where(lane == idxs[k], before, 0.0), axis=-1, keepdims=True)
        rt = jnp.where(lane == k, idxs[k], rt)
        rt = jnp.where(lane == TOP_K + k, es[k] / denom, rt)
        rt = jnp.where(lane == 2 * TOP_K + k, rank, rt)
    rt_ref[0] = rt

    carry_ref[...] = carry_ref[...] + jnp.sum(onehot, axis=0, keepdims=True)
    cnt_ref[...] = carry_ref[...]


def _postmix(attn, conv, x, mod, w_out_bf, g_post_mix, g_pre_ffn, w_router_pad, b_router_pad, tm):
    B, L, _ = x.shape
    row = lambda b, s: (b, s, 0)
    const2 = lambda b, s: (0, 0)
    return pl.pallas_call(
        functools.partial(_postmix_kernel, tm=tm),
        grid=(B, L // tm),
        in_specs=[pl.BlockSpec((1, tm, ATTN_W), row),
                  pl.BlockSpec((1, tm, CONV_CH), row),
                  pl.BlockSpec((1, tm, D_MODEL), row),
                  pl.BlockSpec((1, N_MOD, D_MODEL), lambda b, s: (b, 0, 0)),
                  pl.BlockSpec((D_MODEL, D_MODEL), const2),
                  pl.BlockSpec((1, D_MODEL), const2),
                  pl.BlockSpec((1, D_MODEL), const2),
                  pl.BlockSpec((D_MODEL, LANES), const2),
                  pl.BlockSpec((1, LANES), const2)],
        out_specs=[pl.BlockSpec((1, tm, D_MODEL), row),
                   pl.BlockSpec((1, tm, D_MODEL), row),
                   pl.BlockSpec((1, tm, LANES), row),
                   pl.BlockSpec((8, LANES), const2)],
        out_shape=[jax.ShapeDtypeStruct((B, L, D_MODEL), F32),
                   jax.ShapeDtypeStruct((B, L, D_MODEL), F32),
                   jax.ShapeDtypeStruct((B, L, LANES), F32),
                   jax.ShapeDtypeStruct((8, LANES), F32)],
        scratch_shapes=[pltpu.VMEM((8, LANES), F32)],
        compiler_params=pltpu.CompilerParams(dimension_semantics=("arbitrary", "arbitrary"),
                                             vmem_limit_bytes=VMEM_LIMIT),
        name="postmix",
    )(attn, conv, x, mod, w_out_bf, g_post_mix.reshape(1, D_MODEL), g_pre_ffn.reshape(1, D_MODEL),
      w_router_pad, b_router_pad)


def _expert_kernel(be_ref, gt_ref, st_ref, h_hbm, wgu0_ref, bgu0_ref, wd0_ref, bd0_ref,
                   wgu1_ref, bgu1_ref, wd1_ref, bd1_ref, y_hbm, x0, x1, y0, y1, gsem, ssem, *, blk):
    j = pl.program_id(0)
    last = pl.num_programs(0) - 1
    half = blk // 2
    ff_half = D_FF // 2

    def gather(part, dst, sem, lo, hi):
        for r in range(lo, hi):
            tok = gt_ref[0, 0, part * blk + r]
            pltpu.make_async_copy(h_hbm.at[pl.ds(tok, 1), :], dst.at[pl.ds(r, 1), :], gsem.at[sem]).start()

    def scatter(part, src, sem, lo, hi):
        for r in range(lo, hi):
            row = st_ref[0, 0, part * blk + r]
            pltpu.make_async_copy(src.at[pl.ds(r, 1), :], y_hbm.at[pl.ds(row, 1), :], ssem.at[sem]).start()

    def wait_gather(dst, sem):
        pltpu.make_async_copy(h_hbm.at[pl.ds(0, blk), :], dst, gsem.at[sem]).wait()

    def wait_scatter(src, sem):
        pltpu.make_async_copy(src, y_hbm.at[pl.ds(0, blk), :], ssem.at[sem]).wait()

    @pl.when(j == 0)
    def _():
        y1[...] = jnp.zeros(y1.shape, F32)
        gather(0, x0, 0, 0, blk)

    def block(xc, sem_c, xn, sem_n, gpart, yp, spart, wgu_ref, bgu_ref, wd_ref):
        wait_gather(xc, sem_c)
        xb = xc[...].astype(BF16)
        acc = None
        for c in range(2):
            scatter(spart, yp, sem_n, c * half, (c + 1) * half)
            gather(gpart, xn, sem_n, c * half, (c + 1) * half)
            lo, hi = c * ff_half, (c + 1) * ff_half
            g = _dot(xb, wgu_ref[0, :, lo:hi]) + bgu_ref[0, :, lo:hi]
            lin = _dot(xb, wgu_ref[0, :, D_FF + lo:D_FF + hi]) + bgu_ref[0, :, D_FF + lo:D_FF + hi]
            g = jnp.minimum(g, SWIGLU_LIMIT)
            lin = jnp.clip(lin, -SWIGLU_LIMIT, SWIGLU_LIMIT)
            act = g * _sigmoid(SWIGLU_ALPHA * g) * (lin + 1.0)
            part = _dot(act.astype(BF16), wd_ref[0, lo:hi, :])
            acc = part if acc is None else acc + part
        return acc

    acc = block(x0, 0, x1, 1, 1, y1, 0, wgu0_ref, bgu0_ref, wd0_ref)

    @pl.when(j > 0)
    def _():
        wait_scatter(y0, 0)

    y0[...] = acc + bd0_ref[0]
    acc = block(x1, 1, x0, 0, 2, y0, 1, wgu1_ref, bgu1_ref, wd1_ref)
    wait_scatter(y1, 1)
    y1[...] = acc + bd1_ref[0]

    @pl.when(j == last)
    def _():
        scatter(2, y1, 1, 0, blk)
        wait_scatter(y0, 0)
        wait_scatter(y1, 1)
        wait_gather(x0, 0)


def _experts(h2, block_e, gather_tok, scatter_row, w_gu_bf, b_gu, w_down_bf, b_down, blk):
    T = h2.shape[0]
    n_steps = block_e.shape[0] // 2
    smem = lambda: pl.BlockSpec((1, 1, 3 * blk), lambda j, be: (j, 0, 0), memory_space=pltpu.SMEM)
    weights = []
    for par in range(2):
        ex = lambda j, be, par=par: (be[2 * j + par], 0, 0)
        weights += [pl.BlockSpec((1, D_MODEL, 2 * D_FF), ex), pl.BlockSpec((1, 1, 2 * D_FF), ex),
                    pl.BlockSpec((1, D_FF, D_MODEL), ex), pl.BlockSpec((1, 1, D_MODEL), ex)]
    b_gu3 = b_gu.reshape(N_EXPERTS, 1, 2 * D_FF)
    b_down3 = b_down.reshape(N_EXPERTS, 1, D_MODEL)
    buf = pltpu.VMEM((blk, D_MODEL), F32)
    return pl.pallas_call(
        functools.partial(_expert_kernel, blk=blk),
        grid_spec=pltpu.PrefetchScalarGridSpec(
            num_scalar_prefetch=1,
            grid=(n_steps,),
            in_specs=[smem(), smem(), pl.BlockSpec(memory_space=pl.ANY)] + weights,
            out_specs=pl.BlockSpec(memory_space=pl.ANY),
            scratch_shapes=[buf, buf, buf, buf, pltpu.SemaphoreType.DMA((2,)), pltpu.SemaphoreType.DMA((2,))]),
        out_shape=jax.ShapeDtypeStruct((T * TOP_K + 2 * blk, D_MODEL), F32),
        compiler_params=pltpu.CompilerParams(dimension_semantics=("arbitrary",), vmem_limit_bytes=VMEM_LIMIT),
        name="experts",
    )(block_e, gather_tok.reshape(n_steps, 1, 3 * blk), scatter_row.reshape(n_steps, 1, 3 * blk), h2,
      w_gu_bf, b_gu3, w_down_bf, b_down3, w_gu_bf, b_gu3, w_down_bf, b_down3)


def _combine_kernel(y0_ref, y1_ref, y2_ref, y3_ref, rt_ref, x1_ref, mod_ref, g_ref, o_ref):
    rt = rt_ref[0]
    f = rt[:, TOP_K:TOP_K + 1] * y0_ref[...]
    for k, y_ref in ((1, y1_ref), (2, y2_ref), (3, y3_ref)):
        f = f + rt[:, TOP_K + k:TOP_K + k + 1] * y_ref[...]
    gate_f = mod_ref[0, 5:6, :]
    o_ref[0] = x1_ref[0] + gate_f * (_rms(f) * g_ref[...])


def _combine(y, route, x1, mod, g_post_ffn, tm):
    B, L, _ = x1.shape
    row = lambda b, s: (b, s, 0)
    tiles = B * L // tm
    y_specs = [pl.BlockSpec((tm, D_MODEL), lambda b, s, k=k: (k * tiles + b * (L // tm) + s, 0))
               for k in range(TOP_K)]
    return pl.pallas_call(
        _combine_kernel,
        grid=(B, L // tm),
        in_specs=y_specs + [
                  pl.BlockSpec((1, tm, LANES), row),
                  pl.BlockSpec((1, tm, D_MODEL), row),
                  pl.BlockSpec((1, N_MOD, D_MODEL), lambda b, s: (b, 0, 0)),
                  pl.BlockSpec((1, D_MODEL), lambda b, s: (0, 0))],
        out_specs=pl.BlockSpec((1, tm, D_MODEL), row),
        out_shape=jax.ShapeDtypeStruct((B, L, D_MODEL), F32),
        compiler_params=pltpu.CompilerParams(dimension_semantics=("arbitrary", "arbitrary"),
                                             vmem_limit_bytes=VMEM_LIMIT),
        name="combine",
    )(y, y, y, y, route, x1, mod, g_post_ffn.reshape(1, D_MODEL))


def _routing_tables(route, counts, blk):
    T = route.shape[0]
    TK = T * TOP_K
    idx = route[:, 0:TOP_K].astype(I32)
    rank = route[:, 2 * TOP_K:3 * TOP_K].astype(I32)
    padded = (counts + blk - 1) // blk * blk
    pad_end = jnp.cumsum(padded)
    pad_start = pad_end - padded
    dest = (pad_start[idx] + rank).reshape(TK)
    n_blocks = -(-TK // blk) + N_EXPERTS
    n_blocks += n_blocks % 2
    n_steps = n_blocks // 2
    slot_a = jnp.full((n_blocks * blk,), -1, I32).at[dest].set(jnp.arange(TK, dtype=I32), unique_indices=True)
    valid = slot_a >= 0
    slot = jnp.arange(n_blocks * blk, dtype=I32)
    spare = TK + (slot // blk % 2) * blk + slot % blk
    tok = jnp.where(valid, slot_a // TOP_K, 0).reshape(n_blocks, blk)
    row = jnp.where(valid, (slot_a % TOP_K) * T + slot_a // TOP_K, spare).reshape(n_blocks, blk)
    tok = jnp.concatenate([tok, jnp.zeros((2, blk), I32)], axis=0)
    before = TK + blk + jnp.arange(blk, dtype=I32)[None, :]
    row = jnp.concatenate([before, row, before], axis=0)
    gather_tok = jnp.concatenate([tok[0:n_blocks:2], tok[1:n_blocks:2], tok[2:n_blocks + 2:2]], axis=1)
    scatter_row = jnp.concatenate([row[0:n_blocks:2], row[1:n_blocks + 1:2], row[2:n_blocks + 2:2]], axis=1)
    blk_start = jnp.arange(n_blocks, dtype=I32) * blk
    block_e = jnp.sum((blk_start[:, None] >= pad_end[None, :]).astype(I32), axis=1)
    block_e = jnp.minimum(block_e, N_EXPERTS - 1)
    assert gather_tok.shape == (n_steps, 3 * blk) and scatter_row.shape == (n_steps, 3 * blk)
    return block_e, gather_tok, scatter_row


def _layer(x, mod, buf, past_k, past_v, p, lam_init, tm, tq, blk):
    B, L, _ = x.shape
    q, kb, vb, kf, vf, conv, state = _premix(x, mod, p["g_pre_mix"], p["w_in"], buf, p["w_dw"], p["b_dw"],
                                             p["gmat"], p["g_cn"], p["b_cn"], tm)
    if past_k is None:
        attn = _attention(q, kb, vb, p["lam_rows"], p["g_subln"], lam_init, tq, tq, 0, True)
    else:
        P = past_k.shape[1]
        keys = jnp.concatenate([past_k.reshape(B, P, QK_W).astype(BF16), kb], axis=1)
        vals = jnp.concatenate([past_v.reshape(B, P, ATTN_W).astype(BF16), vb], axis=1)
        attn = _attention(q, keys, vals, p["lam_rows"], p["g_subln"], lam_init, L, P + L, P, False)
    x1, h2, route, cnt = _postmix(attn, conv, x, mod, p["w_out"], p["g_post_mix"], p["g_pre_ffn"],
                                  p["w_router"], p["b_router"], tm)
    T = B * L
    route2 = route.reshape(T, LANES)
    counts = cnt[0, :N_EXPERTS].astype(I32)
    block_e, gather_tok, scatter_row = _routing_tables(route2, counts, blk)
    y = _experts(h2.reshape(T, D_MODEL), block_e, gather_tok, scatter_row, p["w_gu"], p["b_gu"], p["w_down"],
                 p["b_down"], blk)
    out = _combine(y, route, x1, mod, p["g_post_ffn"], min(tm, 256))
    k_new = kf.reshape(B, L, N_HEADS, 2 * HEAD_DIM)
    v_new = vf.reshape(B, L, N_HEADS, V_DIM)
    return out, k_new, v_new, state[:, CONV_HALO - (CONV_W - 1):, :]


def _prepare_params(l, w_ada, b_ada, g_pre_mix, g_post_mix, w_in, lambda_q1, lambda_k1, lambda_q2, lambda_k2,
                    g_subln, w_dw, b_dw, g_cnorm, b_cnorm, w_out, g_pre_ffn, g_post_ffn,
                    w_router, b_router, w_gu, b_gu, w_down, b_down):
    lam_rows = jnp.zeros((8, LANES), F32)
    for r, vec in enumerate((lambda_q1[l], lambda_k1[l], lambda_q2[l], lambda_k2[l])):
        lam_rows = lam_rows.at[r, :HEAD_DIM].set(vec)
    ch = jnp.arange(CONV_CH, dtype=I32) // GROUP_CH
    gmat = (ch[:, None] == ch[None, :]).astype(BF16)
    w_dw_pad = jnp.zeros((CONV_HALO, CONV_CH), F32).at[:CONV_W].set(w_dw[l])
    w_router_pad = jnp.zeros((D_MODEL, LANES), F32).at[:, :N_EXPERTS].set(w_router[l])
    b_router_pad = jnp.full((1, LANES), NEG, F32).at[0, :N_EXPERTS].set(b_router[l])
    return dict(w_ada=w_ada[l], b_ada=b_ada[l], g_pre_mix=g_pre_mix[l], g_post_mix=g_post_mix[l],
                w_in=w_in[l].astype(BF16), lam_rows=lam_rows, g_subln=g_subln[l], w_dw=w_dw_pad, b_dw=b_dw[l],
                gmat=gmat, g_cn=g_cnorm[l], b_cn=b_cnorm[l], w_out=w_out[l].astype(BF16),
                g_pre_ffn=g_pre_ffn[l], g_post_ffn=g_post_ffn[l], w_router=w_router_pad, b_router=b_router_pad,
                w_gu=w_gu[l].astype(BF16), b_gu=b_gu[l], w_down=w_down[l].astype(BF16), b_down=b_down[l])


def kernel(x_prompt, x_sample, cache_k, cache_v, state_conv, c_prompt, c_sample, w_ada, b_ada, g_pre_mix, g_post_mix, w_in, lambda_q1, lambda_k1, lambda_q2, lambda_k2, g_subln, w_dw, b_dw, g_cnorm, b_cnorm, w_out, g_pre_ffn, g_post_ffn, w_router, b_router, w_gu, b_gu, w_down, b_down):
    depth = w_ada.shape[0]
    Bp, Lp, _ = x_prompt.shape
    Bs, Ls, _ = x_sample.shape
    yp, ys = x_prompt, x_sample
    outs = [[] for _ in range(6)]
    for l in range(depth):
        p = _prepare_params(l, w_ada, b_ada, g_pre_mix, g_post_mix, w_in, lambda_q1, lambda_k1, lambda_q2,
                            lambda_k2, g_subln, w_dw, b_dw, g_cnorm, b_cnorm, w_out, g_pre_ffn, g_post_ffn,
                            w_router, b_router, w_gu, b_gu, w_down, b_down)
        lam_init = 0.8 - 0.6 * math.exp(-0.3 * l)
        mod = _ada(jnp.concatenate([c_prompt, c_sample], axis=0), p["w_ada"], p["b_ada"])
        mod = mod.reshape(Bp + Bs, N_MOD, D_MODEL)
        buf_p = jnp.zeros((Bp, CONV_HALO, CONV_CH), F32)
        buf_s = jnp.pad(state_conv[l], ((0, 0), (CONV_HALO - (CONV_W - 1), 0), (0, 0)))
        yp, kp, vp, cp = _layer(yp, mod[:Bp], buf_p, None, None, p, lam_init,
                                tm=min(Lp, 512), tq=min(Lp, 512), blk=EXPERT_ROWS)
        ys, ks, vs, cs = _layer(ys, mod[Bp:], buf_s, cache_k[l], cache_v[l], p, lam_init,
                                tm=Ls, tq=Ls, blk=EXPERT_ROWS)
        for lst, val in zip(outs, (kp, vp, cp, ks, vs, cs)):
            lst.append(val)
    return (yp, ys) + tuple(jnp.stack(o) for o in outs)
```

```python
import functools
import math

import jax
import jax.numpy as jnp
from jax import lax
from jax.experimental import pallas as pl
from jax.experimental.pallas import tpu as pltpu

F32 = jnp.float32
BF16 = jnp.bfloat16
I32 = jnp.int32

D_MODEL = 1024
CHUNK = 64
CHUNK_SHIFT = 6
N_HEADS = 4
V_DIM = 128
HEAD_DIM = 64
QK_W = 512
ATTN_W = 512
CONV_CH = 512
CONV_W = 31
CONV_GROUPS = 8
GROUP_CH = CONV_CH // CONV_GROUPS
IN_W = 2 * QK_W + ATTN_W + 2 * CONV_CH
N_MOD = 6
N_EXPERTS = 32
TOP_K = 4
D_FF = 1024
SWIGLU_ALPHA = 1.702
SWIGLU_LIMIT = 7.0
EPS = 1e-6

LANES = 128
SUBLANES = 8
CONV_HALO = 32
CONV_ROWS = 32
NEG = -1e30
LOG2E = math.log2(math.e)
ATTN_ROWS = 128
EXPERT_ROWS = 256
VMEM_LIMIT = 48 * 1024 * 1024


def _sigmoid(x):
    return 1.0 / (1.0 + jnp.exp(-x))


def _split_bf16(x):
    hi = x.astype(BF16)
    lo = (x - hi.astype(F32)).astype(BF16)
    return hi, lo


def _dot(a, b):
    return jnp.dot(a, b, preferred_element_type=F32)


def _dot3(a, b):
    ah, al = _split_bf16(a)
    bh, bl = _split_bf16(b)
    return _dot(ah, bh) + _dot(ah, bl) + _dot(al, bh)


def _rms(x):
    return x * lax.rsqrt(jnp.mean(x * x, axis=-1, keepdims=True) + EPS)


def _ada_kernel(c_ref, w_ref, b_ref, o_ref):
    c = c_ref[...]
    o_ref[...] = _dot3(c * _sigmoid(c), w_ref[...]) + b_ref[...]


def _ada(c, w_ada, b_ada):
    n = c.shape[0]
    return pl.pallas_call(
        _ada_kernel,
        grid=(N_MOD,),
        in_specs=[pl.BlockSpec((n, D_MODEL), lambda j: (0, 0)),
                  pl.BlockSpec((D_MODEL, D_MODEL), lambda j: (0, j)),
                  pl.BlockSpec((1, D_MODEL), lambda j: (0, j))],
        out_specs=pl.BlockSpec((n, D_MODEL), lambda j: (0, j)),
        out_shape=jax.ShapeDtypeStruct((n, N_MOD * D_MODEL), F32),
        compiler_params=pltpu.CompilerParams(dimension_semantics=("arbitrary",), vmem_limit_bytes=VMEM_LIMIT),
        name="ada",
    )(c, w_ada, b_ada.reshape(1, N_MOD * D_MODEL))


def _premix_kernel(x_ref, mod_ref, g_ref, w_ref, buf_ref, wdw_ref, bdw_ref, gmat_ref, gcn_ref, bcn_ref,
                   q_ref, kb_ref, vb_ref, kf_ref, vf_ref, co_ref, st_ref, ext_ref, sh_ref, y_ref, *, tm):
    @pl.when(pl.program_id(1) == 0)
    def _():
        ext_ref[0:CONV_HALO, :] = buf_ref[0]

    x = x_ref[0]
    shift = mod_ref[0, 0:1, :]
    scale = mod_ref[0, 1:2, :]
    h = _rms(x) * g_ref[...] * (1.0 + scale) + shift
    hb = h.astype(BF16)

    u0 = 2 * QK_W + ATTN_W
    val = _dot(hb, w_ref[:, u0:u0 + CONV_CH])
    gate = _dot(hb, w_ref[:, u0 + CONV_CH:u0 + 2 * CONV_CH])
    ext_ref[CONV_HALO:CONV_HALO + tm, :] = val * _sigmoid(gate)

    off = CONV_HALO - (CONV_W - 1)
    span = tm + CONV_HALO - SUBLANES
    for b in range(1, SUBLANES):
        sh_ref[b - 1, 0:span, :] = ext_ref[b:b + span, :]
    for c in range(tm // CONV_ROWS):
        r0 = c * CONV_ROWS
        acc = jnp.zeros((CONV_ROWS, CONV_CH), F32)
        for j in range(CONV_W):
            b = (j + off) % SUBLANES
            a = r0 + j + off - b
            rows_j = ext_ref[a:a + CONV_ROWS, :] if b == 0 else sh_ref[b - 1, a:a + CONV_ROWS, :]
            acc = acc + wdw_ref[j:j + 1, :] * rows_j
        y_ref[r0:r0 + CONV_ROWS, :] = acc + bdw_ref[...]

    zq = _dot(hb, w_ref[:, 0:QK_W])
    q_ref[0] = (zq * (HEAD_DIM ** -0.5 * LOG2E)).astype(BF16)
    zk = _dot(hb, w_ref[:, QK_W:2 * QK_W])
    kf_ref[0] = zk
    kb_ref[0] = zk.astype(BF16)
    zv = _dot(hb, w_ref[:, 2 * QK_W:2 * QK_W + ATTN_W])
    vf_ref[0] = zv
    vb_ref[0] = zv.astype(BF16)

    y = y_ref[...]
    gm = gmat_ref[...]
    yh, yl = _split_bf16(y)
    mu = (_dot(yh, gm) + _dot(yl, gm)) * (1.0 / GROUP_CH)
    d = y - mu
    dh, dl = _split_bf16(d * d)
    var = (_dot(dh, gm) + _dot(dl, gm)) * (1.0 / GROUP_CH)
    yn = d * lax.rsqrt(var + EPS) * gcn_ref[...] + bcn_ref[...]
    co_ref[0] = (yn * _sigmoid(yn)).astype(BF16)

    tail = ext_ref[tm:tm + CONV_HALO, :]
    st_ref[0] = tail
    ext_ref[0:CONV_HALO, :] = tail


def _premix(x, mod, g_pre_mix, w_in_bf, buf, w_dw, b_dw, gmat, g_cn, b_cn, tm):
    B, L, _ = x.shape
    assert L % tm == 0 and tm % CONV_ROWS == 0
    row = lambda b, s: (b, s, 0)
    const2 = lambda b, s: (0, 0)
    bf_tile = jax.ShapeDtypeStruct((B, L, QK_W), BF16)
    f_tile = jax.ShapeDtypeStruct((B, L, QK_W), F32)
    return pl.pallas_call(
        functools.partial(_premix_kernel, tm=tm),
        grid=(B, L // tm),
        in_specs=[pl.BlockSpec((1, tm, D_MODEL), row),
                  pl.BlockSpec((1, N_MOD, D_MODEL), lambda b, s: (b, 0, 0)),
                  pl.BlockSpec((1, D_MODEL), const2),
                  pl.BlockSpec((D_MODEL, IN_W), const2),
                  pl.BlockSpec((1, CONV_HALO, CONV_CH), lambda b, s: (b, 0, 0)),
                  pl.BlockSpec((CONV_HALO, CONV_CH), const2),
                  pl.BlockSpec((1, CONV_CH), const2),
                  pl.BlockSpec((CONV_CH, CONV_CH), const2),
                  pl.BlockSpec((1, CONV_CH), const2),
                  pl.BlockSpec((1, CONV_CH), const2)],
        out_specs=[pl.BlockSpec((1, tm, QK_W), row)] * 6
                  + [pl.BlockSpec((1, CONV_HALO, CONV_CH), lambda b, s: (b, 0, 0))],
        out_shape=[bf_tile, bf_tile, bf_tile, f_tile, f_tile, bf_tile,
                   jax.ShapeDtypeStruct((B, CONV_HALO, CONV_CH), F32)],
        scratch_shapes=[pltpu.VMEM((CONV_HALO + tm, CONV_CH), F32),
                        pltpu.VMEM((SUBLANES - 1, CONV_HALO + tm, CONV_CH), F32),
                        pltpu.VMEM((tm, CONV_CH), F32)],
        compiler_params=pltpu.CompilerParams(dimension_semantics=("arbitrary", "arbitrary"),
                                             vmem_limit_bytes=VMEM_LIMIT),
        name="premix",
    )(x, mod, g_pre_mix.reshape(1, D_MODEL), w_in_bf, buf, w_dw, b_dw.reshape(1, CONV_CH), gmat,
      g_cn.reshape(1, CONV_CH), b_cn.reshape(1, CONV_CH))


def _attn_kernel(slopes_ref, q_ref, k_ref, v_ref, kb_ref, mt_ref, lam_ref, gs_ref, o_ref,
                 ke1_ref, ke2_ref, ve_ref, m1_ref, a1_ref, m2_ref, a2_ref,
                 *, tq, tk, rows, q_offset, causal, lam_init):
    slope = slopes_ref[pl.program_id(1)] * LOG2E
    lq = q_ref.shape[1]
    lk = k_ref.shape[1]
    nq = lq // tq
    lane = lax.broadcasted_iota(I32, (1, V_DIM), 1)

    low = jnp.broadcast_to(lane < HEAD_DIM, (lk, V_DIM))
    k = k_ref[0]
    zero = jnp.zeros((lk, V_DIM), BF16)
    ke1_ref[...] = jnp.where(low, k, zero)
    ke2_ref[...] = jnp.where(low, zero, k)
    ve_ref[:, 0:V_DIM] = v_ref[0]
    ve_ref[:, V_DIM:2 * V_DIM] = jnp.broadcast_to(jnp.where(lane == 0, 1.0, 0.0), (lk, V_DIM)).astype(BF16)

    nt = (((1,), (1,)), ((), ()))
    maps = ((ke1_ref, m1_ref, a1_ref), (ke2_ref, m2_ref, a2_ref))

    def lanes(x, n):
        if n % LANES == 0:
            return jnp.concatenate([x] * (n // LANES), axis=1)
        return jnp.broadcast_to(x[:, 0:1], (rows, n))

    def tile(q0, k0, masked):
        ve = ve_ref[pl.ds(k0, tk), :]
        q_first = (lax.broadcasted_iota(I32, (1, 1), 0) + (q0 + q_offset)).astype(F32)
        bias = kb_ref[0, :, pl.ds(k0, tk)] - slope * q_first
        for ke_ref, m_ref, a_ref in maps:
            ke = ke_ref[pl.ds(k0, tk), :]
            for r0 in range(0, tq, rows):
                s = lax.dot_general(q_ref[0, pl.ds(q0 + r0, rows), :], ke, nt, preferred_element_type=F32)
                s = s + bias
                if masked:
                    s = s + mt_ref[0, r0:r0 + rows, :]
                m_old = m_ref[r0:r0 + rows, :]
                m_new = jnp.maximum(m_old, jnp.max(s, axis=-1, keepdims=True))
                p = jnp.exp2(s - lanes(m_new, tk))
                alpha = jnp.exp2(m_old - m_new)
                a_ref[r0:r0 + rows, :] = (lanes(alpha, 2 * V_DIM) * a_ref[r0:r0 + rows, :]
                                          + _dot(p.astype(BF16), ve))
                m_ref[r0:r0 + rows, :] = m_new

    def reset():
        for _, m_ref, a_ref in maps:
            m_ref[...] = jnp.full(m_ref.shape, NEG, F32)
            a_ref[...] = jnp.zeros(a_ref.shape, F32)

    lv = lam_ref[...]
    lam = (jnp.exp(jnp.sum(lv[0:1, :] * lv[1:2, :], axis=-1, keepdims=True))
           - jnp.exp(jnp.sum(lv[2:3, :] * lv[3:4, :], axis=-1, keepdims=True)) + lam_init)

    def finish(q0):
        a1 = a1_ref[...]
        a2 = a2_ref[...]
        o = a1[:, 0:V_DIM] / a1[:, V_DIM:V_DIM + 1] - lam * (a2[:, 0:V_DIM] / a2[:, V_DIM:V_DIM + 1])
        o_ref[0, pl.ds(q0, tq), :] = (_rms(o) * gs_ref[...] * (1.0 - lam_init)).astype(BF16)

    if not causal:
        reset()
        tile(0, 0, True)
        finish(0)
        return

    def q_tile(qi, carry):
        q0 = pl.multiple_of(qi * tq, tq)
        reset()

        def pair(i, c):
            tile(q0, pl.multiple_of(2 * i * tk, tk), False)
            tile(q0, pl.multiple_of((2 * i + 1) * tk, tk), False)
            return c

        lax.fori_loop(0, qi // 2, pair, 0)
        odd = qi % 2 == 1

        @pl.when(odd)
        def _():
            tile(q0, pl.multiple_of((qi - 1) * tk, tk), False)
            tile(q0, pl.multiple_of(qi * tk, tk), True)

        @pl.when(jnp.logical_not(odd))
        def _():
            tile(q0, pl.multiple_of(qi * tk, tk), True)

        finish(q0)
        return carry

    lax.fori_loop(0, nq, q_tile, 0)


def _mask_table(slopes, tq, tk, q_offset):
    qpos = q_offset + jnp.arange(tq, dtype=I32)[:, None]
    kpos = jnp.arange(tk, dtype=I32)[None, :]
    visible = jnp.right_shift(kpos, CHUNK_SHIFT) <= jnp.right_shift(qpos, CHUNK_SHIFT)
    fix = jnp.where(kpos > qpos, 2 * (qpos - kpos), 0).astype(F32)
    return jnp.where(visible[None], (slopes * LOG2E)[:, None, None] * fix[None], NEG)


def _attention(q, k, v, lam_rows, g_subln, lam_init, tq, tk, q_offset, causal):
    B, Lq, _ = q.shape
    Lk = k.shape[1]
    assert Lq % tq == 0 and Lk % tk == 0 and (not causal or (tq == tk and tq % CHUNK == 0 and q_offset == 0))
    assert causal or Lk == tk
    rows = min(tq, ATTN_ROWS)
    slopes = jnp.asarray([2.0 ** (-8.0 * (h + 1) / N_HEADS) for h in range(N_HEADS)], F32)
    table = _mask_table(slopes, tq, tk, q_offset)
    key_bias = ((slopes * LOG2E)[:, None] * jnp.arange(Lk, dtype=F32)[None, :]).reshape(N_HEADS, 1, Lk)
    key_ext = pltpu.VMEM((Lk, V_DIM), BF16)
    stat = pltpu.VMEM((tq, LANES), F32)
    acc = pltpu.VMEM((tq, 2 * V_DIM), F32)
    head = lambda b, h, sl: (b, 0, h)
    return pl.pallas_call(
        functools.partial(_attn_kernel, tq=tq, tk=tk, rows=rows, q_offset=q_offset, causal=causal,
                          lam_init=lam_init),
        grid_spec=pltpu.PrefetchScalarGridSpec(
            num_scalar_prefetch=1,
            grid=(B, N_HEADS),
            in_specs=[pl.BlockSpec((1, Lq, V_DIM), head),
                      pl.BlockSpec((1, Lk, V_DIM), head),
                      pl.BlockSpec((1, Lk, V_DIM), head),
                      pl.BlockSpec((1, 1, Lk), lambda b, h, sl: (h, 0, 0)),
                      pl.BlockSpec((1, tq, tk), lambda b, h, sl: (h, 0, 0)),
                      pl.BlockSpec((8, LANES), lambda b, h, sl: (0, 0)),
                      pl.BlockSpec((1, V_DIM), lambda b, h, sl: (0, 0))],
            out_specs=pl.BlockSpec((1, Lq, V_DIM), head),
            scratch_shapes=[key_ext, key_ext, pltpu.VMEM((Lk, 2 * V_DIM), BF16), stat, acc, stat, acc]),
        out_shape=jax.ShapeDtypeStruct((B, Lq, ATTN_W), BF16),
        compiler_params=pltpu.CompilerParams(dimension_semantics=("arbitrary",) * 2,
                                             vmem_limit_bytes=VMEM_LIMIT),
        name="attn",
    )(slopes, q, k, v, key_bias, table, lam_rows, g_subln.reshape(1, V_DIM))


def _postmix_kernel(a_ref, c_ref, x_ref, mod_ref, wo_ref, gpm_ref, gpf_ref, wr_ref, br_ref,
                    x1_ref, h2_ref, rt_ref, cnt_ref, carry_ref, *, tm):
    first = jnp.logical_and(pl.program_id(0) == 0, pl.program_id(1) == 0)

    @pl.when(first)
    def _():
        carry_ref[...] = jnp.zeros(carry_ref.shape, F32)

    mix = _dot(a_ref[0], wo_ref[0:ATTN_W, :]) + _dot(c_ref[0], wo_ref[ATTN_W:ATTN_W + CONV_CH, :])
    gate_a = mod_ref[0, 2:3, :]
    shift_f = mod_ref[0, 3:4, :]
    scale_f = mod_ref[0, 4:5, :]
    x1 = x_ref[0] + gate_a * (_rms(mix) * gpm_ref[...])
    x1_ref[0] = x1
    h2 = _rms(x1) * gpf_ref[...] * (1.0 + scale_f) + shift_f
    h2_ref[0] = h2

    logits = _dot3(h2, wr_ref[...]) + br_ref[...]
    lane = lax.broadcasted_iota(I32, (tm, LANES), 1).astype(F32)
    vals, idxs = [], []
    for _ in range(TOP_K):
        m = jnp.max(logits, axis=-1, keepdims=True)
        idx = jnp.min(jnp.where(logits == m, lane, float(LANES)), axis=-1, keepdims=True)
        vals.append(m)
        idxs.append(idx)
        logits = jnp.where(lane == idx, 2.0 * NEG, logits)
    es = [jnp.exp(v - vals[0]) for v in vals]
    denom = es[0] + es[1] + es[2] + es[3]

    onehot = jnp.zeros((tm, LANES), F32)
    for idx in idxs:
        onehot = jnp.where(lane == idx, 1.0, onehot)
    r_i = lax.broadcasted_iota(I32, (tm, tm), 0)
    c_i = lax.broadcasted_iota(I32, (tm, tm), 1)
    tri = jnp.where(c_i < r_i, 1.0, 0.0).astype(BF16)
    before = _dot(tri, onehot.astype(BF16)) + carry_ref[0:1, :]

    rt = jnp.zeros((tm, LANES), F32)
    for k in range(TOP_K):
        rank = jnp.sum(jnp.where(lane == idxs[k], before, 0.0), axis=-1, keepdims=True)
        rt = jnp.where(lane == k, idxs[k], rt)
        rt = jnp.where(lane == TOP_K + k, es[k] / denom, rt)
        rt = jnp.where(lane == 2 * TOP_K + k, rank, rt)
    rt_ref[0] = rt

    carry_ref[...] = carry_ref[...] + jnp.sum(onehot, axis=0, keepdims=True)
    cnt_ref[...] = carry_ref[...]


def _postmix(attn, conv, x, mod, w_out_bf, g_post_mix, g_pre_ffn, w_router_pad, b_router_pad, tm):
    B, L, _ = x.shape
    row = lambda b, s: (b, s, 0)
    const2 = lambda b, s: (0, 0)
    return pl.pallas_call(
        functools.partial(_postmix_kernel, tm=tm),
        grid=(B, L // tm),
        in_specs=[pl.BlockSpec((1, tm, ATTN_W), row),
                  pl.BlockSpec((1, tm, CONV_CH), row),
                  pl.BlockSpec((1, tm, D_MODEL), row),
                  pl.BlockSpec((1, N_MOD, D_MODEL), lambda b, s: (b, 0, 0)),
                  pl.BlockSpec((D_MODEL, D_MODEL), const2),
                  pl.BlockSpec((1, D_MODEL), const2),
                  pl.BlockSpec((1, D_MODEL), const2),
                  pl.BlockSpec((D_MODEL, LANES), const2),
                  pl.BlockSpec((1, LANES), const2)],
        out_specs=[pl.BlockSpec((1, tm, D_MODEL), row),
                   pl.BlockSpec((1, tm, D_MODEL), row),
                   pl.BlockSpec((1, tm, LANES), row),
                   pl.BlockSpec((8, LANES), const2)],
        out_shape=[jax.ShapeDtypeStruct((B, L, D_MODEL), F32),
                   jax.ShapeDtypeStruct((B, L, D_MODEL), F32),
                   jax.ShapeDtypeStruct((B, L, LANES), F32),
                   jax.ShapeDtypeStruct((8, LANES), F32)],
        scratch_shapes=[pltpu.VMEM((8, LANES), F32)],
        compiler_params=pltpu.CompilerParams(dimension_semantics=("arbitrary", "arbitrary"),
                                             vmem_limit_bytes=VMEM_LIMIT),
        name="postmix",
    )(attn, conv, x, mod, w_out_bf, g_post_mix.reshape(1, D_MODEL), g_pre_ffn.reshape(1, D_MODEL),
      w_router_pad, b_router_pad)


def _expert_kernel(be_ref, gt_ref, st_ref, h_hbm, wgu0_ref, bgu0_ref, wd0_ref, bd0_ref,
                   wgu1_ref, bgu1_ref, wd1_ref, bd1_ref, y_hbm, x0, x1, y0, y1, gsem, ssem, *, blk):
    j = pl.program_id(0)
    last = pl.num_programs(0) - 1
    half = blk // 2
    ff_half = D_FF // 2

    def gather(part, dst, sem, lo, hi):
        for r in range(lo, hi):
            tok = gt_ref[0, 0, part * blk + r]
            pltpu.make_async_copy(h_hbm.at[pl.ds(tok, 1), :], dst.at[pl.ds(r, 1), :], gsem.at[sem]).start()

    def scatter(part, src, sem, lo, hi):
        for r in range(lo, hi):
            row = st_ref[0, 0, part * blk + r]
            pltpu.make_async_copy(src.at[pl.ds(r, 1), :], y_hbm.at[pl.ds(row, 1), :], ssem.at[sem]).start()

    def wait_gather(dst, sem):
        pltpu.make_async_copy(h_hbm.at[pl.ds(0, blk), :], dst, gsem.at[sem]).wait()

    def wait_scatter(src, sem):
        pltpu.make_async_copy(src, y_hbm.at[pl.ds(0, blk), :], ssem.at[sem]).wait()

    @pl.when(j == 0)
    def _():
        y1[...] = jnp.zeros(y1.shape, F32)
        gather(0, x0, 0, 0, blk)

    def block(xc, sem_c, xn, sem_n, gpart, yp, spart, wgu_ref, bgu_ref, wd_ref):
        wait_gather(xc, sem_c)
        xb = xc[...].astype(BF16)
        acc = None
        for c in range(2):
            scatter(spart, yp, sem_n, c * half, (c + 1) * half)
            gather(gpart, xn, sem_n, c * half, (c + 1) * half)
            lo, hi = c * ff_half, (c + 1) * ff_half
            g = _dot(xb, wgu_ref[0, :, lo:hi]) + bgu_ref[0, :, lo:hi]
            lin = _dot(xb, wgu_ref[0, :, D_FF + lo:D_FF + hi]) + bgu_ref[0, :, D_FF + lo:D_FF + hi]
            g = jnp.minimum(g, SWIGLU_LIMIT)
            lin = jnp.clip(lin, -SWIGLU_LIMIT, SWIGLU_LIMIT)
            act = g * _sigmoid(SWIGLU_ALPHA * g) * (lin + 1.0)
            part = _dot(act.astype(BF16), wd_ref[0, lo:hi, :])
            acc = part if acc is None else acc + part
        return acc

    acc = block(x0, 0, x1, 1, 1, y1, 0, wgu0_ref, bgu0_ref, wd0_ref)

    @pl.when(j > 0)
    def _():
        wait_scatter(y0, 0)

    y0[...] = acc + bd0_ref[0]
    acc = block(x1, 1, x0, 0, 2, y0, 1, wgu1_ref, bgu1_ref, wd1_ref)
    wait_scatter(y1, 1)
    y1[...] = acc + bd1_ref[0]

    @pl.when(j == last)
    def _():
        scatter(2, y1, 1, 0, blk)
        wait_scatter(y0, 0)
        wait_scatter(y1, 1)
        wait_gather(x0, 0)


def _experts(h2, block_e, gather_tok, scatter_row, w_gu_bf, b_gu, w_down_bf, b_down, blk):
    T = h2.shape[0]
    n_steps = block_e.shape[0] // 2
    smem = lambda: pl.BlockSpec((1, 1, 3 * blk), lambda j, be: (j, 0, 0), memory_space=pltpu.SMEM)
    weights = []
    for par in range(2):
        ex = lambda j, be, par=par: (be[2 * j + par], 0, 0)
        weights += [pl.BlockSpec((1, D_MODEL, 2 * D_FF), ex), pl.BlockSpec((1, 1, 2 * D_FF), ex),
                    pl.BlockSpec((1, D_FF, D_MODEL), ex), pl.BlockSpec((1, 1, D_MODEL), ex)]
    b_gu3 = b_gu.reshape(N_EXPERTS, 1, 2 * D_FF)
    b_down3 = b_down.reshape(N_EXPERTS, 1, D_MODEL)
    buf = pltpu.VMEM((blk, D_MODEL), F32)
    return pl.pallas_call(
        functools.partial(_expert_kernel, blk=blk),
        grid_spec=pltpu.PrefetchScalarGridSpec(
            num_scalar_prefetch=1,
            grid=(n_steps,),
            in_specs=[smem(), smem(), pl.BlockSpec(memory_space=pl.ANY)] + weights,
            out_specs=pl.BlockSpec(memory_space=pl.ANY),
            scratch_shapes=[buf, buf, buf, buf, pltpu.SemaphoreType.DMA((2,)), pltpu.SemaphoreType.DMA((2,))]),
        out_shape=jax.ShapeDtypeStruct((T * TOP_K + 2 * blk, D_MODEL), F32),
        compiler_params=pltpu.CompilerParams(dimension_semantics=("arbitrary",), vmem_limit_bytes=VMEM_LIMIT),
        name="experts",
    )(block_e, gather_tok.reshape(n_steps, 1, 3 * blk), scatter_row.reshape(n_steps, 1, 3 * blk), h2,
      w_gu_bf, b_gu3, w_down_bf, b_down3, w_gu_bf, b_gu3, w_down_bf, b_down3)


def _combine_kernel(y0_ref, y1_ref, y2_ref, y3_ref, rt_ref, x1_ref, mod_ref, g_ref, o_ref):
    rt = rt_ref[0]
    f = rt[:, TOP_K:TOP_K + 1] * y0_ref[...]
    for k, y_ref in ((1, y1_ref), (2, y2_ref), (3, y3_ref)):
        f = f + rt[:, TOP_K + k:TOP_K + k + 1] * y_ref[...]
    gate_f = mod_ref[0, 5:6, :]
    o_ref[0] = x1_ref[0] + gate_f * (_rms(f) * g_ref[...])


def _combine(y, route, x1, mod, g_post_ffn, tm):
    B, L, _ = x1.shape
    row = lambda b, s: (b, s, 0)
    tiles = B * L // tm
    y_specs = [pl.BlockSpec((tm, D_MODEL), lambda b, s, k=k: (k * tiles + b * (L // tm) + s, 0))
               for k in range(TOP_K)]
    return pl.pallas_call(
        _combine_kernel,
        grid=(B, L // tm),
        in_specs=y_specs + [
                  pl.BlockSpec((1, tm, LANES), row),
                  pl.BlockSpec((1, tm, D_MODEL), row),
                  pl.BlockSpec((1, N_MOD, D_MODEL), lambda b, s: (b, 0, 0)),
                  pl.BlockSpec((1, D_MODEL), lambda b, s: (0, 0))],
        out_specs=pl.BlockSpec((1, tm, D_MODEL), row),
        out_shape=jax.ShapeDtypeStruct((B, L, D_MODEL), F32),
        compiler_params=pltpu.CompilerParams(dimension_semantics=("arbitrary", "arbitrary"),
                                             vmem_limit_bytes=VMEM_LIMIT),
        name="combine",
    )(y, y, y, y, route, x1, mod, g_post_ffn.reshape(1, D_MODEL))


def _routing_tables(route, counts, blk):
    T = route.shape[0]
    TK = T * TOP_K
    idx = route[:, 0:TOP_K].astype(I32)
    rank = route[:, 2 * TOP_K:3 * TOP_K].astype(I32)
    padded = (counts + blk - 1) // blk * blk
    pad_end = jnp.cumsum(padded)
    pad_start = pad_end - padded
    dest = (pad_start[idx] + rank).reshape(TK)
    n_blocks = -(-TK // blk) + N_EXPERTS
    n_blocks += n_blocks % 2
    n_steps = n_blocks // 2
    slot_a = jnp.full((n_blocks * blk,), -1, I32).at[dest].set(jnp.arange(TK, dtype=I32), unique_indices=True)
    valid = slot_a >= 0
    slot = jnp.arange(n_blocks * blk, dtype=I32)
    spare = TK + (slot // blk % 2) * blk + slot % blk
    tok = jnp.where(valid, slot_a // TOP_K, 0).reshape(n_blocks, blk)
    row = jnp.where(valid, (slot_a % TOP_K) * T + slot_a // TOP_K, spare).reshape(n_blocks, blk)
    tok = jnp.concatenate([tok, jnp.zeros((2, blk), I32)], axis=0)
    before = TK + blk + jnp.arange(blk, dtype=I32)[None, :]
    row = jnp.concatenate([before, row, before], axis=0)
    gather_tok = jnp.concatenate([tok[0:n_blocks:2], tok[1:n_blocks:2], tok[2:n_blocks + 2:2]], axis=1)
    scatter_row = jnp.concatenate([row[0:n_blocks:2], row[1:n_blocks + 1:2], row[2:n_blocks + 2:2]], axis=1)
    blk_start = jnp.arange(n_blocks, dtype=I32) * blk
    block_e = jnp.sum((blk_start[:, None] >= pad_end[None, :]).astype(I32), axis=1)
    block_e = jnp.minimum(block_e, N_EXPERTS - 1)
    assert gather_tok.shape == (n_steps, 3 * blk) and scatter_row.shape == (n_steps, 3 * blk)
    return block_e, gather_tok, scatter_row


def _layer(x, mod, buf, past_k, past_v, p, lam_init, tm, tq, blk):
    B, L, _ = x.shape
    q, kb, vb, kf, vf, conv, state = _premix(x, mod, p["g_pre_mix"], p["w_in"], buf, p["w_dw"], p["b_dw"],
                                             p["gmat"], p["g_cn"], p["b_cn"], tm)
    if past_k is None:
        attn = _attention(q, kb, vb, p["lam_rows"], p["g_subln"], lam_init, tq, tq, 0, True)
    else:
        P = past_k.shape[1]
        keys = jnp.concatenate([past_k.reshape(B, P, QK_W).astype(BF16), kb], axis=1)
        vals = jnp.concatenate([past_v.reshape(B, P, ATTN_W).astype(BF16), vb], axis=1)
        attn = _attention(q, keys, vals, p["lam_rows"], p["g_subln"], lam_init, L, P + L, P, False)
    x1, h2, route, cnt = _postmix(attn, conv, x, mod, p["w_out"], p["g_post_mix"], p["g_pre_ffn"],
                                  p["w_router"], p["b_router"], tm)
    T = B * L
    route2 = route.reshape(T, LANES)
    counts = cnt[0, :N_EXPERTS].astype(I32)
    block_e, gather_tok, scatter_row = _routing_tables(route2, counts, blk)
    y = _experts(h2.reshape(T, D_MODEL), block_e, gather_tok, scatter_row, p["w_gu"], p["b_gu"], p["w_down"],
                 p["b_down"], blk)
    out = _combine(y, route, x1, mod, p["g_post_ffn"], min(tm, 256))
    k_new = kf.reshape(B, L, N_HEADS, 2 * HEAD_DIM)
    v_new = vf.reshape(B, L, N_HEADS, V_DIM)
    return out, k_new, v_new, state[:, CONV_HALO - (CONV_W - 1):, :]


def _prepare_params(l, w_ada, b_ada, g_pre_mix, g_post_mix, w_in, lambda_q1, lambda_k1, lambda_q2, lambda_k2,
                    g_subln, w_dw, b_dw, g_cnorm, b_cnorm, w_out, g_pre_ffn, g_post_ffn,
                    w_router, b_router, w_gu, b_gu, w_down, b_down):
    lam_rows = jnp.zeros((8, LANES), F32)
    for r, vec in enumerate((lambda_q1[l], lambda_k1[l], lambda_q2[l], lambda_k2[l])):
        lam_rows = lam_rows.at[r, :HEAD_DIM].set(vec)
    ch = jnp.arange(CONV_CH, dtype=I32) // GROUP_CH
    gmat = (ch[:, None] == ch[None, :]).astype(BF16)
    w_dw_pad = jnp.zeros((CONV_HALO, CONV_CH), F32).at[:CONV_W].set(w_dw[l])
    w_router_pad = jnp.zeros((D_MODEL, LANES), F32).at[:, :N_EXPERTS].set(w_router[l])
    b_router_pad = jnp.full((1, LANES), NEG, F32).at[0, :N_EXPERTS].set(b_router[l])
    return dict(w_ada=w_ada[l], b_ada=b_ada[l], g_pre_mix=g_pre_mix[l], g_post_mix=g_post_mix[l],
                w_in=w_in[l].astype(BF16), lam_rows=lam_rows, g_subln=g_subln[l], w_dw=w_dw_pad, b_dw=b_dw[l],
                gmat=gmat, g_cn=g_cnorm[l], b_cn=b_cnorm[l], w_out=w_out[l].astype(BF16),
                g_pre_ffn=g_pre_ffn[l], g_post_ffn=g_post_ffn[l], w_router=w_router_pad, b_router=b_router_pad,
                w_gu=w_gu[l].astype(BF16), b_gu=b_gu[l], w_down=w_down[l].astype(BF16), b_down=b_down[l])


def kernel(x_prompt, x_sample, cache_k, cache_v, state_conv, c_prompt, c_sample, w_ada, b_ada, g_pre_mix, g_post_mix, w_in, lambda_q1, lambda_k1, lambda_q2, lambda_k2, g_subln, w_dw, b_dw, g_cnorm, b_cnorm, w_out, g_pre_ffn, g_post_ffn, w_router, b_router, w_gu, b_gu, w_down, b_down):
    depth = w_ada.shape[0]
    Bp, Lp, _ = x_prompt.shape
    Bs, Ls, _ = x_sample.shape
    yp, ys = x_prompt, x_sample
    outs = [[] for _ in range(6)]
    for l in range(depth):
        p = _prepare_params(l, w_ada, b_ada, g_pre_mix, g_post_mix, w_in, lambda_q1, lambda_k1, lambda_q2,
                            lambda_k2, g_subln, w_dw, b_dw, g_cnorm, b_cnorm, w_out, g_pre_ffn, g_post_ffn,
                            w_router, b_router, w_gu, b_gu, w_down, b_down)
        lam_init = 0.8 - 0.6 * math.exp(-0.3 * l)
        mod = _ada(jnp.concatenate([c_prompt, c_sample], axis=0), p["w_ada"], p["b_ada"])
        mod = mod.reshape(Bp + Bs, N_MOD, D_MODEL)
        buf_p = jnp.zeros((Bp, CONV_HALO, CONV_CH), F32)
        buf_s = jnp.pad(state_conv[l], ((0, 0), (CONV_HALO - (CONV_W - 1), 0), (0, 0)))
        yp, kp, vp, cp = _layer(yp, mod[:Bp], buf_p, None, None, p, lam_init,
                                tm=min(Lp, 512), tq=min(Lp, 512), blk=EXPERT_ROWS)
        ys, ks, vs, cs = _layer(ys, mod[Bp:], buf_s, cache_k[l], cache_v[l], p, lam_init,
                                tm=Ls, tq=Ls, blk=EXPERT_ROWS)
        for lst, val in zip(outs, (kp, vp, cp, ks, vs, cs)):
            lst.append(val)
    return (yp, ys) + tuple(jnp.stack(o) for o in outs)
```

```python
import functools
import math

import jax
import jax.numpy as jnp
from jax import lax
from jax.experimental import pallas as pl
from jax.experimental.pallas import tpu as pltpu
from jax.experimental.pallas import tpu_sc as plsc

F32 = jnp.float32
BF16 = jnp.bfloat16
I32 = jnp.int32

D_MODEL = 1024
CHUNK = 64
CHUNK_SHIFT = 6
N_HEADS = 4
V_DIM = 128
HEAD_DIM = 64
QK_W = 512
ATTN_W = 512
CONV_CH = 512
CONV_W = 31
CONV_GROUPS = 8
GROUP_CH = CONV_CH // CONV_GROUPS
IN_W = 2 * QK_W + ATTN_W + 2 * CONV_CH
N_MOD = 6
N_EXPERTS = 32
TOP_K = 4
D_FF = 1024
SWIGLU_ALPHA = 1.702
SWIGLU_LIMIT = 7.0
EPS = 1e-6

LANES = 128
SUBLANES = 8
CONV_HALO = 32
CONV_ROWS = 32
NEG = -1e30
LOG2E = math.log2(math.e)
ATTN_ROWS = 128
EXPERT_ROWS = 256
VMEM_LIMIT = 48 * 1024 * 1024


def _sigmoid(x):
    return 1.0 / (1.0 + jnp.exp(-x))


def _split_bf16(x):
    hi = x.astype(BF16)
    lo = (x - hi.astype(F32)).astype(BF16)
    return hi, lo


def _dot(a, b):
    return jnp.dot(a, b, preferred_element_type=F32)


def _dot3(a, b):
    ah, al = _split_bf16(a)
    bh, bl = _split_bf16(b)
    return _dot(ah, bh) + _dot(ah, bl) + _dot(al, bh)


def _rms(x):
    return x * lax.rsqrt(jnp.mean(x * x, axis=-1, keepdims=True) + EPS)


def _ada_kernel(c_ref, w_ref, b_ref, o_ref):
    c = c_ref[...]
    o_ref[...] = _dot3(c * _sigmoid(c), w_ref[...]) + b_ref[...]


def _ada(c, w_ada, b_ada):
    n = c.shape[0]
    return pl.pallas_call(
        _ada_kernel,
        grid=(N_MOD,),
        in_specs=[pl.BlockSpec((n, D_MODEL), lambda j: (0, 0)),
                  pl.BlockSpec((D_MODEL, D_MODEL), lambda j: (0, j)),
                  pl.BlockSpec((1, D_MODEL), lambda j: (0, j))],
        out_specs=pl.BlockSpec((n, D_MODEL), lambda j: (0, j)),
        out_shape=jax.ShapeDtypeStruct((n, N_MOD * D_MODEL), F32),
        compiler_params=pltpu.CompilerParams(dimension_semantics=("arbitrary",), vmem_limit_bytes=VMEM_LIMIT),
        name="ada",
    )(c, w_ada, b_ada.reshape(1, N_MOD * D_MODEL))


def _premix_kernel(x_ref, mod_ref, g_ref, w_ref, buf_ref, wdw_ref, bdw_ref, gmat_ref, gcn_ref, bcn_ref,
                   q_ref, kb_ref, vb_ref, kf_ref, vf_ref, co_ref, st_ref, ext_ref, sh_ref, y_ref, *, tm):
    @pl.when(pl.program_id(1) == 0)
    def _():
        ext_ref[0:CONV_HALO, :] = buf_ref[0]

    x = x_ref[0]
    shift = mod_ref[0, 0:1, :]
    scale = mod_ref[0, 1:2, :]
    h = _rms(x) * g_ref[...] * (1.0 + scale) + shift
    hb = h.astype(BF16)

    u0 = 2 * QK_W + ATTN_W
    val = _dot(hb, w_ref[:, u0:u0 + CONV_CH])
    gate = _dot(hb, w_ref[:, u0 + CONV_CH:u0 + 2 * CONV_CH])
    ext_ref[CONV_HALO:CONV_HALO + tm, :] = val * _sigmoid(gate)

    off = CONV_HALO - (CONV_W - 1)
    span = tm + CONV_HALO - SUBLANES
    for b in range(1, SUBLANES):
        sh_ref[b - 1, 0:span, :] = ext_ref[b:b + span, :]
    for c in range(tm // CONV_ROWS):
        r0 = c * CONV_ROWS
        acc = jnp.zeros((CONV_ROWS, CONV_CH), F32)
        for j in range(CONV_W):
            b = (j + off) % SUBLANES
            a = r0 + j + off - b
            rows_j = ext_ref[a:a + CONV_ROWS, :] if b == 0 else sh_ref[b - 1, a:a + CONV_ROWS, :]
            acc = acc + wdw_ref[j:j + 1, :] * rows_j
        y_ref[r0:r0 + CONV_ROWS, :] = acc + bdw_ref[...]

    zq = _dot(hb, w_ref[:, 0:QK_W])
    q_ref[0] = (zq * (HEAD_DIM ** -0.5 * LOG2E)).astype(BF16)
    zk = _dot(hb, w_ref[:, QK_W:2 * QK_W])
    kf_ref[0] = zk
    kb_ref[0] = zk.astype(BF16)
    zv = _dot(hb, w_ref[:, 2 * QK_W:2 * QK_W + ATTN_W])
    vf_ref[0] = zv
    vb_ref[0] = zv.astype(BF16)

    y = y_ref[...]
    gm = gmat_ref[...]
    yh, yl = _split_bf16(y)
    mu = (_dot(yh, gm) + _dot(yl, gm)) * (1.0 / GROUP_CH)
    d = y - mu
    dh, dl = _split_bf16(d * d)
    var = (_dot(dh, gm) + _dot(dl, gm)) * (1.0 / GROUP_CH)
    yn = d * lax.rsqrt(var + EPS) * gcn_ref[...] + bcn_ref[...]
    co_ref[0] = (yn * _sigmoid(yn)).astype(BF16)

    tail = ext_ref[tm:tm + CONV_HALO, :]
    st_ref[0] = tail
    ext_ref[0:CONV_HALO, :] = tail


def _premix(x, mod, g_pre_mix, w_in_bf, buf, w_dw, b_dw, gmat, g_cn, b_cn, tm):
    B, L, _ = x.shape
    assert L % tm == 0 and tm % CONV_ROWS == 0
    row = lambda b, s: (b, s, 0)
    const2 = lambda b, s: (0, 0)
    bf_tile = jax.ShapeDtypeStruct((B, L, QK_W), BF16)
    f_tile = jax.ShapeDtypeStruct((B, L, QK_W), F32)
    return pl.pallas_call(
        functools.partial(_premix_kernel, tm=tm),
        grid=(B, L // tm),
        in_specs=[pl.BlockSpec((1, tm, D_MODEL), row),
                  pl.BlockSpec((1, N_MOD, D_MODEL), lambda b, s: (b, 0, 0)),
                  pl.BlockSpec((1, D_MODEL), const2),
                  pl.BlockSpec((D_MODEL, IN_W), const2),
                  pl.BlockSpec((1, CONV_HALO, CONV_CH), lambda b, s: (b, 0, 0)),
                  pl.BlockSpec((CONV_HALO, CONV_CH), const2),
                  pl.BlockSpec((1, CONV_CH), const2),
                  pl.BlockSpec((CONV_CH, CONV_CH), const2),
                  pl.BlockSpec((1, CONV_CH), const2),
                  pl.BlockSpec((1, CONV_CH), const2)],
        out_specs=[pl.BlockSpec((1, tm, QK_W), row)] * 6
                  + [pl.BlockSpec((1, CONV_HALO, CONV_CH), lambda b, s: (b, 0, 0))],
        out_shape=[bf_tile, bf_tile, bf_tile, f_tile, f_tile, bf_tile,
                   jax.ShapeDtypeStruct((B, CONV_HALO, CONV_CH), F32)],
        scratch_shapes=[pltpu.VMEM((CONV_HALO + tm, CONV_CH), F32),
                        pltpu.VMEM((SUBLANES - 1, CONV_HALO + tm, CONV_CH), F32),
                        pltpu.VMEM((tm, CONV_CH), F32)],
        compiler_params=pltpu.CompilerParams(dimension_semantics=("arbitrary", "arbitrary"),
                                             vmem_limit_bytes=VMEM_LIMIT),
        name="premix",
    )(x, mod, g_pre_mix.reshape(1, D_MODEL), w_in_bf, buf, w_dw, b_dw.reshape(1, CONV_CH), gmat,
      g_cn.reshape(1, CONV_CH), b_cn.reshape(1, CONV_CH))


def _attn_kernel(slopes_ref, q_ref, k_ref, v_ref, kb_ref, mt_ref, lam_ref, gs_ref, o_ref,
                 ke1_ref, ke2_ref, ve_ref, m1_ref, a1_ref, m2_ref, a2_ref,
                 *, tq, tk, rows, q_offset, causal, lam_init):
    slope = slopes_ref[pl.program_id(1)] * LOG2E
    lq = q_ref.shape[1]
    lk = k_ref.shape[1]
    nq = lq // tq
    lane = lax.broadcasted_iota(I32, (1, V_DIM), 1)

    low = jnp.broadcast_to(lane < HEAD_DIM, (lk, V_DIM))
    k = k_ref[0]
    zero = jnp.zeros((lk, V_DIM), BF16)
    ke1_ref[...] = jnp.where(low, k, zero)
    ke2_ref[...] = jnp.where(low, zero, k)
    ve_ref[:, 0:V_DIM] = v_ref[0]
    ve_ref[:, V_DIM:2 * V_DIM] = jnp.broadcast_to(jnp.where(lane == 0, 1.0, 0.0), (lk, V_DIM)).astype(BF16)

    nt = (((1,), (1,)), ((), ()))
    maps = ((ke1_ref, m1_ref, a1_ref), (ke2_ref, m2_ref, a2_ref))

    def lanes(x, n):
        if n % LANES == 0:
            return jnp.concatenate([x] * (n // LANES), axis=1)
        return jnp.broadcast_to(x[:, 0:1], (rows, n))

    def tile(q0, k0, masked):
        ve = ve_ref[pl.ds(k0, tk), :]
        q_first = (lax.broadcasted_iota(I32, (1, 1), 0) + (q0 + q_offset)).astype(F32)
        bias = kb_ref[0, :, pl.ds(k0, tk)] - slope * q_first
        for ke_ref, m_ref, a_ref in maps:
            ke = ke_ref[pl.ds(k0, tk), :]
            for r0 in range(0, tq, rows):
                s = lax.dot_general(q_ref[0, pl.ds(q0 + r0, rows), :], ke, nt, preferred_element_type=F32)
                s = s + bias
                if masked:
                    s = s + mt_ref[0, r0:r0 + rows, :]
                m_old = m_ref[r0:r0 + rows, :]
                m_new = jnp.maximum(m_old, jnp.max(s, axis=-1, keepdims=True))
                p = jnp.exp2(s - lanes(m_new, tk))
                alpha = jnp.exp2(m_old - m_new)
                a_ref[r0:r0 + rows, :] = (lanes(alpha, 2 * V_DIM) * a_ref[r0:r0 + rows, :]
                                          + _dot(p.astype(BF16), ve))
                m_ref[r0:r0 + rows, :] = m_new

    def reset():
        for _, m_ref, a_ref in maps:
            m_ref[...] = jnp.full(m_ref.shape, NEG, F32)
            a_ref[...] = jnp.zeros(a_ref.shape, F32)

    lv = lam_ref[...]
    lam = (jnp.exp(jnp.sum(lv[0:1, :] * lv[1:2, :], axis=-1, keepdims=True))
           - jnp.exp(jnp.sum(lv[2:3, :] * lv[3:4, :], axis=-1, keepdims=True)) + lam_init)

    def finish(q0):
        a1 = a1_ref[...]
        a2 = a2_ref[...]
        o = a1[:, 0:V_DIM] / a1[:, V_DIM:V_DIM + 1] - lam * (a2[:, 0:V_DIM] / a2[:, V_DIM:V_DIM + 1])
        o_ref[0, pl.ds(q0, tq), :] = (_rms(o) * gs_ref[...] * (1.0 - lam_init)).astype(BF16)

    if not causal:
        reset()
        tile(0, 0, True)
        finish(0)
        return

    def q_tile(qi, carry):
        q0 = pl.multiple_of(qi * tq, tq)
        reset()

        def pair(i, c):
            tile(q0, pl.multiple_of(2 * i * tk, tk), False)
            tile(q0, pl.multiple_of((2 * i + 1) * tk, tk), False)
            return c

        lax.fori_loop(0, qi // 2, pair, 0)
        odd = qi % 2 == 1

        @pl.when(odd)
        def _():
            tile(q0, pl.multiple_of((qi - 1) * tk, tk), False)
            tile(q0, pl.multiple_of(qi * tk, tk), True)

        @pl.when(jnp.logical_not(odd))
        def _():
            tile(q0, pl.multiple_of(qi * tk, tk), True)

        finish(q0)
        return carry

    lax.fori_loop(0, nq, q_tile, 0)


def _mask_table(slopes, tq, tk, q_offset):
    qpos = q_offset + jnp.arange(tq, dtype=I32)[:, None]
    kpos = jnp.arange(tk, dtype=I32)[None, :]
    visible = jnp.right_shift(kpos, CHUNK_SHIFT) <= jnp.right_shift(qpos, CHUNK_SHIFT)
    fix = jnp.where(kpos > qpos, 2 * (qpos - kpos), 0).astype(F32)
    return jnp.where(visible[None], (slopes * LOG2E)[:, None, None] * fix[None], NEG)


def _attention(q, k, v, lam_rows, g_subln, lam_init, tq, tk, q_offset, causal):
    B, Lq, _ = q.shape
    Lk = k.shape[1]
    assert Lq % tq == 0 and Lk % tk == 0 and (not causal or (tq == tk and tq % CHUNK == 0 and q_offset == 0))
    assert causal or Lk == tk
    rows = min(tq, ATTN_ROWS)
    slopes = jnp.asarray([2.0 ** (-8.0 * (h + 1) / N_HEADS) for h in range(N_HEADS)], F32)
    table = _mask_table(slopes, tq, tk, q_offset)
    key_bias = ((slopes * LOG2E)[:, None] * jnp.arange(Lk, dtype=F32)[None, :]).reshape(N_HEADS, 1, Lk)
    key_ext = pltpu.VMEM((Lk, V_DIM), BF16)
    stat = pltpu.VMEM((tq, LANES), F32)
    acc = pltpu.VMEM((tq, 2 * V_DIM), F32)
    head = lambda b, h, sl: (b, 0, h)
    return pl.pallas_call(
        functools.partial(_attn_kernel, tq=tq, tk=tk, rows=rows, q_offset=q_offset, causal=causal,
                          lam_init=lam_init),
        grid_spec=pltpu.PrefetchScalarGridSpec(
            num_scalar_prefetch=1,
            grid=(B, N_HEADS),
            in_specs=[pl.BlockSpec((1, Lq, V_DIM), head),
                      pl.BlockSpec((1, Lk, V_DIM), head),
                      pl.BlockSpec((1, Lk, V_DIM), head),
                      pl.BlockSpec((1, 1, Lk), lambda b, h, sl: (h, 0, 0)),
                      pl.BlockSpec((1, tq, tk), lambda b, h, sl: (h, 0, 0)),
                      pl.BlockSpec((8, LANES), lambda b, h, sl: (0, 0)),
                      pl.BlockSpec((1, V_DIM), lambda b, h, sl: (0, 0))],
            out_specs=pl.BlockSpec((1, Lq, V_DIM), head),
            scratch_shapes=[key_ext, key_ext, pltpu.VMEM((Lk, 2 * V_DIM), BF16), stat, acc, stat, acc]),
        out_shape=jax.ShapeDtypeStruct((B, Lq, ATTN_W), BF16),
        compiler_params=pltpu.CompilerParams(dimension_semantics=("arbitrary",) * 2,
                                             vmem_limit_bytes=VMEM_LIMIT),
        name="attn",
    )(slopes, q, k, v, key_bias, table, lam_rows, g_subln.reshape(1, V_DIM))


def _postmix_kernel(a_ref, c_ref, x_ref, mod_ref, wo_ref, gpm_ref, gpf_ref, wr_ref, br_ref,
                    x1_ref, h2_ref, rt_ref, cnt_ref, carry_ref, *, tm):
    first = jnp.logical_and(pl.program_id(0) == 0, pl.program_id(1) == 0)

    @pl.when(first)
    def _():
        carry_ref[...] = jnp.zeros(carry_ref.shape, F32)

    mix = _dot(a_ref[0], wo_ref[0:ATTN_W, :]) + _dot(c_ref[0], wo_ref[ATTN_W:ATTN_W + CONV_CH, :])
    gate_a = mod_ref[0, 2:3, :]
    shift_f = mod_ref[0, 3:4, :]
    scale_f = mod_ref[0, 4:5, :]
    x1 = x_ref[0] + gate_a * (_rms(mix) * gpm_ref[...])
    x1_ref[0] = x1
    h2 = _rms(x1) * gpf_ref[...] * (1.0 + scale_f) + shift_f
    h2_ref[0] = h2

    logits = _dot3(h2, wr_ref[...]) + br_ref[...]
    lane = lax.broadcasted_iota(I32, (tm, LANES), 1).astype(F32)
    vals, idxs = [], []
    for _ in range(TOP_K):
        m = jnp.max(logits, axis=-1, keepdims=True)
        idx = jnp.min(jnp.where(logits == m, lane, float(LANES)), axis=-1, keepdims=True)
        vals.append(m)
        idxs.append(idx)
        logits = jnp.where(lane == idx, 2.0 * NEG, logits)
    es = [jnp.exp(v - vals[0]) for v in vals]
    denom = es[0] + es[1] + es[2] + es[3]

    onehot = jnp.zeros((tm, LANES), F32)
    for idx in idxs:
        onehot = jnp.where(lane == idx, 1.0, onehot)
    r_i = lax.broadcasted_iota(I32, (tm, tm), 0)
    c_i = lax.broadcasted_iota(I32, (tm, tm), 1)
    tri = jnp.where(c_i < r_i, 1.0, 0.0).astype(BF16)
    before = _dot(tri, onehot.astype(BF16)) + carry_ref[0:1, :]

    rt = jnp.zeros((tm, LANES), F32)
    for k in range(TOP_K):
        rank = jnp.sum(jnp.where(lane == idxs[k], before, 0.0), axis=-1, keepdims=True)
        rt = jnp.where(lane == k, idxs[k], rt)
        rt = jnp.where(lane == TOP_K + k, es[k] / denom, rt)
        rt = jnp.where(lane == 2 * TOP_K + k, rank, rt)
    rt_ref[0] = rt

    carry_ref[...] = carry_ref[...] + jnp.sum(onehot, axis=0, keepdims=True)
    cnt_ref[...] = carry_ref[...]


def _postmix(attn, conv, x, mod, w_out_bf, g_post_mix, g_pre_ffn, w_router_pad, b_router_pad, tm):
    B, L, _ = x.shape
    row = lambda b, s: (b, s, 0)
    const2 = lambda b, s: (0, 0)
    return pl.pallas_call(
        functools.partial(_postmix_kernel, tm=tm),
        grid=(B, L // tm),
        in_specs=[pl.BlockSpec((1, tm, ATTN_W), row),
                  pl.BlockSpec((1, tm, CONV_CH), row),
                  pl.BlockSpec((1, tm, D_MODEL), row),
                  pl.BlockSpec((1, N_MOD, D_MODEL), lambda b, s: (b, 0, 0)),
                  pl.BlockSpec((D_MODEL, D_MODEL), const2),
                  pl.BlockSpec((1, D_MODEL), const2),
                  pl.BlockSpec((1, D_MODEL), const2),
                  pl.BlockSpec((D_MODEL, LANES), const2),
                  pl.BlockSpec((1, LANES), const2)],
        out_specs=[pl.BlockSpec((1, tm, D_MODEL), row),
                   pl.BlockSpec((1, tm, D_MODEL), row),
                   pl.BlockSpec((1, tm, LANES), row),
                   pl.BlockSpec((8, LANES), const2)],
        out_shape=[jax.ShapeDtypeStruct((B, L, D_MODEL), F32),
                   jax.ShapeDtypeStruct((B, L, D_MODEL), F32),
                   jax.ShapeDtypeStruct((B, L, LANES), F32),
                   jax.ShapeDtypeStruct((8, LANES), F32)],
        scratch_shapes=[pltpu.VMEM((8, LANES), F32)],
        compiler_params=pltpu.CompilerParams(dimension_semantics=("arbitrary", "arbitrary"),
                                             vmem_limit_bytes=VMEM_LIMIT),
        name="postmix",
    )(attn, conv, x, mod, w_out_bf, g_post_mix.reshape(1, D_MODEL), g_pre_ffn.reshape(1, D_MODEL),
      w_router_pad, b_router_pad)


def _expert_kernel(be_ref, nv_ref, x_ref, wgu_ref, bgu_ref, wd_ref, bd_ref, y_ref):
    ff_half = D_FF // 2

    @pl.when(nv_ref[pl.program_id(0)] > 0)
    def _():
        xb = x_ref[...].astype(BF16)
        acc = None
        for c in range(2):
            lo, hi = c * ff_half, (c + 1) * ff_half
            g = _dot(xb, wgu_ref[0, :, lo:hi]) + bgu_ref[0, :, lo:hi]
            lin = _dot(xb, wgu_ref[0, :, D_FF + lo:D_FF + hi]) + bgu_ref[0, :, D_FF + lo:D_FF + hi]
            g = jnp.minimum(g, SWIGLU_LIMIT)
            lin = jnp.clip(lin, -SWIGLU_LIMIT, SWIGLU_LIMIT)
            act = g * _sigmoid(SWIGLU_ALPHA * g) * (lin + 1.0)
            part = _dot(act.astype(BF16), wd_ref[0, lo:hi, :])
            acc = part if acc is None else acc + part
        y_ref[...] = acc + bd_ref[0]


def _experts(x_pad, block_e, n_valid, w_gu_bf, b_gu, w_down_bf, b_down, blk):
    n_blocks = block_e.shape[0]
    ex = lambda i, be, nv: (be[i], 0, 0)
    rows = lambda i, be, nv: (i, 0)
    return pl.pallas_call(
        _expert_kernel,
        grid_spec=pltpu.PrefetchScalarGridSpec(
            num_scalar_prefetch=2,
            grid=(n_blocks,),
            in_specs=[pl.BlockSpec((blk, D_MODEL), rows),
                      pl.BlockSpec((1, D_MODEL, 2 * D_FF), ex), pl.BlockSpec((1, 1, 2 * D_FF), ex),
                      pl.BlockSpec((1, D_FF, D_MODEL), ex), pl.BlockSpec((1, 1, D_MODEL), ex)],
            out_specs=pl.BlockSpec((blk, D_MODEL), rows)),
        out_shape=jax.ShapeDtypeStruct((n_blocks * blk, D_MODEL), F32),
        compiler_params=pltpu.CompilerParams(dimension_semantics=("arbitrary",), vmem_limit_bytes=VMEM_LIMIT),
        name="experts",
    )(block_e, n_valid, x_pad, w_gu_bf, b_gu.reshape(N_EXPERTS, 1, 2 * D_FF), w_down_bf,
      b_down.reshape(N_EXPERTS, 1, D_MODEL))


def _combine_kernel(y0_ref, y1_ref, y2_ref, y3_ref, rt_ref, x1_ref, mod_ref, g_ref, o_ref):
    rt = rt_ref[0]
    f = rt[:, TOP_K:TOP_K + 1] * y0_ref[...]
    for k, y_ref in ((1, y1_ref), (2, y2_ref), (3, y3_ref)):
        f = f + rt[:, TOP_K + k:TOP_K + k + 1] * y_ref[...]
    gate_f = mod_ref[0, 5:6, :]
    o_ref[0] = x1_ref[0] + gate_f * (_rms(f) * g_ref[...])


def _combine(y, route, x1, mod, g_post_ffn, tm):
    B, L, _ = x1.shape
    row = lambda b, s: (b, s, 0)
    tiles = B * L // tm
    y_specs = [pl.BlockSpec((tm, D_MODEL), lambda b, s, k=k: (k * tiles + b * (L // tm) + s, 0))
               for k in range(TOP_K)]
    return pl.pallas_call(
        _combine_kernel,
        grid=(B, L // tm),
        in_specs=y_specs + [
                  pl.BlockSpec((1, tm, LANES), row),
                  pl.BlockSpec((1, tm, D_MODEL), row),
                  pl.BlockSpec((1, N_MOD, D_MODEL), lambda b, s: (b, 0, 0)),
                  pl.BlockSpec((1, D_MODEL), lambda b, s: (0, 0))],
        out_specs=pl.BlockSpec((1, tm, D_MODEL), row),
        out_shape=jax.ShapeDtypeStruct((B, L, D_MODEL), F32),
        compiler_params=pltpu.CompilerParams(dimension_semantics=("arbitrary", "arbitrary"),
                                             vmem_limit_bytes=VMEM_LIMIT),
        name="combine",
    )(y, y, y, y, route, x1, mod, g_post_ffn.reshape(1, D_MODEL))


SC_WINDOW = 128


def _invert_slots(dest, n_slots):
    n = dest.shape[0]
    sc = plsc.get_sparse_core_info()
    workers = sc.num_cores * sc.num_subcores
    quantum = workers * SC_WINDOW * SUBLANES
    n_pad = -(-n // quantum) * quantum
    extra = n_pad - n
    dest = jnp.concatenate([dest, n_slots + jnp.arange(extra, dtype=I32)])
    vals = jnp.arange(n_pad, dtype=I32)
    wins = n_pad // (workers * SC_WINDOW)
    mesh = plsc.VectorSubcoreMesh(core_axis_name="c", subcore_axis_name="s")

    @functools.partial(
        pl.kernel, mesh=mesh,
        out_type=jax.ShapeDtypeStruct((n_slots + extra,), I32),
        scratch_types=[pltpu.VMEM((wins, SC_WINDOW), I32), pltpu.VMEM((wins, SC_WINDOW), I32),
                       pltpu.SemaphoreType.DMA],
        name="invert_slots",
    )
    def scatter_kernel(dest_hbm, vals_hbm, out_hbm, idx_v, val_v, sem):
        w = lax.axis_index("s") * sc.num_cores + lax.axis_index("c")
        first = pl.multiple_of(w * wins, SUBLANES)
        pltpu.sync_copy(dest_hbm.at[pl.ds(first, wins)], idx_v)
        pltpu.sync_copy(vals_hbm.at[pl.ds(first, wins)], val_v)

        @pl.loop(0, wins)
        def _(i):
            pltpu.async_copy(val_v.at[i], out_hbm.at[idx_v.at[i]], sem)

        @pl.loop(0, wins)
        def _(i):
            pltpu.make_async_copy(val_v.at[i], out_hbm.at[idx_v.at[i]], sem).wait()

    out = scatter_kernel(dest.reshape(n_pad // SC_WINDOW, SC_WINDOW), vals.reshape(n_pad // SC_WINDOW, SC_WINDOW))
    return out[:n_slots]


SC_GATHER_ROWS = 32


def _gather_rows(table, idx):
    n = idx.shape[0]
    sc = plsc.get_sparse_core_info()
    workers = sc.num_cores * sc.num_subcores
    quantum = workers * SC_GATHER_ROWS * SUBLANES
    n_pad = -(-n // quantum) * quantum
    idx = jnp.concatenate([idx, jnp.zeros((n_pad - n,), I32)]).reshape(n_pad // SC_GATHER_ROWS, SC_GATHER_ROWS)
    per_w = n_pad // workers
    wins = per_w // SC_GATHER_ROWS
    mesh = plsc.VectorSubcoreMesh(core_axis_name="c", subcore_axis_name="s")
    buf = pltpu.VMEM((SC_GATHER_ROWS, D_MODEL), table.dtype)

    @functools.partial(
        pl.kernel, mesh=mesh,
        out_type=jax.ShapeDtypeStruct((n_pad, D_MODEL), table.dtype),
        scratch_types=[pltpu.VMEM((wins, SC_GATHER_ROWS), I32), buf, buf,
                       pltpu.SemaphoreType.DMA, pltpu.SemaphoreType.DMA],
        name="gather_rows",
    )
    def gather_kernel(table_hbm, idx_hbm, out_hbm, idx_v, buf0, buf1, sem0, sem1):
        w = lax.axis_index("s") * sc.num_cores + lax.axis_index("c")
        pltpu.sync_copy(idx_hbm.at[pl.ds(pl.multiple_of(w * wins, SUBLANES), wins)], idx_v)
        base = pl.multiple_of(w * per_w, SC_GATHER_ROWS)

        @pl.loop(0, wins, step=2)
        def _(i):
            g0 = pltpu.async_copy(table_hbm.at[idx_v.at[i]], buf0, sem0)
            g1 = pltpu.async_copy(table_hbm.at[idx_v.at[i + 1]], buf1, sem1)
            g0.wait()
            pltpu.sync_copy(buf0, out_hbm.at[pl.ds(base + i * SC_GATHER_ROWS, SC_GATHER_ROWS)])
            g1.wait()
            pltpu.sync_copy(buf1, out_hbm.at[pl.ds(base + (i + 1) * SC_GATHER_ROWS, SC_GATHER_ROWS)])

    return gather_kernel(table, idx)


def _routing_tables(route, counts, blk):
    T = route.shape[0]
    TK = T * TOP_K
    idx = route[:, 0:TOP_K].astype(I32)
    rank = route[:, 2 * TOP_K:3 * TOP_K].astype(I32)
    padded = (counts + blk - 1) // blk * blk
    pad_end = jnp.cumsum(padded)
    pad_start = pad_end - padded
    dest = pad_start[idx] + rank
    n_blocks = -(-TK // blk) + N_EXPERTS
    blk_start = jnp.arange(n_blocks, dtype=I32) * blk
    block_e = jnp.sum((blk_start[:, None] >= pad_end[None, :]).astype(I32), axis=1)
    block_e = jnp.minimum(block_e, N_EXPERTS - 1)
    last = (pad_start + counts)[block_e]
    n_valid = jnp.clip(last - blk_start, 0, blk)
    slot_a = _invert_slots(dest.reshape(TK), n_blocks * blk)
    slot = jnp.arange(n_blocks * blk, dtype=I32)
    slot_tok = jnp.where(slot < jnp.repeat(last, blk), slot_a // TOP_K, 0)
    return block_e, n_valid, slot_tok, dest


def _layer(x, mod, buf, past_k, past_v, p, lam_init, tm, tq, blk):
    B, L, _ = x.shape
    q, kb, vb, kf, vf, conv, state = _premix(x, mod, p["g_pre_mix"], p["w_in"], buf, p["w_dw"], p["b_dw"],
                                             p["gmat"], p["g_cn"], p["b_cn"], tm)
    if past_k is None:
        attn = _attention(q, kb, vb, p["lam_rows"], p["g_subln"], lam_init, tq, tq, 0, True)
    else:
        P = past_k.shape[1]
        keys = jnp.concatenate([past_k.reshape(B, P, QK_W).astype(BF16), kb], axis=1)
        vals = jnp.concatenate([past_v.reshape(B, P, ATTN_W).astype(BF16), vb], axis=1)
        attn = _attention(q, keys, vals, p["lam_rows"], p["g_subln"], lam_init, L, P + L, P, False)
    x1, h2, route, cnt = _postmix(attn, conv, x, mod, p["w_out"], p["g_post_mix"], p["g_pre_ffn"],
                                  p["w_router"], p["b_router"], tm)
    T = B * L
    route2 = route.reshape(T, LANES)
    counts = cnt[0, :N_EXPERTS].astype(I32)
    block_e, n_valid, slot_tok, dest = _routing_tables(route2, counts, blk)
    x_pad = _gather_rows(h2.reshape(T, D_MODEL), slot_tok)
    y_pad = _experts(x_pad, block_e, n_valid, p["w_gu"], p["b_gu"], p["w_down"], p["b_down"], blk)
    y = _gather_rows(y_pad, dest.T.reshape(T * TOP_K))
    out = _combine(y, route, x1, mod, p["g_post_ffn"], min(tm, 256))
    k_new = kf.reshape(B, L, N_HEADS, 2 * HEAD_DIM)
    v_new = vf.reshape(B, L, N_HEADS, V_DIM)
    return out, k_new, v_new, state[:, CONV_HALO - (CONV_W - 1):, :]


def _prepare_params(l, w_ada, b_ada, g_pre_mix, g_post_mix, w_in, lambda_q1, lambda_k1, lambda_q2, lambda_k2,
                    g_subln, w_dw, b_dw, g_cnorm, b_cnorm, w_out, g_pre_ffn, g_post_ffn,
                    w_router, b_router, w_gu, b_gu, w_down, b_down):
    lam_rows = jnp.zeros((8, LANES), F32)
    for r, vec in enumerate((lambda_q1[l], lambda_k1[l], lambda_q2[l], lambda_k2[l])):
        lam_rows = lam_rows.at[r, :HEAD_DIM].set(vec)
    ch = jnp.arange(CONV_CH, dtype=I32) // GROUP_CH
    gmat = (ch[:, None] == ch[None, :]).astype(BF16)
    w_dw_pad = jnp.zeros((CONV_HALO, CONV_CH), F32).at[:CONV_W].set(w_dw[l])
    w_router_pad = jnp.zeros((D_MODEL, LANES), F32).at[:, :N_EXPERTS].set(w_router[l])
    b_router_pad = jnp.full((1, LANES), NEG, F32).at[0, :N_EXPERTS].set(b_router[l])
    return dict(w_ada=w_ada[l], b_ada=b_ada[l], g_pre_mix=g_pre_mix[l], g_post_mix=g_post_mix[l],
                w_in=w_in[l].astype(BF16), lam_rows=lam_rows, g_subln=g_subln[l], w_dw=w_dw_pad, b_dw=b_dw[l],
                gmat=gmat, g_cn=g_cnorm[l], b_cn=b_cnorm[l], w_out=w_out[l].astype(BF16),
                g_pre_ffn=g_pre_ffn[l], g_post_ffn=g_post_ffn[l], w_router=w_router_pad, b_router=b_router_pad,
                w_gu=w_gu[l].astype(BF16), b_gu=b_gu[l], w_down=w_down[l].astype(BF16), b_down=b_down[l])


def kernel(x_prompt, x_sample, cache_k, cache_v, state_conv, c_prompt, c_sample, w_ada, b_ada, g_pre_mix, g_post_mix, w_in, lambda_q1, lambda_k1, lambda_q2, lambda_k2, g_subln, w_dw, b_dw, g_cnorm, b_cnorm, w_out, g_pre_ffn, g_post_ffn, w_router, b_router, w_gu, b_gu, w_down, b_down):
    depth = w_ada.shape[0]
    Bp, Lp, _ = x_prompt.shape
    Bs, Ls, _ = x_sample.shape
    yp, ys = x_prompt, x_sample
    outs = [[] for _ in range(6)]
    for l in range(depth):
        p = _prepare_params(l, w_ada, b_ada, g_pre_mix, g_post_mix, w_in, lambda_q1, lambda_k1, lambda_q2,
                            lambda_k2, g_subln, w_dw, b_dw, g_cnorm, b_cnorm, w_out, g_pre_ffn, g_post_ffn,
                            w_router, b_router, w_gu, b_gu, w_down, b_down)
        lam_init = 0.8 - 0.6 * math.exp(-0.3 * l)
        mod = _ada(jnp.concatenate([c_prompt, c_sample], axis=0), p["w_ada"], p["b_ada"])
        mod = mod.reshape(Bp + Bs, N_MOD, D_MODEL)
        buf_p = jnp.zeros((Bp, CONV_HALO, CONV_CH), F32)
        buf_s = jnp.pad(state_conv[l], ((0, 0), (CONV_HALO - (CONV_W - 1), 0), (0, 0)))
        yp, kp, vp, cp = _layer(yp, mod[:Bp], buf_p, None, None, p, lam_init,
                                tm=min(Lp, 512), tq=min(Lp, 512), blk=EXPERT_ROWS)
        ys, ks, vs, cs = _layer(ys, mod[Bp:], buf_s, cache_k[l], cache_v[l], p, lam_init,
                                tm=Ls, tq=Ls, blk=EXPERT_ROWS)
        for lst, val in zip(outs, (kp, vp, cp, ks, vs, cs)):
            lst.append(val)
    return (yp, ys) + tuple(jnp.stack(o) for o in outs)
```

```python
import functools
import math

import jax
import jax.numpy as jnp
from jax import lax
from jax.experimental import pallas as pl
from jax.experimental.pallas import tpu as pltpu
from jax.experimental.pallas import tpu_sc as plsc

F32 = jnp.float32
BF16 = jnp.bfloat16
I32 = jnp.int32

D_MODEL = 1024
CHUNK = 64
CHUNK_SHIFT = 6
N_HEADS = 4
V_DIM = 128
HEAD_DIM = 64
QK_W = 512
ATTN_W = 512
CONV_CH = 512
CONV_W = 31
CONV_GROUPS = 8
GROUP_CH = CONV_CH // CONV_GROUPS
IN_W = 2 * QK_W + ATTN_W + 2 * CONV_CH
N_MOD = 6
N_EXPERTS = 32
TOP_K = 4
D_FF = 1024
SWIGLU_ALPHA = 1.702
SWIGLU_LIMIT = 7.0
EPS = 1e-6

LANES = 128
SUBLANES = 8
CONV_HALO = 32
CONV_ROWS = 32
NEG = -1e30
LOG2E = math.log2(math.e)
ATTN_ROWS = 128
EXPERT_ROWS = 512
VMEM_LIMIT = 48 * 1024 * 1024


def _sigmoid(x):
    return 1.0 / (1.0 + jnp.exp(-x))


def _split_bf16(x):
    hi = x.astype(BF16)
    lo = (x - hi.astype(F32)).astype(BF16)
    return hi, lo


def _dot(a, b):
    return jnp.dot(a, b, preferred_element_type=F32)


def _dot3(a, b):
    ah, al = _split_bf16(a)
    bh, bl = _split_bf16(b)
    return _dot(ah, bh) + _dot(ah, bl) + _dot(al, bh)


def _rms(x):
    return x * lax.rsqrt(jnp.mean(x * x, axis=-1, keepdims=True) + EPS)


def _ada_kernel(c_ref, w_ref, b_ref, o_ref):
    c = c_ref[...]
    o_ref[...] = _dot3(c * _sigmoid(c), w_ref[...]) + b_ref[...]


def _ada(c, w_ada, b_ada):
    n = c.shape[0]
    return pl.pallas_call(
        _ada_kernel,
        grid=(N_MOD,),
        in_specs=[pl.BlockSpec((n, D_MODEL), lambda j: (0, 0)),
                  pl.BlockSpec((D_MODEL, D_MODEL), lambda j: (0, j)),
                  pl.BlockSpec((1, D_MODEL), lambda j: (0, j))],
        out_specs=pl.BlockSpec((n, D_MODEL), lambda j: (0, j)),
        out_shape=jax.ShapeDtypeStruct((n, N_MOD * D_MODEL), F32),
        compiler_params=pltpu.CompilerParams(dimension_semantics=("arbitrary",), vmem_limit_bytes=VMEM_LIMIT),
        name="ada",
    )(c, w_ada, b_ada.reshape(1, N_MOD * D_MODEL))


def _premix_kernel(x_ref, mod_ref, g_ref, w_ref, buf_ref, wdw_ref, bdw_ref, gmat_ref, gcn_ref, bcn_ref,
                   q_ref, kb_ref, vb_ref, kf_ref, vf_ref, co_ref, st_ref, ext_ref, sh_ref, y_ref, *, tm):
    @pl.when(pl.program_id(1) == 0)
    def _():
        ext_ref[0:CONV_HALO, :] = buf_ref[0]

    x = x_ref[0]
    shift = mod_ref[0, 0:1, :]
    scale = mod_ref[0, 1:2, :]
    h = _rms(x) * g_ref[...] * (1.0 + scale) + shift
    hb = h.astype(BF16)

    u0 = 2 * QK_W + ATTN_W
    val = _dot(hb, w_ref[:, u0:u0 + CONV_CH])
    gate = _dot(hb, w_ref[:, u0 + CONV_CH:u0 + 2 * CONV_CH])
    ext_ref[CONV_HALO:CONV_HALO + tm, :] = val * _sigmoid(gate)

    off = CONV_HALO - (CONV_W - 1)
    span = tm + CONV_HALO - SUBLANES
    for b in range(1, SUBLANES):
        sh_ref[b - 1, 0:span, :] = ext_ref[b:b + span, :]
    for c in range(tm // CONV_ROWS):
        r0 = c * CONV_ROWS
        acc = jnp.zeros((CONV_ROWS, CONV_CH), F32)
        for j in range(CONV_W):
            b = (j + off) % SUBLANES
            a = r0 + j + off - b
            rows_j = ext_ref[a:a + CONV_ROWS, :] if b == 0 else sh_ref[b - 1, a:a + CONV_ROWS, :]
            acc = acc + wdw_ref[j:j + 1, :] * rows_j
        y_ref[r0:r0 + CONV_ROWS, :] = acc + bdw_ref[...]

    zq = _dot(hb, w_ref[:, 0:QK_W])
    q_ref[0] = (zq * (HEAD_DIM ** -0.5 * LOG2E)).astype(BF16)
    zk = _dot(hb, w_ref[:, QK_W:2 * QK_W])
    kf_ref[0] = zk
    kb_ref[0] = zk.astype(BF16)
    zv = _dot(hb, w_ref[:, 2 * QK_W:2 * QK_W + ATTN_W])
    vf_ref[0] = zv
    vb_ref[0] = zv.astype(BF16)

    y = y_ref[...]
    gm = gmat_ref[...]
    yh, yl = _split_bf16(y)
    mu = (_dot(yh, gm) + _dot(yl, gm)) * (1.0 / GROUP_CH)
    d = y - mu
    dh, dl = _split_bf16(d * d)
    var = (_dot(dh, gm) + _dot(dl, gm)) * (1.0 / GROUP_CH)
    yn = d * lax.rsqrt(var + EPS) * gcn_ref[...] + bcn_ref[...]
    co_ref[0] = (yn * _sigmoid(yn)).astype(BF16)

    tail = ext_ref[tm:tm + CONV_HALO, :]
    st_ref[0] = tail
    ext_ref[0:CONV_HALO, :] = tail


def _premix(x, mod, g_pre_mix, w_in_bf, buf, w_dw, b_dw, gmat, g_cn, b_cn, tm):
    B, L, _ = x.shape
    assert L % tm == 0 and tm % CONV_ROWS == 0
    row = lambda b, s: (b, s, 0)
    const2 = lambda b, s: (0, 0)
    bf_tile = jax.ShapeDtypeStruct((B, L, QK_W), BF16)
    f_tile = jax.ShapeDtypeStruct((B, L, QK_W), F32)
    return pl.pallas_call(
        functools.partial(_premix_kernel, tm=tm),
        grid=(B, L // tm),
        in_specs=[pl.BlockSpec((1, tm, D_MODEL), row),
                  pl.BlockSpec((1, N_MOD, D_MODEL), lambda b, s: (b, 0, 0)),
                  pl.BlockSpec((1, D_MODEL), const2),
                  pl.BlockSpec((D_MODEL, IN_W), const2),
                  pl.BlockSpec((1, CONV_HALO, CONV_CH), lambda b, s: (b, 0, 0)),
                  pl.BlockSpec((CONV_HALO, CONV_CH), const2),
                  pl.BlockSpec((1, CONV_CH), const2),
                  pl.BlockSpec((CONV_CH, CONV_CH), const2),
                  pl.BlockSpec((1, CONV_CH), const2),
                  pl.BlockSpec((1, CONV_CH), const2)],
        out_specs=[pl.BlockSpec((1, tm, QK_W), row)] * 6
                  + [pl.BlockSpec((1, CONV_HALO, CONV_CH), lambda b, s: (b, 0, 0))],
        out_shape=[bf_tile, bf_tile, bf_tile, f_tile, f_tile, bf_tile,
                   jax.ShapeDtypeStruct((B, CONV_HALO, CONV_CH), F32)],
        scratch_shapes=[pltpu.VMEM((CONV_HALO + tm, CONV_CH), F32),
                        pltpu.VMEM((SUBLANES - 1, CONV_HALO + tm, CONV_CH), F32),
                        pltpu.VMEM((tm, CONV_CH), F32)],
        compiler_params=pltpu.CompilerParams(dimension_semantics=("arbitrary", "arbitrary"),
                                             vmem_limit_bytes=VMEM_LIMIT),
        name="premix",
    )(x, mod, g_pre_mix.reshape(1, D_MODEL), w_in_bf, buf, w_dw, b_dw.reshape(1, CONV_CH), gmat,
      g_cn.reshape(1, CONV_CH), b_cn.reshape(1, CONV_CH))


def _attn_kernel(slopes_ref, q_ref, k_ref, v_ref, kb_ref, mt_ref, lam_ref, gs_ref, o_ref,
                 ke1_ref, ke2_ref, ve_ref, m1_ref, a1_ref, m2_ref, a2_ref,
                 *, tq, tk, rows, q_offset, causal, lam_init):
    slope = slopes_ref[pl.program_id(1)] * LOG2E
    lq = q_ref.shape[1]
    lk = k_ref.shape[1]
    nq = lq // tq
    lane = lax.broadcasted_iota(I32, (1, V_DIM), 1)

    low = jnp.broadcast_to(lane < HEAD_DIM, (lk, V_DIM))
    k = k_ref[0]
    zero = jnp.zeros((lk, V_DIM), BF16)
    ke1_ref[...] = jnp.where(low, k, zero)
    ke2_ref[...] = jnp.where(low, zero, k)
    ve_ref[:, 0:V_DIM] = v_ref[0]
    ve_ref[:, V_DIM:2 * V_DIM] = jnp.broadcast_to(jnp.where(lane == 0, 1.0, 0.0), (lk, V_DIM)).astype(BF16)

    nt = (((1,), (1,)), ((), ()))
    maps = ((ke1_ref, m1_ref, a1_ref), (ke2_ref, m2_ref, a2_ref))

    def lanes(x, n):
        if n % LANES == 0:
            return jnp.concatenate([x] * (n // LANES), axis=1)
        return jnp.broadcast_to(x[:, 0:1], (rows, n))

    def tile(q0, k0, masked):
        ve = ve_ref[pl.ds(k0, tk), :]
        q_first = (lax.broadcasted_iota(I32, (1, 1), 0) + (q0 + q_offset)).astype(F32)
        bias = kb_ref[0, :, pl.ds(k0, tk)] - slope * q_first
        for ke_ref, m_ref, a_ref in maps:
            ke = ke_ref[pl.ds(k0, tk), :]
            for r0 in range(0, tq, rows):
                s = lax.dot_general(q_ref[0, pl.ds(q0 + r0, rows), :], ke, nt, preferred_element_type=F32)
                s = s + bias
                if masked:
                    s = s + mt_ref[0, r0:r0 + rows, :]
                m_old = m_ref[r0:r0 + rows, :]
                m_new = jnp.maximum(m_old, jnp.max(s, axis=-1, keepdims=True))
                p = jnp.exp2(s - lanes(m_new, tk))
                alpha = jnp.exp2(m_old - m_new)
                a_ref[r0:r0 + rows, :] = (lanes(alpha, 2 * V_DIM) * a_ref[r0:r0 + rows, :]
                                          + _dot(p.astype(BF16), ve))
                m_ref[r0:r0 + rows, :] = m_new

    def reset():
        for _, m_ref, a_ref in maps:
            m_ref[...] = jnp.full(m_ref.shape, NEG, F32)
            a_ref[...] = jnp.zeros(a_ref.shape, F32)

    lv = lam_ref[...]
    lam = (jnp.exp(jnp.sum(lv[0:1, :] * lv[1:2, :], axis=-1, keepdims=True))
           - jnp.exp(jnp.sum(lv[2:3, :] * lv[3:4, :], axis=-1, keepdims=True)) + lam_init)

    def finish(q0):
        a1 = a1_ref[...]
        a2 = a2_ref[...]
        o = a1[:, 0:V_DIM] / a1[:, V_DIM:V_DIM + 1] - lam * (a2[:, 0:V_DIM] / a2[:, V_DIM:V_DIM + 1])
        o_ref[0, pl.ds(q0, tq), :] = (_rms(o) * gs_ref[...] * (1.0 - lam_init)).astype(BF16)

    if not causal:
        reset()
        tile(0, 0, True)
        finish(0)
        return

    def q_tile(qi, carry):
        q0 = pl.multiple_of(qi * tq, tq)
        reset()

        def pair(i, c):
            tile(q0, pl.multiple_of(2 * i * tk, tk), False)
            tile(q0, pl.multiple_of((2 * i + 1) * tk, tk), False)
            return c

        lax.fori_loop(0, qi // 2, pair, 0)
        odd = qi % 2 == 1

        @pl.when(odd)
        def _():
            tile(q0, pl.multiple_of((qi - 1) * tk, tk), False)
            tile(q0, pl.multiple_of(qi * tk, tk), True)

        @pl.when(jnp.logical_not(odd))
        def _():
            tile(q0, pl.multiple_of(qi * tk, tk), True)

        finish(q0)
        return carry

    lax.fori_loop(0, nq, q_tile, 0)


def _mask_table(slopes, tq, tk, q_offset):
    qpos = q_offset + jnp.arange(tq, dtype=I32)[:, None]
    kpos = jnp.arange(tk, dtype=I32)[None, :]
    visible = jnp.right_shift(kpos, CHUNK_SHIFT) <= jnp.right_shift(qpos, CHUNK_SHIFT)
    fix = jnp.where(kpos > qpos, 2 * (qpos - kpos), 0).astype(F32)
    return jnp.where(visible[None], (slopes * LOG2E)[:, None, None] * fix[None], NEG)


def _attention(q, k, v, lam_rows, g_subln, lam_init, tq, tk, q_offset, causal):
    B, Lq, _ = q.shape
    Lk = k.shape[1]
    assert Lq % tq == 0 and Lk % tk == 0 and (not causal or (tq == tk and tq % CHUNK == 0 and q_offset == 0))
    assert causal or Lk == tk
    rows = min(tq, ATTN_ROWS)
    slopes = jnp.asarray([2.0 ** (-8.0 * (h + 1) / N_HEADS) for h in range(N_HEADS)], F32)
    table = _mask_table(slopes, tq, tk, q_offset)
    key_bias = ((slopes * LOG2E)[:, None] * jnp.arange(Lk, dtype=F32)[None, :]).reshape(N_HEADS, 1, Lk)
    key_ext = pltpu.VMEM((Lk, V_DIM), BF16)
    stat = pltpu.VMEM((tq, LANES), F32)
    acc = pltpu.VMEM((tq, 2 * V_DIM), F32)
    head = lambda b, h, sl: (b, 0, h)
    return pl.pallas_call(
        functools.partial(_attn_kernel, tq=tq, tk=tk, rows=rows, q_offset=q_offset, causal=causal,
                          lam_init=lam_init),
        grid_spec=pltpu.PrefetchScalarGridSpec(
            num_scalar_prefetch=1,
            grid=(B, N_HEADS),
            in_specs=[pl.BlockSpec((1, Lq, V_DIM), head),
                      pl.BlockSpec((1, Lk, V_DIM), head),
                      pl.BlockSpec((1, Lk, V_DIM), head),
                      pl.BlockSpec((1, 1, Lk), lambda b, h, sl: (h, 0, 0)),
                      pl.BlockSpec((1, tq, tk), lambda b, h, sl: (h, 0, 0)),
                      pl.BlockSpec((8, LANES), lambda b, h, sl: (0, 0)),
                      pl.BlockSpec((1, V_DIM), lambda b, h, sl: (0, 0))],
            out_specs=pl.BlockSpec((1, Lq, V_DIM), head),
            scratch_shapes=[key_ext, key_ext, pltpu.VMEM((Lk, 2 * V_DIM), BF16), stat, acc, stat, acc]),
        out_shape=jax.ShapeDtypeStruct((B, Lq, ATTN_W), BF16),
        compiler_params=pltpu.CompilerParams(dimension_semantics=("arbitrary",) * 2,
                                             vmem_limit_bytes=VMEM_LIMIT),
        name="attn",
    )(slopes, q, k, v, key_bias, table, lam_rows, g_subln.reshape(1, V_DIM))


def _postmix_kernel(a_ref, c_ref, x_ref, mod_ref, wo_ref, gpm_ref, gpf_ref, wr_ref, br_ref,
                    x1_ref, h2_ref, rt_ref, cnt_ref, carry_ref, *, tm):
    first = jnp.logical_and(pl.program_id(0) == 0, pl.program_id(1) == 0)

    @pl.when(first)
    def _():
        carry_ref[...] = jnp.zeros(carry_ref.shape, F32)

    mix = _dot(a_ref[0], wo_ref[0:ATTN_W, :]) + _dot(c_ref[0], wo_ref[ATTN_W:ATTN_W + CONV_CH, :])
    gate_a = mod_ref[0, 2:3, :]
    shift_f = mod_ref[0, 3:4, :]
    scale_f = mod_ref[0, 4:5, :]
    x1 = x_ref[0] + gate_a * (_rms(mix) * gpm_ref[...])
    x1_ref[0] = x1
    h2 = _rms(x1) * gpf_ref[...] * (1.0 + scale_f) + shift_f
    h2_ref[0] = h2

    logits = _dot3(h2, wr_ref[...]) + br_ref[...]
    lane = lax.broadcasted_iota(I32, (tm, LANES), 1).astype(F32)
    vals, idxs = [], []
    for _ in range(TOP_K):
        m = jnp.max(logits, axis=-1, keepdims=True)
        idx = jnp.min(jnp.where(logits == m, lane, float(LANES)), axis=-1, keepdims=True)
        vals.append(m)
        idxs.append(idx)
        logits = jnp.where(lane == idx, 2.0 * NEG, logits)
    es = [jnp.exp(v - vals[0]) for v in vals]
    denom = es[0] + es[1] + es[2] + es[3]

    onehot = jnp.zeros((tm, LANES), F32)
    for idx in idxs:
        onehot = jnp.where(lane == idx, 1.0, onehot)
    r_i = lax.broadcasted_iota(I32, (tm, tm), 0)
    c_i = lax.broadcasted_iota(I32, (tm, tm), 1)
    tri = jnp.where(c_i < r_i, 1.0, 0.0).astype(BF16)
    before = _dot(tri, onehot.astype(BF16)) + carry_ref[0:1, :]

    rt = jnp.zeros((tm, LANES), F32)
    for k in range(TOP_K):
        rank = jnp.sum(jnp.where(lane == idxs[k], before, 0.0), axis=-1, keepdims=True)
        rt = jnp.where(lane == k, idxs[k], rt)
        rt = jnp.where(lane == TOP_K + k, es[k] / denom, rt)
        rt = jnp.where(lane == 2 * TOP_K + k, rank, rt)
    rt_ref[0] = rt

    carry_ref[...] = carry_ref[...] + jnp.sum(onehot, axis=0, keepdims=True)
    cnt_ref[...] = carry_ref[...]


def _postmix(attn, conv, x, mod, w_out_bf, g_post_mix, g_pre_ffn, w_router_pad, b_router_pad, tm):
    B, L, _ = x.shape
    row = lambda b, s: (b, s, 0)
    const2 = lambda b, s: (0, 0)
    return pl.pallas_call(
        functools.partial(_postmix_kernel, tm=tm),
        grid=(B, L // tm),
        in_specs=[pl.BlockSpec((1, tm, ATTN_W), row),
                  pl.BlockSpec((1, tm, CONV_CH), row),
                  pl.BlockSpec((1, tm, D_MODEL), row),
                  pl.BlockSpec((1, N_MOD, D_MODEL), lambda b, s: (b, 0, 0)),
                  pl.BlockSpec((D_MODEL, D_MODEL), const2),
                  pl.BlockSpec((1, D_MODEL), const2),
                  pl.BlockSpec((1, D_MODEL), const2),
                  pl.BlockSpec((D_MODEL, LANES), const2),
                  pl.BlockSpec((1, LANES), const2)],
        out_specs=[pl.BlockSpec((1, tm, D_MODEL), row),
                   pl.BlockSpec((1, tm, D_MODEL), row),
                   pl.BlockSpec((1, tm, LANES), row),
                   pl.BlockSpec((8, LANES), const2)],
        out_shape=[jax.ShapeDtypeStruct((B, L, D_MODEL), F32),
                   jax.ShapeDtypeStruct((B, L, D_MODEL), F32),
                   jax.ShapeDtypeStruct((B, L, LANES), F32),
                   jax.ShapeDtypeStruct((8, LANES), F32)],
        scratch_shapes=[pltpu.VMEM((8, LANES), F32)],
        compiler_params=pltpu.CompilerParams(dimension_semantics=("arbitrary", "arbitrary"),
                                             vmem_limit_bytes=VMEM_LIMIT),
        name="postmix",
    )(attn, conv, x, mod, w_out_bf, g_post_mix.reshape(1, D_MODEL), g_pre_ffn.reshape(1, D_MODEL),
      w_router_pad, b_router_pad)


def _expert_kernel(be_ref, nv_ref, x_ref, wgu_ref, bgu_ref, wd_ref, bd_ref, y_ref):
    ff_half = D_FF // 2

    @pl.when(nv_ref[pl.program_id(0)] > 0)
    def _():
        xb = x_ref[...].astype(BF16)
        acc = None
        for c in range(2):
            lo, hi = c * ff_half, (c + 1) * ff_half
            g = _dot(xb, wgu_ref[0, :, lo:hi]) + bgu_ref[0, :, lo:hi]
            lin = _dot(xb, wgu_ref[0, :, D_FF + lo:D_FF + hi]) + bgu_ref[0, :, D_FF + lo:D_FF + hi]
            g = jnp.minimum(g, SWIGLU_LIMIT)
            lin = jnp.clip(lin, -SWIGLU_LIMIT, SWIGLU_LIMIT)
            act = g * _sigmoid(SWIGLU_ALPHA * g) * (lin + 1.0)
            part = _dot(act.astype(BF16), wd_ref[0, lo:hi, :])
            acc = part if acc is None else acc + part
        y_ref[...] = acc + bd_ref[0]


def _experts(x_pad, block_e, n_valid, w_gu_bf, b_gu, w_down_bf, b_down, blk):
    n_blocks = block_e.shape[0]
    ex = lambda i, be, nv: (be[i], 0, 0)
    rows = lambda i, be, nv: (i, 0)
    return pl.pallas_call(
        _expert_kernel,
        grid_spec=pltpu.PrefetchScalarGridSpec(
            num_scalar_prefetch=2,
            grid=(n_blocks,),
            in_specs=[pl.BlockSpec((blk, D_MODEL), rows),
                      pl.BlockSpec((1, D_MODEL, 2 * D_FF), ex), pl.BlockSpec((1, 1, 2 * D_FF), ex),
                      pl.BlockSpec((1, D_FF, D_MODEL), ex), pl.BlockSpec((1, 1, D_MODEL), ex)],
            out_specs=pl.BlockSpec((blk, D_MODEL), rows)),
        out_shape=jax.ShapeDtypeStruct((n_blocks * blk, D_MODEL), F32),
        compiler_params=pltpu.CompilerParams(dimension_semantics=("arbitrary",), vmem_limit_bytes=VMEM_LIMIT),
        name="experts",
    )(block_e, n_valid, x_pad, w_gu_bf, b_gu.reshape(N_EXPERTS, 1, 2 * D_FF), w_down_bf,
      b_down.reshape(N_EXPERTS, 1, D_MODEL))


def _combine_kernel(y0_ref, y1_ref, y2_ref, y3_ref, rt_ref, x1_ref, mod_ref, g_ref, o_ref):
    rt = rt_ref[0]
    f = rt[:, TOP_K:TOP_K + 1] * y0_ref[...]
    for k, y_ref in ((1, y1_ref), (2, y2_ref), (3, y3_ref)):
        f = f + rt[:, TOP_K + k:TOP_K + k + 1] * y_ref[...]
    gate_f = mod_ref[0, 5:6, :]
    o_ref[0] = x1_ref[0] + gate_f * (_rms(f) * g_ref[...])


def _combine(y, route, x1, mod, g_post_ffn, tm):
    B, L, _ = x1.shape
    row = lambda b, s: (b, s, 0)
    tiles = B * L // tm
    y_specs = [pl.BlockSpec((tm, D_MODEL), lambda b, s, k=k: (k * tiles + b * (L // tm) + s, 0))
               for k in range(TOP_K)]
    return pl.pallas_call(
        _combine_kernel,
        grid=(B, L // tm),
        in_specs=y_specs + [
                  pl.BlockSpec((1, tm, LANES), row),
                  pl.BlockSpec((1, tm, D_MODEL), row),
                  pl.BlockSpec((1, N_MOD, D_MODEL), lambda b, s: (b, 0, 0)),
                  pl.BlockSpec((1, D_MODEL), lambda b, s: (0, 0))],
        out_specs=pl.BlockSpec((1, tm, D_MODEL), row),
        out_shape=jax.ShapeDtypeStruct((B, L, D_MODEL), F32),
        compiler_params=pltpu.CompilerParams(dimension_semantics=("arbitrary", "arbitrary"),
                                             vmem_limit_bytes=VMEM_LIMIT),
        name="combine",
    )(y, y, y, y, route, x1, mod, g_post_ffn.reshape(1, D_MODEL))


SC_WINDOW = 128


def _invert_slots(dest, n_slots):
    n = dest.shape[0]
    sc = plsc.get_sparse_core_info()
    workers = sc.num_cores * sc.num_subcores
    quantum = workers * SC_WINDOW * SUBLANES
    n_pad = -(-n // quantum) * quantum
    extra = n_pad - n
    dest = jnp.concatenate([dest, n_slots + jnp.arange(extra, dtype=I32)])
    vals = jnp.arange(n_pad, dtype=I32)
    wins = n_pad // (workers * SC_WINDOW)
    mesh = plsc.VectorSubcoreMesh(core_axis_name="c", subcore_axis_name="s")

    @functools.partial(
        pl.kernel, mesh=mesh,
        out_type=jax.ShapeDtypeStruct((n_slots + extra,), I32),
        scratch_types=[pltpu.VMEM((wins, SC_WINDOW), I32), pltpu.VMEM((wins, SC_WINDOW), I32),
                       pltpu.SemaphoreType.DMA],
        name="invert_slots",
    )
    def scatter_kernel(dest_hbm, vals_hbm, out_hbm, idx_v, val_v, sem):
        w = lax.axis_index("s") * sc.num_cores + lax.axis_index("c")
        first = pl.multiple_of(w * wins, SUBLANES)
        pltpu.sync_copy(dest_hbm.at[pl.ds(first, wins)], idx_v)
        pltpu.sync_copy(vals_hbm.at[pl.ds(first, wins)], val_v)

        @pl.loop(0, wins)
        def _(i):
            pltpu.async_copy(val_v.at[i], out_hbm.at[idx_v.at[i]], sem)

        @pl.loop(0, wins)
        def _(i):
            pltpu.make_async_copy(val_v.at[i], out_hbm.at[idx_v.at[i]], sem).wait()

    out = scatter_kernel(dest.reshape(n_pad // SC_WINDOW, SC_WINDOW), vals.reshape(n_pad // SC_WINDOW, SC_WINDOW))
    return out[:n_slots]


SC_GATHER_ROWS = 32


def _gather_rows(table, idx):
    n = idx.shape[0]
    sc = plsc.get_sparse_core_info()
    workers = sc.num_cores * sc.num_subcores
    quantum = workers * SC_GATHER_ROWS * SUBLANES
    n_pad = -(-n // quantum) * quantum
    filler = jnp.arange(n_pad - n, dtype=I32) % table.shape[0]
    idx = jnp.concatenate([idx, filler]).reshape(n_pad // SC_GATHER_ROWS, SC_GATHER_ROWS)
    per_w = n_pad // workers
    wins = per_w // SC_GATHER_ROWS
    mesh = plsc.VectorSubcoreMesh(core_axis_name="c", subcore_axis_name="s")
    buf = pltpu.VMEM((SC_GATHER_ROWS, D_MODEL), table.dtype)

    @functools.partial(
        pl.kernel, mesh=mesh,
        out_type=jax.ShapeDtypeStruct((n_pad, D_MODEL), table.dtype),
        scratch_types=[pltpu.VMEM((wins, SC_GATHER_ROWS), I32), buf, buf,
                       pltpu.SemaphoreType.DMA, pltpu.SemaphoreType.DMA],
        name="gather_rows",
    )
    def gather_kernel(table_hbm, idx_hbm, out_hbm, idx_v, buf0, buf1, sem0, sem1):
        w = lax.axis_index("s") * sc.num_cores + lax.axis_index("c")
        pltpu.sync_copy(idx_hbm.at[pl.ds(pl.multiple_of(w * wins, SUBLANES), wins)], idx_v)
        base = pl.multiple_of(w * per_w, SC_GATHER_ROWS)

        @pl.loop(0, wins, step=2)
        def _(i):
            g0 = pltpu.async_copy(table_hbm.at[idx_v.at[i]], buf0, sem0)
            g1 = pltpu.async_copy(table_hbm.at[idx_v.at[i + 1]], buf1, sem1)
            g0.wait()
            pltpu.sync_copy(buf0, out_hbm.at[pl.ds(base + i * SC_GATHER_ROWS, SC_GATHER_ROWS)])
            g1.wait()
            pltpu.sync_copy(buf1, out_hbm.at[pl.ds(base + (i + 1) * SC_GATHER_ROWS, SC_GATHER_ROWS)])

    return gather_kernel(table, idx)


def _routing_tables(route, counts, blk):
    T = route.shape[0]
    TK = T * TOP_K
    idx = route[:, 0:TOP_K].astype(I32)
    rank = route[:, 2 * TOP_K:3 * TOP_K].astype(I32)
    padded = (counts + blk - 1) // blk * blk
    pad_end = jnp.cumsum(padded)
    pad_start = pad_end - padded
    dest = pad_start[idx] + rank
    n_blocks = -(-TK // blk) + N_EXPERTS
    blk_start = jnp.arange(n_blocks, dtype=I32) * blk
    block_e = jnp.sum((blk_start[:, None] >= pad_end[None, :]).astype(I32), axis=1)
    block_e = jnp.minimum(block_e, N_EXPERTS - 1)
    last = (pad_start + counts)[block_e]
    n_valid = jnp.clip(last - blk_start, 0, blk)
    slot_a = _invert_slots(dest.reshape(TK), n_blocks * blk)
    slot = jnp.arange(n_blocks * blk, dtype=I32)
    slot_tok = jnp.where(slot < jnp.repeat(last, blk), slot_a // TOP_K, slot % T)
    return block_e, n_valid, slot_tok, dest


def _layer(x, mod, buf, past_k, past_v, p, lam_init, tm, tq, blk):
    B, L, _ = x.shape
    q, kb, vb, kf, vf, conv, state = _premix(x, mod, p["g_pre_mix"], p["w_in"], buf, p["w_dw"], p["b_dw"],
                                             p["gmat"], p["g_cn"], p["b_cn"], tm)
    if past_k is None:
        attn = _attention(q, kb, vb, p["lam_rows"], p["g_subln"], lam_init, tq, tq, 0, True)
    else:
        P = past_k.shape[1]
        keys = jnp.concatenate([past_k.reshape(B, P, QK_W).astype(BF16), kb], axis=1)
        vals = jnp.concatenate([past_v.reshape(B, P, ATTN_W).astype(BF16), vb], axis=1)
        attn = _attention(q, keys, vals, p["lam_rows"], p["g_subln"], lam_init, L, P + L, P, False)
    x1, h2, route, cnt = _postmix(attn, conv, x, mod, p["w_out"], p["g_post_mix"], p["g_pre_ffn"],
                                  p["w_router"], p["b_router"], tm)
    T = B * L
    route2 = route.reshape(T, LANES)
    counts = cnt[0, :N_EXPERTS].astype(I32)
    block_e, n_valid, slot_tok, dest = _routing_tables(route2, counts, blk)
    x_pad = _gather_rows(h2.reshape(T, D_MODEL), slot_tok)
    y_pad = _experts(x_pad, block_e, n_valid, p["w_gu"], p["b_gu"], p["w_down"], p["b_down"], blk)
    y = _gather_rows(y_pad, dest.T.reshape(T * TOP_K))
    out = _combine(y, route, x1, mod, p["g_post_ffn"], min(tm, 256))
    k_new = kf.reshape(B, L, N_HEADS, 2 * HEAD_DIM)
    v_new = vf.reshape(B, L, N_HEADS, V_DIM)
    return out, k_new, v_new, state[:, CONV_HALO - (CONV_W - 1):, :]


def _prepare_params(l, w_ada, b_ada, g_pre_mix, g_post_mix, w_in, lambda_q1, lambda_k1, lambda_q2, lambda_k2,
                    g_subln, w_dw, b_dw, g_cnorm, b_cnorm, w_out, g_pre_ffn, g_post_ffn,
                    w_router, b_router, w_gu, b_gu, w_down, b_down):
    lam_rows = jnp.zeros((8, LANES), F32)
    for r, vec in enumerate((lambda_q1[l], lambda_k1[l], lambda_q2[l], lambda_k2[l])):
        lam_rows = lam_rows.at[r, :HEAD_DIM].set(vec)
    ch = jnp.arange(CONV_CH, dtype=I32) // GROUP_CH
    gmat = (ch[:, None] == ch[None, :]).astype(BF16)
    w_dw_pad = jnp.zeros((CONV_HALO, CONV_CH), F32).at[:CONV_W].set(w_dw[l])
    w_router_pad = jnp.zeros((D_MODEL, LANES), F32).at[:, :N_EXPERTS].set(w_router[l])
    b_router_pad = jnp.full((1, LANES), NEG, F32).at[0, :N_EXPERTS].set(b_router[l])
    return dict(w_ada=w_ada[l], b_ada=b_ada[l], g_pre_mix=g_pre_mix[l], g_post_mix=g_post_mix[l],
                w_in=w_in[l].astype(BF16), lam_rows=lam_rows, g_subln=g_subln[l], w_dw=w_dw_pad, b_dw=b_dw[l],
                gmat=gmat, g_cn=g_cnorm[l], b_cn=b_cnorm[l], w_out=w_out[l].astype(BF16),
                g_pre_ffn=g_pre_ffn[l], g_post_ffn=g_post_ffn[l], w_router=w_router_pad, b_router=b_router_pad,
                w_gu=w_gu[l].astype(BF16), b_gu=b_gu[l], w_down=w_down[l].astype(BF16), b_down=b_down[l])


def kernel(x_prompt, x_sample, cache_k, cache_v, state_conv, c_prompt, c_sample, w_ada, b_ada, g_pre_mix, g_post_mix, w_in, lambda_q1, lambda_k1, lambda_q2, lambda_k2, g_subln, w_dw, b_dw, g_cnorm, b_cnorm, w_out, g_pre_ffn, g_post_ffn, w_router, b_router, w_gu, b_gu, w_down, b_down):
    depth = w_ada.shape[0]
    Bp, Lp, _ = x_prompt.shape
    Bs, Ls, _ = x_sample.shape
    yp, ys = x_prompt, x_sample
    outs = [[] for _ in range(6)]
    for l in range(depth):
        p = _prepare_params(l, w_ada, b_ada, g_pre_mix, g_post_mix, w_in, lambda_q1, lambda_k1, lambda_q2,
                            lambda_k2, g_subln, w_dw, b_dw, g_cnorm, b_cnorm, w_out, g_pre_ffn, g_post_ffn,
                            w_router, b_router, w_gu, b_gu, w_down, b_down)
        lam_init = 0.8 - 0.6 * math.exp(-0.3 * l)
        mod = _ada(jnp.concatenate([c_prompt, c_sample], axis=0), p["w_ada"], p["b_ada"])
        mod = mod.reshape(Bp + Bs, N_MOD, D_MODEL)
        buf_p = jnp.zeros((Bp, CONV_HALO, CONV_CH), F32)
        buf_s = jnp.pad(state_conv[l], ((0, 0), (CONV_HALO - (CONV_W - 1), 0), (0, 0)))
        yp, kp, vp, cp = _layer(yp, mod[:Bp], buf_p, None, None, p, lam_init,
                                tm=min(Lp, 512), tq=min(Lp, 512), blk=EXPERT_ROWS)
        ys, ks, vs, cs = _layer(ys, mod[Bp:], buf_s, cache_k[l], cache_v[l], p, lam_init,
                                tm=Ls, tq=Ls, blk=EXPERT_ROWS)
        for lst, val in zip(outs, (kp, vp, cp, ks, vs, cs)):
            lst.append(val)
    return (yp, ys) + tuple(jnp.stack(o) for o in outs)
```

```python
import functools
import math

import jax
import jax.numpy as jnp
from jax import lax
from jax.experimental import pallas as pl
from jax.experimental.pallas import tpu as pltpu
from jax.experimental.pallas import tpu_sc as plsc

F32 = jnp.float32
BF16 = jnp.bfloat16
I32 = jnp.int32

D_MODEL = 1024
CHUNK = 64
CHUNK_SHIFT = 6
N_HEADS = 4
V_DIM = 128
HEAD_DIM = 64
QK_W = 512
ATTN_W = 512
CONV_CH = 512
CONV_W = 31
CONV_GROUPS = 8
GROUP_CH = CONV_CH // CONV_GROUPS
IN_W = 2 * QK_W + ATTN_W + 2 * CONV_CH
N_MOD = 6
N_EXPERTS = 32
TOP_K = 4
D_FF = 1024
SWIGLU_ALPHA = 1.702
SWIGLU_LIMIT = 7.0
EPS = 1e-6

LANES = 128
SUBLANES = 8
CONV_HALO = 32
CONV_ROWS = 32
NEG = -1e30
LOG2E = math.log2(math.e)
ATTN_ROWS = 128
EXPERT_ROWS = 512
VMEM_LIMIT = 48 * 1024 * 1024
EXPERT_VMEM_LIMIT = 56 * 1024 * 1024


def _sigmoid(x):
    return 1.0 / (1.0 + jnp.exp(-x))


def _split_bf16(x):
    hi = x.astype(BF16)
    lo = (x - hi.astype(F32)).astype(BF16)
    return hi, lo


def _dot(a, b):
    return jnp.dot(a, b, preferred_element_type=F32)


def _dot3(a, b):
    ah, al = _split_bf16(a)
    bh, bl = _split_bf16(b)
    return _dot(ah, bh) + _dot(ah, bl) + _dot(al, bh)


def _rms(x):
    return x * lax.rsqrt(jnp.mean(x * x, axis=-1, keepdims=True) + EPS)


HALF = D_MODEL // 2
HIGH16 = -65536


def _pack_halves(x):
    lo = lax.bitcast_convert_type(x[:, :HALF].astype(BF16).astype(F32), I32)
    hi = lax.bitcast_convert_type(x[:, HALF:].astype(BF16).astype(F32), I32)
    return jnp.bitwise_or(jnp.bitwise_and(jnp.right_shift(lo, 16), 0xFFFF), jnp.bitwise_and(hi, HIGH16))


def _unpack_halves(w):
    lo = lax.bitcast_convert_type(jnp.left_shift(w, 16), F32)
    hi = lax.bitcast_convert_type(jnp.bitwise_and(w, HIGH16), F32)
    return lo, hi


def _ada_kernel(c_ref, w_ref, b_ref, o_ref):
    c = c_ref[...]
    o_ref[...] = _dot3(c * _sigmoid(c), w_ref[...]) + b_ref[...]


def _ada(c, w_ada, b_ada):
    n = c.shape[0]
    return pl.pallas_call(
        _ada_kernel,
        grid=(N_MOD,),
        in_specs=[pl.BlockSpec((n, D_MODEL), lambda j: (0, 0)),
                  pl.BlockSpec((D_MODEL, D_MODEL), lambda j: (0, j)),
                  pl.BlockSpec((1, D_MODEL), lambda j: (0, j))],
        out_specs=pl.BlockSpec((n, D_MODEL), lambda j: (0, j)),
        out_shape=jax.ShapeDtypeStruct((n, N_MOD * D_MODEL), F32),
        compiler_params=pltpu.CompilerParams(dimension_semantics=("arbitrary",), vmem_limit_bytes=VMEM_LIMIT),
        name="ada",
    )(c, w_ada, b_ada.reshape(1, N_MOD * D_MODEL))


def _premix_kernel(x_ref, mod_ref, g_ref, w_ref, buf_ref, wdw_ref, bdw_ref, gmat_ref, gcn_ref, bcn_ref,
                   q_ref, kb_ref, vb_ref, kf_ref, vf_ref, co_ref, st_ref, ext_ref, sh_ref, y_ref, *, tm):
    @pl.when(pl.program_id(1) == 0)
    def _():
        ext_ref[0:CONV_HALO, :] = buf_ref[0]

    x = x_ref[0]
    shift = mod_ref[0, 0:1, :]
    scale = mod_ref[0, 1:2, :]
    h = _rms(x) * g_ref[...] * (1.0 + scale) + shift
    hb = h.astype(BF16)

    u0 = 2 * QK_W + ATTN_W
    val = _dot(hb, w_ref[:, u0:u0 + CONV_CH])
    gate = _dot(hb, w_ref[:, u0 + CONV_CH:u0 + 2 * CONV_CH])
    ext_ref[CONV_HALO:CONV_HALO + tm, :] = val * _sigmoid(gate)

    off = CONV_HALO - (CONV_W - 1)
    span = tm + CONV_HALO - SUBLANES
    for b in range(1, SUBLANES):
        sh_ref[b - 1, 0:span, :] = ext_ref[b:b + span, :]
    for c in range(tm // CONV_ROWS):
        r0 = c * CONV_ROWS
        acc = jnp.zeros((CONV_ROWS, CONV_CH), F32)
        for j in range(CONV_W):
            b = (j + off) % SUBLANES
            a = r0 + j + off - b
            rows_j = ext_ref[a:a + CONV_ROWS, :] if b == 0 else sh_ref[b - 1, a:a + CONV_ROWS, :]
            acc = acc + wdw_ref[j:j + 1, :] * rows_j
        y_ref[r0:r0 + CONV_ROWS, :] = acc + bdw_ref[...]

    zq = _dot(hb, w_ref[:, 0:QK_W])
    q_ref[0] = (zq * (HEAD_DIM ** -0.5 * LOG2E)).astype(BF16)
    zk = _dot(hb, w_ref[:, QK_W:2 * QK_W])
    kf_ref[0] = zk
    kb_ref[0] = zk.astype(BF16)
    zv = _dot(hb, w_ref[:, 2 * QK_W:2 * QK_W + ATTN_W])
    vf_ref[0] = zv
    vb_ref[0] = zv.astype(BF16)

    y = y_ref[...]
    gm = gmat_ref[...]
    yh, yl = _split_bf16(y)
    mu = (_dot(yh, gm) + _dot(yl, gm)) * (1.0 / GROUP_CH)
    d = y - mu
    dh, dl = _split_bf16(d * d)
    var = (_dot(dh, gm) + _dot(dl, gm)) * (1.0 / GROUP_CH)
    yn = d * lax.rsqrt(var + EPS) * gcn_ref[...] + bcn_ref[...]
    co_ref[0] = (yn * _sigmoid(yn)).astype(BF16)

    tail = ext_ref[tm:tm + CONV_HALO, :]
    st_ref[0] = tail
    ext_ref[0:CONV_HALO, :] = tail


def _premix(x, mod, g_pre_mix, w_in_bf, buf, w_dw, b_dw, gmat, g_cn, b_cn, tm):
    B, L, _ = x.shape
    assert L % tm == 0 and tm % CONV_ROWS == 0
    row = lambda b, s: (b, s, 0)
    const2 = lambda b, s: (0, 0)
    bf_tile = jax.ShapeDtypeStruct((B, L, QK_W), BF16)
    f_tile = jax.ShapeDtypeStruct((B, L, QK_W), F32)
    return pl.pallas_call(
        functools.partial(_premix_kernel, tm=tm),
        grid=(B, L // tm),
        in_specs=[pl.BlockSpec((1, tm, D_MODEL), row),
                  pl.BlockSpec((1, N_MOD, D_MODEL), lambda b, s: (b, 0, 0)),
                  pl.BlockSpec((1, D_MODEL), const2),
                  pl.BlockSpec((D_MODEL, IN_W), const2),
                  pl.BlockSpec((1, CONV_HALO, CONV_CH), lambda b, s: (b, 0, 0)),
                  pl.BlockSpec((CONV_HALO, CONV_CH), const2),
                  pl.BlockSpec((1, CONV_CH), const2),
                  pl.BlockSpec((CONV_CH, CONV_CH), const2),
                  pl.BlockSpec((1, CONV_CH), const2),
                  pl.BlockSpec((1, CONV_CH), const2)],
        out_specs=[pl.BlockSpec((1, tm, QK_W), row)] * 6
                  + [pl.BlockSpec((1, CONV_HALO, CONV_CH), lambda b, s: (b, 0, 0))],
        out_shape=[bf_tile, bf_tile, bf_tile, f_tile, f_tile, bf_tile,
                   jax.ShapeDtypeStruct((B, CONV_HALO, CONV_CH), F32)],
        scratch_shapes=[pltpu.VMEM((CONV_HALO + tm, CONV_CH), F32),
                        pltpu.VMEM((SUBLANES - 1, CONV_HALO + tm, CONV_CH), F32),
                        pltpu.VMEM((tm, CONV_CH), F32)],
        compiler_params=pltpu.CompilerParams(dimension_semantics=("arbitrary", "arbitrary"),
                                             vmem_limit_bytes=VMEM_LIMIT),
        name="premix",
    )(x, mod, g_pre_mix.reshape(1, D_MODEL), w_in_bf, buf, w_dw, b_dw.reshape(1, CONV_CH), gmat,
      g_cn.reshape(1, CONV_CH), b_cn.reshape(1, CONV_CH))


def _attn_kernel(slopes_ref, q_ref, k_ref, v_ref, kb_ref, mt_ref, lam_ref, gs_ref, o_ref,
                 ke1_ref, ke2_ref, ve_ref, m1_ref, a1_ref, m2_ref, a2_ref,
                 *, tq, tk, rows, q_offset, causal, lam_init):
    slope = slopes_ref[pl.program_id(1)] * LOG2E
    lq = q_ref.shape[1]
    lk = k_ref.shape[1]
    nq = lq // tq
    lane = lax.broadcasted_iota(I32, (1, V_DIM), 1)

    low = jnp.broadcast_to(lane < HEAD_DIM, (lk, V_DIM))
    k = k_ref[0]
    zero = jnp.zeros((lk, V_DIM), BF16)
    ke1_ref[...] = jnp.where(low, k, zero)
    ke2_ref[...] = jnp.where(low, zero, k)
    ve_ref[:, 0:V_DIM] = v_ref[0]
    ve_ref[:, V_DIM:2 * V_DIM] = jnp.broadcast_to(jnp.where(lane == 0, 1.0, 0.0), (lk, V_DIM)).astype(BF16)

    nt = (((1,), (1,)), ((), ()))
    maps = ((ke1_ref, m1_ref, a1_ref), (ke2_ref, m2_ref, a2_ref))

    def lanes(x, n):
        if n % LANES == 0:
            return jnp.concatenate([x] * (n // LANES), axis=1)
        return jnp.broadcast_to(x[:, 0:1], (rows, n))

    def tile(q0, k0, masked):
        ve = ve_ref[pl.ds(k0, tk), :]
        q_first = (lax.broadcasted_iota(I32, (1, 1), 0) + (q0 + q_offset)).astype(F32)
        bias = kb_ref[0, :, pl.ds(k0, tk)] - slope * q_first
        for ke_ref, m_ref, a_ref in maps:
            ke = ke_ref[pl.ds(k0, tk), :]
            for r0 in range(0, tq, rows):
                s = lax.dot_general(q_ref[0, pl.ds(q0 + r0, rows), :], ke, nt, preferred_element_type=F32)
                s = s + bias
                if masked:
                    s = s + mt_ref[0, r0:r0 + rows, :]
                m_old = m_ref[r0:r0 + rows, :]
                m_new = jnp.maximum(m_old, jnp.max(s, axis=-1, keepdims=True))
                p = jnp.exp2(s - lanes(m_new, tk))
                alpha = jnp.exp2(m_old - m_new)
                a_ref[r0:r0 + rows, :] = (lanes(alpha, 2 * V_DIM) * a_ref[r0:r0 + rows, :]
                                          + _dot(p.astype(BF16), ve))
                m_ref[r0:r0 + rows, :] = m_new

    def reset():
        for _, m_ref, a_ref in maps:
            m_ref[...] = jnp.full(m_ref.shape, NEG, F32)
            a_ref[...] = jnp.zeros(a_ref.shape, F32)

    lv = lam_ref[...]
    lam = (jnp.exp(jnp.sum(lv[0:1, :] * lv[1:2, :], axis=-1, keepdims=True))
           - jnp.exp(jnp.sum(lv[2:3, :] * lv[3:4, :], axis=-1, keepdims=True)) + lam_init)

    def finish(q0):
        a1 = a1_ref[...]
        a2 = a2_ref[...]
        o = a1[:, 0:V_DIM] / a1[:, V_DIM:V_DIM + 1] - lam * (a2[:, 0:V_DIM] / a2[:, V_DIM:V_DIM + 1])
        o_ref[0, pl.ds(q0, tq), :] = (_rms(o) * gs_ref[...] * (1.0 - lam_init)).astype(BF16)

    if not causal:
        reset()
        tile(0, 0, True)
        finish(0)
        return

    def q_tile(qi, carry):
        q0 = pl.multiple_of(qi * tq, tq)
        reset()

        def pair(i, c):
            tile(q0, pl.multiple_of(2 * i * tk, tk), False)
            tile(q0, pl.multiple_of((2 * i + 1) * tk, tk), False)
            return c

        lax.fori_loop(0, qi // 2, pair, 0)
        odd = qi % 2 == 1

        @pl.when(odd)
        def _():
            tile(q0, pl.multiple_of((qi - 1) * tk, tk), False)
            tile(q0, pl.multiple_of(qi * tk, tk), True)

        @pl.when(jnp.logical_not(odd))
        def _():
            tile(q0, pl.multiple_of(qi * tk, tk), True)

        finish(q0)
        return carry

    lax.fori_loop(0, nq, q_tile, 0)


def _mask_table(slopes, tq, tk, q_offset):
    qpos = q_offset + jnp.arange(tq, dtype=I32)[:, None]
    kpos = jnp.arange(tk, dtype=I32)[None, :]
    visible = jnp.right_shift(kpos, CHUNK_SHIFT) <= jnp.right_shift(qpos, CHUNK_SHIFT)
    fix = jnp.where(kpos > qpos, 2 * (qpos - kpos), 0).astype(F32)
    return jnp.where(visible[None], (slopes * LOG2E)[:, None, None] * fix[None], NEG)


def _attention(q, k, v, lam_rows, g_subln, lam_init, tq, tk, q_offset, causal):
    B, Lq, _ = q.shape
    Lk = k.shape[1]
    assert Lq % tq == 0 and Lk % tk == 0 and (not causal or (tq == tk and tq % CHUNK == 0 and q_offset == 0))
    assert causal or Lk == tk
    rows = min(tq, ATTN_ROWS)
    slopes = jnp.asarray([2.0 ** (-8.0 * (h + 1) / N_HEADS) for h in range(N_HEADS)], F32)
    table = _mask_table(slopes, tq, tk, q_offset)
    key_bias = ((slopes * LOG2E)[:, None] * jnp.arange(Lk, dtype=F32)[None, :]).reshape(N_HEADS, 1, Lk)
    key_ext = pltpu.VMEM((Lk, V_DIM), BF16)
    stat = pltpu.VMEM((tq, LANES), F32)
    acc = pltpu.VMEM((tq, 2 * V_DIM), F32)
    head = lambda b, h, sl: (b, 0, h)
    return pl.pallas_call(
        functools.partial(_attn_kernel, tq=tq, tk=tk, rows=rows, q_offset=q_offset, causal=causal,
                          lam_init=lam_init),
        grid_spec=pltpu.PrefetchScalarGridSpec(
            num_scalar_prefetch=1,
            grid=(B, N_HEADS),
            in_specs=[pl.BlockSpec((1, Lq, V_DIM), head),
                      pl.BlockSpec((1, Lk, V_DIM), head),
                      pl.BlockSpec((1, Lk, V_DIM), head),
                      pl.BlockSpec((1, 1, Lk), lambda b, h, sl: (h, 0, 0)),
                      pl.BlockSpec((1, tq, tk), lambda b, h, sl: (h, 0, 0)),
                      pl.BlockSpec((8, LANES), lambda b, h, sl: (0, 0)),
                      pl.BlockSpec((1, V_DIM), lambda b, h, sl: (0, 0))],
            out_specs=pl.BlockSpec((1, Lq, V_DIM), head),
            scratch_shapes=[key_ext, key_ext, pltpu.VMEM((Lk, 2 * V_DIM), BF16), stat, acc, stat, acc]),
        out_shape=jax.ShapeDtypeStruct((B, Lq, ATTN_W), BF16),
        compiler_params=pltpu.CompilerParams(dimension_semantics=("arbitrary",) * 2,
                                             vmem_limit_bytes=VMEM_LIMIT),
        name="attn",
    )(slopes, q, k, v, key_bias, table, lam_rows, g_subln.reshape(1, V_DIM))


def _postmix_kernel(a_ref, c_ref, x_ref, mod_ref, wo_ref, gpm_ref, gpf_ref, wr_ref, br_ref,
                    x1_ref, h2_ref, rt_ref, cnt_ref, carry_ref, *, tm):
    first = jnp.logical_and(pl.program_id(0) == 0, pl.program_id(1) == 0)

    @pl.when(first)
    def _():
        carry_ref[...] = jnp.zeros(carry_ref.shape, F32)

    mix = _dot(a_ref[0], wo_ref[0:ATTN_W, :]) + _dot(c_ref[0], wo_ref[ATTN_W:ATTN_W + CONV_CH, :])
    gate_a = mod_ref[0, 2:3, :]
    shift_f = mod_ref[0, 3:4, :]
    scale_f = mod_ref[0, 4:5, :]
    x1 = x_ref[0] + gate_a * (_rms(mix) * gpm_ref[...])
    x1_ref[0] = x1
    h2 = _rms(x1) * gpf_ref[...] * (1.0 + scale_f) + shift_f
    h2_ref[0] = _pack_halves(h2)

    logits = _dot3(h2, wr_ref[...]) + br_ref[...]
    lane = lax.broadcasted_iota(I32, (tm, LANES), 1).astype(F32)
    vals, idxs = [], []
    for _ in range(TOP_K):
        m = jnp.max(logits, axis=-1, keepdims=True)
        idx = jnp.min(jnp.where(logits == m, lane, float(LANES)), axis=-1, keepdims=True)
        vals.append(m)
        idxs.append(idx)
        logits = jnp.where(lane == idx, 2.0 * NEG, logits)
    es = [jnp.exp(v - vals[0]) for v in vals]
    denom = es[0] + es[1] + es[2] + es[3]

    onehot = jnp.zeros((tm, LANES), F32)
    for idx in idxs:
        onehot = jnp.where(lane == idx, 1.0, onehot)
    r_i = lax.broadcasted_iota(I32, (tm, tm), 0)
    c_i = lax.broadcasted_iota(I32, (tm, tm), 1)
    tri = jnp.where(c_i < r_i, 1.0, 0.0).astype(BF16)
    before = _dot(tri, onehot.astype(BF16)) + carry_ref[0:1, :]

    rt = jnp.zeros((tm, LANES), F32)
    for k in range(TOP_K):
        rank = jnp.sum(jnp.where(lane == idxs[k], before, 0.0), axis=-1, keepdims=True)
        rt = jnp.where(lane == k, idxs[k], rt)
        rt = jnp.where(lane == TOP_K + k, es[k] / denom, rt)
        rt = jnp.where(lane == 2 * TOP_K + k, rank, rt)
    rt_ref[0] = rt

    carry_ref[...] = carry_ref[...] + jnp.sum(onehot, axis=0, keepdims=True)
    cnt_ref[...] = carry_ref[...]


def _postmix(attn, conv, x, mod, w_out_bf, g_post_mix, g_pre_ffn, w_router_pad, b_router_pad, tm):
    B, L, _ = x.shape
    row = lambda b, s: (b, s, 0)
    const2 = lambda b, s: (0, 0)
    return pl.pallas_call(
        functools.partial(_postmix_kernel, tm=tm),
        grid=(B, L // tm),
        in_specs=[pl.BlockSpec((1, tm, ATTN_W), row),
                  pl.BlockSpec((1, tm, CONV_CH), row),
                  pl.BlockSpec((1, tm, D_MODEL), row),
                  pl.BlockSpec((1, N_MOD, D_MODEL), lambda b, s: (b, 0, 0)),
                  pl.BlockSpec((D_MODEL, D_MODEL), const2),
                  pl.BlockSpec((1, D_MODEL), const2),
                  pl.BlockSpec((1, D_MODEL), const2),
                  pl.BlockSpec((D_MODEL, LANES), const2),
                  pl.BlockSpec((1, LANES), const2)],
        out_specs=[pl.BlockSpec((1, tm, D_MODEL), row),
                   pl.BlockSpec((1, tm, HALF), row),
                   pl.BlockSpec((1, tm, LANES), row),
                   pl.BlockSpec((8, LANES), const2)],
        out_shape=[jax.ShapeDtypeStruct((B, L, D_MODEL), F32),
                   jax.ShapeDtypeStruct((B, L, HALF), I32),
                   jax.ShapeDtypeStruct((B, L, LANES), F32),
                   jax.ShapeDtypeStruct((8, LANES), F32)],
        scratch_shapes=[pltpu.VMEM((8, LANES), F32)],
        compiler_params=pltpu.CompilerParams(dimension_semantics=("arbitrary", "arbitrary"),
                                             vmem_limit_bytes=VMEM_LIMIT),
        name="postmix",
    )(attn, conv, x, mod, w_out_bf, g_post_mix.reshape(1, D_MODEL), g_pre_ffn.reshape(1, D_MODEL),
      w_router_pad, b_router_pad)


def _expert_kernel(be_ref, nv_ref, x_ref, wgu_ref, bgu_ref, wd_ref, bd_ref, y_ref, wgu_bf, wd_bf):
    i = pl.program_id(0)
    ff_half = D_FF // 2

    @pl.when(jnp.logical_or(i == 0, be_ref[i] != be_ref[jnp.maximum(i - 1, 0)]))
    def _():
        wgu_bf[...] = wgu_ref[0].astype(BF16)
        wd_bf[...] = wd_ref[0].astype(BF16)

    @pl.when(nv_ref[i] > 0)
    def _():
        x_lo, x_hi = _unpack_halves(x_ref[...])
        x_lo = x_lo.astype(BF16)
        x_hi = x_hi.astype(BF16)
        acc = None
        for c in range(2):
            lo, hi = c * ff_half, (c + 1) * ff_half
            g = (_dot(x_lo, wgu_bf[0:HALF, lo:hi]) + _dot(x_hi, wgu_bf[HALF:D_MODEL, lo:hi])
                 + bgu_ref[0, :, lo:hi])
            lin = (_dot(x_lo, wgu_bf[0:HALF, D_FF + lo:D_FF + hi]) + _dot(x_hi, wgu_bf[HALF:D_MODEL, D_FF + lo:D_FF + hi])
                   + bgu_ref[0, :, D_FF + lo:D_FF + hi])
            g = jnp.minimum(g, SWIGLU_LIMIT)
            lin = jnp.clip(lin, -SWIGLU_LIMIT, SWIGLU_LIMIT)
            act = g * _sigmoid(SWIGLU_ALPHA * g) * (lin + 1.0)
            part = _dot(act.astype(BF16), wd_bf[lo:hi, :])
            acc = part if acc is None else acc + part
        y_ref[...] = _pack_halves(acc + bd_ref[0])


def _experts(x_pad, block_e, n_valid, w_gu, b_gu, w_down, b_down, blk):
    n_blocks = block_e.shape[0]
    ex = lambda i, be, nv: (be[i], 0, 0)
    rows = lambda i, be, nv: (i, 0)
    return pl.pallas_call(
        _expert_kernel,
        grid_spec=pltpu.PrefetchScalarGridSpec(
            num_scalar_prefetch=2,
            grid=(n_blocks,),
            in_specs=[pl.BlockSpec((blk, HALF), rows),
                      pl.BlockSpec((1, D_MODEL, 2 * D_FF), ex), pl.BlockSpec((1, 1, 2 * D_FF), ex),
                      pl.BlockSpec((1, D_FF, D_MODEL), ex), pl.BlockSpec((1, 1, D_MODEL), ex)],
            out_specs=pl.BlockSpec((blk, HALF), rows),
            scratch_shapes=[pltpu.VMEM((D_MODEL, 2 * D_FF), BF16), pltpu.VMEM((D_FF, D_MODEL), BF16)]),
        out_shape=jax.ShapeDtypeStruct((n_blocks * blk, HALF), I32),
        compiler_params=pltpu.CompilerParams(dimension_semantics=("arbitrary",),
                                             vmem_limit_bytes=EXPERT_VMEM_LIMIT),
        name="experts",
    )(block_e, n_valid, x_pad, w_gu, b_gu.reshape(N_EXPERTS, 1, 2 * D_FF), w_down,
      b_down.reshape(N_EXPERTS, 1, D_MODEL))


def _combine_kernel(y0_ref, y1_ref, y2_ref, y3_ref, rt_ref, x1_ref, mod_ref, g_ref, o_ref):
    rt = rt_ref[0]
    f_lo = f_hi = None
    for k, y_ref in enumerate((y0_ref, y1_ref, y2_ref, y3_ref)):
        lo, hi = _unpack_halves(y_ref[...])
        w = rt[:, TOP_K + k:TOP_K + k + 1]
        f_lo = w * lo if f_lo is None else f_lo + w * lo
        f_hi = w * hi if f_hi is None else f_hi + w * hi
    ms = (jnp.sum(f_lo * f_lo, axis=-1, keepdims=True) + jnp.sum(f_hi * f_hi, axis=-1, keepdims=True)) / D_MODEL
    r = lax.rsqrt(ms + EPS)
    gate_f = mod_ref[0, 5:6, :]
    g = g_ref[...]
    o_ref[0, :, 0:HALF] = x1_ref[0, :, 0:HALF] + gate_f[:, 0:HALF] * (f_lo * r * g[:, 0:HALF])
    o_ref[0, :, HALF:D_MODEL] = x1_ref[0, :, HALF:D_MODEL] + gate_f[:, HALF:D_MODEL] * (f_hi * r * g[:, HALF:D_MODEL])


def _combine(y, route, x1, mod, g_post_ffn, tm):
    B, L, _ = x1.shape
    row = lambda b, s: (b, s, 0)
    tiles = B * L // tm
    y_specs = [pl.BlockSpec((tm, HALF), lambda b, s, k=k: (k * tiles + b * (L // tm) + s, 0))
               for k in range(TOP_K)]
    return pl.pallas_call(
        _combine_kernel,
        grid=(B, L // tm),
        in_specs=y_specs + [
                  pl.BlockSpec((1, tm, LANES), row),
                  pl.BlockSpec((1, tm, D_MODEL), row),
                  pl.BlockSpec((1, N_MOD, D_MODEL), lambda b, s: (b, 0, 0)),
                  pl.BlockSpec((1, D_MODEL), lambda b, s: (0, 0))],
        out_specs=pl.BlockSpec((1, tm, D_MODEL), row),
        out_shape=jax.ShapeDtypeStruct((B, L, D_MODEL), F32),
        compiler_params=pltpu.CompilerParams(dimension_semantics=("arbitrary", "arbitrary"),
                                             vmem_limit_bytes=VMEM_LIMIT),
        name="combine",
    )(y, y, y, y, route, x1, mod, g_post_ffn.reshape(1, D_MODEL))


SC_WINDOW = 128


def _invert_slots(dest, n_slots):
    n = dest.shape[0]
    sc = plsc.get_sparse_core_info()
    workers = sc.num_cores * sc.num_subcores
    quantum = workers * SC_WINDOW * SUBLANES
    n_pad = -(-n // quantum) * quantum
    extra = n_pad - n
    dest = jnp.concatenate([dest, n_slots + jnp.arange(extra, dtype=I32)])
    vals = jnp.arange(n_pad, dtype=I32)
    wins = n_pad // (workers * SC_WINDOW)
    mesh = plsc.VectorSubcoreMesh(core_axis_name="c", subcore_axis_name="s")

    @functools.partial(
        pl.kernel, mesh=mesh,
        out_type=jax.ShapeDtypeStruct((n_slots + extra,), I32),
        scratch_types=[pltpu.VMEM((wins, SC_WINDOW), I32), pltpu.VMEM((wins, SC_WINDOW), I32),
                       pltpu.SemaphoreType.DMA],
        name="invert_slots",
    )
    def scatter_kernel(dest_hbm, vals_hbm, out_hbm, idx_v, val_v, sem):
        w = lax.axis_index("s") * sc.num_cores + lax.axis_index("c")
        first = pl.multiple_of(w * wins, SUBLANES)
        pltpu.sync_copy(dest_hbm.at[pl.ds(first, wins)], idx_v)
        pltpu.sync_copy(vals_hbm.at[pl.ds(first, wins)], val_v)

        @pl.loop(0, wins)
        def _(i):
            pltpu.async_copy(val_v.at[i], out_hbm.at[idx_v.at[i]], sem)

        @pl.loop(0, wins)
        def _(i):
            pltpu.make_async_copy(val_v.at[i], out_hbm.at[idx_v.at[i]], sem).wait()

    out = scatter_kernel(dest.reshape(n_pad // SC_WINDOW, SC_WINDOW), vals.reshape(n_pad // SC_WINDOW, SC_WINDOW))
    return out[:n_slots]


SC_GATHER_BYTES = 128 * 1024


def _gather_rows(table, idx):
    n = idx.shape[0]
    width = table.shape[1]
    rows = SC_GATHER_BYTES // (width * 4)
    sc = plsc.get_sparse_core_info()
    workers = sc.num_cores * sc.num_subcores
    quantum = workers * rows * SUBLANES
    n_pad = -(-n // quantum) * quantum
    filler = jnp.arange(n_pad - n, dtype=I32) % table.shape[0]
    idx = jnp.concatenate([idx, filler]).reshape(n_pad // rows, rows)
    per_w = n_pad // workers
    wins = per_w // rows
    mesh = plsc.VectorSubcoreMesh(core_axis_name="c", subcore_axis_name="s")
    buf = pltpu.VMEM((rows, width), table.dtype)

    @functools.partial(
        pl.kernel, mesh=mesh,
        out_type=jax.ShapeDtypeStruct((n_pad, width), table.dtype),
        scratch_types=[pltpu.VMEM((wins, rows), I32), buf, buf, pltpu.SemaphoreType.DMA, pltpu.SemaphoreType.DMA],
        name="gather_rows",
    )
    def gather_kernel(table_hbm, idx_hbm, out_hbm, idx_v, buf0, buf1, sem0, sem1):
        w = lax.axis_index("s") * sc.num_cores + lax.axis_index("c")
        pltpu.sync_copy(idx_hbm.at[pl.ds(pl.multiple_of(w * wins, SUBLANES), wins)], idx_v)
        base = pl.multiple_of(w * per_w, rows)

        @pl.loop(0, wins, step=2)
        def _(i):
            g0 = pltpu.async_copy(table_hbm.at[idx_v.at[i]], buf0, sem0)
            g1 = pltpu.async_copy(table_hbm.at[idx_v.at[i + 1]], buf1, sem1)
            g0.wait()
            pltpu.sync_copy(buf0, out_hbm.at[pl.ds(base + i * rows, rows)])
            g1.wait()
            pltpu.sync_copy(buf1, out_hbm.at[pl.ds(base + (i + 1) * rows, rows)])

    return gather_kernel(table, idx)


def _routing_tables(route, counts, blk):
    T = route.shape[0]
    TK = T * TOP_K
    idx = route[:, 0:TOP_K].astype(I32)
    rank = route[:, 2 * TOP_K:3 * TOP_K].astype(I32)
    padded = (counts + blk - 1) // blk * blk
    pad_end = jnp.cumsum(padded)
    pad_start = pad_end - padded
    dest = pad_start[idx] + rank
    n_blocks = -(-TK // blk) + N_EXPERTS
    blk_start = jnp.arange(n_blocks, dtype=I32) * blk
    block_e = jnp.sum((blk_start[:, None] >= pad_end[None, :]).astype(I32), axis=1)
    block_e = jnp.minimum(block_e, N_EXPERTS - 1)
    last = (pad_start + counts)[block_e]
    n_valid = jnp.clip(last - blk_start, 0, blk)
    slot_a = _invert_slots(dest.reshape(TK), n_blocks * blk)
    slot = jnp.arange(n_blocks * blk, dtype=I32)
    slot_tok = jnp.where(slot < jnp.repeat(last, blk), slot_a // TOP_K, slot % T)
    return block_e, n_valid, slot_tok, dest


def _layer(x, mod, buf, past_k, past_v, p, lam_init, tm, tq, blk):
    B, L, _ = x.shape
    q, kb, vb, kf, vf, conv, state = _premix(x, mod, p["g_pre_mix"], p["w_in"], buf, p["w_dw"], p["b_dw"],
                                             p["gmat"], p["g_cn"], p["b_cn"], tm)
    if past_k is None:
        attn = _attention(q, kb, vb, p["lam_rows"], p["g_subln"], lam_init, tq, tq, 0, True)
    else:
        P = past_k.shape[1]
        keys = jnp.concatenate([past_k.reshape(B, P, QK_W).astype(BF16), kb], axis=1)
        vals = jnp.concatenate([past_v.reshape(B, P, ATTN_W).astype(BF16), vb], axis=1)
        attn = _attention(q, keys, vals, p["lam_rows"], p["g_subln"], lam_init, L, P + L, P, False)
    x1, h2, route, cnt = _postmix(attn, conv, x, mod, p["w_out"], p["g_post_mix"], p["g_pre_ffn"],
                                  p["w_router"], p["b_router"], tm)
    T = B * L
    route2 = route.reshape(T, LANES)
    counts = cnt[0, :N_EXPERTS].astype(I32)
    block_e, n_valid, slot_tok, dest = _routing_tables(route2, counts, blk)
    x_pad = _gather_rows(h2.reshape(T, HALF), slot_tok)
    y_pad = _experts(x_pad, block_e, n_valid, p["w_gu"], p["b_gu"], p["w_down"], p["b_down"], blk)
    y = _gather_rows(y_pad, dest.T.reshape(T * TOP_K))
    out = _combine(y, route, x1, mod, p["g_post_ffn"], min(tm, 256))
    k_new = kf.reshape(B, L, N_HEADS, 2 * HEAD_DIM)
    v_new = vf.reshape(B, L, N_HEADS, V_DIM)
    return out, k_new, v_new, state[:, CONV_HALO - (CONV_W - 1):, :]


def _prepare_params(l, w_ada, b_ada, g_pre_mix, g_post_mix, w_in, lambda_q1, lambda_k1, lambda_q2, lambda_k2,
                    g_subln, w_dw, b_dw, g_cnorm, b_cnorm, w_out, g_pre_ffn, g_post_ffn,
                    w_router, b_router, w_gu, b_gu, w_down, b_down):
    lam_rows = jnp.zeros((8, LANES), F32)
    for r, vec in enumerate((lambda_q1[l], lambda_k1[l], lambda_q2[l], lambda_k2[l])):
        lam_rows = lam_rows.at[r, :HEAD_DIM].set(vec)
    ch = jnp.arange(CONV_CH, dtype=I32) // GROUP_CH
    gmat = (ch[:, None] == ch[None, :]).astype(BF16)
    w_dw_pad = jnp.zeros((CONV_HALO, CONV_CH), F32).at[:CONV_W].set(w_dw[l])
    w_router_pad = jnp.zeros((D_MODEL, LANES), F32).at[:, :N_EXPERTS].set(w_router[l])
    b_router_pad = jnp.full((1, LANES), NEG, F32).at[0, :N_EXPERTS].set(b_router[l])
    return dict(w_ada=w_ada[l], b_ada=b_ada[l], g_pre_mix=g_pre_mix[l], g_post_mix=g_post_mix[l],
                w_in=w_in[l].astype(BF16), lam_rows=lam_rows, g_subln=g_subln[l], w_dw=w_dw_pad, b_dw=b_dw[l],
                gmat=gmat, g_cn=g_cnorm[l], b_cn=b_cnorm[l], w_out=w_out[l].astype(BF16),
                g_pre_ffn=g_pre_ffn[l], g_post_ffn=g_post_ffn[l], w_router=w_router_pad, b_router=b_router_pad,
                w_gu=w_gu[l], b_gu=b_gu[l], w_down=w_down[l], b_down=b_down[l])


def kernel(x_prompt, x_sample, cache_k, cache_v, state_conv, c_prompt, c_sample, w_ada, b_ada, g_pre_mix, g_post_mix, w_in, lambda_q1, lambda_k1, lambda_q2, lambda_k2, g_subln, w_dw, b_dw, g_cnorm, b_cnorm, w_out, g_pre_ffn, g_post_ffn, w_router, b_router, w_gu, b_gu, w_down, b_down):
    depth = w_ada.shape[0]
    Bp, Lp, _ = x_prompt.shape
    Bs, Ls, _ = x_sample.shape
    yp, ys = x_prompt, x_sample
    outs = [[] for _ in range(6)]
    for l in range(depth):
        p = _prepare_params(l, w_ada, b_ada, g_pre_mix, g_post_mix, w_in, lambda_q1, lambda_k1, lambda_q2,
                            lambda_k2, g_subln, w_dw, b_dw, g_cnorm, b_cnorm, w_out, g_pre_ffn, g_post_ffn,
                            w_router, b_router, w_gu, b_gu, w_down, b_down)
        lam_init = 0.8 - 0.6 * math.exp(-0.3 * l)
        mod = _ada(jnp.concatenate([c_prompt, c_sample], axis=0), p["w_ada"], p["b_ada"])
        mod = mod.reshape(Bp + Bs, N_MOD, D_MODEL)
        buf_p = jnp.zeros((Bp, CONV_HALO, CONV_CH), F32)
        buf_s = jnp.pad(state_conv[l], ((0, 0), (CONV_HALO - (CONV_W - 1), 0), (0, 0)))
        yp, kp, vp, cp = _layer(yp, mod[:Bp], buf_p, None, None, p, lam_init,
                                tm=min(Lp, 512), tq=min(Lp, 512), blk=EXPERT_ROWS)
        ys, ks, vs, cs = _layer(ys, mod[Bp:], buf_s, cache_k[l], cache_v[l], p, lam_init,
                                tm=Ls, tq=Ls, blk=EXPERT_ROWS)
        for lst, val in zip(outs, (kp, vp, cp, ks, vs, cs)):
            lst.append(val)
    return (yp, ys) + tuple(jnp.stack(o) for o in outs)
```

```python
import functools
import math

import jax
import jax.numpy as jnp
from jax import lax
from jax.experimental import pallas as pl
from jax.experimental.pallas import tpu as pltpu
from jax.experimental.pallas import tpu_sc as plsc

F32 = jnp.float32
BF16 = jnp.bfloat16
I32 = jnp.int32

D_MODEL = 1024
CHUNK = 64
CHUNK_SHIFT = 6
N_HEADS = 4
V_DIM = 128
HEAD_DIM = 64
QK_W = 512
ATTN_W = 512
CONV_CH = 512
CONV_W = 31
CONV_GROUPS = 8
GROUP_CH = CONV_CH // CONV_GROUPS
IN_W = 2 * QK_W + ATTN_W + 2 * CONV_CH
N_MOD = 6
N_EXPERTS = 32
TOP_K = 4
D_FF = 1024
SWIGLU_ALPHA = 1.702
SWIGLU_LIMIT = 7.0
EPS = 1e-6

LANES = 128
SUBLANES = 8
CONV_HALO = 32
CONV_ROWS = 32
NEG = -1e30
LOG2E = math.log2(math.e)
ATTN_ROWS = 128
EXPERT_ROWS = 512
VMEM_LIMIT = 48 * 1024 * 1024
EXPERT_VMEM_LIMIT = 56 * 1024 * 1024


def _sigmoid(x):
    return 1.0 / (1.0 + jnp.exp(-x))


def _split_bf16(x):
    hi = x.astype(BF16)
    lo = (x - hi.astype(F32)).astype(BF16)
    return hi, lo


def _dot(a, b):
    return jnp.dot(a, b, preferred_element_type=F32)


def _dot3(a, b):
    ah, al = _split_bf16(a)
    bh, bl = _split_bf16(b)
    return _dot(ah, bh) + _dot(ah, bl) + _dot(al, bh)


def _rms(x):
    return x * lax.rsqrt(jnp.mean(x * x, axis=-1, keepdims=True) + EPS)


HALF = D_MODEL // 2
HIGH16 = -65536


def _pack_halves(x):
    lo = lax.bitcast_convert_type(x[:, :HALF].astype(BF16).astype(F32), I32)
    hi = lax.bitcast_convert_type(x[:, HALF:].astype(BF16).astype(F32), I32)
    return jnp.bitwise_or(jnp.bitwise_and(jnp.right_shift(lo, 16), 0xFFFF), jnp.bitwise_and(hi, HIGH16))


def _unpack_halves(w):
    lo = lax.bitcast_convert_type(jnp.left_shift(w, 16), F32)
    hi = lax.bitcast_convert_type(jnp.bitwise_and(w, HIGH16), F32)
    return lo, hi


def _ada_kernel(c_ref, w_ref, b_ref, o_ref):
    c = c_ref[...]
    o_ref[...] = _dot3(c * _sigmoid(c), w_ref[...]) + b_ref[...]


def _ada(c, w_ada, b_ada):
    n = c.shape[0]
    return pl.pallas_call(
        _ada_kernel,
        grid=(N_MOD,),
        in_specs=[pl.BlockSpec((n, D_MODEL), lambda j: (0, 0)),
                  pl.BlockSpec((D_MODEL, D_MODEL), lambda j: (0, j)),
                  pl.BlockSpec((1, D_MODEL), lambda j: (0, j))],
        out_specs=pl.BlockSpec((n, D_MODEL), lambda j: (0, j)),
        out_shape=jax.ShapeDtypeStruct((n, N_MOD * D_MODEL), F32),
        compiler_params=pltpu.CompilerParams(dimension_semantics=("arbitrary",), vmem_limit_bytes=VMEM_LIMIT),
        name="ada",
    )(c, w_ada, b_ada.reshape(1, N_MOD * D_MODEL))


def _premix_kernel(x_ref, mod_ref, g_ref, w_ref, buf_ref, wdw_ref, bdw_ref, gmat_ref, gcn_ref, bcn_ref,
                   q_ref, kb_ref, vb_ref, kf_ref, vf_ref, co_ref, st_ref, ext_ref, sh_ref, y_ref, *, tm):
    @pl.when(pl.program_id(1) == 0)
    def _():
        ext_ref[0:CONV_HALO, :] = buf_ref[0]

    x = x_ref[0]
    shift = mod_ref[0, 0:1, :]
    scale = mod_ref[0, 1:2, :]
    h = _rms(x) * g_ref[...] * (1.0 + scale) + shift
    hb = h.astype(BF16)

    u0 = 2 * QK_W + ATTN_W
    val = _dot(hb, w_ref[:, u0:u0 + CONV_CH])
    gate = _dot(hb, w_ref[:, u0 + CONV_CH:u0 + 2 * CONV_CH])
    ext_ref[CONV_HALO:CONV_HALO + tm, :] = val * _sigmoid(gate)

    off = CONV_HALO - (CONV_W - 1)
    span = tm + CONV_HALO - SUBLANES
    for b in range(1, SUBLANES):
        sh_ref[b - 1, 0:span, :] = ext_ref[b:b + span, :]
    for c in range(tm // CONV_ROWS):
        r0 = c * CONV_ROWS
        acc = jnp.zeros((CONV_ROWS, CONV_CH), F32)
        for j in range(CONV_W):
            b = (j + off) % SUBLANES
            a = r0 + j + off - b
            rows_j = ext_ref[a:a + CONV_ROWS, :] if b == 0 else sh_ref[b - 1, a:a + CONV_ROWS, :]
            acc = acc + wdw_ref[j:j + 1, :] * rows_j
        y_ref[r0:r0 + CONV_ROWS, :] = acc + bdw_ref[...]

    zq = _dot(hb, w_ref[:, 0:QK_W])
    q_ref[0] = (zq * (HEAD_DIM ** -0.5 * LOG2E)).astype(BF16)
    zk = _dot(hb, w_ref[:, QK_W:2 * QK_W])
    kf_ref[0] = zk
    kb_ref[0] = zk.astype(BF16)
    zv = _dot(hb, w_ref[:, 2 * QK_W:2 * QK_W + ATTN_W])
    vf_ref[0] = zv
    vb_ref[0] = zv.astype(BF16)

    y = y_ref[...]
    gm = gmat_ref[...]
    yh, yl = _split_bf16(y)
    mu = (_dot(yh, gm) + _dot(yl, gm)) * (1.0 / GROUP_CH)
    d = y - mu
    dh, dl = _split_bf16(d * d)
    var = (_dot(dh, gm) + _dot(dl, gm)) * (1.0 / GROUP_CH)
    yn = d * lax.rsqrt(var + EPS) * gcn_ref[...] + bcn_ref[...]
    co_ref[0] = (yn * _sigmoid(yn)).astype(BF16)

    tail = ext_ref[tm:tm + CONV_HALO, :]
    st_ref[0] = tail
    ext_ref[0:CONV_HALO, :] = tail


def _premix(x, mod, g_pre_mix, w_in_bf, buf, w_dw, b_dw, gmat, g_cn, b_cn, tm):
    B, L, _ = x.shape
    assert L % tm == 0 and tm % CONV_ROWS == 0
    row = lambda b, s: (b, s, 0)
    const2 = lambda b, s: (0, 0)
    bf_tile = jax.ShapeDtypeStruct((B, L, QK_W), BF16)
    f_tile = jax.ShapeDtypeStruct((B, L, QK_W), F32)
    return pl.pallas_call(
        functools.partial(_premix_kernel, tm=tm),
        grid=(B, L // tm),
        in_specs=[pl.BlockSpec((1, tm, D_MODEL), row),
                  pl.BlockSpec((1, N_MOD, D_MODEL), lambda b, s: (b, 0, 0)),
                  pl.BlockSpec((1, D_MODEL), const2),
                  pl.BlockSpec((D_MODEL, IN_W), const2),
                  pl.BlockSpec((1, CONV_HALO, CONV_CH), lambda b, s: (b, 0, 0)),
                  pl.BlockSpec((CONV_HALO, CONV_CH), const2),
                  pl.BlockSpec((1, CONV_CH), const2),
                  pl.BlockSpec((CONV_CH, CONV_CH), const2),
                  pl.BlockSpec((1, CONV_CH), const2),
                  pl.BlockSpec((1, CONV_CH), const2)],
        out_specs=[pl.BlockSpec((1, tm, QK_W), row)] * 6
                  + [pl.BlockSpec((1, CONV_HALO, CONV_CH), lambda b, s: (b, 0, 0))],
        out_shape=[bf_tile, bf_tile, bf_tile, f_tile, f_tile, bf_tile,
                   jax.ShapeDtypeStruct((B, CONV_HALO, CONV_CH), F32)],
        scratch_shapes=[pltpu.VMEM((CONV_HALO + tm, CONV_CH), F32),
                        pltpu.VMEM((SUBLANES - 1, CONV_HALO + tm, CONV_CH), F32),
                        pltpu.VMEM((tm, CONV_CH), F32)],
        compiler_params=pltpu.CompilerParams(dimension_semantics=("arbitrary", "arbitrary"),
                                             vmem_limit_bytes=VMEM_LIMIT),
        name="premix",
    )(x, mod, g_pre_mix.reshape(1, D_MODEL), w_in_bf, buf, w_dw, b_dw.reshape(1, CONV_CH), gmat,
      g_cn.reshape(1, CONV_CH), b_cn.reshape(1, CONV_CH))


def _attn_kernel(slopes_ref, q_ref, k_ref, v_ref, kb_ref, mt_ref, lam_ref, gs_ref, o_ref,
                 ke1_ref, ke2_ref, ve_ref, m1_ref, a1_ref, m2_ref, a2_ref,
                 *, tq, tk, rows, q_offset, causal, lam_init):
    slope = slopes_ref[pl.program_id(1)] * LOG2E
    lq = q_ref.shape[1]
    lk = k_ref.shape[1]
    nq = lq // tq
    lane = lax.broadcasted_iota(I32, (1, V_DIM), 1)

    low = jnp.broadcast_to(lane < HEAD_DIM, (lk, V_DIM))
    k = k_ref[0]
    zero = jnp.zeros((lk, V_DIM), BF16)
    ke1_ref[...] = jnp.where(low, k, zero)
    ke2_ref[...] = jnp.where(low, zero, k)
    ve_ref[:, 0:V_DIM] = v_ref[0]
    ve_ref[:, V_DIM:2 * V_DIM] = jnp.broadcast_to(jnp.where(lane == 0, 1.0, 0.0), (lk, V_DIM)).astype(BF16)

    nt = (((1,), (1,)), ((), ()))
    maps = ((ke1_ref, m1_ref, a1_ref), (ke2_ref, m2_ref, a2_ref))

    def lanes(x, n):
        if n % LANES == 0:
            return jnp.concatenate([x] * (n // LANES), axis=1)
        return jnp.broadcast_to(x[:, 0:1], (rows, n))

    def tile(q0, k0, masked):
        ve = ve_ref[pl.ds(k0, tk), :]
        q_first = (lax.broadcasted_iota(I32, (1, 1), 0) + (q0 + q_offset)).astype(F32)
        bias = kb_ref[0, :, pl.ds(k0, tk)] - slope * q_first
        for ke_ref, m_ref, a_ref in maps:
            ke = ke_ref[pl.ds(k0, tk), :]
            for r0 in range(0, tq, rows):
                s = lax.dot_general(q_ref[0, pl.ds(q0 + r0, rows), :], ke, nt, preferred_element_type=F32)
                s = s + bias
                if masked:
                    s = s + mt_ref[0, r0:r0 + rows, :]
                m_old = m_ref[r0:r0 + rows, :]
                m_new = jnp.maximum(m_old, jnp.max(s, axis=-1, keepdims=True))
                p = jnp.exp2(s - lanes(m_new, tk))
                alpha = jnp.exp2(m_old - m_new)
                a_ref[r0:r0 + rows, :] = (lanes(alpha, 2 * V_DIM) * a_ref[r0:r0 + rows, :]
                                          + _dot(p.astype(BF16), ve))
                m_ref[r0:r0 + rows, :] = m_new

    def reset():
        for _, m_ref, a_ref in maps:
            m_ref[...] = jnp.full(m_ref.shape, NEG, F32)
            a_ref[...] = jnp.zeros(a_ref.shape, F32)

    lv = lam_ref[...]
    lam = (jnp.exp(jnp.sum(lv[0:1, :] * lv[1:2, :], axis=-1, keepdims=True))
           - jnp.exp(jnp.sum(lv[2:3, :] * lv[3:4, :], axis=-1, keepdims=True)) + lam_init)

    def finish(q0):
        a1 = a1_ref[...]
        a2 = a2_ref[...]
        o = a1[:, 0:V_DIM] / a1[:, V_DIM:V_DIM + 1] - lam * (a2[:, 0:V_DIM] / a2[:, V_DIM:V_DIM + 1])
        o_ref[0, pl.ds(q0, tq), :] = (_rms(o) * gs_ref[...] * (1.0 - lam_init)).astype(BF16)

    if not causal:
        reset()
        tile(0, 0, True)
        finish(0)
        return

    def q_tile(qi, carry):
        q0 = pl.multiple_of(qi * tq, tq)
        reset()

        def pair(i, c):
            tile(q0, pl.multiple_of(2 * i * tk, tk), False)
            tile(q0, pl.multiple_of((2 * i + 1) * tk, tk), False)
            return c

        lax.fori_loop(0, qi // 2, pair, 0)
        odd = qi % 2 == 1

        @pl.when(odd)
        def _():
            tile(q0, pl.multiple_of((qi - 1) * tk, tk), False)
            tile(q0, pl.multiple_of(qi * tk, tk), True)

        @pl.when(jnp.logical_not(odd))
        def _():
            tile(q0, pl.multiple_of(qi * tk, tk), True)

        finish(q0)
        return carry

    lax.fori_loop(0, nq, q_tile, 0)


def _mask_table(slopes, tq, tk, q_offset):
    qpos = q_offset + jnp.arange(tq, dtype=I32)[:, None]
    kpos = jnp.arange(tk, dtype=I32)[None, :]
    visible = jnp.right_shift(kpos, CHUNK_SHIFT) <= jnp.right_shift(qpos, CHUNK_SHIFT)
    fix = jnp.where(kpos > qpos, 2 * (qpos - kpos), 0).astype(F32)
    return jnp.where(visible[None], (slopes * LOG2E)[:, None, None] * fix[None], NEG)


def _attention(q, k, v, lam_rows, g_subln, lam_init, tq, tk, q_offset, causal):
    B, Lq, _ = q.shape
    Lk = k.shape[1]
    assert Lq % tq == 0 and Lk % tk == 0 and (not causal or (tq == tk and tq % CHUNK == 0 and q_offset == 0))
    assert causal or Lk == tk
    rows = min(tq, ATTN_ROWS)
    slopes = jnp.asarray([2.0 ** (-8.0 * (h + 1) / N_HEADS) for h in range(N_HEADS)], F32)
    table = _mask_table(slopes, tq, tk, q_offset)
    key_bias = ((slopes * LOG2E)[:, None] * jnp.arange(Lk, dtype=F32)[None, :]).reshape(N_HEADS, 1, Lk)
    key_ext = pltpu.VMEM((Lk, V_DIM), BF16)
    stat = pltpu.VMEM((tq, LANES), F32)
    acc = pltpu.VMEM((tq, 2 * V_DIM), F32)
    head = lambda b, h, sl: (b, 0, h)
    return pl.pallas_call(
        functools.partial(_attn_kernel, tq=tq, tk=tk, rows=rows, q_offset=q_offset, causal=causal,
                          lam_init=lam_init),
        grid_spec=pltpu.PrefetchScalarGridSpec(
            num_scalar_prefetch=1,
            grid=(B, N_HEADS),
            in_specs=[pl.BlockSpec((1, Lq, V_DIM), head),
                      pl.BlockSpec((1, Lk, V_DIM), head),
                      pl.BlockSpec((1, Lk, V_DIM), head),
                      pl.BlockSpec((1, 1, Lk), lambda b, h, sl: (h, 0, 0)),
                      pl.BlockSpec((1, tq, tk), lambda b, h, sl: (h, 0, 0)),
                      pl.BlockSpec((8, LANES), lambda b, h, sl: (0, 0)),
                      pl.BlockSpec((1, V_DIM), lambda b, h, sl: (0, 0))],
            out_specs=pl.BlockSpec((1, Lq, V_DIM), head),
            scratch_shapes=[key_ext, key_ext, pltpu.VMEM((Lk, 2 * V_DIM), BF16), stat, acc, stat, acc]),
        out_shape=jax.ShapeDtypeStruct((B, Lq, ATTN_W), BF16),
        compiler_params=pltpu.CompilerParams(dimension_semantics=("arbitrary",) * 2,
                                             vmem_limit_bytes=VMEM_LIMIT),
        name="attn",
    )(slopes, q, k, v, key_bias, table, lam_rows, g_subln.reshape(1, V_DIM))


def _postmix_kernel(a_ref, c_ref, x_ref, mod_ref, wo_ref, gpm_ref, gpf_ref, wr_ref, br_ref, cin_ref,
                    x1_ref, h2_ref, rt_ref, cnt_ref, carry_ref, *, tm):
    first = jnp.logical_and(pl.program_id(0) == 0, pl.program_id(1) == 0)

    @pl.when(first)
    def _():
        carry_ref[...] = cin_ref[...]

    mix = _dot(a_ref[0], wo_ref[0:ATTN_W, :]) + _dot(c_ref[0], wo_ref[ATTN_W:ATTN_W + CONV_CH, :])
    gate_a = mod_ref[0, 2:3, :]
    shift_f = mod_ref[0, 3:4, :]
    scale_f = mod_ref[0, 4:5, :]
    x1 = x_ref[0] + gate_a * (_rms(mix) * gpm_ref[...])
    x1_ref[0] = x1
    h2 = _rms(x1) * gpf_ref[...] * (1.0 + scale_f) + shift_f
    h2_ref[0] = _pack_halves(h2)

    logits = _dot3(h2, wr_ref[...]) + br_ref[...]
    lane = lax.broadcasted_iota(I32, (tm, LANES), 1).astype(F32)
    vals, idxs = [], []
    for _ in range(TOP_K):
        m = jnp.max(logits, axis=-1, keepdims=True)
        idx = jnp.min(jnp.where(logits == m, lane, float(LANES)), axis=-1, keepdims=True)
        vals.append(m)
        idxs.append(idx)
        logits = jnp.where(lane == idx, 2.0 * NEG, logits)
    es = [jnp.exp(v - vals[0]) for v in vals]
    denom = es[0] + es[1] + es[2] + es[3]

    onehot = jnp.zeros((tm, LANES), F32)
    for idx in idxs:
        onehot = jnp.where(lane == idx, 1.0, onehot)
    r_i = lax.broadcasted_iota(I32, (tm, tm), 0)
    c_i = lax.broadcasted_iota(I32, (tm, tm), 1)
    tri = jnp.where(c_i < r_i, 1.0, 0.0).astype(BF16)
    before = _dot(tri, onehot.astype(BF16)) + carry_ref[0:1, :]

    rt = jnp.zeros((tm, LANES), F32)
    for k in range(TOP_K):
        rank = jnp.sum(jnp.where(lane == idxs[k], before, 0.0), axis=-1, keepdims=True)
        rt = jnp.where(lane == k, idxs[k], rt)
        rt = jnp.where(lane == TOP_K + k, es[k] / denom, rt)
        rt = jnp.where(lane == 2 * TOP_K + k, rank, rt)
    rt_ref[0] = rt

    carry_ref[...] = carry_ref[...] + jnp.sum(onehot, axis=0, keepdims=True)
    cnt_ref[...] = carry_ref[...]


def _postmix(attn, conv, x, mod, w_out_bf, g_post_mix, g_pre_ffn, w_router_pad, b_router_pad, counts_in, tm):
    B, L, _ = x.shape
    row = lambda b, s: (b, s, 0)
    const2 = lambda b, s: (0, 0)
    return pl.pallas_call(
        functools.partial(_postmix_kernel, tm=tm),
        grid=(B, L // tm),
        in_specs=[pl.BlockSpec((1, tm, ATTN_W), row),
                  pl.BlockSpec((1, tm, CONV_CH), row),
                  pl.BlockSpec((1, tm, D_MODEL), row),
                  pl.BlockSpec((1, N_MOD, D_MODEL), lambda b, s: (b, 0, 0)),
                  pl.BlockSpec((D_MODEL, D_MODEL), const2),
                  pl.BlockSpec((1, D_MODEL), const2),
                  pl.BlockSpec((1, D_MODEL), const2),
                  pl.BlockSpec((D_MODEL, LANES), const2),
                  pl.BlockSpec((1, LANES), const2),
                  pl.BlockSpec((8, LANES), const2)],
        out_specs=[pl.BlockSpec((1, tm, D_MODEL), row),
                   pl.BlockSpec((1, tm, HALF), row),
                   pl.BlockSpec((1, tm, LANES), row),
                   pl.BlockSpec((8, LANES), const2)],
        out_shape=[jax.ShapeDtypeStruct((B, L, D_MODEL), F32),
                   jax.ShapeDtypeStruct((B, L, HALF), I32),
                   jax.ShapeDtypeStruct((B, L, LANES), F32),
                   jax.ShapeDtypeStruct((8, LANES), F32)],
        scratch_shapes=[pltpu.VMEM((8, LANES), F32)],
        compiler_params=pltpu.CompilerParams(dimension_semantics=("arbitrary", "arbitrary"),
                                             vmem_limit_bytes=VMEM_LIMIT),
        name="postmix",
    )(attn, conv, x, mod, w_out_bf, g_post_mix.reshape(1, D_MODEL), g_pre_ffn.reshape(1, D_MODEL),
      w_router_pad, b_router_pad, counts_in)


def _expert_kernel(be_ref, nv_ref, x_ref, wgu_ref, bgu_ref, wd_ref, bd_ref, y_ref, wgu_bf, wd_bf):
    i = pl.program_id(0)
    ff_half = D_FF // 2

    @pl.when(jnp.logical_or(i == 0, be_ref[i] != be_ref[jnp.maximum(i - 1, 0)]))
    def _():
        wgu_bf[...] = wgu_ref[0].astype(BF16)
        wd_bf[...] = wd_ref[0].astype(BF16)

    @pl.when(nv_ref[i] > 0)
    def _():
        x_lo, x_hi = _unpack_halves(x_ref[...])
        x_lo = x_lo.astype(BF16)
        x_hi = x_hi.astype(BF16)
        acc = None
        for c in range(2):
            lo, hi = c * ff_half, (c + 1) * ff_half
            g = (_dot(x_lo, wgu_bf[0:HALF, lo:hi]) + _dot(x_hi, wgu_bf[HALF:D_MODEL, lo:hi])
                 + bgu_ref[0, :, lo:hi])
            lin = (_dot(x_lo, wgu_bf[0:HALF, D_FF + lo:D_FF + hi]) + _dot(x_hi, wgu_bf[HALF:D_MODEL, D_FF + lo:D_FF + hi])
                   + bgu_ref[0, :, D_FF + lo:D_FF + hi])
            g = jnp.minimum(g, SWIGLU_LIMIT)
            lin = jnp.clip(lin, -SWIGLU_LIMIT, SWIGLU_LIMIT)
            act = g * _sigmoid(SWIGLU_ALPHA * g) * (lin + 1.0)
            part = _dot(act.astype(BF16), wd_bf[lo:hi, :])
            acc = part if acc is None else acc + part
        y_ref[...] = _pack_halves(acc + bd_ref[0])


def _experts(x_pad, block_e, n_valid, w_gu, b_gu, w_down, b_down, blk):
    n_blocks = block_e.shape[0]
    ex = lambda i, be, nv: (be[i], 0, 0)
    rows = lambda i, be, nv: (i, 0)
    return pl.pallas_call(
        _expert_kernel,
        grid_spec=pltpu.PrefetchScalarGridSpec(
            num_scalar_prefetch=2,
            grid=(n_blocks,),
            in_specs=[pl.BlockSpec((blk, HALF), rows),
                      pl.BlockSpec((1, D_MODEL, 2 * D_FF), ex), pl.BlockSpec((1, 1, 2 * D_FF), ex),
                      pl.BlockSpec((1, D_FF, D_MODEL), ex), pl.BlockSpec((1, 1, D_MODEL), ex)],
            out_specs=pl.BlockSpec((blk, HALF), rows),
            scratch_shapes=[pltpu.VMEM((D_MODEL, 2 * D_FF), BF16), pltpu.VMEM((D_FF, D_MODEL), BF16)]),
        out_shape=jax.ShapeDtypeStruct((n_blocks * blk, HALF), I32),
        compiler_params=pltpu.CompilerParams(dimension_semantics=("arbitrary",),
                                             vmem_limit_bytes=EXPERT_VMEM_LIMIT),
        name="experts",
    )(block_e, n_valid, x_pad, w_gu, b_gu.reshape(N_EXPERTS, 1, 2 * D_FF), w_down,
      b_down.reshape(N_EXPERTS, 1, D_MODEL))


def _combine_kernel(y0_ref, y1_ref, y2_ref, y3_ref, rt_ref, x1_ref, mod_ref, g_ref, o_ref):
    rt = rt_ref[0]
    f_lo = f_hi = None
    for k, y_ref in enumerate((y0_ref, y1_ref, y2_ref, y3_ref)):
        lo, hi = _unpack_halves(y_ref[...])
        w = rt[:, TOP_K + k:TOP_K + k + 1]
        f_lo = w * lo if f_lo is None else f_lo + w * lo
        f_hi = w * hi if f_hi is None else f_hi + w * hi
    ms = (jnp.sum(f_lo * f_lo, axis=-1, keepdims=True) + jnp.sum(f_hi * f_hi, axis=-1, keepdims=True)) / D_MODEL
    r = lax.rsqrt(ms + EPS)
    gate_f = mod_ref[0, 5:6, :]
    g = g_ref[...]
    o_ref[0, :, 0:HALF] = x1_ref[0, :, 0:HALF] + gate_f[:, 0:HALF] * (f_lo * r * g[:, 0:HALF])
    o_ref[0, :, HALF:D_MODEL] = x1_ref[0, :, HALF:D_MODEL] + gate_f[:, HALF:D_MODEL] * (f_hi * r * g[:, HALF:D_MODEL])


def _combine(y, route, x1, mod, g_post_ffn, tm, t_all, t_first):
    B, L, _ = x1.shape
    assert t_all % tm == 0 and t_first % tm == 0
    row = lambda b, s: (b, s, 0)
    y_specs = [pl.BlockSpec((tm, HALF), lambda b, s, k=k: ((k * t_all + t_first) // tm + b * (L // tm) + s, 0))
               for k in range(TOP_K)]
    return pl.pallas_call(
        _combine_kernel,
        grid=(B, L // tm),
        in_specs=y_specs + [
                  pl.BlockSpec((1, tm, LANES), row),
                  pl.BlockSpec((1, tm, D_MODEL), row),
                  pl.BlockSpec((1, N_MOD, D_MODEL), lambda b, s: (b, 0, 0)),
                  pl.BlockSpec((1, D_MODEL), lambda b, s: (0, 0))],
        out_specs=pl.BlockSpec((1, tm, D_MODEL), row),
        out_shape=jax.ShapeDtypeStruct((B, L, D_MODEL), F32),
        compiler_params=pltpu.CompilerParams(dimension_semantics=("arbitrary", "arbitrary"),
                                             vmem_limit_bytes=VMEM_LIMIT),
        name="combine",
    )(y, y, y, y, route, x1, mod, g_post_ffn.reshape(1, D_MODEL))


SC_WINDOW = 128


def _invert_slots(dest, n_slots):
    n = dest.shape[0]
    sc = plsc.get_sparse_core_info()
    workers = sc.num_cores * sc.num_subcores
    quantum = workers * SC_WINDOW * SUBLANES
    n_pad = -(-n // quantum) * quantum
    extra = n_pad - n
    dest = jnp.concatenate([dest, n_slots + jnp.arange(extra, dtype=I32)])
    vals = jnp.arange(n_pad, dtype=I32)
    wins = n_pad // (workers * SC_WINDOW)
    mesh = plsc.VectorSubcoreMesh(core_axis_name="c", subcore_axis_name="s")

    @functools.partial(
        pl.kernel, mesh=mesh,
        out_type=jax.ShapeDtypeStruct((n_slots + extra,), I32),
        scratch_types=[pltpu.VMEM((wins, SC_WINDOW), I32), pltpu.VMEM((wins, SC_WINDOW), I32),
                       pltpu.SemaphoreType.DMA],
        name="invert_slots",
    )
    def scatter_kernel(dest_hbm, vals_hbm, out_hbm, idx_v, val_v, sem):
        w = lax.axis_index("s") * sc.num_cores + lax.axis_index("c")
        first = pl.multiple_of(w * wins, SUBLANES)
        pltpu.sync_copy(dest_hbm.at[pl.ds(first, wins)], idx_v)
        pltpu.sync_copy(vals_hbm.at[pl.ds(first, wins)], val_v)

        @pl.loop(0, wins)
        def _(i):
            pltpu.async_copy(val_v.at[i], out_hbm.at[idx_v.at[i]], sem)

        @pl.loop(0, wins)
        def _(i):
            pltpu.make_async_copy(val_v.at[i], out_hbm.at[idx_v.at[i]], sem).wait()

    out = scatter_kernel(dest.reshape(n_pad // SC_WINDOW, SC_WINDOW), vals.reshape(n_pad // SC_WINDOW, SC_WINDOW))
    return out[:n_slots]


SC_GATHER_BYTES = 128 * 1024


def _gather_rows(table, idx):
    n = idx.shape[0]
    width = table.shape[1]
    rows = SC_GATHER_BYTES // (width * 4)
    sc = plsc.get_sparse_core_info()
    workers = sc.num_cores * sc.num_subcores
    quantum = workers * rows * SUBLANES
    n_pad = -(-n // quantum) * quantum
    filler = jnp.arange(n_pad - n, dtype=I32) % table.shape[0]
    idx = jnp.concatenate([idx, filler]).reshape(n_pad // rows, rows)
    per_w = n_pad // workers
    wins = per_w // rows
    mesh = plsc.VectorSubcoreMesh(core_axis_name="c", subcore_axis_name="s")
    buf = pltpu.VMEM((rows, width), table.dtype)

    @functools.partial(
        pl.kernel, mesh=mesh,
        out_type=jax.ShapeDtypeStruct((n_pad, width), table.dtype),
        scratch_types=[pltpu.VMEM((wins, rows), I32), buf, buf, pltpu.SemaphoreType.DMA, pltpu.SemaphoreType.DMA],
        name="gather_rows",
    )
    def gather_kernel(table_hbm, idx_hbm, out_hbm, idx_v, buf0, buf1, sem0, sem1):
        w = lax.axis_index("s") * sc.num_cores + lax.axis_index("c")
        pltpu.sync_copy(idx_hbm.at[pl.ds(pl.multiple_of(w * wins, SUBLANES), wins)], idx_v)
        base = pl.multiple_of(w * per_w, rows)

        @pl.loop(0, wins, step=2)
        def _(i):
            g0 = pltpu.async_copy(table_hbm.at[idx_v.at[i]], buf0, sem0)
            g1 = pltpu.async_copy(table_hbm.at[idx_v.at[i + 1]], buf1, sem1)
            g0.wait()
            pltpu.sync_copy(buf0, out_hbm.at[pl.ds(base + i * rows, rows)])
            g1.wait()
            pltpu.sync_copy(buf1, out_hbm.at[pl.ds(base + (i + 1) * rows, rows)])

    return gather_kernel(table, idx)


def _routing_tables(route, counts, blk):
    T = route.shape[0]
    TK = T * TOP_K
    idx = route[:, 0:TOP_K].astype(I32)
    rank = route[:, 2 * TOP_K:3 * TOP_K].astype(I32)
    padded = (counts + blk - 1) // blk * blk
    pad_end = jnp.cumsum(padded)
    pad_start = pad_end - padded
    dest = pad_start[idx] + rank
    n_blocks = -(-TK // blk) + N_EXPERTS
    blk_start = jnp.arange(n_blocks, dtype=I32) * blk
    block_e = jnp.sum((blk_start[:, None] >= pad_end[None, :]).astype(I32), axis=1)
    block_e = jnp.minimum(block_e, N_EXPERTS - 1)
    last = (pad_start + counts)[block_e]
    n_valid = jnp.clip(last - blk_start, 0, blk)
    slot_a = _invert_slots(dest.reshape(TK), n_blocks * blk)
    slot = jnp.arange(n_blocks * blk, dtype=I32)
    slot_tok = jnp.where(slot < jnp.repeat(last, blk), slot_a // TOP_K, slot % T)
    return block_e, n_valid, slot_tok, dest


def _mix(x, mod, buf, past_k, past_v, p, lam_init, counts_in, tm, tq):
    B, L, _ = x.shape
    q, kb, vb, kf, vf, conv, state = _premix(x, mod, p["g_pre_mix"], p["w_in"], buf, p["w_dw"], p["b_dw"],
                                             p["gmat"], p["g_cn"], p["b_cn"], tm)
    if past_k is None:
        attn = _attention(q, kb, vb, p["lam_rows"], p["g_subln"], lam_init, tq, tq, 0, True)
    else:
        P = past_k.shape[1]
        keys = jnp.concatenate([past_k.reshape(B, P, QK_W).astype(BF16), kb], axis=1)
        vals = jnp.concatenate([past_v.reshape(B, P, ATTN_W).astype(BF16), vb], axis=1)
        attn = _attention(q, keys, vals, p["lam_rows"], p["g_subln"], lam_init, L, P + L, P, False)
    x1, h2, route, cnt = _postmix(attn, conv, x, mod, p["w_out"], p["g_post_mix"], p["g_pre_ffn"],
                                  p["w_router"], p["b_router"], counts_in, tm)
    k_new = kf.reshape(B, L, N_HEADS, 2 * HEAD_DIM)
    v_new = vf.reshape(B, L, N_HEADS, V_DIM)
    return x1, h2, route, cnt, k_new, v_new, state[:, CONV_HALO - (CONV_W - 1):, :]


def _moe_rows(h2_groups, route_groups, counts, p, blk):
    h2 = jnp.concatenate([h.reshape(-1, HALF) for h in h2_groups], axis=0)
    route = jnp.concatenate([r.reshape(-1, LANES) for r in route_groups], axis=0)
    T = h2.shape[0]
    block_e, n_valid, slot_tok, dest = _routing_tables(route, counts, blk)
    x_pad = _gather_rows(h2, slot_tok)
    y_pad = _experts(x_pad, block_e, n_valid, p["w_gu"], p["b_gu"], p["w_down"], p["b_down"], blk)
    return _gather_rows(y_pad, dest.T.reshape(T * TOP_K))


def _prepare_params(l, w_ada, b_ada, g_pre_mix, g_post_mix, w_in, lambda_q1, lambda_k1, lambda_q2, lambda_k2,
                    g_subln, w_dw, b_dw, g_cnorm, b_cnorm, w_out, g_pre_ffn, g_post_ffn,
                    w_router, b_router, w_gu, b_gu, w_down, b_down):
    lam_rows = jnp.zeros((8, LANES), F32)
    for r, vec in enumerate((lambda_q1[l], lambda_k1[l], lambda_q2[l], lambda_k2[l])):
        lam_rows = lam_rows.at[r, :HEAD_DIM].set(vec)
    ch = jnp.arange(CONV_CH, dtype=I32) // GROUP_CH
    gmat = (ch[:, None] == ch[None, :]).astype(BF16)
    w_dw_pad = jnp.zeros((CONV_HALO, CONV_CH), F32).at[:CONV_W].set(w_dw[l])
    w_router_pad = jnp.zeros((D_MODEL, LANES), F32).at[:, :N_EXPERTS].set(w_router[l])
    b_router_pad = jnp.full((1, LANES), NEG, F32).at[0, :N_EXPERTS].set(b_router[l])
    return dict(w_ada=w_ada[l], b_ada=b_ada[l], g_pre_mix=g_pre_mix[l], g_post_mix=g_post_mix[l],
                w_in=w_in[l].astype(BF16), lam_rows=lam_rows, g_subln=g_subln[l], w_dw=w_dw_pad, b_dw=b_dw[l],
                gmat=gmat, g_cn=g_cnorm[l], b_cn=b_cnorm[l], w_out=w_out[l].astype(BF16),
                g_pre_ffn=g_pre_ffn[l], g_post_ffn=g_post_ffn[l], w_router=w_router_pad, b_router=b_router_pad,
                w_gu=w_gu[l], b_gu=b_gu[l], w_down=w_down[l], b_down=b_down[l])


def kernel(x_prompt, x_sample, cache_k, cache_v, state_conv, c_prompt, c_sample, w_ada, b_ada, g_pre_mix, g_post_mix, w_in, lambda_q1, lambda_k1, lambda_q2, lambda_k2, g_subln, w_dw, b_dw, g_cnorm, b_cnorm, w_out, g_pre_ffn, g_post_ffn, w_router, b_router, w_gu, b_gu, w_down, b_down):
    depth = w_ada.shape[0]
    Bp, Lp, _ = x_prompt.shape
    Bs, Ls, _ = x_sample.shape
    yp, ys = x_prompt, x_sample
    outs = [[] for _ in range(6)]
    for l in range(depth):
        p = _prepare_params(l, w_ada, b_ada, g_pre_mix, g_post_mix, w_in, lambda_q1, lambda_k1, lambda_q2,
                            lambda_k2, g_subln, w_dw, b_dw, g_cnorm, b_cnorm, w_out, g_pre_ffn, g_post_ffn,
                            w_router, b_router, w_gu, b_gu, w_down, b_down)
        lam_init = 0.8 - 0.6 * math.exp(-0.3 * l)
        mod = _ada(jnp.concatenate([c_prompt, c_sample], axis=0), p["w_ada"], p["b_ada"])
        mod = mod.reshape(Bp + Bs, N_MOD, D_MODEL)
        buf_p = jnp.zeros((Bp, CONV_HALO, CONV_CH), F32)
        buf_s = jnp.pad(state_conv[l], ((0, 0), (CONV_HALO - (CONV_W - 1), 0), (0, 0)))
        tm_p = min(Lp, 512)
        zero_counts = jnp.zeros((SUBLANES, LANES), F32)
        x1p, h2p, rtp, cnt_p, kp, vp, cp = _mix(yp, mod[:Bp], buf_p, None, None, p, lam_init, zero_counts,
                                                tm=tm_p, tq=tm_p)
        x1s, h2s, rts, cnt, ks, vs, cs = _mix(ys, mod[Bp:], buf_s, cache_k[l], cache_v[l], p, lam_init, cnt_p,
                                              tm=Ls, tq=Ls)
        y = _moe_rows((h2p, h2s), (rtp, rts), cnt[0, :N_EXPERTS].astype(I32), p, EXPERT_ROWS)
        t_p, t_all = Bp * Lp, Bp * Lp + Bs * Ls
        yp = _combine(y, rtp, x1p, mod[:Bp], p["g_post_ffn"], min(tm_p, 256), t_all, 0)
        ys = _combine(y, rts, x1s, mod[Bp:], p["g_post_ffn"], Ls, t_all, t_p)
        for lst, val in zip(outs, (kp, vp, cp, ks, vs, cs)):
            lst.append(val)
    return (yp, ys) + tuple(jnp.stack(o) for o in outs)
```

```python
import functools
import math

import jax
import jax.numpy as jnp
from jax import lax
from jax.experimental import pallas as pl
from jax.experimental.pallas import tpu as pltpu
from jax.experimental.pallas import tpu_sc as plsc

F32 = jnp.float32
BF16 = jnp.bfloat16
I32 = jnp.int32

D_MODEL = 1024
CHUNK = 64
CHUNK_SHIFT = 6
N_HEADS = 4
V_DIM = 128
HEAD_DIM = 64
QK_W = 512
ATTN_W = 512
CONV_CH = 512
CONV_W = 31
CONV_GROUPS = 8
GROUP_CH = CONV_CH // CONV_GROUPS
IN_W = 2 * QK_W + ATTN_W + 2 * CONV_CH
N_MOD = 6
N_EXPERTS = 32
TOP_K = 4
D_FF = 1024
SWIGLU_ALPHA = 1.702
SWIGLU_LIMIT = 7.0
EPS = 1e-6

LANES = 128
SUBLANES = 8
CONV_HALO = 32
CONV_ROWS = 32
NEG = -1e30
LOG2E = math.log2(math.e)
ATTN_ROWS = 128
EXPERT_ROWS = 512
VMEM_LIMIT = 48 * 1024 * 1024
EXPERT_VMEM_LIMIT = 56 * 1024 * 1024


def _sigmoid(x):
    return 1.0 / (1.0 + jnp.exp(-x))


def _split_bf16(x):
    hi = x.astype(BF16)
    lo = (x - hi.astype(F32)).astype(BF16)
    return hi, lo


def _dot(a, b):
    return jnp.dot(a, b, preferred_element_type=F32)


def _dot3(a, b):
    ah, al = _split_bf16(a)
    bh, bl = _split_bf16(b)
    return _dot(ah, bh) + _dot(ah, bl) + _dot(al, bh)


def _rms(x):
    return x * lax.rsqrt(jnp.mean(x * x, axis=-1, keepdims=True) + EPS)


HALF = D_MODEL // 2
HIGH16 = -65536


def _pack_halves(x):
    lo = lax.bitcast_convert_type(x[:, :HALF].astype(BF16).astype(F32), I32)
    hi = lax.bitcast_convert_type(x[:, HALF:].astype(BF16).astype(F32), I32)
    return jnp.bitwise_or(jnp.bitwise_and(jnp.right_shift(lo, 16), 0xFFFF), jnp.bitwise_and(hi, HIGH16))


def _unpack_halves(w):
    lo = lax.bitcast_convert_type(jnp.left_shift(w, 16), F32)
    hi = lax.bitcast_convert_type(jnp.bitwise_and(w, HIGH16), F32)
    return lo, hi


def _ada_kernel(c_ref, w_ref, b_ref, o_ref):
    c = c_ref[...]
    o_ref[...] = _dot3(c * _sigmoid(c), w_ref[...]) + b_ref[...]


def _ada(c, w_ada, b_ada):
    n = c.shape[0]
    return pl.pallas_call(
        _ada_kernel,
        grid=(N_MOD,),
        in_specs=[pl.BlockSpec((n, D_MODEL), lambda j: (0, 0)),
                  pl.BlockSpec((D_MODEL, D_MODEL), lambda j: (0, j)),
                  pl.BlockSpec((1, D_MODEL), lambda j: (0, j))],
        out_specs=pl.BlockSpec((n, D_MODEL), lambda j: (0, j)),
        out_shape=jax.ShapeDtypeStruct((n, N_MOD * D_MODEL), F32),
        compiler_params=pltpu.CompilerParams(dimension_semantics=("arbitrary",), vmem_limit_bytes=VMEM_LIMIT),
        name="ada",
    )(c, w_ada, b_ada.reshape(1, N_MOD * D_MODEL))


def _premix_kernel(x_ref, mod_ref, g_ref, w_ref, buf_ref, wdw_ref, bdw_ref, gmat_ref, gcn_ref, bcn_ref,
                   q_ref, kb_ref, vb_ref, kf_ref, vf_ref, co_ref, st_ref, ext_ref, sh_ref, y_ref, *, tm):
    @pl.when(pl.program_id(1) == 0)
    def _():
        ext_ref[0:CONV_HALO, :] = buf_ref[0]

    x = x_ref[0]
    shift = mod_ref[0, 0:1, :]
    scale = mod_ref[0, 1:2, :]
    h = _rms(x) * g_ref[...] * (1.0 + scale) + shift
    hb = h.astype(BF16)

    u0 = 2 * QK_W + ATTN_W
    val = _dot(hb, w_ref[:, u0:u0 + CONV_CH])
    gate = _dot(hb, w_ref[:, u0 + CONV_CH:u0 + 2 * CONV_CH])
    ext_ref[CONV_HALO:CONV_HALO + tm, :] = val * _sigmoid(gate)

    off = CONV_HALO - (CONV_W - 1)
    span = tm + CONV_HALO - SUBLANES
    for b in range(1, SUBLANES):
        sh_ref[b - 1, 0:span, :] = ext_ref[b:b + span, :]
    for c in range(tm // CONV_ROWS):
        r0 = c * CONV_ROWS
        acc = jnp.zeros((CONV_ROWS, CONV_CH), F32)
        for j in range(CONV_W):
            b = (j + off) % SUBLANES
            a = r0 + j + off - b
            rows_j = ext_ref[a:a + CONV_ROWS, :] if b == 0 else sh_ref[b - 1, a:a + CONV_ROWS, :]
            acc = acc + wdw_ref[j:j + 1, :] * rows_j
        y_ref[r0:r0 + CONV_ROWS, :] = acc + bdw_ref[...]

    zq = _dot(hb, w_ref[:, 0:QK_W])
    q_ref[0] = (zq * (HEAD_DIM ** -0.5 * LOG2E)).astype(BF16)
    zk = _dot(hb, w_ref[:, QK_W:2 * QK_W])
    kf_ref[0] = zk
    kb_ref[0] = zk.astype(BF16)
    zv = _dot(hb, w_ref[:, 2 * QK_W:2 * QK_W + ATTN_W])
    vf_ref[0] = zv
    vb_ref[0] = zv.astype(BF16)

    y = y_ref[...]
    gm = gmat_ref[...]
    yh, yl = _split_bf16(y)
    mu = (_dot(yh, gm) + _dot(yl, gm)) * (1.0 / GROUP_CH)
    d = y - mu
    dh, dl = _split_bf16(d * d)
    var = (_dot(dh, gm) + _dot(dl, gm)) * (1.0 / GROUP_CH)
    yn = d * lax.rsqrt(var + EPS) * gcn_ref[...] + bcn_ref[...]
    co_ref[0] = (yn * _sigmoid(yn)).astype(BF16)

    tail = ext_ref[tm:tm + CONV_HALO, :]
    st_ref[0] = tail
    ext_ref[0:CONV_HALO, :] = tail


def _premix(x, mod, g_pre_mix, w_in_bf, buf, w_dw, b_dw, gmat, g_cn, b_cn, tm):
    B, L, _ = x.shape
    assert L % tm == 0 and tm % CONV_ROWS == 0
    row = lambda b, s: (b, s, 0)
    const2 = lambda b, s: (0, 0)
    bf_tile = jax.ShapeDtypeStruct((B, L, QK_W), BF16)
    f_tile = jax.ShapeDtypeStruct((B, L, QK_W), F32)
    return pl.pallas_call(
        functools.partial(_premix_kernel, tm=tm),
        grid=(B, L // tm),
        in_specs=[pl.BlockSpec((1, tm, D_MODEL), row),
                  pl.BlockSpec((1, N_MOD, D_MODEL), lambda b, s: (b, 0, 0)),
                  pl.BlockSpec((1, D_MODEL), const2),
                  pl.BlockSpec((D_MODEL, IN_W), const2),
                  pl.BlockSpec((1, CONV_HALO, CONV_CH), lambda b, s: (b, 0, 0)),
                  pl.BlockSpec((CONV_HALO, CONV_CH), const2),
                  pl.BlockSpec((1, CONV_CH), const2),
                  pl.BlockSpec((CONV_CH, CONV_CH), const2),
                  pl.BlockSpec((1, CONV_CH), const2),
                  pl.BlockSpec((1, CONV_CH), const2)],
        out_specs=[pl.BlockSpec((1, tm, QK_W), row)] * 6
                  + [pl.BlockSpec((1, CONV_HALO, CONV_CH), lambda b, s: (b, 0, 0))],
        out_shape=[bf_tile, bf_tile, bf_tile, f_tile, f_tile, bf_tile,
                   jax.ShapeDtypeStruct((B, CONV_HALO, CONV_CH), F32)],
        scratch_shapes=[pltpu.VMEM((CONV_HALO + tm, CONV_CH), F32),
                        pltpu.VMEM((SUBLANES - 1, CONV_HALO + tm, CONV_CH), F32),
                        pltpu.VMEM((tm, CONV_CH), F32)],
        compiler_params=pltpu.CompilerParams(dimension_semantics=("arbitrary", "arbitrary"),
                                             vmem_limit_bytes=VMEM_LIMIT),
        name="premix",
    )(x, mod, g_pre_mix.reshape(1, D_MODEL), w_in_bf, buf, w_dw, b_dw.reshape(1, CONV_CH), gmat,
      g_cn.reshape(1, CONV_CH), b_cn.reshape(1, CONV_CH))


def _attn_kernel(slopes_ref, q_ref, k_ref, v_ref, kb_ref, mt_ref, lam_ref, gs_ref, o_ref,
                 ke1_ref, ke2_ref, ve_ref, m1_ref, a1_ref, m2_ref, a2_ref,
                 *, tq, tk, rows, q_offset, causal, lam_init):
    slope = slopes_ref[pl.program_id(1)] * LOG2E
    lq = q_ref.shape[1]
    lk = k_ref.shape[1]
    nq = lq // tq
    lane = lax.broadcasted_iota(I32, (1, V_DIM), 1)

    low = jnp.broadcast_to(lane < HEAD_DIM, (lk, V_DIM))
    k = k_ref[0]
    zero = jnp.zeros((lk, V_DIM), BF16)
    ke1_ref[...] = jnp.where(low, k, zero)
    ke2_ref[...] = jnp.where(low, zero, k)
    ve_ref[:, 0:V_DIM] = v_ref[0]
    ve_ref[:, V_DIM:2 * V_DIM] = jnp.broadcast_to(jnp.where(lane == 0, 1.0, 0.0), (lk, V_DIM)).astype(BF16)

    nt = (((1,), (1,)), ((), ()))
    maps = ((ke1_ref, m1_ref, a1_ref), (ke2_ref, m2_ref, a2_ref))

    def lanes(x, n):
        if n % LANES == 0:
            return jnp.concatenate([x] * (n // LANES), axis=1)
        return jnp.broadcast_to(x[:, 0:1], (rows, n))

    def tile(q0, k0, masked):
        ve = ve_ref[pl.ds(k0, tk), :]
        q_first = (lax.broadcasted_iota(I32, (1, 1), 0) + (q0 + q_offset)).astype(F32)
        bias = kb_ref[0, :, pl.ds(k0, tk)] - slope * q_first
        for ke_ref, m_ref, a_ref in maps:
            ke = ke_ref[pl.ds(k0, tk), :]
            for r0 in range(0, tq, rows):
                s = lax.dot_general(q_ref[0, pl.ds(q0 + r0, rows), :], ke, nt, preferred_element_type=F32)
                s = s + bias
                if masked:
                    s = s + mt_ref[0, r0:r0 + rows, :]
                m_old = m_ref[r0:r0 + rows, :]
                m_new = jnp.maximum(m_old, jnp.max(s, axis=-1, keepdims=True))
                p = jnp.exp2(s - lanes(m_new, tk))
                alpha = jnp.exp2(m_old - m_new)
                a_ref[r0:r0 + rows, :] = (lanes(alpha, 2 * V_DIM) * a_ref[r0:r0 + rows, :]
                                          + _dot(p.astype(BF16), ve))
                m_ref[r0:r0 + rows, :] = m_new

    def reset():
        for _, m_ref, a_ref in maps:
            m_ref[...] = jnp.full(m_ref.shape, NEG, F32)
            a_ref[...] = jnp.zeros(a_ref.shape, F32)

    lv = lam_ref[...]
    lam = (jnp.exp(jnp.sum(lv[0:1, :] * lv[1:2, :], axis=-1, keepdims=True))
           - jnp.exp(jnp.sum(lv[2:3, :] * lv[3:4, :], axis=-1, keepdims=True)) + lam_init)

    def finish(q0):
        a1 = a1_ref[...]
        a2 = a2_ref[...]
        o = a1[:, 0:V_DIM] / a1[:, V_DIM:V_DIM + 1] - lam * (a2[:, 0:V_DIM] / a2[:, V_DIM:V_DIM + 1])
        o_ref[0, pl.ds(q0, tq), :] = (_rms(o) * gs_ref[...] * (1.0 - lam_init)).astype(BF16)

    if not causal:
        reset()
        tile(0, 0, True)
        finish(0)
        return

    def q_tile(qi, carry):
        q0 = pl.multiple_of(qi * tq, tq)
        reset()

        def pair(i, c):
            tile(q0, pl.multiple_of(2 * i * tk, tk), False)
            tile(q0, pl.multiple_of((2 * i + 1) * tk, tk), False)
            return c

        lax.fori_loop(0, qi // 2, pair, 0)
        odd = qi % 2 == 1

        @pl.when(odd)
        def _():
            tile(q0, pl.multiple_of((qi - 1) * tk, tk), False)
            tile(q0, pl.multiple_of(qi * tk, tk), True)

        @pl.when(jnp.logical_not(odd))
        def _():
            tile(q0, pl.multiple_of(qi * tk, tk), True)

        finish(q0)
        return carry

    lax.fori_loop(0, nq, q_tile, 0)


def _mask_table(slopes, tq, tk, q_offset):
    qpos = q_offset + jnp.arange(tq, dtype=I32)[:, None]
    kpos = jnp.arange(tk, dtype=I32)[None, :]
    visible = jnp.right_shift(kpos, CHUNK_SHIFT) <= jnp.right_shift(qpos, CHUNK_SHIFT)
    fix = jnp.where(kpos > qpos, 2 * (qpos - kpos), 0).astype(F32)
    return jnp.where(visible[None], (slopes * LOG2E)[:, None, None] * fix[None], NEG)


def _attention(q, k, v, lam_rows, g_subln, lam_init, tq, tk, q_offset, causal):
    B, Lq, _ = q.shape
    Lk = k.shape[1]
    assert Lq % tq == 0 and Lk % tk == 0 and (not causal or (tq == tk and tq % CHUNK == 0 and q_offset == 0))
    assert causal or Lk == tk
    rows = min(tq, ATTN_ROWS)
    slopes = jnp.asarray([2.0 ** (-8.0 * (h + 1) / N_HEADS) for h in range(N_HEADS)], F32)
    table = _mask_table(slopes, tq, tk, q_offset)
    key_bias = ((slopes * LOG2E)[:, None] * jnp.arange(Lk, dtype=F32)[None, :]).reshape(N_HEADS, 1, Lk)
    key_ext = pltpu.VMEM((Lk, V_DIM), BF16)
    stat = pltpu.VMEM((tq, LANES), F32)
    acc = pltpu.VMEM((tq, 2 * V_DIM), F32)
    head = lambda b, h, sl: (b, 0, h)
    return pl.pallas_call(
        functools.partial(_attn_kernel, tq=tq, tk=tk, rows=rows, q_offset=q_offset, causal=causal,
                          lam_init=lam_init),
        grid_spec=pltpu.PrefetchScalarGridSpec(
            num_scalar_prefetch=1,
            grid=(B, N_HEADS),
            in_specs=[pl.BlockSpec((1, Lq, V_DIM), head),
                      pl.BlockSpec((1, Lk, V_DIM), head),
                      pl.BlockSpec((1, Lk, V_DIM), head),
                      pl.BlockSpec((1, 1, Lk), lambda b, h, sl: (h, 0, 0)),
                      pl.BlockSpec((1, tq, tk), lambda b, h, sl: (h, 0, 0)),
                      pl.BlockSpec((8, LANES), lambda b, h, sl: (0, 0)),
                      pl.BlockSpec((1, V_DIM), lambda b, h, sl: (0, 0))],
            out_specs=pl.BlockSpec((1, Lq, V_DIM), head),
            scratch_shapes=[key_ext, key_ext, pltpu.VMEM((Lk, 2 * V_DIM), BF16), stat, acc, stat, acc]),
        out_shape=jax.ShapeDtypeStruct((B, Lq, ATTN_W), BF16),
        compiler_params=pltpu.CompilerParams(dimension_semantics=("arbitrary",) * 2,
                                             vmem_limit_bytes=VMEM_LIMIT),
        name="attn",
    )(slopes, q, k, v, key_bias, table, lam_rows, g_subln.reshape(1, V_DIM))


def _postmix_kernel(a_ref, c_ref, x_ref, mod_ref, wo_ref, gpm_ref, gpf_ref, wr_ref, br_ref, cin_ref,
                    x1_ref, h2_ref, rt_ref, cnt_ref, carry_ref, *, tm):
    first = jnp.logical_and(pl.program_id(0) == 0, pl.program_id(1) == 0)

    @pl.when(first)
    def _():
        carry_ref[...] = cin_ref[...]

    mix = _dot(a_ref[0], wo_ref[0:ATTN_W, :]) + _dot(c_ref[0], wo_ref[ATTN_W:ATTN_W + CONV_CH, :])
    gate_a = mod_ref[0, 2:3, :]
    shift_f = mod_ref[0, 3:4, :]
    scale_f = mod_ref[0, 4:5, :]
    x1 = x_ref[0] + gate_a * (_rms(mix) * gpm_ref[...])
    x1_ref[0] = x1
    h2 = _rms(x1) * gpf_ref[...] * (1.0 + scale_f) + shift_f
    h2_ref[0] = _pack_halves(h2)

    logits = _dot3(h2, wr_ref[...]) + br_ref[...]
    lane = lax.broadcasted_iota(I32, (tm, LANES), 1).astype(F32)
    vals, idxs = [], []
    for _ in range(TOP_K):
        m = jnp.max(logits, axis=-1, keepdims=True)
        idx = jnp.min(jnp.where(logits == m, lane, float(LANES)), axis=-1, keepdims=True)
        vals.append(m)
        idxs.append(idx)
        logits = jnp.where(lane == idx, 2.0 * NEG, logits)
    es = [jnp.exp(v - vals[0]) for v in vals]
    denom = es[0] + es[1] + es[2] + es[3]

    onehot = jnp.zeros((tm, LANES), F32)
    for idx in idxs:
        onehot = jnp.where(lane == idx, 1.0, onehot)
    r_i = lax.broadcasted_iota(I32, (tm, tm), 0)
    c_i = lax.broadcasted_iota(I32, (tm, tm), 1)
    tri = jnp.where(c_i < r_i, 1.0, 0.0).astype(BF16)
    before = _dot(tri, onehot.astype(BF16)) + carry_ref[0:1, :]

    rt = jnp.zeros((tm, LANES), F32)
    for k in range(TOP_K):
        rank = jnp.sum(jnp.where(lane == idxs[k], before, 0.0), axis=-1, keepdims=True)
        rt = jnp.where(lane == k, idxs[k], rt)
        rt = jnp.where(lane == TOP_K + k, es[k] / denom, rt)
        rt = jnp.where(lane == 2 * TOP_K + k, rank, rt)
    rt_ref[0] = rt

    carry_ref[...] = carry_ref[...] + jnp.sum(onehot, axis=0, keepdims=True)
    cnt_ref[...] = carry_ref[...]


def _postmix(attn, conv, x, mod, w_out_bf, g_post_mix, g_pre_ffn, w_router_pad, b_router_pad, counts_in, tm):
    B, L, _ = x.shape
    row = lambda b, s: (b, s, 0)
    const2 = lambda b, s: (0, 0)
    return pl.pallas_call(
        functools.partial(_postmix_kernel, tm=tm),
        grid=(B, L // tm),
        in_specs=[pl.BlockSpec((1, tm, ATTN_W), row),
                  pl.BlockSpec((1, tm, CONV_CH), row),
                  pl.BlockSpec((1, tm, D_MODEL), row),
                  pl.BlockSpec((1, N_MOD, D_MODEL), lambda b, s: (b, 0, 0)),
                  pl.BlockSpec((D_MODEL, D_MODEL), const2),
                  pl.BlockSpec((1, D_MODEL), const2),
                  pl.BlockSpec((1, D_MODEL), const2),
                  pl.BlockSpec((D_MODEL, LANES), const2),
                  pl.BlockSpec((1, LANES), const2),
                  pl.BlockSpec((8, LANES), const2)],
        out_specs=[pl.BlockSpec((1, tm, D_MODEL), row),
                   pl.BlockSpec((1, tm, HALF), row),
                   pl.BlockSpec((1, tm, LANES), row),
                   pl.BlockSpec((8, LANES), const2)],
        out_shape=[jax.ShapeDtypeStruct((B, L, D_MODEL), F32),
                   jax.ShapeDtypeStruct((B, L, HALF), I32),
                   jax.ShapeDtypeStruct((B, L, LANES), F32),
                   jax.ShapeDtypeStruct((8, LANES), F32)],
        scratch_shapes=[pltpu.VMEM((8, LANES), F32)],
        compiler_params=pltpu.CompilerParams(dimension_semantics=("arbitrary", "arbitrary"),
                                             vmem_limit_bytes=VMEM_LIMIT),
        name="postmix",
    )(attn, conv, x, mod, w_out_bf, g_post_mix.reshape(1, D_MODEL), g_pre_ffn.reshape(1, D_MODEL),
      w_router_pad, b_router_pad, counts_in)


def _expert_kernel(be_ref, nv_ref, x_ref, wgu_ref, bgu_ref, wd_ref, bd_ref, y_ref, wgu_bf, wd_bf):
    i = pl.program_id(0)
    ff_half = D_FF // 2

    @pl.when(jnp.logical_or(i == 0, be_ref[i] != be_ref[jnp.maximum(i - 1, 0)]))
    def _():
        wgu_bf[...] = wgu_ref[0].astype(BF16)
        wd_bf[...] = wd_ref[0].astype(BF16)

    @pl.when(nv_ref[i] > 0)
    def _():
        x_lo, x_hi = _unpack_halves(x_ref[...])
        x_lo = x_lo.astype(BF16)
        x_hi = x_hi.astype(BF16)
        acc = None
        for c in range(2):
            lo, hi = c * ff_half, (c + 1) * ff_half
            g = (_dot(x_lo, wgu_bf[0:HALF, lo:hi]) + _dot(x_hi, wgu_bf[HALF:D_MODEL, lo:hi])
                 + bgu_ref[0, :, lo:hi])
            lin = (_dot(x_lo, wgu_bf[0:HALF, D_FF + lo:D_FF + hi]) + _dot(x_hi, wgu_bf[HALF:D_MODEL, D_FF + lo:D_FF + hi])
                   + bgu_ref[0, :, D_FF + lo:D_FF + hi])
            g = jnp.minimum(g, SWIGLU_LIMIT)
            lin = jnp.clip(lin, -SWIGLU_LIMIT, SWIGLU_LIMIT)
            act = g * _sigmoid(SWIGLU_ALPHA * g) * (lin + 1.0)
            part = _dot(act.astype(BF16), wd_bf[lo:hi, :])
            acc = part if acc is None else acc + part
        y_ref[...] = _pack_halves(acc + bd_ref[0])


def _experts(x_pad, block_e, n_valid, w_gu, b_gu, w_down, b_down, blk):
    n_blocks = block_e.shape[0]
    ex = lambda i, be, nv: (be[i], 0, 0)
    rows = lambda i, be, nv: (i, 0)
    return pl.pallas_call(
        _expert_kernel,
        grid_spec=pltpu.PrefetchScalarGridSpec(
            num_scalar_prefetch=2,
            grid=(n_blocks,),
            in_specs=[pl.BlockSpec((blk, HALF), rows),
                      pl.BlockSpec((1, D_MODEL, 2 * D_FF), ex), pl.BlockSpec((1, 1, 2 * D_FF), ex),
                      pl.BlockSpec((1, D_FF, D_MODEL), ex), pl.BlockSpec((1, 1, D_MODEL), ex)],
            out_specs=pl.BlockSpec((blk, HALF), rows),
            scratch_shapes=[pltpu.VMEM((D_MODEL, 2 * D_FF), BF16), pltpu.VMEM((D_FF, D_MODEL), BF16)]),
        out_shape=jax.ShapeDtypeStruct((n_blocks * blk, HALF), I32),
        compiler_params=pltpu.CompilerParams(dimension_semantics=("arbitrary",),
                                             vmem_limit_bytes=EXPERT_VMEM_LIMIT),
        name="experts",
    )(block_e, n_valid, x_pad, w_gu, b_gu.reshape(N_EXPERTS, 1, 2 * D_FF), w_down,
      b_down.reshape(N_EXPERTS, 1, D_MODEL))


def _combine_kernel(y0_ref, y1_ref, y2_ref, y3_ref, rt_ref, x1_ref, mod_ref, g_ref, o_ref):
    rt = rt_ref[0]
    f_lo = f_hi = None
    for k, y_ref in enumerate((y0_ref, y1_ref, y2_ref, y3_ref)):
        lo, hi = _unpack_halves(y_ref[...])
        w = rt[:, TOP_K + k:TOP_K + k + 1]
        f_lo = w * lo if f_lo is None else f_lo + w * lo
        f_hi = w * hi if f_hi is None else f_hi + w * hi
    ms = (jnp.sum(f_lo * f_lo, axis=-1, keepdims=True) + jnp.sum(f_hi * f_hi, axis=-1, keepdims=True)) / D_MODEL
    r = lax.rsqrt(ms + EPS)
    gate_f = mod_ref[0, 5:6, :]
    g = g_ref[...]
    o_ref[0, :, 0:HALF] = x1_ref[0, :, 0:HALF] + gate_f[:, 0:HALF] * (f_lo * r * g[:, 0:HALF])
    o_ref[0, :, HALF:D_MODEL] = x1_ref[0, :, HALF:D_MODEL] + gate_f[:, HALF:D_MODEL] * (f_hi * r * g[:, HALF:D_MODEL])


def _combine(y, route, x1, mod, g_post_ffn, tm, t_all, t_first):
    B, L, _ = x1.shape
    assert t_all % tm == 0 and t_first % tm == 0
    row = lambda b, s: (b, s, 0)
    y_specs = [pl.BlockSpec((tm, HALF), lambda b, s, k=k: ((k * t_all + t_first) // tm + b * (L // tm) + s, 0))
               for k in range(TOP_K)]
    return pl.pallas_call(
        _combine_kernel,
        grid=(B, L // tm),
        in_specs=y_specs + [
                  pl.BlockSpec((1, tm, LANES), row),
                  pl.BlockSpec((1, tm, D_MODEL), row),
                  pl.BlockSpec((1, N_MOD, D_MODEL), lambda b, s: (b, 0, 0)),
                  pl.BlockSpec((1, D_MODEL), lambda b, s: (0, 0))],
        out_specs=pl.BlockSpec((1, tm, D_MODEL), row),
        out_shape=jax.ShapeDtypeStruct((B, L, D_MODEL), F32),
        compiler_params=pltpu.CompilerParams(dimension_semantics=("arbitrary", "arbitrary"),
                                             vmem_limit_bytes=VMEM_LIMIT),
        name="combine",
    )(y, y, y, y, route, x1, mod, g_post_ffn.reshape(1, D_MODEL))


SC_GATHER_BYTES = 128 * 1024


def _scatter_rows(table, dest, n_slots):
    n, width = table.shape
    rows = SC_GATHER_BYTES // (width * 4)
    sc = plsc.get_sparse_core_info()
    workers = sc.num_cores * sc.num_subcores
    quantum = workers * rows * 2
    n_pad = -(-n // quantum) * quantum
    extra = n_pad - n
    table = jnp.concatenate([table, jnp.zeros((extra, width), table.dtype)], axis=0)
    spare = n_slots + jnp.arange(extra * TOP_K, dtype=I32).reshape(extra, TOP_K)
    per_w = n_pad // workers
    wins = per_w // rows
    wins_tile = -(-wins // SUBLANES) * SUBLANES
    idx = jnp.concatenate([dest, spare], axis=0).T.reshape(TOP_K, workers, wins, rows)
    idx = jnp.pad(idx, ((0, 0), (0, 0), (0, wins_tile - wins), (0, 0))).reshape(TOP_K, workers * wins_tile, rows)
    mesh = plsc.VectorSubcoreMesh(core_axis_name="c", subcore_axis_name="s")
    buf = pltpu.VMEM((rows, width), table.dtype)

    @functools.partial(
        pl.kernel, mesh=mesh,
        out_type=jax.ShapeDtypeStruct((n_slots + extra * TOP_K, width), table.dtype),
        scratch_types=[pltpu.VMEM((TOP_K, wins_tile, rows), I32), buf, buf,
                       pltpu.SemaphoreType.DMA, pltpu.SemaphoreType.DMA],
        name="scatter_rows",
    )
    def scatter_kernel(table_hbm, idx_hbm, out_hbm, idx_v, buf0, buf1, sem0, sem1):
        w = lax.axis_index("s") * sc.num_cores + lax.axis_index("c")
        first = pl.multiple_of(w * wins_tile, SUBLANES)
        for k in range(TOP_K):
            pltpu.sync_copy(idx_hbm.at[k, pl.ds(first, wins_tile)], idx_v.at[k])
        base = pl.multiple_of(w * per_w, rows)

        def window(i, buf, sem):
            pltpu.sync_copy(table_hbm.at[pl.ds(base + i * rows, rows)], buf)
            copies = [pltpu.async_copy(buf, out_hbm.at[idx_v.at[k, i]], sem) for k in range(TOP_K)]
            return copies

        @pl.loop(0, wins, step=2)
        def _(i):
            c0 = window(i, buf0, sem0)
            c1 = window(i + 1, buf1, sem1)
            for c in c0 + c1:
                c.wait()

    return scatter_kernel(table, idx)


def _gather_rows(table, idx):
    n = idx.shape[0]
    width = table.shape[1]
    rows = SC_GATHER_BYTES // (width * 4)
    sc = plsc.get_sparse_core_info()
    workers = sc.num_cores * sc.num_subcores
    quantum = workers * rows * 2
    n_pad = -(-n // quantum) * quantum
    filler = jnp.arange(n_pad - n, dtype=I32) % table.shape[0]
    per_w = n_pad // workers
    wins = per_w // rows
    wins_tile = -(-wins // SUBLANES) * SUBLANES
    idx = jnp.concatenate([idx, filler]).reshape(workers, wins, rows)
    idx = jnp.pad(idx, ((0, 0), (0, wins_tile - wins), (0, 0))).reshape(workers * wins_tile, rows)
    mesh = plsc.VectorSubcoreMesh(core_axis_name="c", subcore_axis_name="s")
    buf = pltpu.VMEM((rows, width), table.dtype)

    @functools.partial(
        pl.kernel, mesh=mesh,
        out_type=jax.ShapeDtypeStruct((n_pad, width), table.dtype),
        scratch_types=[pltpu.VMEM((wins_tile, rows), I32), buf, buf,
                       pltpu.SemaphoreType.DMA, pltpu.SemaphoreType.DMA],
        name="gather_rows",
    )
    def gather_kernel(table_hbm, idx_hbm, out_hbm, idx_v, buf0, buf1, sem0, sem1):
        w = lax.axis_index("s") * sc.num_cores + lax.axis_index("c")
        pltpu.sync_copy(idx_hbm.at[pl.ds(pl.multiple_of(w * wins_tile, SUBLANES), wins_tile)], idx_v)
        base = pl.multiple_of(w * per_w, rows)

        @pl.loop(0, wins, step=2)
        def _(i):
            g0 = pltpu.async_copy(table_hbm.at[idx_v.at[i]], buf0, sem0)
            g1 = pltpu.async_copy(table_hbm.at[idx_v.at[i + 1]], buf1, sem1)
            g0.wait()
            pltpu.sync_copy(buf0, out_hbm.at[pl.ds(base + i * rows, rows)])
            g1.wait()
            pltpu.sync_copy(buf1, out_hbm.at[pl.ds(base + (i + 1) * rows, rows)])

    return gather_kernel(table, idx)


def _routing_tables(route, counts, blk):
    T = route.shape[0]
    TK = T * TOP_K
    idx = route[:, 0:TOP_K].astype(I32)
    rank = route[:, 2 * TOP_K:3 * TOP_K].astype(I32)
    padded = (counts + blk - 1) // blk * blk
    pad_end = jnp.cumsum(padded)
    pad_start = pad_end - padded
    dest = pad_start[idx] + rank
    n_blocks = -(-TK // blk) + N_EXPERTS
    blk_start = jnp.arange(n_blocks, dtype=I32) * blk
    block_e = jnp.sum((blk_start[:, None] >= pad_end[None, :]).astype(I32), axis=1)
    block_e = jnp.minimum(block_e, N_EXPERTS - 1)
    last = (pad_start + counts)[block_e]
    n_valid = jnp.clip(last - blk_start, 0, blk)
    return block_e, n_valid, dest


def _mix(x, mod, buf, past_k, past_v, p, lam_init, counts_in, tm, tq):
    B, L, _ = x.shape
    q, kb, vb, kf, vf, conv, state = _premix(x, mod, p["g_pre_mix"], p["w_in"], buf, p["w_dw"], p["b_dw"],
                                             p["gmat"], p["g_cn"], p["b_cn"], tm)
    if past_k is None:
        attn = _attention(q, kb, vb, p["lam_rows"], p["g_subln"], lam_init, tq, tq, 0, True)
    else:
        P = past_k.shape[1]
        keys = jnp.concatenate([past_k.reshape(B, P, QK_W).astype(BF16), kb], axis=1)
        vals = jnp.concatenate([past_v.reshape(B, P, ATTN_W).astype(BF16), vb], axis=1)
        attn = _attention(q, keys, vals, p["lam_rows"], p["g_subln"], lam_init, L, P + L, P, False)
    x1, h2, route, cnt = _postmix(attn, conv, x, mod, p["w_out"], p["g_post_mix"], p["g_pre_ffn"],
                                  p["w_router"], p["b_router"], counts_in, tm)
    k_new = kf.reshape(B, L, N_HEADS, 2 * HEAD_DIM)
    v_new = vf.reshape(B, L, N_HEADS, V_DIM)
    return x1, h2, route, cnt, k_new, v_new, state[:, CONV_HALO - (CONV_W - 1):, :]


def _moe_rows(h2_groups, route_groups, counts, p, blk):
    h2 = jnp.concatenate([h.reshape(-1, HALF) for h in h2_groups], axis=0)
    route = jnp.concatenate([r.reshape(-1, LANES) for r in route_groups], axis=0)
    T = h2.shape[0]
    block_e, n_valid, dest = _routing_tables(route, counts, blk)
    x_pad = _scatter_rows(h2, dest, block_e.shape[0] * blk)
    y_pad = _experts(x_pad, block_e, n_valid, p["w_gu"], p["b_gu"], p["w_down"], p["b_down"], blk)
    return _gather_rows(y_pad, dest.T.reshape(T * TOP_K))


def _prepare_params(l, w_ada, b_ada, g_pre_mix, g_post_mix, w_in, lambda_q1, lambda_k1, lambda_q2, lambda_k2,
                    g_subln, w_dw, b_dw, g_cnorm, b_cnorm, w_out, g_pre_ffn, g_post_ffn,
                    w_router, b_router, w_gu, b_gu, w_down, b_down):
    lam_rows = jnp.zeros((8, LANES), F32)
    for r, vec in enumerate((lambda_q1[l], lambda_k1[l], lambda_q2[l], lambda_k2[l])):
        lam_rows = lam_rows.at[r, :HEAD_DIM].set(vec)
    ch = jnp.arange(CONV_CH, dtype=I32) // GROUP_CH
    gmat = (ch[:, None] == ch[None, :]).astype(BF16)
    w_dw_pad = jnp.zeros((CONV_HALO, CONV_CH), F32).at[:CONV_W].set(w_dw[l])
    w_router_pad = jnp.zeros((D_MODEL, LANES), F32).at[:, :N_EXPERTS].set(w_router[l])
    b_router_pad = jnp.full((1, LANES), NEG, F32).at[0, :N_EXPERTS].set(b_router[l])
    return dict(w_ada=w_ada[l], b_ada=b_ada[l], g_pre_mix=g_pre_mix[l], g_post_mix=g_post_mix[l],
                w_in=w_in[l].astype(BF16), lam_rows=lam_rows, g_subln=g_subln[l], w_dw=w_dw_pad, b_dw=b_dw[l],
                gmat=gmat, g_cn=g_cnorm[l], b_cn=b_cnorm[l], w_out=w_out[l].astype(BF16),
                g_pre_ffn=g_pre_ffn[l], g_post_ffn=g_post_ffn[l], w_router=w_router_pad, b_router=b_router_pad,
                w_gu=w_gu[l], b_gu=b_gu[l], w_down=w_down[l], b_down=b_down[l])


def kernel(x_prompt, x_sample, cache_k, cache_v, state_conv, c_prompt, c_sample, w_ada, b_ada, g_pre_mix, g_post_mix, w_in, lambda_q1, lambda_k1, lambda_q2, lambda_k2, g_subln, w_dw, b_dw, g_cnorm, b_cnorm, w_out, g_pre_ffn, g_post_ffn, w_router, b_router, w_gu, b_gu, w_down, b_down):
    depth = w_ada.shape[0]
    Bp, Lp, _ = x_prompt.shape
    Bs, Ls, _ = x_sample.shape
    yp, ys = x_prompt, x_sample
    outs = [[] for _ in range(6)]
    for l in range(depth):
        p = _prepare_params(l, w_ada, b_ada, g_pre_mix, g_post_mix, w_in, lambda_q1, lambda_k1, lambda_q2,
                            lambda_k2, g_subln, w_dw, b_dw, g_cnorm, b_cnorm, w_out, g_pre_ffn, g_post_ffn,
                            w_router, b_router, w_gu, b_gu, w_down, b_down)
        lam_init = 0.8 - 0.6 * math.exp(-0.3 * l)
        mod = _ada(jnp.concatenate([c_prompt, c_sample], axis=0), p["w_ada"], p["b_ada"])
        mod = mod.reshape(Bp + Bs, N_MOD, D_MODEL)
        buf_p = jnp.zeros((Bp, CONV_HALO, CONV_CH), F32)
        buf_s = jnp.pad(state_conv[l], ((0, 0), (CONV_HALO - (CONV_W - 1), 0), (0, 0)))
        tm_p = min(Lp, 512)
        zero_counts = jnp.zeros((SUBLANES, LANES), F32)
        x1p, h2p, rtp, cnt_p, kp, vp, cp = _mix(yp, mod[:Bp], buf_p, None, None, p, lam_init, zero_counts,
                                                tm=tm_p, tq=tm_p)
        x1s, h2s, rts, cnt, ks, vs, cs = _mix(ys, mod[Bp:], buf_s, cache_k[l], cache_v[l], p, lam_init, cnt_p,
                                              tm=Ls, tq=Ls)
        y = _moe_rows((h2p, h2s), (rtp, rts), cnt[0, :N_EXPERTS].astype(I32), p, EXPERT_ROWS)
        t_p, t_all = Bp * Lp, Bp * Lp + Bs * Ls
        yp = _combine(y, rtp, x1p, mod[:Bp], p["g_post_ffn"], min(tm_p, 256), t_all, 0)
        ys = _combine(y, rts, x1s, mod[Bp:], p["g_post_ffn"], Ls, t_all, t_p)
        for lst, val in zip(outs, (kp, vp, cp, ks, vs, cs)):
            lst.append(val)
    return (yp, ys) + tuple(jnp.stack(o) for o in outs)
```

```python
import functools
import math

import jax
import jax.numpy as jnp
from jax import lax
from jax.experimental import pallas as pl
from jax.experimental.pallas import tpu as pltpu
from jax.experimental.pallas import tpu_sc as plsc

F32 = jnp.float32
BF16 = jnp.bfloat16
I32 = jnp.int32

D_MODEL = 1024
CHUNK = 64
CHUNK_SHIFT = 6
N_HEADS = 4
V_DIM = 128
HEAD_DIM = 64
QK_W = 512
ATTN_W = 512
CONV_CH = 512
CONV_W = 31
CONV_GROUPS = 8
GROUP_CH = CONV_CH // CONV_GROUPS
IN_W = 2 * QK_W + ATTN_W + 2 * CONV_CH
N_MOD = 6
N_EXPERTS = 32
TOP_K = 4
D_FF = 1024
SWIGLU_ALPHA = 1.702
SWIGLU_LIMIT = 7.0
EPS = 1e-6

LANES = 128
SUBLANES = 8
CONV_HALO = 32
CONV_ROWS = 32
NEG = -1e30
LOG2E = math.log2(math.e)
ATTN_ROWS = 128
EXPERT_ROWS = 512
VMEM_LIMIT = 48 * 1024 * 1024
EXPERT_VMEM_LIMIT = 56 * 1024 * 1024


def _sigmoid(x):
    return 1.0 / (1.0 + jnp.exp(-x))


def _split_bf16(x):
    hi = x.astype(BF16)
    lo = (x - hi.astype(F32)).astype(BF16)
    return hi, lo


def _dot(a, b):
    return jnp.dot(a, b, preferred_element_type=F32)


def _dot3(a, b):
    ah, al = _split_bf16(a)
    bh, bl = _split_bf16(b)
    return _dot(ah, bh) + _dot(ah, bl) + _dot(al, bh)


def _rms(x):
    return x * lax.rsqrt(jnp.mean(x * x, axis=-1, keepdims=True) + EPS)


HALF = D_MODEL // 2
HIGH16 = -65536


def _pack_halves(x):
    lo = lax.bitcast_convert_type(x[:, :HALF].astype(BF16).astype(F32), I32)
    hi = lax.bitcast_convert_type(x[:, HALF:].astype(BF16).astype(F32), I32)
    return jnp.bitwise_or(jnp.bitwise_and(jnp.right_shift(lo, 16), 0xFFFF), jnp.bitwise_and(hi, HIGH16))


def _unpack_halves(w):
    lo = lax.bitcast_convert_type(jnp.left_shift(w, 16), F32)
    hi = lax.bitcast_convert_type(jnp.bitwise_and(w, HIGH16), F32)
    return lo, hi


def _ada_kernel(c_ref, w_ref, b_ref, o_ref):
    c = c_ref[...]
    o_ref[...] = _dot3(c * _sigmoid(c), w_ref[...]) + b_ref[...]


def _ada(c, w_ada, b_ada):
    n = c.shape[0]
    return pl.pallas_call(
        _ada_kernel,
        grid=(N_MOD,),
        in_specs=[pl.BlockSpec((n, D_MODEL), lambda j: (0, 0)),
                  pl.BlockSpec((D_MODEL, D_MODEL), lambda j: (0, j)),
                  pl.BlockSpec((1, D_MODEL), lambda j: (0, j))],
        out_specs=pl.BlockSpec((n, D_MODEL), lambda j: (0, j)),
        out_shape=jax.ShapeDtypeStruct((n, N_MOD * D_MODEL), F32),
        compiler_params=pltpu.CompilerParams(dimension_semantics=("arbitrary",), vmem_limit_bytes=VMEM_LIMIT),
        name="ada",
    )(c, w_ada, b_ada.reshape(1, N_MOD * D_MODEL))


def _premix_kernel(x_ref, mod_ref, g_ref, w_ref, buf_ref, wdw_ref, bdw_ref, gmat_ref, gcn_ref, bcn_ref,
                   q_ref, kb_ref, vb_ref, kf_hbm, vf_hbm, co_ref, st_ref, ext_ref, sh_ref, y_ref, kv_buf, kv_sem,
                   *, tm):
    batch = pl.program_id(0)
    s = pl.program_id(1)
    step = batch * pl.num_programs(1) + s
    last = pl.num_programs(0) * pl.num_programs(1) - 1
    slot = step % 2

    def kv_copies(slot):
        return [pltpu.make_async_copy(kv_buf.at[slot, i, :, pl.ds(h * V_DIM, V_DIM)],
                                      out.at[batch, pl.ds(s * tm, tm), h, :], kv_sem.at[slot, i, h])
                for i, out in enumerate((kf_hbm, vf_hbm)) for h in range(N_HEADS)]

    @pl.when(step >= 2)
    def _():
        for c in kv_copies(slot):
            c.wait()

    @pl.when(s == 0)
    def _():
        ext_ref[0:CONV_HALO, :] = buf_ref[0]

    x = x_ref[0]
    shift = mod_ref[0, 0:1, :]
    scale = mod_ref[0, 1:2, :]
    h = _rms(x) * g_ref[...] * (1.0 + scale) + shift
    hb = h.astype(BF16)

    u0 = 2 * QK_W + ATTN_W
    val = _dot(hb, w_ref[:, u0:u0 + CONV_CH])
    gate = _dot(hb, w_ref[:, u0 + CONV_CH:u0 + 2 * CONV_CH])
    ext_ref[CONV_HALO:CONV_HALO + tm, :] = val * _sigmoid(gate)

    off = CONV_HALO - (CONV_W - 1)
    span = tm + CONV_HALO - SUBLANES
    for b in range(1, SUBLANES):
        sh_ref[b - 1, 0:span, :] = ext_ref[b:b + span, :]
    for c in range(tm // CONV_ROWS):
        r0 = c * CONV_ROWS
        acc = jnp.zeros((CONV_ROWS, CONV_CH), F32)
        for j in range(CONV_W):
            b = (j + off) % SUBLANES
            a = r0 + j + off - b
            rows_j = ext_ref[a:a + CONV_ROWS, :] if b == 0 else sh_ref[b - 1, a:a + CONV_ROWS, :]
            acc = acc + wdw_ref[j:j + 1, :] * rows_j
        y_ref[r0:r0 + CONV_ROWS, :] = acc + bdw_ref[...]

    zq = _dot(hb, w_ref[:, 0:QK_W])
    q_ref[0] = (zq * (HEAD_DIM ** -0.5 * LOG2E)).astype(BF16)
    zk = _dot(hb, w_ref[:, QK_W:2 * QK_W])
    kv_buf[slot, 0] = zk
    kb_ref[0] = zk.astype(BF16)
    zv = _dot(hb, w_ref[:, 2 * QK_W:2 * QK_W + ATTN_W])
    kv_buf[slot, 1] = zv
    vb_ref[0] = zv.astype(BF16)

    y = y_ref[...]
    gm = gmat_ref[...]
    yh, yl = _split_bf16(y)
    mu = (_dot(yh, gm) + _dot(yl, gm)) * (1.0 / GROUP_CH)
    d = y - mu
    dh, dl = _split_bf16(d * d)
    var = (_dot(dh, gm) + _dot(dl, gm)) * (1.0 / GROUP_CH)
    yn = d * lax.rsqrt(var + EPS) * gcn_ref[...] + bcn_ref[...]
    co_ref[0] = (yn * _sigmoid(yn)).astype(BF16)

    tail = ext_ref[tm:tm + CONV_HALO, :]
    st_ref[0] = tail
    ext_ref[0:CONV_HALO, :] = tail

    for c in kv_copies(slot):
        c.start()

    @pl.when(step == last)
    def _():
        for c in kv_copies(slot):
            c.wait()

    @pl.when(jnp.logical_and(step == last, step >= 1))
    def _():
        for c in kv_copies(1 - slot):
            c.wait()


def _premix(x, mod, g_pre_mix, w_in_bf, buf, w_dw, b_dw, gmat, g_cn, b_cn, tm):
    B, L, _ = x.shape
    assert L % tm == 0 and tm % CONV_ROWS == 0
    row = lambda b, s: (b, s, 0)
    const2 = lambda b, s: (0, 0)
    bf_tile = jax.ShapeDtypeStruct((B, L, QK_W), BF16)
    f_heads = jax.ShapeDtypeStruct((B, L, N_HEADS, V_DIM), F32)
    return pl.pallas_call(
        functools.partial(_premix_kernel, tm=tm),
        grid=(B, L // tm),
        in_specs=[pl.BlockSpec((1, tm, D_MODEL), row),
                  pl.BlockSpec((1, N_MOD, D_MODEL), lambda b, s: (b, 0, 0)),
                  pl.BlockSpec((1, D_MODEL), const2),
                  pl.BlockSpec((D_MODEL, IN_W), const2),
                  pl.BlockSpec((1, CONV_HALO, CONV_CH), lambda b, s: (b, 0, 0)),
                  pl.BlockSpec((CONV_HALO, CONV_CH), const2),
                  pl.BlockSpec((1, CONV_CH), const2),
                  pl.BlockSpec((CONV_CH, CONV_CH), const2),
                  pl.BlockSpec((1, CONV_CH), const2),
                  pl.BlockSpec((1, CONV_CH), const2)],
        out_specs=[pl.BlockSpec((1, tm, QK_W), row)] * 3
                  + [pl.BlockSpec(memory_space=pl.ANY)] * 2
                  + [pl.BlockSpec((1, tm, QK_W), row),
                     pl.BlockSpec((1, CONV_HALO, CONV_CH), lambda b, s: (b, 0, 0))],
        out_shape=[bf_tile, bf_tile, bf_tile, f_heads, f_heads, bf_tile,
                   jax.ShapeDtypeStruct((B, CONV_HALO, CONV_CH), F32)],
        scratch_shapes=[pltpu.VMEM((CONV_HALO + tm, CONV_CH), F32),
                        pltpu.VMEM((SUBLANES - 1, CONV_HALO + tm, CONV_CH), F32),
                        pltpu.VMEM((tm, CONV_CH), F32),
                        pltpu.VMEM((2, 2, tm, QK_W), F32),
                        pltpu.SemaphoreType.DMA((2, 2, N_HEADS))],
        compiler_params=pltpu.CompilerParams(dimension_semantics=("arbitrary", "arbitrary"),
                                             vmem_limit_bytes=VMEM_LIMIT),
        name="premix",
    )(x, mod, g_pre_mix.reshape(1, D_MODEL), w_in_bf, buf, w_dw, b_dw.reshape(1, CONV_CH), gmat,
      g_cn.reshape(1, CONV_CH), b_cn.reshape(1, CONV_CH))


def _attn_kernel(slopes_ref, q_ref, k_ref, v_ref, kb_ref, mt_ref, lam_ref, gs_ref, o_ref,
                 ke1_ref, ke2_ref, ve_ref, m1_ref, a1_ref, m2_ref, a2_ref,
                 *, tq, tk, rows, q_offset, causal, lam_init):
    slope = slopes_ref[pl.program_id(1)] * LOG2E
    lq = q_ref.shape[1]
    lk = k_ref.shape[1]
    nq = lq // tq
    lane = lax.broadcasted_iota(I32, (1, V_DIM), 1)

    low = jnp.broadcast_to(lane < HEAD_DIM, (lk, V_DIM))
    k = k_ref[0]
    zero = jnp.zeros((lk, V_DIM), BF16)
    ke1_ref[...] = jnp.where(low, k, zero)
    ke2_ref[...] = jnp.where(low, zero, k)
    ve_ref[:, 0:V_DIM] = v_ref[0]
    ve_ref[:, V_DIM:2 * V_DIM] = jnp.broadcast_to(jnp.where(lane == 0, 1.0, 0.0), (lk, V_DIM)).astype(BF16)

    nt = (((1,), (1,)), ((), ()))
    maps = ((ke1_ref, m1_ref, a1_ref), (ke2_ref, m2_ref, a2_ref))

    def lanes(x, n):
        if n % LANES == 0:
            return jnp.concatenate([x] * (n // LANES), axis=1)
        return jnp.broadcast_to(x[:, 0:1], (rows, n))

    def tile(q0, k0, masked):
        ve = ve_ref[pl.ds(k0, tk), :]
        q_first = (lax.broadcasted_iota(I32, (1, 1), 0) + (q0 + q_offset)).astype(F32)
        bias = kb_ref[0, :, pl.ds(k0, tk)] - slope * q_first
        for ke_ref, m_ref, a_ref in maps:
            ke = ke_ref[pl.ds(k0, tk), :]
            for r0 in range(0, tq, rows):
                s = lax.dot_general(q_ref[0, pl.ds(q0 + r0, rows), :], ke, nt, preferred_element_type=F32)
                s = s + bias
                if masked:
                    s = s + mt_ref[0, r0:r0 + rows, :]
                m_old = m_ref[r0:r0 + rows, :]
                m_new = jnp.maximum(m_old, jnp.max(s, axis=-1, keepdims=True))
                p = jnp.exp2(s - lanes(m_new, tk))
                alpha = jnp.exp2(m_old - m_new)
                a_ref[r0:r0 + rows, :] = (lanes(alpha, 2 * V_DIM) * a_ref[r0:r0 + rows, :]
                                          + _dot(p.astype(BF16), ve))
                m_ref[r0:r0 + rows, :] = m_new

    def reset():
        for _, m_ref, a_ref in maps:
            m_ref[...] = jnp.full(m_ref.shape, NEG, F32)
            a_ref[...] = jnp.zeros(a_ref.shape, F32)

    lv = lam_ref[...]
    lam = (jnp.exp(jnp.sum(lv[0:1, :] * lv[1:2, :], axis=-1, keepdims=True))
           - jnp.exp(jnp.sum(lv[2:3, :] * lv[3:4, :], axis=-1, keepdims=True)) + lam_init)

    def finish(q0):
        a1 = a1_ref[...]
        a2 = a2_ref[...]
        o = a1[:, 0:V_DIM] / a1[:, V_DIM:V_DIM + 1] - lam * (a2[:, 0:V_DIM] / a2[:, V_DIM:V_DIM + 1])
        o_ref[0, pl.ds(q0, tq), :] = (_rms(o) * gs_ref[...] * (1.0 - lam_init)).astype(BF16)

    if not causal:
        reset()
        tile(0, 0, True)
        finish(0)
        return

    def q_tile(qi, carry):
        q0 = pl.multiple_of(qi * tq, tq)
        reset()

        def pair(i, c):
            tile(q0, pl.multiple_of(2 * i * tk, tk), False)
            tile(q0, pl.multiple_of((2 * i + 1) * tk, tk), False)
            return c

        lax.fori_loop(0, qi // 2, pair, 0)
        odd = qi % 2 == 1

        @pl.when(odd)
        def _():
            tile(q0, pl.multiple_of((qi - 1) * tk, tk), False)
            tile(q0, pl.multiple_of(qi * tk, tk), True)

        @pl.when(jnp.logical_not(odd))
        def _():
            tile(q0, pl.multiple_of(qi * tk, tk), True)

        finish(q0)
        return carry

    lax.fori_loop(0, nq, q_tile, 0)


def _mask_table(slopes, tq, tk, q_offset):
    qpos = q_offset + jnp.arange(tq, dtype=I32)[:, None]
    kpos = jnp.arange(tk, dtype=I32)[None, :]
    visible = jnp.right_shift(kpos, CHUNK_SHIFT) <= jnp.right_shift(qpos, CHUNK_SHIFT)
    fix = jnp.where(kpos > qpos, 2 * (qpos - kpos), 0).astype(F32)
    return jnp.where(visible[None], (slopes * LOG2E)[:, None, None] * fix[None], NEG)


def _attention(q, k, v, lam_rows, g_subln, lam_init, tq, tk, q_offset, causal):
    B, Lq, _ = q.shape
    Lk = k.shape[1]
    assert Lq % tq == 0 and Lk % tk == 0 and (not causal or (tq == tk and tq % CHUNK == 0 and q_offset == 0))
    assert causal or Lk == tk
    rows = min(tq, ATTN_ROWS)
    slopes = jnp.asarray([2.0 ** (-8.0 * (h + 1) / N_HEADS) for h in range(N_HEADS)], F32)
    table = _mask_table(slopes, tq, tk, q_offset)
    key_bias = ((slopes * LOG2E)[:, None] * jnp.arange(Lk, dtype=F32)[None, :]).reshape(N_HEADS, 1, Lk)
    key_ext = pltpu.VMEM((Lk, V_DIM), BF16)
    stat = pltpu.VMEM((tq, LANES), F32)
    acc = pltpu.VMEM((tq, 2 * V_DIM), F32)
    head = lambda b, h, sl: (b, 0, h)
    return pl.pallas_call(
        functools.partial(_attn_kernel, tq=tq, tk=tk, rows=rows, q_offset=q_offset, causal=causal,
                          lam_init=lam_init),
        grid_spec=pltpu.PrefetchScalarGridSpec(
            num_scalar_prefetch=1,
            grid=(B, N_HEADS),
            in_specs=[pl.BlockSpec((1, Lq, V_DIM), head),
                      pl.BlockSpec((1, Lk, V_DIM), head),
                      pl.BlockSpec((1, Lk, V_DIM), head),
                      pl.BlockSpec((1, 1, Lk), lambda b, h, sl: (h, 0, 0)),
                      pl.BlockSpec((1, tq, tk), lambda b, h, sl: (h, 0, 0)),
                      pl.BlockSpec((8, LANES), lambda b, h, sl: (0, 0)),
                      pl.BlockSpec((1, V_DIM), lambda b, h, sl: (0, 0))],
            out_specs=pl.BlockSpec((1, Lq, V_DIM), head),
            scratch_shapes=[key_ext, key_ext, pltpu.VMEM((Lk, 2 * V_DIM), BF16), stat, acc, stat, acc]),
        out_shape=jax.ShapeDtypeStruct((B, Lq, ATTN_W), BF16),
        compiler_params=pltpu.CompilerParams(dimension_semantics=("arbitrary",) * 2,
                                             vmem_limit_bytes=VMEM_LIMIT),
        name="attn",
    )(slopes, q, k, v, key_bias, table, lam_rows, g_subln.reshape(1, V_DIM))


def _postmix_kernel(a_ref, c_ref, x_ref, mod_ref, wo_ref, gpm_ref, gpf_ref, wr_ref, br_ref, cin_ref,
                    x1_ref, h2_ref, rt_ref, cnt_ref, carry_ref, *, tm):
    first = jnp.logical_and(pl.program_id(0) == 0, pl.program_id(1) == 0)

    @pl.when(first)
    def _():
        carry_ref[...] = cin_ref[...]

    mix = _dot(a_ref[0], wo_ref[0:ATTN_W, :]) + _dot(c_ref[0], wo_ref[ATTN_W:ATTN_W + CONV_CH, :])
    gate_a = mod_ref[0, 2:3, :]
    shift_f = mod_ref[0, 3:4, :]
    scale_f = mod_ref[0, 4:5, :]
    x1 = x_ref[0] + gate_a * (_rms(mix) * gpm_ref[...])
    x1_ref[0] = x1
    h2 = _rms(x1) * gpf_ref[...] * (1.0 + scale_f) + shift_f
    h2_ref[0] = _pack_halves(h2)

    logits = _dot3(h2, wr_ref[...]) + br_ref[...]
    lane = lax.broadcasted_iota(I32, (tm, LANES), 1).astype(F32)
    vals, idxs = [], []
    for _ in range(TOP_K):
        m = jnp.max(logits, axis=-1, keepdims=True)
        idx = jnp.min(jnp.where(logits == m, lane, float(LANES)), axis=-1, keepdims=True)
        vals.append(m)
        idxs.append(idx)
        logits = jnp.where(lane == idx, 2.0 * NEG, logits)
    es = [jnp.exp(v - vals[0]) for v in vals]
    denom = es[0] + es[1] + es[2] + es[3]

    onehot = jnp.zeros((tm, LANES), F32)
    for idx in idxs:
        onehot = jnp.where(lane == idx, 1.0, onehot)
    r_i = lax.broadcasted_iota(I32, (tm, tm), 0)
    c_i = lax.broadcasted_iota(I32, (tm, tm), 1)
    tri = jnp.where(c_i < r_i, 1.0, 0.0).astype(BF16)
    before = _dot(tri, onehot.astype(BF16)) + carry_ref[0:1, :]

    rt = jnp.zeros((tm, LANES), F32)
    for k in range(TOP_K):
        rank = jnp.sum(jnp.where(lane == idxs[k], before, 0.0), axis=-1, keepdims=True)
        rt = jnp.where(lane == k, idxs[k], rt)
        rt = jnp.where(lane == TOP_K + k, es[k] / denom, rt)
        rt = jnp.where(lane == 2 * TOP_K + k, rank, rt)
    rt_ref[0] = rt

    carry_ref[...] = carry_ref[...] + jnp.sum(onehot, axis=0, keepdims=True)
    cnt_ref[...] = carry_ref[...]


def _postmix(attn, conv, x, mod, w_out_bf, g_post_mix, g_pre_ffn, w_router_pad, b_router_pad, counts_in, tm):
    B, L, _ = x.shape
    row = lambda b, s: (b, s, 0)
    const2 = lambda b, s: (0, 0)
    return pl.pallas_call(
        functools.partial(_postmix_kernel, tm=tm),
        grid=(B, L // tm),
        in_specs=[pl.BlockSpec((1, tm, ATTN_W), row),
                  pl.BlockSpec((1, tm, CONV_CH), row),
                  pl.BlockSpec((1, tm, D_MODEL), row),
                  pl.BlockSpec((1, N_MOD, D_MODEL), lambda b, s: (b, 0, 0)),
                  pl.BlockSpec((D_MODEL, D_MODEL), const2),
                  pl.BlockSpec((1, D_MODEL), const2),
                  pl.BlockSpec((1, D_MODEL), const2),
                  pl.BlockSpec((D_MODEL, LANES), const2),
                  pl.BlockSpec((1, LANES), const2),
                  pl.BlockSpec((8, LANES), const2)],
        out_specs=[pl.BlockSpec((1, tm, D_MODEL), row),
                   pl.BlockSpec((1, tm, HALF), row),
                   pl.BlockSpec((1, tm, LANES), row),
                   pl.BlockSpec((8, LANES), const2)],
        out_shape=[jax.ShapeDtypeStruct((B, L, D_MODEL), F32),
                   jax.ShapeDtypeStruct((B, L, HALF), I32),
                   jax.ShapeDtypeStruct((B, L, LANES), F32),
                   jax.ShapeDtypeStruct((8, LANES), F32)],
        scratch_shapes=[pltpu.VMEM((8, LANES), F32)],
        compiler_params=pltpu.CompilerParams(dimension_semantics=("arbitrary", "arbitrary"),
                                             vmem_limit_bytes=VMEM_LIMIT),
        name="postmix",
    )(attn, conv, x, mod, w_out_bf, g_post_mix.reshape(1, D_MODEL), g_pre_ffn.reshape(1, D_MODEL),
      w_router_pad, b_router_pad, counts_in)


def _expert_kernel(be_ref, nv_ref, x_ref, wgu_ref, bgu_ref, wd_ref, bd_ref, y_ref, wgu_bf, wd_bf):
    i = pl.program_id(0)
    ff_half = D_FF // 2

    @pl.when(jnp.logical_or(i == 0, be_ref[i] != be_ref[jnp.maximum(i - 1, 0)]))
    def _():
        wgu_bf[...] = wgu_ref[0].astype(BF16)
        wd_bf[...] = wd_ref[0].astype(BF16)

    @pl.when(nv_ref[i] > 0)
    def _():
        x_lo, x_hi = _unpack_halves(x_ref[...])
        x_lo = x_lo.astype(BF16)
        x_hi = x_hi.astype(BF16)
        acc = None
        for c in range(2):
            lo, hi = c * ff_half, (c + 1) * ff_half
            g = (_dot(x_lo, wgu_bf[0:HALF, lo:hi]) + _dot(x_hi, wgu_bf[HALF:D_MODEL, lo:hi])
                 + bgu_ref[0, :, lo:hi])
            lin = (_dot(x_lo, wgu_bf[0:HALF, D_FF + lo:D_FF + hi]) + _dot(x_hi, wgu_bf[HALF:D_MODEL, D_FF + lo:D_FF + hi])
                   + bgu_ref[0, :, D_FF + lo:D_FF + hi])
            g = jnp.minimum(g, SWIGLU_LIMIT)
            lin = jnp.clip(lin, -SWIGLU_LIMIT, SWIGLU_LIMIT)
            act = g * _sigmoid(SWIGLU_ALPHA * g) * (lin + 1.0)
            part = _dot(act.astype(BF16), wd_bf[lo:hi, :])
            acc = part if acc is None else acc + part
        y_ref[...] = _pack_halves(acc + bd_ref[0])


def _experts(x_pad, block_e, n_valid, w_gu, b_gu, w_down, b_down, blk):
    n_blocks = block_e.shape[0]
    ex = lambda i, be, nv: (be[i], 0, 0)
    rows = lambda i, be, nv: (i, 0)
    return pl.pallas_call(
        _expert_kernel,
        grid_spec=pltpu.PrefetchScalarGridSpec(
            num_scalar_prefetch=2,
            grid=(n_blocks,),
            in_specs=[pl.BlockSpec((blk, HALF), rows),
                      pl.BlockSpec((1, D_MODEL, 2 * D_FF), ex), pl.BlockSpec((1, 1, 2 * D_FF), ex),
                      pl.BlockSpec((1, D_FF, D_MODEL), ex), pl.BlockSpec((1, 1, D_MODEL), ex)],
            out_specs=pl.BlockSpec((blk, HALF), rows),
            scratch_shapes=[pltpu.VMEM((D_MODEL, 2 * D_FF), BF16), pltpu.VMEM((D_FF, D_MODEL), BF16)]),
        out_shape=jax.ShapeDtypeStruct((n_blocks * blk, HALF), I32),
        compiler_params=pltpu.CompilerParams(dimension_semantics=("arbitrary",),
                                             vmem_limit_bytes=EXPERT_VMEM_LIMIT),
        name="experts",
    )(block_e, n_valid, x_pad, w_gu, b_gu.reshape(N_EXPERTS, 1, 2 * D_FF), w_down,
      b_down.reshape(N_EXPERTS, 1, D_MODEL))


def _combine_kernel(y0_ref, y1_ref, y2_ref, y3_ref, rt_ref, x1_ref, mod_ref, g_ref, o_ref):
    rt = rt_ref[0]
    f_lo = f_hi = None
    for k, y_ref in enumerate((y0_ref, y1_ref, y2_ref, y3_ref)):
        lo, hi = _unpack_halves(y_ref[...])
        w = rt[:, TOP_K + k:TOP_K + k + 1]
        f_lo = w * lo if f_lo is None else f_lo + w * lo
        f_hi = w * hi if f_hi is None else f_hi + w * hi
    ms = (jnp.sum(f_lo * f_lo, axis=-1, keepdims=True) + jnp.sum(f_hi * f_hi, axis=-1, keepdims=True)) / D_MODEL
    r = lax.rsqrt(ms + EPS)
    gate_f = mod_ref[0, 5:6, :]
    g = g_ref[...]
    o_ref[0, :, 0:HALF] = x1_ref[0, :, 0:HALF] + gate_f[:, 0:HALF] * (f_lo * r * g[:, 0:HALF])
    o_ref[0, :, HALF:D_MODEL] = x1_ref[0, :, HALF:D_MODEL] + gate_f[:, HALF:D_MODEL] * (f_hi * r * g[:, HALF:D_MODEL])


def _combine(y, route, x1, mod, g_post_ffn, tm, t_all, t_first):
    B, L, _ = x1.shape
    assert t_all % tm == 0 and t_first % tm == 0
    row = lambda b, s: (b, s, 0)
    y_specs = [pl.BlockSpec((tm, HALF), lambda b, s, k=k: ((k * t_all + t_first) // tm + b * (L // tm) + s, 0))
               for k in range(TOP_K)]
    return pl.pallas_call(
        _combine_kernel,
        grid=(B, L // tm),
        in_specs=y_specs + [
                  pl.BlockSpec((1, tm, LANES), row),
                  pl.BlockSpec((1, tm, D_MODEL), row),
                  pl.BlockSpec((1, N_MOD, D_MODEL), lambda b, s: (b, 0, 0)),
                  pl.BlockSpec((1, D_MODEL), lambda b, s: (0, 0))],
        out_specs=pl.BlockSpec((1, tm, D_MODEL), row),
        out_shape=jax.ShapeDtypeStruct((B, L, D_MODEL), F32),
        compiler_params=pltpu.CompilerParams(dimension_semantics=("arbitrary", "arbitrary"),
                                             vmem_limit_bytes=VMEM_LIMIT),
        name="combine",
    )(y, y, y, y, route, x1, mod, g_post_ffn.reshape(1, D_MODEL))


SC_GATHER_BYTES = 128 * 1024


def _scatter_rows(table, dest, n_slots):
    n, width = table.shape
    rows = SC_GATHER_BYTES // (width * 4)
    sc = plsc.get_sparse_core_info()
    workers = sc.num_cores * sc.num_subcores
    quantum = workers * rows * 2
    n_pad = -(-n // quantum) * quantum
    extra = n_pad - n
    table = jnp.concatenate([table, jnp.zeros((extra, width), table.dtype)], axis=0)
    spare = n_slots + jnp.arange(extra * TOP_K, dtype=I32).reshape(extra, TOP_K)
    per_w = n_pad // workers
    wins = per_w // rows
    wins_tile = -(-wins // SUBLANES) * SUBLANES
    idx = jnp.concatenate([dest, spare], axis=0).T.reshape(TOP_K, workers, wins, rows)
    idx = jnp.pad(idx, ((0, 0), (0, 0), (0, wins_tile - wins), (0, 0))).reshape(TOP_K, workers * wins_tile, rows)
    mesh = plsc.VectorSubcoreMesh(core_axis_name="c", subcore_axis_name="s")
    buf = pltpu.VMEM((rows, width), table.dtype)

    @functools.partial(
        pl.kernel, mesh=mesh,
        out_type=jax.ShapeDtypeStruct((n_slots + extra * TOP_K, width), table.dtype),
        scratch_types=[pltpu.VMEM((TOP_K, wins_tile, rows), I32), buf, buf,
                       pltpu.SemaphoreType.DMA, pltpu.SemaphoreType.DMA],
        name="scatter_rows",
    )
    def scatter_kernel(table_hbm, idx_hbm, out_hbm, idx_v, buf0, buf1, sem0, sem1):
        w = lax.axis_index("s") * sc.num_cores + lax.axis_index("c")
        first = pl.multiple_of(w * wins_tile, SUBLANES)
        for k in range(TOP_K):
            pltpu.sync_copy(idx_hbm.at[k, pl.ds(first, wins_tile)], idx_v.at[k])
        base = pl.multiple_of(w * per_w, rows)

        def window(i, buf, sem):
            pltpu.sync_copy(table_hbm.at[pl.ds(base + i * rows, rows)], buf)
            copies = [pltpu.async_copy(buf, out_hbm.at[idx_v.at[k, i]], sem) for k in range(TOP_K)]
            return copies

        @pl.loop(0, wins, step=2)
        def _(i):
            c0 = window(i, buf0, sem0)
            c1 = window(i + 1, buf1, sem1)
            for c in c0 + c1:
                c.wait()

    return scatter_kernel(table, idx)


def _gather_rows(table, idx):
    n = idx.shape[0]
    width = table.shape[1]
    rows = SC_GATHER_BYTES // (width * 4)
    sc = plsc.get_sparse_core_info()
    workers = sc.num_cores * sc.num_subcores
    quantum = workers * rows * 2
    n_pad = -(-n // quantum) * quantum
    filler = jnp.arange(n_pad - n, dtype=I32) % table.shape[0]
    per_w = n_pad // workers
    wins = per_w // rows
    wins_tile = -(-wins // SUBLANES) * SUBLANES
    idx = jnp.concatenate([idx, filler]).reshape(workers, wins, rows)
    idx = jnp.pad(idx, ((0, 0), (0, wins_tile - wins), (0, 0))).reshape(workers * wins_tile, rows)
    mesh = plsc.VectorSubcoreMesh(core_axis_name="c", subcore_axis_name="s")
    buf = pltpu.VMEM((rows, width), table.dtype)

    @functools.partial(
        pl.kernel, mesh=mesh,
        out_type=jax.ShapeDtypeStruct((n_pad, width), table.dtype),
        scratch_types=[pltpu.VMEM((wins_tile, rows), I32), buf, buf,
                       pltpu.SemaphoreType.DMA, pltpu.SemaphoreType.DMA],
        name="gather_rows",
    )
    def gather_kernel(table_hbm, idx_hbm, out_hbm, idx_v, buf0, buf1, sem0, sem1):
        w = lax.axis_index("s") * sc.num_cores + lax.axis_index("c")
        pltpu.sync_copy(idx_hbm.at[pl.ds(pl.multiple_of(w * wins_tile, SUBLANES), wins_tile)], idx_v)
        base = pl.multiple_of(w * per_w, rows)

        @pl.loop(0, wins, step=2)
        def _(i):
            g0 = pltpu.async_copy(table_hbm.at[idx_v.at[i]], buf0, sem0)
            g1 = pltpu.async_copy(table_hbm.at[idx_v.at[i + 1]], buf1, sem1)
            g0.wait()
            pltpu.sync_copy(buf0, out_hbm.at[pl.ds(base + i * rows, rows)])
            g1.wait()
            pltpu.sync_copy(buf1, out_hbm.at[pl.ds(base + (i + 1) * rows, rows)])

    return gather_kernel(table, idx)


def _routing_tables(route, counts, blk):
    T = route.shape[0]
    TK = T * TOP_K
    idx = route[:, 0:TOP_K].astype(I32)
    rank = route[:, 2 * TOP_K:3 * TOP_K].astype(I32)
    padded = (counts + blk - 1) // blk * blk
    pad_end = jnp.cumsum(padded)
    pad_start = pad_end - padded
    dest = pad_start[idx] + rank
    n_blocks = -(-TK // blk) + N_EXPERTS
    blk_start = jnp.arange(n_blocks, dtype=I32) * blk
    block_e = jnp.sum((blk_start[:, None] >= pad_end[None, :]).astype(I32), axis=1)
    block_e = jnp.minimum(block_e, N_EXPERTS - 1)
    last = (pad_start + counts)[block_e]
    n_valid = jnp.clip(last - blk_start, 0, blk)
    return block_e, n_valid, dest


def _mix(x, mod, buf, past_k, past_v, p, lam_init, counts_in, tm, tq):
    B, L, _ = x.shape
    q, kb, vb, kf, vf, conv, state = _premix(x, mod, p["g_pre_mix"], p["w_in"], buf, p["w_dw"], p["b_dw"],
                                             p["gmat"], p["g_cn"], p["b_cn"], tm)
    if past_k is None:
        attn = _attention(q, kb, vb, p["lam_rows"], p["g_subln"], lam_init, tq, tq, 0, True)
    else:
        P = past_k.shape[1]
        keys = jnp.concatenate([past_k.reshape(B, P, QK_W).astype(BF16), kb], axis=1)
        vals = jnp.concatenate([past_v.reshape(B, P, ATTN_W).astype(BF16), vb], axis=1)
        attn = _attention(q, keys, vals, p["lam_rows"], p["g_subln"], lam_init, L, P + L, P, False)
    x1, h2, route, cnt = _postmix(attn, conv, x, mod, p["w_out"], p["g_post_mix"], p["g_pre_ffn"],
                                  p["w_router"], p["b_router"], counts_in, tm)
    return x1, h2, route, cnt, kf, vf, state[:, CONV_HALO - (CONV_W - 1):, :]


def _moe_rows(h2_groups, route_groups, counts, p, blk):
    h2 = jnp.concatenate([h.reshape(-1, HALF) for h in h2_groups], axis=0)
    route = jnp.concatenate([r.reshape(-1, LANES) for r in route_groups], axis=0)
    T = h2.shape[0]
    block_e, n_valid, dest = _routing_tables(route, counts, blk)
    x_pad = _scatter_rows(h2, dest, block_e.shape[0] * blk)
    y_pad = _experts(x_pad, block_e, n_valid, p["w_gu"], p["b_gu"], p["w_down"], p["b_down"], blk)
    return _gather_rows(y_pad, dest.T.reshape(T * TOP_K))


def _prepare_params(l, w_ada, b_ada, g_pre_mix, g_post_mix, w_in, lambda_q1, lambda_k1, lambda_q2, lambda_k2,
                    g_subln, w_dw, b_dw, g_cnorm, b_cnorm, w_out, g_pre_ffn, g_post_ffn,
                    w_router, b_router, w_gu, b_gu, w_down, b_down):
    lam_rows = jnp.zeros((8, LANES), F32)
    for r, vec in enumerate((lambda_q1[l], lambda_k1[l], lambda_q2[l], lambda_k2[l])):
        lam_rows = lam_rows.at[r, :HEAD_DIM].set(vec)
    ch = jnp.arange(CONV_CH, dtype=I32) // GROUP_CH
    gmat = (ch[:, None] == ch[None, :]).astype(BF16)
    w_dw_pad = jnp.zeros((CONV_HALO, CONV_CH), F32).at[:CONV_W].set(w_dw[l])
    w_router_pad = jnp.zeros((D_MODEL, LANES), F32).at[:, :N_EXPERTS].set(w_router[l])
    b_router_pad = jnp.full((1, LANES), NEG, F32).at[0, :N_EXPERTS].set(b_router[l])
    return dict(w_ada=w_ada[l], b_ada=b_ada[l], g_pre_mix=g_pre_mix[l], g_post_mix=g_post_mix[l],
                w_in=w_in[l].astype(BF16), lam_rows=lam_rows, g_subln=g_subln[l], w_dw=w_dw_pad, b_dw=b_dw[l],
                gmat=gmat, g_cn=g_cnorm[l], b_cn=b_cnorm[l], w_out=w_out[l].astype(BF16),
                g_pre_ffn=g_pre_ffn[l], g_post_ffn=g_post_ffn[l], w_router=w_router_pad, b_router=b_router_pad,
                w_gu=w_gu[l], b_gu=b_gu[l], w_down=w_down[l], b_down=b_down[l])


def kernel(x_prompt, x_sample, cache_k, cache_v, state_conv, c_prompt, c_sample, w_ada, b_ada, g_pre_mix, g_post_mix, w_in, lambda_q1, lambda_k1, lambda_q2, lambda_k2, g_subln, w_dw, b_dw, g_cnorm, b_cnorm, w_out, g_pre_ffn, g_post_ffn, w_router, b_router, w_gu, b_gu, w_down, b_down):
    depth = w_ada.shape[0]
    Bp, Lp, _ = x_prompt.shape
    Bs, Ls, _ = x_sample.shape
    yp, ys = x_prompt, x_sample
    outs = [[] for _ in range(6)]
    for l in range(depth):
        p = _prepare_params(l, w_ada, b_ada, g_pre_mix, g_post_mix, w_in, lambda_q1, lambda_k1, lambda_q2,
                            lambda_k2, g_subln, w_dw, b_dw, g_cnorm, b_cnorm, w_out, g_pre_ffn, g_post_ffn,
                            w_router, b_router, w_gu, b_gu, w_down, b_down)
        lam_init = 0.8 - 0.6 * math.exp(-0.3 * l)
        mod = _ada(jnp.concatenate([c_prompt, c_sample], axis=0), p["w_ada"], p["b_ada"])
        mod = mod.reshape(Bp + Bs, N_MOD, D_MODEL)
        buf_p = jnp.zeros((Bp, CONV_HALO, CONV_CH), F32)
        buf_s = jnp.pad(state_conv[l], ((0, 0), (CONV_HALO - (CONV_W - 1), 0), (0, 0)))
        tm_p = min(Lp, 512)
        zero_counts = jnp.zeros((SUBLANES, LANES), F32)
        x1p, h2p, rtp, cnt_p, kp, vp, cp = _mix(yp, mod[:Bp], buf_p, None, None, p, lam_init, zero_counts,
                                                tm=tm_p, tq=tm_p)
        x1s, h2s, rts, cnt, ks, vs, cs = _mix(ys, mod[Bp:], buf_s, cache_k[l], cache_v[l], p, lam_init, cnt_p,
                                              tm=Ls, tq=Ls)
        y = _moe_rows((h2p, h2s), (rtp, rts), cnt[0, :N_EXPERTS].astype(I32), p, EXPERT_ROWS)
        t_p, t_all = Bp * Lp, Bp * Lp + Bs * Ls
        yp = _combine(y, rtp, x1p, mod[:Bp], p["g_post_ffn"], min(tm_p, 256), t_all, 0)
        ys = _combine(y, rts, x1s, mod[Bp:], p["g_post_ffn"], Ls, t_all, t_p)
        for lst, val in zip(outs, (kp, vp, cp, ks, vs, cs)):
            lst.append(val)
    return (yp, ys) + tuple(jnp.stack(o) for o in outs)
```

```python
import functools
import math

import jax
import jax.numpy as jnp
from jax import lax
from jax.experimental import pallas as pl
from jax.experimental.pallas import tpu as pltpu
from jax.experimental.pallas import tpu_sc as plsc

F32 = jnp.float32
BF16 = jnp.bfloat16
I32 = jnp.int32

D_MODEL = 1024
CHUNK = 64
CHUNK_SHIFT = 6
N_HEADS = 4
V_DIM = 128
HEAD_DIM = 64
QK_W = 512
ATTN_W = 512
CONV_CH = 512
CONV_W = 31
CONV_GROUPS = 8
GROUP_CH = CONV_CH // CONV_GROUPS
IN_W = 2 * QK_W + ATTN_W + 2 * CONV_CH
N_MOD = 6
N_EXPERTS = 32
TOP_K = 4
D_FF = 1024
SWIGLU_ALPHA = 1.702
SWIGLU_LIMIT = 7.0
EPS = 1e-6

LANES = 128
SUBLANES = 8
CONV_HALO = 32
CONV_ROWS = 32
NEG = -1e30
LOG2E = math.log2(math.e)
ATTN_ROWS = 128
EXPERT_ROWS = 512
VMEM_LIMIT = 48 * 1024 * 1024
EXPERT_VMEM_LIMIT = 56 * 1024 * 1024


def _sigmoid(x):
    return 1.0 / (1.0 + jnp.exp(-x))


def _split_bf16(x):
    hi = x.astype(BF16)
    lo = (x - hi.astype(F32)).astype(BF16)
    return hi, lo


def _dot(a, b):
    return jnp.dot(a, b, preferred_element_type=F32)


def _dot3(a, b):
    ah, al = _split_bf16(a)
    bh, bl = _split_bf16(b)
    return _dot(ah, bh) + _dot(ah, bl) + _dot(al, bh)


def _rms(x):
    return x * lax.rsqrt(jnp.mean(x * x, axis=-1, keepdims=True) + EPS)


HALF = D_MODEL // 2
HIGH16 = -65536


def _pack_halves(x):
    lo = lax.bitcast_convert_type(x[:, :HALF].astype(BF16).astype(F32), I32)
    hi = lax.bitcast_convert_type(x[:, HALF:].astype(BF16).astype(F32), I32)
    return jnp.bitwise_or(jnp.bitwise_and(jnp.right_shift(lo, 16), 0xFFFF), jnp.bitwise_and(hi, HIGH16))


def _unpack_halves(w):
    lo = lax.bitcast_convert_type(jnp.left_shift(w, 16), F32)
    hi = lax.bitcast_convert_type(jnp.bitwise_and(w, HIGH16), F32)
    return lo, hi


def _ada_kernel(c_ref, w_ref, b_ref, o_ref):
    c = c_ref[...]
    o_ref[...] = _dot3(c * _sigmoid(c), w_ref[...]) + b_ref[...]


def _ada(c, w_ada, b_ada):
    n = c.shape[0]
    return pl.pallas_call(
        _ada_kernel,
        grid=(N_MOD,),
        in_specs=[pl.BlockSpec((n, D_MODEL), lambda j: (0, 0)),
                  pl.BlockSpec((D_MODEL, D_MODEL), lambda j: (0, j)),
                  pl.BlockSpec((1, D_MODEL), lambda j: (0, j))],
        out_specs=pl.BlockSpec((n, D_MODEL), lambda j: (0, j)),
        out_shape=jax.ShapeDtypeStruct((n, N_MOD * D_MODEL), F32),
        compiler_params=pltpu.CompilerParams(dimension_semantics=("arbitrary",), vmem_limit_bytes=VMEM_LIMIT),
        name="ada",
    )(c, w_ada, b_ada.reshape(1, N_MOD * D_MODEL))


def _premix_kernel(x_ref, mod_ref, g_ref, w_ref, buf_ref, wdw_ref, bdw_ref, gmat_ref, gcn_ref, bcn_ref,
                   q_ref, kb_ref, vb_ref, kf_hbm, vf_hbm, co_ref, st_ref, ext_ref, sh_ref, y_ref, kv_buf, kv_sem,
                   *, tm):
    batch = pl.program_id(0)
    s = pl.program_id(1)
    step = batch * pl.num_programs(1) + s
    last = pl.num_programs(0) * pl.num_programs(1) - 1
    slot = step % 2

    def kv_copies(slot):
        return [pltpu.make_async_copy(kv_buf.at[slot, i, :, pl.ds(h * V_DIM, V_DIM)],
                                      out.at[batch, pl.ds(s * tm, tm), h, :], kv_sem.at[slot, i, h])
                for i, out in enumerate((kf_hbm, vf_hbm)) for h in range(N_HEADS)]

    @pl.when(step >= 2)
    def _():
        for c in kv_copies(slot):
            c.wait()

    @pl.when(s == 0)
    def _():
        ext_ref[0:CONV_HALO, :] = buf_ref[0]

    x = x_ref[0]
    shift = mod_ref[0, 0:1, :]
    scale = mod_ref[0, 1:2, :]
    h = _rms(x) * g_ref[...] * (1.0 + scale) + shift
    hb = h.astype(BF16)

    u0 = 2 * QK_W + ATTN_W
    val = _dot(hb, w_ref[:, u0:u0 + CONV_CH])
    gate = _dot(hb, w_ref[:, u0 + CONV_CH:u0 + 2 * CONV_CH])
    ext_ref[CONV_HALO:CONV_HALO + tm, :] = val * _sigmoid(gate)

    off = CONV_HALO - (CONV_W - 1)
    span = tm + CONV_HALO - SUBLANES
    for b in range(1, SUBLANES):
        sh_ref[b - 1, 0:span, :] = ext_ref[b:b + span, :]
    for c in range(tm // CONV_ROWS):
        r0 = c * CONV_ROWS
        acc = jnp.zeros((CONV_ROWS, CONV_CH), F32)
        for j in range(CONV_W):
            b = (j + off) % SUBLANES
            a = r0 + j + off - b
            rows_j = ext_ref[a:a + CONV_ROWS, :] if b == 0 else sh_ref[b - 1, a:a + CONV_ROWS, :]
            acc = acc + wdw_ref[j:j + 1, :] * rows_j
        y_ref[r0:r0 + CONV_ROWS, :] = acc + bdw_ref[...]

    zq = _dot(hb, w_ref[:, 0:QK_W])
    q_ref[0] = (zq * (HEAD_DIM ** -0.5 * LOG2E)).astype(BF16)
    zk = _dot(hb, w_ref[:, QK_W:2 * QK_W])
    kv_buf[slot, 0] = zk
    kb_ref[0] = zk.astype(BF16)
    zv = _dot(hb, w_ref[:, 2 * QK_W:2 * QK_W + ATTN_W])
    kv_buf[slot, 1] = zv
    vb_ref[0] = zv.astype(BF16)

    y = y_ref[...]
    gm = gmat_ref[...]
    yh, yl = _split_bf16(y)
    mu = (_dot(yh, gm) + _dot(yl, gm)) * (1.0 / GROUP_CH)
    d = y - mu
    dh, dl = _split_bf16(d * d)
    var = (_dot(dh, gm) + _dot(dl, gm)) * (1.0 / GROUP_CH)
    yn = d * lax.rsqrt(var + EPS) * gcn_ref[...] + bcn_ref[...]
    co_ref[0] = (yn * _sigmoid(yn)).astype(BF16)

    tail = ext_ref[tm:tm + CONV_HALO, :]
    st_ref[0] = tail
    ext_ref[0:CONV_HALO, :] = tail

    for c in kv_copies(slot):
        c.start()

    @pl.when(step == last)
    def _():
        for c in kv_copies(slot):
            c.wait()

    @pl.when(jnp.logical_and(step == last, step >= 1))
    def _():
        for c in kv_copies(1 - slot):
            c.wait()


def _premix(x, mod, g_pre_mix, w_in_bf, buf, w_dw, b_dw, gmat, g_cn, b_cn, tm):
    B, L, _ = x.shape
    assert L % tm == 0 and tm % CONV_ROWS == 0
    row = lambda b, s: (b, s, 0)
    const2 = lambda b, s: (0, 0)
    bf_tile = jax.ShapeDtypeStruct((B, L, QK_W), BF16)
    f_heads = jax.ShapeDtypeStruct((B, L, N_HEADS, V_DIM), F32)
    return pl.pallas_call(
        functools.partial(_premix_kernel, tm=tm),
        grid=(B, L // tm),
        in_specs=[pl.BlockSpec((1, tm, D_MODEL), row),
                  pl.BlockSpec((1, N_MOD, D_MODEL), lambda b, s: (b, 0, 0)),
                  pl.BlockSpec((1, D_MODEL), const2),
                  pl.BlockSpec((D_MODEL, IN_W), const2),
                  pl.BlockSpec((1, CONV_HALO, CONV_CH), lambda b, s: (b, 0, 0)),
                  pl.BlockSpec((CONV_HALO, CONV_CH), const2),
                  pl.BlockSpec((1, CONV_CH), const2),
                  pl.BlockSpec((CONV_CH, CONV_CH), const2),
                  pl.BlockSpec((1, CONV_CH), const2),
                  pl.BlockSpec((1, CONV_CH), const2)],
        out_specs=[pl.BlockSpec((1, tm, QK_W), row)] * 3
                  + [pl.BlockSpec(memory_space=pl.ANY)] * 2
                  + [pl.BlockSpec((1, tm, QK_W), row),
                     pl.BlockSpec((1, CONV_HALO, CONV_CH), lambda b, s: (b, 0, 0))],
        out_shape=[bf_tile, bf_tile, bf_tile, f_heads, f_heads, bf_tile,
                   jax.ShapeDtypeStruct((B, CONV_HALO, CONV_CH), F32)],
        scratch_shapes=[pltpu.VMEM((CONV_HALO + tm, CONV_CH), F32),
                        pltpu.VMEM((SUBLANES - 1, CONV_HALO + tm, CONV_CH), F32),
                        pltpu.VMEM((tm, CONV_CH), F32),
                        pltpu.VMEM((2, 2, tm, QK_W), F32),
                        pltpu.SemaphoreType.DMA((2, 2, N_HEADS))],
        compiler_params=pltpu.CompilerParams(dimension_semantics=("arbitrary", "arbitrary"),
                                             vmem_limit_bytes=VMEM_LIMIT),
        name="premix",
    )(x, mod, g_pre_mix.reshape(1, D_MODEL), w_in_bf, buf, w_dw, b_dw.reshape(1, CONV_CH), gmat,
      g_cn.reshape(1, CONV_CH), b_cn.reshape(1, CONV_CH))


def _attn_kernel(slopes_ref, q_ref, k_ref, v_ref, kb_ref, mt_ref, lam_ref, gs_ref, o_ref,
                 ke1_ref, ke2_ref, ve_ref, m1_ref, a1_ref, m2_ref, a2_ref,
                 *, tq, tk, rows, q_offset, causal, lam_init):
    slope = slopes_ref[pl.program_id(1)] * LOG2E
    lq = q_ref.shape[1]
    lk = k_ref.shape[1]
    nq = lq // tq
    lane = lax.broadcasted_iota(I32, (1, V_DIM), 1)

    low = jnp.broadcast_to(lane < HEAD_DIM, (lk, V_DIM))
    k = k_ref[0]
    zero = jnp.zeros((lk, V_DIM), BF16)
    ke1_ref[...] = jnp.where(low, k, zero)
    ke2_ref[...] = jnp.where(low, zero, k)
    ve_ref[:, 0:V_DIM] = v_ref[0]
    ve_ref[:, V_DIM:2 * V_DIM] = jnp.broadcast_to(jnp.where(lane == 0, 1.0, 0.0), (lk, V_DIM)).astype(BF16)

    nt = (((1,), (1,)), ((), ()))
    maps = ((ke1_ref, m1_ref, a1_ref), (ke2_ref, m2_ref, a2_ref))

    def lanes(x, n):
        if n % LANES == 0:
            return jnp.concatenate([x] * (n // LANES), axis=1)
        return jnp.broadcast_to(x[:, 0:1], (rows, n))

    def tile(q0, k0, masked):
        ve = ve_ref[pl.ds(k0, tk), :]
        q_first = (lax.broadcasted_iota(I32, (1, 1), 0) + (q0 + q_offset)).astype(F32)
        bias = kb_ref[0, :, pl.ds(k0, tk)] - slope * q_first
        for ke_ref, m_ref, a_ref in maps:
            ke = ke_ref[pl.ds(k0, tk), :]
            for r0 in range(0, tq, rows):
                s = lax.dot_general(q_ref[0, pl.ds(q0 + r0, rows), :], ke, nt, preferred_element_type=F32)
                s = s + bias
                if masked:
                    s = s + mt_ref[0, r0:r0 + rows, :]
                m_old = m_ref[r0:r0 + rows, :]
                m_new = jnp.maximum(m_old, jnp.max(s, axis=-1, keepdims=True))
                p = jnp.exp2(s - lanes(m_new, tk))
                alpha = jnp.exp2(m_old - m_new)
                a_ref[r0:r0 + rows, :] = (lanes(alpha, 2 * V_DIM) * a_ref[r0:r0 + rows, :]
                                          + _dot(p.astype(BF16), ve))
                m_ref[r0:r0 + rows, :] = m_new

    def reset():
        for _, m_ref, a_ref in maps:
            m_ref[...] = jnp.full(m_ref.shape, NEG, F32)
            a_ref[...] = jnp.zeros(a_ref.shape, F32)

    lv = lam_ref[...]
    lam = (jnp.exp(jnp.sum(lv[0:1, :] * lv[1:2, :], axis=-1, keepdims=True))
           - jnp.exp(jnp.sum(lv[2:3, :] * lv[3:4, :], axis=-1, keepdims=True)) + lam_init)

    def finish(q0):
        a1 = a1_ref[...]
        a2 = a2_ref[...]
        o = a1[:, 0:V_DIM] / a1[:, V_DIM:V_DIM + 1] - lam * (a2[:, 0:V_DIM] / a2[:, V_DIM:V_DIM + 1])
        o_ref[0, pl.ds(q0, tq), :] = (_rms(o) * gs_ref[...] * (1.0 - lam_init)).astype(BF16)

    if not causal:
        reset()
        tile(0, 0, True)
        finish(0)
        return

    def q_tile(qi, carry):
        q0 = pl.multiple_of(qi * tq, tq)
        reset()

        def pair(i, c):
            tile(q0, pl.multiple_of(2 * i * tk, tk), False)
            tile(q0, pl.multiple_of((2 * i + 1) * tk, tk), False)
            return c

        lax.fori_loop(0, qi // 2, pair, 0)
        odd = qi % 2 == 1

        @pl.when(odd)
        def _():
            tile(q0, pl.multiple_of((qi - 1) * tk, tk), False)
            tile(q0, pl.multiple_of(qi * tk, tk), True)

        @pl.when(jnp.logical_not(odd))
        def _():
            tile(q0, pl.multiple_of(qi * tk, tk), True)

        finish(q0)
        return carry

    lax.fori_loop(0, nq, q_tile, 0)


def _mask_table(slopes, tq, tk, q_offset):
    qpos = q_offset + jnp.arange(tq, dtype=I32)[:, None]
    kpos = jnp.arange(tk, dtype=I32)[None, :]
    visible = jnp.right_shift(kpos, CHUNK_SHIFT) <= jnp.right_shift(qpos, CHUNK_SHIFT)
    fix = jnp.where(kpos > qpos, 2 * (qpos - kpos), 0).astype(F32)
    return jnp.where(visible[None], (slopes * LOG2E)[:, None, None] * fix[None], NEG)


def _attention(q, k, v, lam_rows, g_subln, lam_init, tq, tk, q_offset, causal):
    B, Lq, _ = q.shape
    Lk = k.shape[1]
    assert Lq % tq == 0 and Lk % tk == 0 and (not causal or (tq == tk and tq % CHUNK == 0 and q_offset == 0))
    assert causal or Lk == tk
    rows = min(tq, ATTN_ROWS)
    slopes = jnp.asarray([2.0 ** (-8.0 * (h + 1) / N_HEADS) for h in range(N_HEADS)], F32)
    table = _mask_table(slopes, tq, tk, q_offset)
    key_bias = ((slopes * LOG2E)[:, None] * jnp.arange(Lk, dtype=F32)[None, :]).reshape(N_HEADS, 1, Lk)
    key_ext = pltpu.VMEM((Lk, V_DIM), BF16)
    stat = pltpu.VMEM((tq, LANES), F32)
    acc = pltpu.VMEM((tq, 2 * V_DIM), F32)
    head = lambda b, h, sl: (b, 0, h)
    return pl.pallas_call(
        functools.partial(_attn_kernel, tq=tq, tk=tk, rows=rows, q_offset=q_offset, causal=causal,
                          lam_init=lam_init),
        grid_spec=pltpu.PrefetchScalarGridSpec(
            num_scalar_prefetch=1,
            grid=(B, N_HEADS),
            in_specs=[pl.BlockSpec((1, Lq, V_DIM), head),
                      pl.BlockSpec((1, Lk, V_DIM), head),
                      pl.BlockSpec((1, Lk, V_DIM), head),
                      pl.BlockSpec((1, 1, Lk), lambda b, h, sl: (h, 0, 0)),
                      pl.BlockSpec((1, tq, tk), lambda b, h, sl: (h, 0, 0)),
                      pl.BlockSpec((8, LANES), lambda b, h, sl: (0, 0)),
                      pl.BlockSpec((1, V_DIM), lambda b, h, sl: (0, 0))],
            out_specs=pl.BlockSpec((1, Lq, V_DIM), head),
            scratch_shapes=[key_ext, key_ext, pltpu.VMEM((Lk, 2 * V_DIM), BF16), stat, acc, stat, acc]),
        out_shape=jax.ShapeDtypeStruct((B, Lq, ATTN_W), BF16),
        compiler_params=pltpu.CompilerParams(dimension_semantics=("arbitrary",) * 2,
                                             vmem_limit_bytes=VMEM_LIMIT),
        name="attn",
    )(slopes, q, k, v, key_bias, table, lam_rows, g_subln.reshape(1, V_DIM))


def _postmix_kernel(a_ref, c_ref, x_ref, mod_ref, wo_ref, gpm_ref, gpf_ref, wr_ref, br_ref, cin_ref,
                    x1_ref, h2_ref, rt_ref, cnt_ref, carry_ref, *, tm):
    first = jnp.logical_and(pl.program_id(0) == 0, pl.program_id(1) == 0)

    @pl.when(first)
    def _():
        carry_ref[...] = cin_ref[...]

    mix = _dot(a_ref[0], wo_ref[0:ATTN_W, :]) + _dot(c_ref[0], wo_ref[ATTN_W:ATTN_W + CONV_CH, :])
    gate_a = mod_ref[0, 2:3, :]
    shift_f = mod_ref[0, 3:4, :]
    scale_f = mod_ref[0, 4:5, :]
    x1 = x_ref[0] + gate_a * (_rms(mix) * gpm_ref[...])
    x1_ref[0] = x1
    h2 = _rms(x1) * gpf_ref[...] * (1.0 + scale_f) + shift_f
    h2_ref[0] = _pack_halves(h2)

    logits = _dot3(h2, wr_ref[...]) + br_ref[...]
    lane = lax.broadcasted_iota(I32, (tm, LANES), 1).astype(F32)
    vals, idxs = [], []
    for _ in range(TOP_K):
        m = jnp.max(logits, axis=-1, keepdims=True)
        idx = jnp.min(jnp.where(logits == m, lane, float(LANES)), axis=-1, keepdims=True)
        vals.append(m)
        idxs.append(idx)
        logits = jnp.where(lane == idx, 2.0 * NEG, logits)
    es = [jnp.exp(v - vals[0]) for v in vals]
    denom = es[0] + es[1] + es[2] + es[3]

    onehot = jnp.zeros((tm, LANES), F32)
    for idx in idxs:
        onehot = jnp.where(lane == idx, 1.0, onehot)
    r_i = lax.broadcasted_iota(I32, (tm, tm), 0)
    c_i = lax.broadcasted_iota(I32, (tm, tm), 1)
    tri = jnp.where(c_i < r_i, 1.0, 0.0).astype(BF16)
    before = _dot(tri, onehot.astype(BF16)) + carry_ref[0:1, :]

    rt = jnp.zeros((tm, LANES), F32)
    for k in range(TOP_K):
        rank = jnp.sum(jnp.where(lane == idxs[k], before, 0.0), axis=-1, keepdims=True)
        rt = jnp.where(lane == k, idxs[k], rt)
        rt = jnp.where(lane == TOP_K + k, es[k] / denom, rt)
        rt = jnp.where(lane == 2 * TOP_K + k, rank, rt)
    rt_ref[0] = rt

    carry_ref[...] = carry_ref[...] + jnp.sum(onehot, axis=0, keepdims=True)
    cnt_ref[0] = carry_ref[...]


def _postmix(attn, conv, x, mod, w_out_bf, g_post_mix, g_pre_ffn, w_router_pad, b_router_pad, counts_in, tm):
    B, L, _ = x.shape
    row = lambda b, s: (b, s, 0)
    const2 = lambda b, s: (0, 0)
    return pl.pallas_call(
        functools.partial(_postmix_kernel, tm=tm),
        grid=(B, L // tm),
        in_specs=[pl.BlockSpec((1, tm, ATTN_W), row),
                  pl.BlockSpec((1, tm, CONV_CH), row),
                  pl.BlockSpec((1, tm, D_MODEL), row),
                  pl.BlockSpec((1, N_MOD, D_MODEL), lambda b, s: (b, 0, 0)),
                  pl.BlockSpec((D_MODEL, D_MODEL), const2),
                  pl.BlockSpec((1, D_MODEL), const2),
                  pl.BlockSpec((1, D_MODEL), const2),
                  pl.BlockSpec((D_MODEL, LANES), const2),
                  pl.BlockSpec((1, LANES), const2),
                  pl.BlockSpec((8, LANES), const2)],
        out_specs=[pl.BlockSpec((1, tm, D_MODEL), row),
                   pl.BlockSpec((1, tm, HALF), row),
                   pl.BlockSpec((1, tm, LANES), row),
                   pl.BlockSpec((1, SUBLANES, LANES), lambda b, s: (b, 0, 0))],
        out_shape=[jax.ShapeDtypeStruct((B, L, D_MODEL), F32),
                   jax.ShapeDtypeStruct((B, L, HALF), I32),
                   jax.ShapeDtypeStruct((B, L, LANES), F32),
                   jax.ShapeDtypeStruct((B, SUBLANES, LANES), F32)],
        scratch_shapes=[pltpu.VMEM((8, LANES), F32)],
        compiler_params=pltpu.CompilerParams(dimension_semantics=("arbitrary", "arbitrary"),
                                             vmem_limit_bytes=VMEM_LIMIT),
        name="postmix",
    )(attn, conv, x, mod, w_out_bf, g_post_mix.reshape(1, D_MODEL), g_pre_ffn.reshape(1, D_MODEL),
      w_router_pad, b_router_pad, counts_in)


def _expert_kernel(be_ref, nv_ref, x_ref, wgu_ref, bgu_ref, wd_ref, bd_ref, y_ref, wgu_bf, wd_bf):
    i = pl.program_id(0)
    ff_half = D_FF // 2

    @pl.when(jnp.logical_or(i == 0, be_ref[i] != be_ref[jnp.maximum(i - 1, 0)]))
    def _():
        wgu_bf[...] = wgu_ref[0].astype(BF16)
        wd_bf[...] = wd_ref[0].astype(BF16)

    @pl.when(nv_ref[i] > 0)
    def _():
        x_lo, x_hi = _unpack_halves(x_ref[...])
        x_lo = x_lo.astype(BF16)
        x_hi = x_hi.astype(BF16)
        acc = None
        for c in range(2):
            lo, hi = c * ff_half, (c + 1) * ff_half
            g = (_dot(x_lo, wgu_bf[0:HALF, lo:hi]) + _dot(x_hi, wgu_bf[HALF:D_MODEL, lo:hi])
                 + bgu_ref[0, :, lo:hi])
            lin = (_dot(x_lo, wgu_bf[0:HALF, D_FF + lo:D_FF + hi]) + _dot(x_hi, wgu_bf[HALF:D_MODEL, D_FF + lo:D_FF + hi])
                   + bgu_ref[0, :, D_FF + lo:D_FF + hi])
            g = jnp.minimum(g, SWIGLU_LIMIT)
            lin = jnp.clip(lin, -SWIGLU_LIMIT, SWIGLU_LIMIT)
            act = g * _sigmoid(SWIGLU_ALPHA * g) * (lin + 1.0)
            part = _dot(act.astype(BF16), wd_bf[lo:hi, :])
            acc = part if acc is None else acc + part
        y_ref[...] = _pack_halves(acc + bd_ref[0])


def _experts(x_pad, block_e, n_valid, w_gu, b_gu, w_down, b_down, blk):
    n_blocks = block_e.shape[0]
    ex = lambda i, be, nv: (be[i], 0, 0)
    rows = lambda i, be, nv: (i, 0)
    return pl.pallas_call(
        _expert_kernel,
        grid_spec=pltpu.PrefetchScalarGridSpec(
            num_scalar_prefetch=2,
            grid=(n_blocks,),
            in_specs=[pl.BlockSpec((blk, HALF), rows),
                      pl.BlockSpec((1, D_MODEL, 2 * D_FF), ex), pl.BlockSpec((1, 1, 2 * D_FF), ex),
                      pl.BlockSpec((1, D_FF, D_MODEL), ex), pl.BlockSpec((1, 1, D_MODEL), ex)],
            out_specs=pl.BlockSpec((blk, HALF), rows),
            scratch_shapes=[pltpu.VMEM((D_MODEL, 2 * D_FF), BF16), pltpu.VMEM((D_FF, D_MODEL), BF16)]),
        out_shape=jax.ShapeDtypeStruct((n_blocks * blk, HALF), I32),
        compiler_params=pltpu.CompilerParams(dimension_semantics=("arbitrary",),
                                             vmem_limit_bytes=EXPERT_VMEM_LIMIT),
        name="experts",
    )(block_e, n_valid, x_pad, w_gu, b_gu.reshape(N_EXPERTS, 1, 2 * D_FF), w_down,
      b_down.reshape(N_EXPERTS, 1, D_MODEL))


def _combine_kernel(y0_ref, y1_ref, y2_ref, y3_ref, rt_ref, x1_ref, mod_ref, g_ref, o_ref):
    rt = rt_ref[0]
    f_lo = f_hi = None
    for k, y_ref in enumerate((y0_ref, y1_ref, y2_ref, y3_ref)):
        lo, hi = _unpack_halves(y_ref[...])
        w = rt[:, TOP_K + k:TOP_K + k + 1]
        f_lo = w * lo if f_lo is None else f_lo + w * lo
        f_hi = w * hi if f_hi is None else f_hi + w * hi
    ms = (jnp.sum(f_lo * f_lo, axis=-1, keepdims=True) + jnp.sum(f_hi * f_hi, axis=-1, keepdims=True)) / D_MODEL
    r = lax.rsqrt(ms + EPS)
    gate_f = mod_ref[0, 5:6, :]
    g = g_ref[...]
    o_ref[0, :, 0:HALF] = x1_ref[0, :, 0:HALF] + gate_f[:, 0:HALF] * (f_lo * r * g[:, 0:HALF])
    o_ref[0, :, HALF:D_MODEL] = x1_ref[0, :, HALF:D_MODEL] + gate_f[:, HALF:D_MODEL] * (f_hi * r * g[:, HALF:D_MODEL])


def _combine_into_kernel(prev_ref, *refs):
    del prev_ref
    _combine_kernel(*refs)


def _combine(y, route, x1, mod, g_post_ffn, tm, t_all, t_first, b_first=0, n_batch=None, out_prev=None):
    B, L, _ = x1.shape
    n_batch = B if n_batch is None else n_batch
    assert t_all % tm == 0 and t_first % tm == 0
    row = lambda b, s: (b + b_first, s, 0)
    y_specs = [pl.BlockSpec((tm, HALF), lambda b, s, k=k: ((k * t_all + t_first) // tm + b * (L // tm) + s, 0))
               for k in range(TOP_K)]
    in_specs = y_specs + [pl.BlockSpec((1, tm, LANES), row),
                          pl.BlockSpec((1, tm, D_MODEL), row),
                          pl.BlockSpec((1, N_MOD, D_MODEL), lambda b, s: (b + b_first, 0, 0)),
                          pl.BlockSpec((1, D_MODEL), lambda b, s: (0, 0))]
    args = (y, y, y, y, route, x1, mod, g_post_ffn.reshape(1, D_MODEL))
    body, aliases = _combine_kernel, {}
    if out_prev is not None:
        body, aliases = _combine_into_kernel, {0: 0}
        in_specs = [pl.BlockSpec(memory_space=pl.ANY)] + in_specs
        args = (out_prev,) + args
    return pl.pallas_call(
        body,
        grid=(n_batch, L // tm),
        in_specs=in_specs,
        out_specs=pl.BlockSpec((1, tm, D_MODEL), row),
        out_shape=jax.ShapeDtypeStruct((B, L, D_MODEL), F32),
        input_output_aliases=aliases,
        compiler_params=pltpu.CompilerParams(dimension_semantics=("arbitrary", "arbitrary"),
                                             vmem_limit_bytes=VMEM_LIMIT),
        name="combine",
    )(*args)


SC_GATHER_BYTES = 128 * 1024


def _scatter_rows(table, dest, n_slots):
    n, width = table.shape
    rows = SC_GATHER_BYTES // (width * 4)
    sc = plsc.get_sparse_core_info()
    workers = sc.num_cores * sc.num_subcores
    quantum = workers * rows * 2
    n_pad = -(-n // quantum) * quantum
    extra = n_pad - n
    table = jnp.concatenate([table, jnp.zeros((extra, width), table.dtype)], axis=0)
    spare = n_slots + jnp.arange(extra * TOP_K, dtype=I32).reshape(extra, TOP_K)
    per_w = n_pad // workers
    wins = per_w // rows
    wins_tile = -(-wins // SUBLANES) * SUBLANES
    idx = jnp.concatenate([dest, spare], axis=0).T.reshape(TOP_K, workers, wins, rows)
    idx = jnp.pad(idx, ((0, 0), (0, 0), (0, wins_tile - wins), (0, 0))).reshape(TOP_K, workers * wins_tile, rows)
    mesh = plsc.VectorSubcoreMesh(core_axis_name="c", subcore_axis_name="s")
    buf = pltpu.VMEM((rows, width), table.dtype)

    @functools.partial(
        pl.kernel, mesh=mesh,
        out_type=jax.ShapeDtypeStruct((n_slots + extra * TOP_K, width), table.dtype),
        scratch_types=[pltpu.VMEM((TOP_K, wins_tile, rows), I32), buf, buf,
                       pltpu.SemaphoreType.DMA, pltpu.SemaphoreType.DMA],
        name="scatter_rows",
    )
    def scatter_kernel(table_hbm, idx_hbm, out_hbm, idx_v, buf0, buf1, sem0, sem1):
        w = lax.axis_index("s") * sc.num_cores + lax.axis_index("c")
        first = pl.multiple_of(w * wins_tile, SUBLANES)
        for k in range(TOP_K):
            pltpu.sync_copy(idx_hbm.at[k, pl.ds(first, wins_tile)], idx_v.at[k])
        base = pl.multiple_of(w * per_w, rows)

        def window(i, buf, sem):
            pltpu.sync_copy(table_hbm.at[pl.ds(base + i * rows, rows)], buf)
            copies = [pltpu.async_copy(buf, out_hbm.at[idx_v.at[k, i]], sem) for k in range(TOP_K)]
            return copies

        @pl.loop(0, wins, step=2)
        def _(i):
            c0 = window(i, buf0, sem0)
            c1 = window(i + 1, buf1, sem1)
            for c in c0 + c1:
                c.wait()

    return scatter_kernel(table, idx)


def _gather_rows(table, idx):
    n = idx.shape[0]
    width = table.shape[1]
    rows = SC_GATHER_BYTES // (width * 4)
    sc = plsc.get_sparse_core_info()
    workers = sc.num_cores * sc.num_subcores
    quantum = workers * rows * 2
    n_pad = -(-n // quantum) * quantum
    filler = jnp.arange(n_pad - n, dtype=I32) % table.shape[0]
    per_w = n_pad // workers
    wins = per_w // rows
    wins_tile = -(-wins // SUBLANES) * SUBLANES
    idx = jnp.concatenate([idx, filler]).reshape(workers, wins, rows)
    idx = jnp.pad(idx, ((0, 0), (0, wins_tile - wins), (0, 0))).reshape(workers * wins_tile, rows)
    mesh = plsc.VectorSubcoreMesh(core_axis_name="c", subcore_axis_name="s")
    buf = pltpu.VMEM((rows, width), table.dtype)

    @functools.partial(
        pl.kernel, mesh=mesh,
        out_type=jax.ShapeDtypeStruct((n_pad, width), table.dtype),
        scratch_types=[pltpu.VMEM((wins_tile, rows), I32), buf, buf,
                       pltpu.SemaphoreType.DMA, pltpu.SemaphoreType.DMA],
        name="gather_rows",
    )
    def gather_kernel(table_hbm, idx_hbm, out_hbm, idx_v, buf0, buf1, sem0, sem1):
        w = lax.axis_index("s") * sc.num_cores + lax.axis_index("c")
        pltpu.sync_copy(idx_hbm.at[pl.ds(pl.multiple_of(w * wins_tile, SUBLANES), wins_tile)], idx_v)
        base = pl.multiple_of(w * per_w, rows)

        @pl.loop(0, wins, step=2)
        def _(i):
            g0 = pltpu.async_copy(table_hbm.at[idx_v.at[i]], buf0, sem0)
            g1 = pltpu.async_copy(table_hbm.at[idx_v.at[i + 1]], buf1, sem1)
            g0.wait()
            pltpu.sync_copy(buf0, out_hbm.at[pl.ds(base + i * rows, rows)])
            g1.wait()
            pltpu.sync_copy(buf1, out_hbm.at[pl.ds(base + (i + 1) * rows, rows)])

    return gather_kernel(table, idx)


def _routing_tables(route, counts, counts_before, blk):
    T = route.shape[0]
    TK = T * TOP_K
    idx = route[:, 0:TOP_K].astype(I32)
    rank = route[:, 2 * TOP_K:3 * TOP_K].astype(I32) - counts_before[idx]
    padded = (counts + blk - 1) // blk * blk
    pad_end = jnp.cumsum(padded)
    pad_start = pad_end - padded
    dest = pad_start[idx] + rank
    n_blocks = -(-TK // blk) + N_EXPERTS
    blk_start = jnp.arange(n_blocks, dtype=I32) * blk
    block_e = jnp.sum((blk_start[:, None] >= pad_end[None, :]).astype(I32), axis=1)
    block_e = jnp.minimum(block_e, N_EXPERTS - 1)
    last = (pad_start + counts)[block_e]
    n_valid = jnp.clip(last - blk_start, 0, blk)
    return block_e, n_valid, dest


def _mix(x, mod, buf, past_k, past_v, p, lam_init, counts_in, tm, tq):
    B, L, _ = x.shape
    q, kb, vb, kf, vf, conv, state = _premix(x, mod, p["g_pre_mix"], p["w_in"], buf, p["w_dw"], p["b_dw"],
                                             p["gmat"], p["g_cn"], p["b_cn"], tm)
    if past_k is None:
        attn = _attention(q, kb, vb, p["lam_rows"], p["g_subln"], lam_init, tq, tq, 0, True)
    else:
        P = past_k.shape[1]
        keys = jnp.concatenate([past_k.reshape(B, P, QK_W).astype(BF16), kb], axis=1)
        vals = jnp.concatenate([past_v.reshape(B, P, ATTN_W).astype(BF16), vb], axis=1)
        attn = _attention(q, keys, vals, p["lam_rows"], p["g_subln"], lam_init, L, P + L, P, False)
    x1, h2, route, cnt = _postmix(attn, conv, x, mod, p["w_out"], p["g_post_mix"], p["g_pre_ffn"],
                                  p["w_router"], p["b_router"], counts_in, tm)
    return x1, h2, route, cnt, kf, vf, state[:, CONV_HALO - (CONV_W - 1):, :]


def _moe_rows(h2_groups, route_groups, counts, counts_before, p, blk):
    h2 = jnp.concatenate([h.reshape(-1, HALF) for h in h2_groups], axis=0)
    route = jnp.concatenate([r.reshape(-1, LANES) for r in route_groups], axis=0)
    T = h2.shape[0]
    block_e, n_valid, dest = _routing_tables(route, counts, counts_before, blk)
    x_pad = _scatter_rows(h2, dest, block_e.shape[0] * blk)
    y_pad = _experts(x_pad, block_e, n_valid, p["w_gu"], p["b_gu"], p["w_down"], p["b_down"], blk)
    return _gather_rows(y_pad, dest.T.reshape(T * TOP_K))


def _prepare_params(l, w_ada, b_ada, g_pre_mix, g_post_mix, w_in, lambda_q1, lambda_k1, lambda_q2, lambda_k2,
                    g_subln, w_dw, b_dw, g_cnorm, b_cnorm, w_out, g_pre_ffn, g_post_ffn,
                    w_router, b_router, w_gu, b_gu, w_down, b_down):
    lam_rows = jnp.zeros((8, LANES), F32)
    for r, vec in enumerate((lambda_q1[l], lambda_k1[l], lambda_q2[l], lambda_k2[l])):
        lam_rows = lam_rows.at[r, :HEAD_DIM].set(vec)
    ch = jnp.arange(CONV_CH, dtype=I32) // GROUP_CH
    gmat = (ch[:, None] == ch[None, :]).astype(BF16)
    w_dw_pad = jnp.zeros((CONV_HALO, CONV_CH), F32).at[:CONV_W].set(w_dw[l])
    w_router_pad = jnp.zeros((D_MODEL, LANES), F32).at[:, :N_EXPERTS].set(w_router[l])
    b_router_pad = jnp.full((1, LANES), NEG, F32).at[0, :N_EXPERTS].set(b_router[l])
    return dict(w_ada=w_ada[l], b_ada=b_ada[l], g_pre_mix=g_pre_mix[l], g_post_mix=g_post_mix[l],
                w_in=w_in[l].astype(BF16), lam_rows=lam_rows, g_subln=g_subln[l], w_dw=w_dw_pad, b_dw=b_dw[l],
                gmat=gmat, g_cn=g_cnorm[l], b_cn=b_cnorm[l], w_out=w_out[l].astype(BF16),
                g_pre_ffn=g_pre_ffn[l], g_post_ffn=g_post_ffn[l], w_router=w_router_pad, b_router=b_router_pad,
                w_gu=w_gu[l], b_gu=b_gu[l], w_down=w_down[l], b_down=b_down[l])


def kernel(x_prompt, x_sample, cache_k, cache_v, state_conv, c_prompt, c_sample, w_ada, b_ada, g_pre_mix, g_post_mix, w_in, lambda_q1, lambda_k1, lambda_q2, lambda_k2, g_subln, w_dw, b_dw, g_cnorm, b_cnorm, w_out, g_pre_ffn, g_post_ffn, w_router, b_router, w_gu, b_gu, w_down, b_down):
    depth = w_ada.shape[0]
    Bp, Lp, _ = x_prompt.shape
    Bs, Ls, _ = x_sample.shape
    yp, ys = x_prompt, x_sample
    outs = [[] for _ in range(6)]
    for l in range(depth):
        p = _prepare_params(l, w_ada, b_ada, g_pre_mix, g_post_mix, w_in, lambda_q1, lambda_k1, lambda_q2,
                            lambda_k2, g_subln, w_dw, b_dw, g_cnorm, b_cnorm, w_out, g_pre_ffn, g_post_ffn,
                            w_router, b_router, w_gu, b_gu, w_down, b_down)
        lam_init = 0.8 - 0.6 * math.exp(-0.3 * l)
        mod = _ada(jnp.concatenate([c_prompt, c_sample], axis=0), p["w_ada"], p["b_ada"])
        mod = mod.reshape(Bp + Bs, N_MOD, D_MODEL)
        buf_p = jnp.zeros((Bp, CONV_HALO, CONV_CH), F32)
        buf_s = jnp.pad(state_conv[l], ((0, 0), (CONV_HALO - (CONV_W - 1), 0), (0, 0)))
        tm_p = min(Lp, 512)
        zero_counts = jnp.zeros((SUBLANES, LANES), F32)
        x1p, h2p, rtp, cnt_p, kp, vp, cp = _mix(yp, mod[:Bp], buf_p, None, None, p, lam_init, zero_counts,
                                                tm=tm_p, tq=tm_p)
        x1s, h2s, rts, cnt_s, ks, vs, cs = _mix(ys, mod[Bp:], buf_s, cache_k[l], cache_v[l], p, lam_init,
                                                cnt_p[Bp - 1], tm=Ls, tq=Ls)
        assert Bp % 2 == 0
        hb = Bp // 2
        n_a = cnt_p[hb - 1, 0, :N_EXPERTS].astype(I32)
        n_b = cnt_s[Bs - 1, 0, :N_EXPERTS].astype(I32) - n_a
        y_a = _moe_rows((h2p[:hb],), (rtp[:hb],), n_a, jnp.zeros_like(n_a), p, EXPERT_ROWS)
        y_b = _moe_rows((h2p[hb:], h2s), (rtp[hb:], rts), n_b, n_a, p, EXPERT_ROWS)
        t_a, t_b = hb * Lp, hb * Lp + Bs * Ls
        tm_c = min(tm_p, 256)
        yp = _combine(y_a, rtp, x1p, mod[:Bp], p["g_post_ffn"], tm_c, t_a, 0, 0, hb)
        yp = _combine(y_b, rtp, x1p, mod[:Bp], p["g_post_ffn"], tm_c, t_b, 0, hb, hb, out_prev=yp)
        ys = _combine(y_b, rts, x1s, mod[Bp:], p["g_post_ffn"], Ls, t_b, hb * Lp)
        for lst, val in zip(outs, (kp, vp, cp, ks, vs, cs)):
            lst.append(val)
    return (yp, ys) + tuple(jnp.stack(o) for o in outs)
```

```python
import functools
import math

import jax
import jax.numpy as jnp
from jax import lax
from jax.experimental import pallas as pl
from jax.experimental.pallas import tpu as pltpu
from jax.experimental.pallas import tpu_sc as plsc

F32 = jnp.float32
BF16 = jnp.bfloat16
I32 = jnp.int32

D_MODEL = 1024
CHUNK = 64
CHUNK_SHIFT = 6
N_HEADS = 4
V_DIM = 128
HEAD_DIM = 64
QK_W = 512
ATTN_W = 512
CONV_CH = 512
CONV_W = 31
CONV_GROUPS = 8
GROUP_CH = CONV_CH // CONV_GROUPS
IN_W = 2 * QK_W + ATTN_W + 2 * CONV_CH
N_MOD = 6
N_EXPERTS = 32
TOP_K = 4
D_FF = 1024
SWIGLU_ALPHA = 1.702
SWIGLU_LIMIT = 7.0
EPS = 1e-6

LANES = 128
SUBLANES = 8
CONV_HALO = 32
CONV_ROWS = 32
NEG = -1e30
LOG2E = math.log2(math.e)
ATTN_ROWS = 128
EXPERT_ROWS = 512
VMEM_LIMIT = 48 * 1024 * 1024
EXPERT_VMEM_LIMIT = 56 * 1024 * 1024


def _sigmoid(x):
    return 1.0 / (1.0 + jnp.exp(-x))


def _split_bf16(x):
    hi = x.astype(BF16)
    lo = (x - hi.astype(F32)).astype(BF16)
    return hi, lo


def _dot(a, b):
    return jnp.dot(a, b, preferred_element_type=F32)


def _dot3(a, b):
    ah, al = _split_bf16(a)
    bh, bl = _split_bf16(b)
    return _dot(ah, bh) + _dot(ah, bl) + _dot(al, bh)


def _rms(x):
    return x * lax.rsqrt(jnp.mean(x * x, axis=-1, keepdims=True) + EPS)


HALF = D_MODEL // 2
HIGH16 = -65536


def _pack_halves(x):
    lo = lax.bitcast_convert_type(x[:, :HALF].astype(BF16).astype(F32), I32)
    hi = lax.bitcast_convert_type(x[:, HALF:].astype(BF16).astype(F32), I32)
    return jnp.bitwise_or(jnp.bitwise_and(jnp.right_shift(lo, 16), 0xFFFF), jnp.bitwise_and(hi, HIGH16))


def _unpack_halves(w):
    lo = lax.bitcast_convert_type(jnp.left_shift(w, 16), F32)
    hi = lax.bitcast_convert_type(jnp.bitwise_and(w, HIGH16), F32)
    return lo, hi


def _ada_kernel(c_ref, w_ref, b_ref, o_ref):
    c = c_ref[...]
    o_ref[...] = _dot3(c * _sigmoid(c), w_ref[...]) + b_ref[...]


def _ada(c, w_ada, b_ada):
    n = c.shape[0]
    return pl.pallas_call(
        _ada_kernel,
        grid=(N_MOD,),
        in_specs=[pl.BlockSpec((n, D_MODEL), lambda j: (0, 0)),
                  pl.BlockSpec((D_MODEL, D_MODEL), lambda j: (0, j)),
                  pl.BlockSpec((1, D_MODEL), lambda j: (0, j))],
        out_specs=pl.BlockSpec((n, D_MODEL), lambda j: (0, j)),
        out_shape=jax.ShapeDtypeStruct((n, N_MOD * D_MODEL), F32),
        compiler_params=pltpu.CompilerParams(dimension_semantics=("arbitrary",), vmem_limit_bytes=VMEM_LIMIT),
        name="ada",
    )(c, w_ada, b_ada.reshape(1, N_MOD * D_MODEL))


def _premix_kernel(x_ref, mod_ref, g_ref, w_ref, buf_ref, wdw_ref, bdw_ref, gmat_ref, gcn_ref, bcn_ref,
                   q_ref, kb_ref, vb_ref, kf_hbm, vf_hbm, co_ref, st_ref, ext_ref, sh_ref, y_ref, kv_buf, kv_sem,
                   *, tm):
    batch = pl.program_id(0)
    s = pl.program_id(1)
    step = batch * pl.num_programs(1) + s
    last = pl.num_programs(0) * pl.num_programs(1) - 1
    slot = step % 2

    def kv_copies(slot):
        return [pltpu.make_async_copy(kv_buf.at[slot, i, :, pl.ds(h * V_DIM, V_DIM)],
                                      out.at[batch, pl.ds(s * tm, tm), h, :], kv_sem.at[slot, i, h])
                for i, out in enumerate((kf_hbm, vf_hbm)) for h in range(N_HEADS)]

    @pl.when(step >= 2)
    def _():
        for c in kv_copies(slot):
            c.wait()

    @pl.when(s == 0)
    def _():
        ext_ref[0:CONV_HALO, :] = buf_ref[0]

    x = x_ref[0]
    shift = mod_ref[0, 0:1, :]
    scale = mod_ref[0, 1:2, :]
    h = _rms(x) * g_ref[...] * (1.0 + scale) + shift
    hb = h.astype(BF16)

    u0 = 2 * QK_W + ATTN_W
    val = _dot(hb, w_ref[:, u0:u0 + CONV_CH])
    gate = _dot(hb, w_ref[:, u0 + CONV_CH:u0 + 2 * CONV_CH])
    ext_ref[CONV_HALO:CONV_HALO + tm, :] = val * _sigmoid(gate)

    off = CONV_HALO - (CONV_W - 1)
    span = tm + CONV_HALO - SUBLANES
    for b in range(1, SUBLANES):
        sh_ref[b - 1, 0:span, :] = ext_ref[b:b + span, :]
    for c in range(tm // CONV_ROWS):
        r0 = c * CONV_ROWS
        acc = jnp.zeros((CONV_ROWS, CONV_CH), F32)
        for j in range(CONV_W):
            b = (j + off) % SUBLANES
            a = r0 + j + off - b
            rows_j = ext_ref[a:a + CONV_ROWS, :] if b == 0 else sh_ref[b - 1, a:a + CONV_ROWS, :]
            acc = acc + wdw_ref[j:j + 1, :] * rows_j
        y_ref[r0:r0 + CONV_ROWS, :] = acc + bdw_ref[...]

    zq = _dot(hb, w_ref[:, 0:QK_W])
    q_ref[0] = (zq * (HEAD_DIM ** -0.5 * LOG2E)).astype(BF16)
    zk = _dot(hb, w_ref[:, QK_W:2 * QK_W])
    kv_buf[slot, 0] = zk
    kb_ref[0] = zk.astype(BF16)
    zv = _dot(hb, w_ref[:, 2 * QK_W:2 * QK_W + ATTN_W])
    kv_buf[slot, 1] = zv
    vb_ref[0] = zv.astype(BF16)

    y = y_ref[...]
    gm = gmat_ref[...]
    yh, yl = _split_bf16(y)
    mu = (_dot(yh, gm) + _dot(yl, gm)) * (1.0 / GROUP_CH)
    d = y - mu
    dh, dl = _split_bf16(d * d)
    var = (_dot(dh, gm) + _dot(dl, gm)) * (1.0 / GROUP_CH)
    yn = d * lax.rsqrt(var + EPS) * gcn_ref[...] + bcn_ref[...]
    co_ref[0] = (yn * _sigmoid(yn)).astype(BF16)

    tail = ext_ref[tm:tm + CONV_HALO, :]
    st_ref[0] = tail
    ext_ref[0:CONV_HALO, :] = tail

    for c in kv_copies(slot):
        c.start()

    @pl.when(step == last)
    def _():
        for c in kv_copies(slot):
            c.wait()

    @pl.when(jnp.logical_and(step == last, step >= 1))
    def _():
        for c in kv_copies(1 - slot):
            c.wait()


def _premix(x, mod, g_pre_mix, w_in_bf, buf, w_dw, b_dw, gmat, g_cn, b_cn, tm):
    B, L, _ = x.shape
    assert L % tm == 0 and tm % CONV_ROWS == 0
    row = lambda b, s: (b, s, 0)
    const2 = lambda b, s: (0, 0)
    bf_tile = jax.ShapeDtypeStruct((B, L, QK_W), BF16)
    f_heads = jax.ShapeDtypeStruct((B, L, N_HEADS, V_DIM), F32)
    return pl.pallas_call(
        functools.partial(_premix_kernel, tm=tm),
        grid=(B, L // tm),
        in_specs=[pl.BlockSpec((1, tm, D_MODEL), row),
                  pl.BlockSpec((1, N_MOD, D_MODEL), lambda b, s: (b, 0, 0)),
                  pl.BlockSpec((1, D_MODEL), const2),
                  pl.BlockSpec((D_MODEL, IN_W), const2),
                  pl.BlockSpec((1, CONV_HALO, CONV_CH), lambda b, s: (b, 0, 0)),
                  pl.BlockSpec((CONV_HALO, CONV_CH), const2),
                  pl.BlockSpec((1, CONV_CH), const2),
                  pl.BlockSpec((CONV_CH, CONV_CH), const2),
                  pl.BlockSpec((1, CONV_CH), const2),
                  pl.BlockSpec((1, CONV_CH), const2)],
        out_specs=[pl.BlockSpec((1, tm, QK_W), row)] * 3
                  + [pl.BlockSpec(memory_space=pl.ANY)] * 2
                  + [pl.BlockSpec((1, tm, QK_W), row),
                     pl.BlockSpec((1, CONV_HALO, CONV_CH), lambda b, s: (b, 0, 0))],
        out_shape=[bf_tile, bf_tile, bf_tile, f_heads, f_heads, bf_tile,
                   jax.ShapeDtypeStruct((B, CONV_HALO, CONV_CH), F32)],
        scratch_shapes=[pltpu.VMEM((CONV_HALO + tm, CONV_CH), F32),
                        pltpu.VMEM((SUBLANES - 1, CONV_HALO + tm, CONV_CH), F32),
                        pltpu.VMEM((tm, CONV_CH), F32),
                        pltpu.VMEM((2, 2, tm, QK_W), F32),
                        pltpu.SemaphoreType.DMA((2, 2, N_HEADS))],
        compiler_params=pltpu.CompilerParams(dimension_semantics=("arbitrary", "arbitrary"),
                                             vmem_limit_bytes=VMEM_LIMIT),
        name="premix",
    )(x, mod, g_pre_mix.reshape(1, D_MODEL), w_in_bf, buf, w_dw, b_dw.reshape(1, CONV_CH), gmat,
      g_cn.reshape(1, CONV_CH), b_cn.reshape(1, CONV_CH))


def _attn_kernel(slopes_ref, q_ref, k_ref, v_ref, kb_ref, mt_ref, lam_ref, gs_ref, o_ref,
                 ke1_ref, ke2_ref, ve_ref, m1_ref, a1_ref, m2_ref, a2_ref,
                 *, tq, tk, rows, q_offset, causal, lam_init):
    slope = slopes_ref[pl.program_id(1)] * LOG2E
    lq = q_ref.shape[1]
    lk = k_ref.shape[1]
    nq = lq // tq
    lane = lax.broadcasted_iota(I32, (1, V_DIM), 1)

    low = jnp.broadcast_to(lane < HEAD_DIM, (lk, V_DIM))
    k = k_ref[0]
    zero = jnp.zeros((lk, V_DIM), BF16)
    ke1_ref[...] = jnp.where(low, k, zero)
    ke2_ref[...] = jnp.where(low, zero, k)
    ve_ref[:, 0:V_DIM] = v_ref[0]
    ve_ref[:, V_DIM:2 * V_DIM] = jnp.broadcast_to(jnp.where(lane == 0, 1.0, 0.0), (lk, V_DIM)).astype(BF16)

    nt = (((1,), (1,)), ((), ()))
    maps = ((ke1_ref, m1_ref, a1_ref), (ke2_ref, m2_ref, a2_ref))

    def lanes(x, n):
        if n % LANES == 0:
            return jnp.concatenate([x] * (n // LANES), axis=1)
        return jnp.broadcast_to(x[:, 0:1], (rows, n))

    def tile(q0, k0, masked):
        ve = ve_ref[pl.ds(k0, tk), :]
        q_first = (lax.broadcasted_iota(I32, (1, 1), 0) + (q0 + q_offset)).astype(F32)
        bias = kb_ref[0, :, pl.ds(k0, tk)] - slope * q_first
        for ke_ref, m_ref, a_ref in maps:
            ke = ke_ref[pl.ds(k0, tk), :]
            for r0 in range(0, tq, rows):
                s = lax.dot_general(q_ref[0, pl.ds(q0 + r0, rows), :], ke, nt, preferred_element_type=F32)
                s = s + bias
                if masked:
                    s = s + mt_ref[0, r0:r0 + rows, :]
                m_old = m_ref[r0:r0 + rows, :]
                m_new = jnp.maximum(m_old, jnp.max(s, axis=-1, keepdims=True))
                p = jnp.exp2(s - lanes(m_new, tk))
                alpha = jnp.exp2(m_old - m_new)
                a_ref[r0:r0 + rows, :] = (lanes(alpha, 2 * V_DIM) * a_ref[r0:r0 + rows, :]
                                          + _dot(p.astype(BF16), ve))
                m_ref[r0:r0 + rows, :] = m_new

    def reset():
        for _, m_ref, a_ref in maps:
            m_ref[...] = jnp.full(m_ref.shape, NEG, F32)
            a_ref[...] = jnp.zeros(a_ref.shape, F32)

    lv = lam_ref[...]
    lam = (jnp.exp(jnp.sum(lv[0:1, :] * lv[1:2, :], axis=-1, keepdims=True))
           - jnp.exp(jnp.sum(lv[2:3, :] * lv[3:4, :], axis=-1, keepdims=True)) + lam_init)

    def finish(q0):
        a1 = a1_ref[...]
        a2 = a2_ref[...]
        o = a1[:, 0:V_DIM] / a1[:, V_DIM:V_DIM + 1] - lam * (a2[:, 0:V_DIM] / a2[:, V_DIM:V_DIM + 1])
        o_ref[0, pl.ds(q0, tq), :] = (_rms(o) * gs_ref[...] * (1.0 - lam_init)).astype(BF16)

    if not causal:
        reset()
        tile(0, 0, True)
        finish(0)
        return

    def q_tile(qi, carry):
        q0 = pl.multiple_of(qi * tq, tq)
        reset()

        def pair(i, c):
            tile(q0, pl.multiple_of(2 * i * tk, tk), False)
            tile(q0, pl.multiple_of((2 * i + 1) * tk, tk), False)
            return c

        lax.fori_loop(0, qi // 2, pair, 0)
        odd = qi % 2 == 1

        @pl.when(odd)
        def _():
            tile(q0, pl.multiple_of((qi - 1) * tk, tk), False)
            tile(q0, pl.multiple_of(qi * tk, tk), True)

        @pl.when(jnp.logical_not(odd))
        def _():
            tile(q0, pl.multiple_of(qi * tk, tk), True)

        finish(q0)
        return carry

    lax.fori_loop(0, nq, q_tile, 0)


def _mask_table(slopes, tq, tk, q_offset):
    qpos = q_offset + jnp.arange(tq, dtype=I32)[:, None]
    kpos = jnp.arange(tk, dtype=I32)[None, :]
    visible = jnp.right_shift(kpos, CHUNK_SHIFT) <= jnp.right_shift(qpos, CHUNK_SHIFT)
    fix = jnp.where(kpos > qpos, 2 * (qpos - kpos), 0).astype(F32)
    return jnp.where(visible[None], (slopes * LOG2E)[:, None, None] * fix[None], NEG)


def _attention(q, k, v, lam_rows, g_subln, lam_init, tq, tk, q_offset, causal):
    B, Lq, _ = q.shape
    Lk = k.shape[1]
    assert Lq % tq == 0 and Lk % tk == 0 and (not causal or (tq == tk and tq % CHUNK == 0 and q_offset == 0))
    assert causal or Lk == tk
    rows = min(tq, ATTN_ROWS)
    slopes = jnp.asarray([2.0 ** (-8.0 * (h + 1) / N_HEADS) for h in range(N_HEADS)], F32)
    table = _mask_table(slopes, tq, tk, q_offset)
    key_bias = ((slopes * LOG2E)[:, None] * jnp.arange(Lk, dtype=F32)[None, :]).reshape(N_HEADS, 1, Lk)
    key_ext = pltpu.VMEM((Lk, V_DIM), BF16)
    stat = pltpu.VMEM((tq, LANES), F32)
    acc = pltpu.VMEM((tq, 2 * V_DIM), F32)
    head = lambda b, h, sl: (b, 0, h)
    return pl.pallas_call(
        functools.partial(_attn_kernel, tq=tq, tk=tk, rows=rows, q_offset=q_offset, causal=causal,
                          lam_init=lam_init),
        grid_spec=pltpu.PrefetchScalarGridSpec(
            num_scalar_prefetch=1,
            grid=(B, N_HEADS),
            in_specs=[pl.BlockSpec((1, Lq, V_DIM), head),
                      pl.BlockSpec((1, Lk, V_DIM), head),
                      pl.BlockSpec((1, Lk, V_DIM), head),
                      pl.BlockSpec((1, 1, Lk), lambda b, h, sl: (h, 0, 0)),
                      pl.BlockSpec((1, tq, tk), lambda b, h, sl: (h, 0, 0)),
                      pl.BlockSpec((8, LANES), lambda b, h, sl: (0, 0)),
                      pl.BlockSpec((1, V_DIM), lambda b, h, sl: (0, 0))],
            out_specs=pl.BlockSpec((1, Lq, V_DIM), head),
            scratch_shapes=[key_ext, key_ext, pltpu.VMEM((Lk, 2 * V_DIM), BF16), stat, acc, stat, acc]),
        out_shape=jax.ShapeDtypeStruct((B, Lq, ATTN_W), BF16),
        compiler_params=pltpu.CompilerParams(dimension_semantics=("arbitrary",) * 2,
                                             vmem_limit_bytes=VMEM_LIMIT),
        name="attn",
    )(slopes, q, k, v, key_bias, table, lam_rows, g_subln.reshape(1, V_DIM))


def _postmix_kernel(a_ref, c_ref, x_ref, mod_ref, wo_ref, gpm_ref, gpf_ref, wr_ref, br_ref, cin_ref,
                    x1_ref, h2_ref, rt_ref, cnt_ref, carry_ref, *, tm):
    first = jnp.logical_and(pl.program_id(0) == 0, pl.program_id(1) == 0)

    @pl.when(first)
    def _():
        carry_ref[...] = cin_ref[...]

    mix = _dot(a_ref[0], wo_ref[0:ATTN_W, :]) + _dot(c_ref[0], wo_ref[ATTN_W:ATTN_W + CONV_CH, :])
    gate_a = mod_ref[0, 2:3, :]
    shift_f = mod_ref[0, 3:4, :]
    scale_f = mod_ref[0, 4:5, :]
    x1 = x_ref[0] + gate_a * (_rms(mix) * gpm_ref[...])
    x1_ref[0] = x1
    h2 = _rms(x1) * gpf_ref[...] * (1.0 + scale_f) + shift_f
    h2_ref[...] = _pack_halves(h2)

    logits = _dot3(h2, wr_ref[...]) + br_ref[...]
    lane = lax.broadcasted_iota(I32, (tm, LANES), 1).astype(F32)
    vals, idxs = [], []
    for _ in range(TOP_K):
        m = jnp.max(logits, axis=-1, keepdims=True)
        idx = jnp.min(jnp.where(logits == m, lane, float(LANES)), axis=-1, keepdims=True)
        vals.append(m)
        idxs.append(idx)
        logits = jnp.where(lane == idx, 2.0 * NEG, logits)
    es = [jnp.exp(v - vals[0]) for v in vals]
    denom = es[0] + es[1] + es[2] + es[3]

    onehot = jnp.zeros((tm, LANES), F32)
    for idx in idxs:
        onehot = jnp.where(lane == idx, 1.0, onehot)
    r_i = lax.broadcasted_iota(I32, (tm, tm), 0)
    c_i = lax.broadcasted_iota(I32, (tm, tm), 1)
    tri = jnp.where(c_i < r_i, 1.0, 0.0).astype(BF16)
    before = _dot(tri, onehot.astype(BF16)) + carry_ref[0:1, :]

    rt = jnp.zeros((tm, LANES), F32)
    for k in range(TOP_K):
        rank = jnp.sum(jnp.where(lane == idxs[k], before, 0.0), axis=-1, keepdims=True)
        rt = jnp.where(lane == k, idxs[k], rt)
        rt = jnp.where(lane == TOP_K + k, es[k] / denom, rt)
        rt = jnp.where(lane == 2 * TOP_K + k, rank, rt)
    rt_ref[0] = rt

    carry_ref[...] = carry_ref[...] + jnp.sum(onehot, axis=0, keepdims=True)
    cnt_ref[0] = carry_ref[...]


def _postmix_into_kernel(pool_ref, *refs, tm):
    del pool_ref
    _postmix_kernel(*refs, tm=tm)


def _postmix(attn, conv, x, mod, w_out_bf, g_post_mix, g_pre_ffn, w_router_pad, b_router_pad, counts_in, tm,
             pool, pool_rows, pool_first):
    B, L, _ = x.shape
    assert pool_first % tm == 0
    row = lambda b, s: (b, s, 0)
    const2 = lambda b, s: (0, 0)
    body, aliases, lead_specs, lead_args = _postmix_kernel, {}, [], ()
    if pool is not None:
        body, aliases = _postmix_into_kernel, {0: 1}
        lead_specs, lead_args = [pl.BlockSpec(memory_space=pl.ANY)], (pool,)
    return pl.pallas_call(
        functools.partial(body, tm=tm),
        grid=(B, L // tm),
        input_output_aliases=aliases,
        in_specs=lead_specs + [
                  pl.BlockSpec((1, tm, ATTN_W), row),
                  pl.BlockSpec((1, tm, CONV_CH), row),
                  pl.BlockSpec((1, tm, D_MODEL), row),
                  pl.BlockSpec((1, N_MOD, D_MODEL), lambda b, s: (b, 0, 0)),
                  pl.BlockSpec((D_MODEL, D_MODEL), const2),
                  pl.BlockSpec((1, D_MODEL), const2),
                  pl.BlockSpec((1, D_MODEL), const2),
                  pl.BlockSpec((D_MODEL, LANES), const2),
                  pl.BlockSpec((1, LANES), const2),
                  pl.BlockSpec((8, LANES), const2)],
        out_specs=[pl.BlockSpec((1, tm, D_MODEL), row),
                   pl.BlockSpec((tm, HALF), lambda b, s: (pool_first // tm + b * (L // tm) + s, 0)),
                   pl.BlockSpec((1, tm, LANES), row),
                   pl.BlockSpec((1, SUBLANES, LANES), lambda b, s: (b, 0, 0))],
        out_shape=[jax.ShapeDtypeStruct((B, L, D_MODEL), F32),
                   jax.ShapeDtypeStruct((pool_rows, HALF), I32),
                   jax.ShapeDtypeStruct((B, L, LANES), F32),
                   jax.ShapeDtypeStruct((B, SUBLANES, LANES), F32)],
        scratch_shapes=[pltpu.VMEM((8, LANES), F32)],
        compiler_params=pltpu.CompilerParams(dimension_semantics=("arbitrary", "arbitrary"),
                                             vmem_limit_bytes=VMEM_LIMIT),
        name="postmix",
    )(*lead_args, attn, conv, x, mod, w_out_bf, g_post_mix.reshape(1, D_MODEL), g_pre_ffn.reshape(1, D_MODEL),
      w_router_pad, b_router_pad, counts_in)


def _expert_kernel(be_ref, nv_ref, x_ref, wgu_ref, bgu_ref, wd_ref, bd_ref, y_ref, wgu_bf, wd_bf):
    i = pl.program_id(0)
    ff_half = D_FF // 2

    @pl.when(jnp.logical_or(i == 0, be_ref[i] != be_ref[jnp.maximum(i - 1, 0)]))
    def _():
        wgu_bf[...] = wgu_ref[0].astype(BF16)
        wd_bf[...] = wd_ref[0].astype(BF16)

    @pl.when(nv_ref[i] > 0)
    def _():
        x_lo, x_hi = _unpack_halves(x_ref[...])
        x_lo = x_lo.astype(BF16)
        x_hi = x_hi.astype(BF16)
        acc = None
        for c in range(2):
            lo, hi = c * ff_half, (c + 1) * ff_half
            g = (_dot(x_lo, wgu_bf[0:HALF, lo:hi]) + _dot(x_hi, wgu_bf[HALF:D_MODEL, lo:hi])
                 + bgu_ref[0, :, lo:hi])
            lin = (_dot(x_lo, wgu_bf[0:HALF, D_FF + lo:D_FF + hi]) + _dot(x_hi, wgu_bf[HALF:D_MODEL, D_FF + lo:D_FF + hi])
                   + bgu_ref[0, :, D_FF + lo:D_FF + hi])
            g = jnp.minimum(g, SWIGLU_LIMIT)
            lin = jnp.clip(lin, -SWIGLU_LIMIT, SWIGLU_LIMIT)
            act = g * _sigmoid(SWIGLU_ALPHA * g) * (lin + 1.0)
            part = _dot(act.astype(BF16), wd_bf[lo:hi, :])
            acc = part if acc is None else acc + part
        y_ref[...] = _pack_halves(acc + bd_ref[0])


def _experts(x_pad, block_e, n_valid, w_gu, b_gu, w_down, b_down, blk):
    n_blocks = block_e.shape[0]
    ex = lambda i, be, nv: (be[i], 0, 0)
    rows = lambda i, be, nv: (i, 0)
    return pl.pallas_call(
        _expert_kernel,
        grid_spec=pltpu.PrefetchScalarGridSpec(
            num_scalar_prefetch=2,
            grid=(n_blocks,),
            in_specs=[pl.BlockSpec((blk, HALF), rows),
                      pl.BlockSpec((1, D_MODEL, 2 * D_FF), ex), pl.BlockSpec((1, 1, 2 * D_FF), ex),
                      pl.BlockSpec((1, D_FF, D_MODEL), ex), pl.BlockSpec((1, 1, D_MODEL), ex)],
            out_specs=pl.BlockSpec((blk, HALF), rows),
            scratch_shapes=[pltpu.VMEM((D_MODEL, 2 * D_FF), BF16), pltpu.VMEM((D_FF, D_MODEL), BF16)]),
        out_shape=jax.ShapeDtypeStruct((n_blocks * blk, HALF), I32),
        compiler_params=pltpu.CompilerParams(dimension_semantics=("arbitrary",),
                                             vmem_limit_bytes=EXPERT_VMEM_LIMIT),
        name="experts",
    )(block_e, n_valid, x_pad, w_gu, b_gu.reshape(N_EXPERTS, 1, 2 * D_FF), w_down,
      b_down.reshape(N_EXPERTS, 1, D_MODEL))


def _combine_kernel(y0_ref, y1_ref, y2_ref, y3_ref, rt_ref, x1_ref, mod_ref, g_ref, o_ref):
    rt = rt_ref[0]
    f_lo = f_hi = None
    for k, y_ref in enumerate((y0_ref, y1_ref, y2_ref, y3_ref)):
        lo, hi = _unpack_halves(y_ref[...])
        w = rt[:, TOP_K + k:TOP_K + k + 1]
        f_lo = w * lo if f_lo is None else f_lo + w * lo
        f_hi = w * hi if f_hi is None else f_hi + w * hi
    ms = (jnp.sum(f_lo * f_lo, axis=-1, keepdims=True) + jnp.sum(f_hi * f_hi, axis=-1, keepdims=True)) / D_MODEL
    r = lax.rsqrt(ms + EPS)
    gate_f = mod_ref[0, 5:6, :]
    g = g_ref[...]
    o_ref[0, :, 0:HALF] = x1_ref[0, :, 0:HALF] + gate_f[:, 0:HALF] * (f_lo * r * g[:, 0:HALF])
    o_ref[0, :, HALF:D_MODEL] = x1_ref[0, :, HALF:D_MODEL] + gate_f[:, HALF:D_MODEL] * (f_hi * r * g[:, HALF:D_MODEL])


def _combine_into_kernel(prev_ref, *refs):
    del prev_ref
    _combine_kernel(*refs)


def _combine(y, route, x1, mod, g_post_ffn, tm, t_all, t_first, b_first=0, n_batch=None, out_prev=None):
    B, L, _ = x1.shape
    n_batch = B if n_batch is None else n_batch
    assert t_all % tm == 0 and t_first % tm == 0
    row = lambda b, s: (b + b_first, s, 0)
    y_specs = [pl.BlockSpec((tm, HALF), lambda b, s, k=k: ((k * t_all + t_first) // tm + b * (L // tm) + s, 0))
               for k in range(TOP_K)]
    in_specs = y_specs + [pl.BlockSpec((1, tm, LANES), row),
                          pl.BlockSpec((1, tm, D_MODEL), row),
                          pl.BlockSpec((1, N_MOD, D_MODEL), lambda b, s: (b + b_first, 0, 0)),
                          pl.BlockSpec((1, D_MODEL), lambda b, s: (0, 0))]
    args = (y, y, y, y, route, x1, mod, g_post_ffn.reshape(1, D_MODEL))
    body, aliases = _combine_kernel, {}
    if out_prev is not None:
        body, aliases = _combine_into_kernel, {0: 0}
        in_specs = [pl.BlockSpec(memory_space=pl.ANY)] + in_specs
        args = (out_prev,) + args
    return pl.pallas_call(
        body,
        grid=(n_batch, L // tm),
        in_specs=in_specs,
        out_specs=pl.BlockSpec((1, tm, D_MODEL), row),
        out_shape=jax.ShapeDtypeStruct((B, L, D_MODEL), F32),
        input_output_aliases=aliases,
        compiler_params=pltpu.CompilerParams(dimension_semantics=("arbitrary", "arbitrary"),
                                             vmem_limit_bytes=VMEM_LIMIT),
        name="combine",
    )(*args)


SC_GATHER_BYTES = 128 * 1024


def _scatter_quantum(width):
    rows = SC_GATHER_BYTES // (width * 4)
    sc = plsc.get_sparse_core_info()
    return rows, sc.num_cores * sc.num_subcores * rows * 2


def _scatter_rows(table, row_first, dest, n_slots):
    n, width = dest.shape[0], table.shape[1]
    rows, quantum = _scatter_quantum(width)
    sc = plsc.get_sparse_core_info()
    workers = sc.num_cores * sc.num_subcores
    n_pad = -(-n // quantum) * quantum
    extra = n_pad - n
    assert row_first % rows == 0 and row_first + n_pad <= table.shape[0]
    spare = n_slots + jnp.arange(extra * TOP_K, dtype=I32).reshape(extra, TOP_K)
    per_w = n_pad // workers
    wins = per_w // rows
    wins_tile = -(-wins // SUBLANES) * SUBLANES
    idx = jnp.concatenate([dest, spare], axis=0).T.reshape(TOP_K, workers, wins, rows)
    idx = jnp.pad(idx, ((0, 0), (0, 0), (0, wins_tile - wins), (0, 0))).reshape(TOP_K, workers * wins_tile, rows)
    mesh = plsc.VectorSubcoreMesh(core_axis_name="c", subcore_axis_name="s")
    buf = pltpu.VMEM((rows, width), table.dtype)

    @functools.partial(
        pl.kernel, mesh=mesh,
        out_type=jax.ShapeDtypeStruct((n_slots + extra * TOP_K, width), table.dtype),
        scratch_types=[pltpu.VMEM((TOP_K, wins_tile, rows), I32), buf, buf,
                       pltpu.SemaphoreType.DMA, pltpu.SemaphoreType.DMA],
        name="scatter_rows",
    )
    def scatter_kernel(table_hbm, idx_hbm, out_hbm, idx_v, buf0, buf1, sem0, sem1):
        w = lax.axis_index("s") * sc.num_cores + lax.axis_index("c")
        first = pl.multiple_of(w * wins_tile, SUBLANES)
        for k in range(TOP_K):
            pltpu.sync_copy(idx_hbm.at[k, pl.ds(first, wins_tile)], idx_v.at[k])
        base = pl.multiple_of(row_first + w * per_w, rows)

        def window(i, buf, sem):
            pltpu.sync_copy(table_hbm.at[pl.ds(base + i * rows, rows)], buf)
            copies = [pltpu.async_copy(buf, out_hbm.at[idx_v.at[k, i]], sem) for k in range(TOP_K)]
            return copies

        @pl.loop(0, wins, step=2)
        def _(i):
            c0 = window(i, buf0, sem0)
            c1 = window(i + 1, buf1, sem1)
            for c in c0 + c1:
                c.wait()

    return scatter_kernel(table, idx)


def _gather_rows(table, idx):
    n = idx.shape[0]
    width = table.shape[1]
    rows = SC_GATHER_BYTES // (width * 4)
    sc = plsc.get_sparse_core_info()
    workers = sc.num_cores * sc.num_subcores
    quantum = workers * rows * 2
    n_pad = -(-n // quantum) * quantum
    filler = jnp.arange(n_pad - n, dtype=I32) % table.shape[0]
    per_w = n_pad // workers
    wins = per_w // rows
    wins_tile = -(-wins // SUBLANES) * SUBLANES
    idx = jnp.concatenate([idx, filler]).reshape(workers, wins, rows)
    idx = jnp.pad(idx, ((0, 0), (0, wins_tile - wins), (0, 0))).reshape(workers * wins_tile, rows)
    mesh = plsc.VectorSubcoreMesh(core_axis_name="c", subcore_axis_name="s")
    buf = pltpu.VMEM((rows, width), table.dtype)

    @functools.partial(
        pl.kernel, mesh=mesh,
        out_type=jax.ShapeDtypeStruct((n_pad, width), table.dtype),
        scratch_types=[pltpu.VMEM((wins_tile, rows), I32), buf, buf,
                       pltpu.SemaphoreType.DMA, pltpu.SemaphoreType.DMA],
        name="gather_rows",
    )
    def gather_kernel(table_hbm, idx_hbm, out_hbm, idx_v, buf0, buf1, sem0, sem1):
        w = lax.axis_index("s") * sc.num_cores + lax.axis_index("c")
        pltpu.sync_copy(idx_hbm.at[pl.ds(pl.multiple_of(w * wins_tile, SUBLANES), wins_tile)], idx_v)
        base = pl.multiple_of(w * per_w, rows)

        @pl.loop(0, wins, step=2)
        def _(i):
            g0 = pltpu.async_copy(table_hbm.at[idx_v.at[i]], buf0, sem0)
            g1 = pltpu.async_copy(table_hbm.at[idx_v.at[i + 1]], buf1, sem1)
            g0.wait()
            pltpu.sync_copy(buf0, out_hbm.at[pl.ds(base + i * rows, rows)])
            g1.wait()
            pltpu.sync_copy(buf1, out_hbm.at[pl.ds(base + (i + 1) * rows, rows)])

    return gather_kernel(table, idx)


def _routing_tables(route, counts, counts_before, blk):
    T = route.shape[0]
    TK = T * TOP_K
    idx = route[:, 0:TOP_K].astype(I32)
    rank = route[:, 2 * TOP_K:3 * TOP_K].astype(I32) - counts_before[idx]
    padded = (counts + blk - 1) // blk * blk
    pad_end = jnp.cumsum(padded)
    pad_start = pad_end - padded
    dest = pad_start[idx] + rank
    n_blocks = -(-TK // blk) + N_EXPERTS
    blk_start = jnp.arange(n_blocks, dtype=I32) * blk
    block_e = jnp.sum((blk_start[:, None] >= pad_end[None, :]).astype(I32), axis=1)
    block_e = jnp.minimum(block_e, N_EXPERTS - 1)
    last = (pad_start + counts)[block_e]
    n_valid = jnp.clip(last - blk_start, 0, blk)
    return block_e, n_valid, dest


def _mix(x, mod, buf, past_k, past_v, p, lam_init, counts_in, tm, tq, pool, pool_rows, pool_first):
    B, L, _ = x.shape
    q, kb, vb, kf, vf, conv, state = _premix(x, mod, p["g_pre_mix"], p["w_in"], buf, p["w_dw"], p["b_dw"],
                                             p["gmat"], p["g_cn"], p["b_cn"], tm)
    if past_k is None:
        attn = _attention(q, kb, vb, p["lam_rows"], p["g_subln"], lam_init, tq, tq, 0, True)
    else:
        P = past_k.shape[1]
        keys = jnp.concatenate([past_k.reshape(B, P, QK_W).astype(BF16), kb], axis=1)
        vals = jnp.concatenate([past_v.reshape(B, P, ATTN_W).astype(BF16), vb], axis=1)
        attn = _attention(q, keys, vals, p["lam_rows"], p["g_subln"], lam_init, L, P + L, P, False)
    x1, h2, route, cnt = _postmix(attn, conv, x, mod, p["w_out"], p["g_post_mix"], p["g_pre_ffn"],
                                  p["w_router"], p["b_router"], counts_in, tm, pool, pool_rows, pool_first)
    return x1, h2, route, cnt, kf, vf, state[:, CONV_HALO - (CONV_W - 1):, :]


def _moe_rows(pool, row_first, route_groups, counts, counts_before, p, blk):
    route = jnp.concatenate([r.reshape(-1, LANES) for r in route_groups], axis=0)
    T = route.shape[0]
    block_e, n_valid, dest = _routing_tables(route, counts, counts_before, blk)
    x_pad = _scatter_rows(pool, row_first, dest, block_e.shape[0] * blk)
    y_pad = _experts(x_pad, block_e, n_valid, p["w_gu"], p["b_gu"], p["w_down"], p["b_down"], blk)
    return _gather_rows(y_pad, dest.T.reshape(T * TOP_K))


def _prepare_params(l, w_ada, b_ada, g_pre_mix, g_post_mix, w_in, lambda_q1, lambda_k1, lambda_q2, lambda_k2,
                    g_subln, w_dw, b_dw, g_cnorm, b_cnorm, w_out, g_pre_ffn, g_post_ffn,
                    w_router, b_router, w_gu, b_gu, w_down, b_down):
    lam_rows = jnp.zeros((8, LANES), F32)
    for r, vec in enumerate((lambda_q1[l], lambda_k1[l], lambda_q2[l], lambda_k2[l])):
        lam_rows = lam_rows.at[r, :HEAD_DIM].set(vec)
    ch = jnp.arange(CONV_CH, dtype=I32) // GROUP_CH
    gmat = (ch[:, None] == ch[None, :]).astype(BF16)
    w_dw_pad = jnp.zeros((CONV_HALO, CONV_CH), F32).at[:CONV_W].set(w_dw[l])
    w_router_pad = jnp.zeros((D_MODEL, LANES), F32).at[:, :N_EXPERTS].set(w_router[l])
    b_router_pad = jnp.full((1, LANES), NEG, F32).at[0, :N_EXPERTS].set(b_router[l])
    return dict(w_ada=w_ada[l], b_ada=b_ada[l], g_pre_mix=g_pre_mix[l], g_post_mix=g_post_mix[l],
                w_in=w_in[l].astype(BF16), lam_rows=lam_rows, g_subln=g_subln[l], w_dw=w_dw_pad, b_dw=b_dw[l],
                gmat=gmat, g_cn=g_cnorm[l], b_cn=b_cnorm[l], w_out=w_out[l].astype(BF16),
                g_pre_ffn=g_pre_ffn[l], g_post_ffn=g_post_ffn[l], w_router=w_router_pad, b_router=b_router_pad,
                w_gu=w_gu[l], b_gu=b_gu[l], w_down=w_down[l], b_down=b_down[l])


def kernel(x_prompt, x_sample, cache_k, cache_v, state_conv, c_prompt, c_sample, w_ada, b_ada, g_pre_mix, g_post_mix, w_in, lambda_q1, lambda_k1, lambda_q2, lambda_k2, g_subln, w_dw, b_dw, g_cnorm, b_cnorm, w_out, g_pre_ffn, g_post_ffn, w_router, b_router, w_gu, b_gu, w_down, b_down):
    depth = w_ada.shape[0]
    Bp, Lp, _ = x_prompt.shape
    Bs, Ls, _ = x_sample.shape
    yp, ys = x_prompt, x_sample
    outs = [[] for _ in range(6)]
    for l in range(depth):
        p = _prepare_params(l, w_ada, b_ada, g_pre_mix, g_post_mix, w_in, lambda_q1, lambda_k1, lambda_q2,
                            lambda_k2, g_subln, w_dw, b_dw, g_cnorm, b_cnorm, w_out, g_pre_ffn, g_post_ffn,
                            w_router, b_router, w_gu, b_gu, w_down, b_down)
        lam_init = 0.8 - 0.6 * math.exp(-0.3 * l)
        mod = _ada(jnp.concatenate([c_prompt, c_sample], axis=0), p["w_ada"], p["b_ada"])
        mod = mod.reshape(Bp + Bs, N_MOD, D_MODEL)
        buf_p = jnp.zeros((Bp, CONV_HALO, CONV_CH), F32)
        buf_s = jnp.pad(state_conv[l], ((0, 0), (CONV_HALO - (CONV_W - 1), 0), (0, 0)))
        tm_p = min(Lp, 512)
        assert Bp % 2 == 0
        hb = Bp // 2
        t_a, t_b = hb * Lp, hb * Lp + Bs * Ls
        quantum = _scatter_quantum(HALF)[1]
        pool_rows = t_a + -(-t_b // quantum) * quantum
        zero_counts = jnp.zeros((SUBLANES, LANES), F32)
        x1p, pool, rtp, cnt_p, kp, vp, cp = _mix(yp, mod[:Bp], buf_p, None, None, p, lam_init, zero_counts,
                                                 tm_p, tm_p, None, pool_rows, 0)
        x1s, pool, rts, cnt_s, ks, vs, cs = _mix(ys, mod[Bp:], buf_s, cache_k[l], cache_v[l], p, lam_init,
                                                 cnt_p[Bp - 1], Ls, Ls, pool, pool_rows, Bp * Lp)
        n_a = cnt_p[hb - 1, 0, :N_EXPERTS].astype(I32)
        n_b = cnt_s[Bs - 1, 0, :N_EXPERTS].astype(I32) - n_a
        y_a = _moe_rows(pool, 0, (rtp[:hb],), n_a, jnp.zeros_like(n_a), p, EXPERT_ROWS)
        y_b = _moe_rows(pool, t_a, (rtp[hb:], rts), n_b, n_a, p, EXPERT_ROWS)
        tm_c = min(tm_p, 256)
        yp = _combine(y_a, rtp, x1p, mod[:Bp], p["g_post_ffn"], tm_c, t_a, 0, 0, hb)
        yp = _combine(y_b, rtp, x1p, mod[:Bp], p["g_post_ffn"], tm_c, t_b, 0, hb, hb, out_prev=yp)
        ys = _combine(y_b, rts, x1s, mod[Bp:], p["g_post_ffn"], Ls, t_b, hb * Lp)
        for lst, val in zip(outs, (kp, vp, cp, ks, vs, cs)):
            lst.append(val)
    return (yp, ys) + tuple(jnp.stack(o) for o in outs)
```

```python
import functools
import math

import jax
import jax.numpy as jnp
from jax import lax
from jax.experimental import pallas as pl
from jax.experimental.pallas import tpu as pltpu
from jax.experimental.pallas import tpu_sc as plsc

F32 = jnp.float32
BF16 = jnp.bfloat16
I32 = jnp.int32

D_MODEL = 1024
CHUNK = 64
CHUNK_SHIFT = 6
N_HEADS = 4
V_DIM = 128
HEAD_DIM = 64
QK_W = 512
ATTN_W = 512
CONV_CH = 512
CONV_W = 31
CONV_GROUPS = 8
GROUP_CH = CONV_CH // CONV_GROUPS
IN_W = 2 * QK_W + ATTN_W + 2 * CONV_CH
N_MOD = 6
N_EXPERTS = 32
TOP_K = 4
D_FF = 1024
SWIGLU_ALPHA = 1.702
SWIGLU_LIMIT = 7.0
EPS = 1e-6

LANES = 128
SUBLANES = 8
CONV_HALO = 32
CONV_ROWS = 32
NEG = -1e30
LOG2E = math.log2(math.e)
ATTN_ROWS = 128
EXPERT_ROWS = 512
VMEM_LIMIT = 48 * 1024 * 1024
EXPERT_VMEM_LIMIT = 56 * 1024 * 1024


def _sigmoid(x):
    return 1.0 / (1.0 + jnp.exp(-x))


def _split_bf16(x):
    hi = x.astype(BF16)
    lo = (x - hi.astype(F32)).astype(BF16)
    return hi, lo


def _dot(a, b):
    return jnp.dot(a, b, preferred_element_type=F32)


def _dot3(a, b):
    ah, al = _split_bf16(a)
    bh, bl = _split_bf16(b)
    return _dot(ah, bh) + _dot(ah, bl) + _dot(al, bh)


def _rms(x):
    return x * lax.rsqrt(jnp.mean(x * x, axis=-1, keepdims=True) + EPS)


HALF = D_MODEL // 2
HIGH16 = -65536


def _pack_halves(x):
    lo = lax.bitcast_convert_type(x[:, :HALF].astype(BF16).astype(F32), I32)
    hi = lax.bitcast_convert_type(x[:, HALF:].astype(BF16).astype(F32), I32)
    return jnp.bitwise_or(jnp.bitwise_and(jnp.right_shift(lo, 16), 0xFFFF), jnp.bitwise_and(hi, HIGH16))


def _unpack_halves(w):
    lo = lax.bitcast_convert_type(jnp.left_shift(w, 16), F32)
    hi = lax.bitcast_convert_type(jnp.bitwise_and(w, HIGH16), F32)
    return lo, hi


def _ada_kernel(c_ref, w_ref, b_ref, o_ref):
    c = c_ref[...]
    o_ref[...] = _dot3(c * _sigmoid(c), w_ref[...]) + b_ref[...]


def _ada(c, w_ada, b_ada):
    n = c.shape[0]
    return pl.pallas_call(
        _ada_kernel,
        grid=(N_MOD,),
        in_specs=[pl.BlockSpec((n, D_MODEL), lambda j: (0, 0)),
                  pl.BlockSpec((D_MODEL, D_MODEL), lambda j: (0, j)),
                  pl.BlockSpec((1, D_MODEL), lambda j: (0, j))],
        out_specs=pl.BlockSpec((n, D_MODEL), lambda j: (0, j)),
        out_shape=jax.ShapeDtypeStruct((n, N_MOD * D_MODEL), F32),
        compiler_params=pltpu.CompilerParams(dimension_semantics=("arbitrary",), vmem_limit_bytes=VMEM_LIMIT),
        name="ada",
    )(c, w_ada, b_ada.reshape(1, N_MOD * D_MODEL))


def _premix_kernel(x_ref, mod_ref, g_ref, w_ref, buf_ref, wdw_ref, bdw_ref, gmat_ref, gcn_ref, bcn_ref,
                   q_ref, kb_ref, vb_ref, kf_hbm, vf_hbm, co_ref, st_ref, ext_ref, sh_ref, y_ref, kv_buf, kv_sem,
                   *, tm):
    batch = pl.program_id(0)
    s = pl.program_id(1)
    step = batch * pl.num_programs(1) + s
    last = pl.num_programs(0) * pl.num_programs(1) - 1
    slot = step % 2

    def kv_copies(slot):
        return [pltpu.make_async_copy(kv_buf.at[slot, i, :, pl.ds(h * V_DIM, V_DIM)],
                                      out.at[batch, pl.ds(s * tm, tm), h, :], kv_sem.at[slot, i, h])
                for i, out in enumerate((kf_hbm, vf_hbm)) for h in range(N_HEADS)]

    @pl.when(step >= 2)
    def _():
        for c in kv_copies(slot):
            c.wait()

    @pl.when(s == 0)
    def _():
        ext_ref[0:CONV_HALO, :] = buf_ref[0]

    x = x_ref[0]
    shift = mod_ref[0, 0:1, :]
    scale = mod_ref[0, 1:2, :]
    h = _rms(x) * g_ref[...] * (1.0 + scale) + shift
    hb = h.astype(BF16)

    u0 = 2 * QK_W + ATTN_W
    val = _dot(hb, w_ref[:, u0:u0 + CONV_CH])
    gate = _dot(hb, w_ref[:, u0 + CONV_CH:u0 + 2 * CONV_CH])
    ext_ref[CONV_HALO:CONV_HALO + tm, :] = val * _sigmoid(gate)

    off = CONV_HALO - (CONV_W - 1)
    span = tm + CONV_HALO - SUBLANES
    for b in range(1, SUBLANES):
        sh_ref[b - 1, 0:span, :] = ext_ref[b:b + span, :]
    for c in range(tm // CONV_ROWS):
        r0 = c * CONV_ROWS
        acc = jnp.zeros((CONV_ROWS, CONV_CH), F32)
        for j in range(CONV_W):
            b = (j + off) % SUBLANES
            a = r0 + j + off - b
            rows_j = ext_ref[a:a + CONV_ROWS, :] if b == 0 else sh_ref[b - 1, a:a + CONV_ROWS, :]
            acc = acc + wdw_ref[j:j + 1, :] * rows_j
        y_ref[r0:r0 + CONV_ROWS, :] = acc + bdw_ref[...]

    zq = _dot(hb, w_ref[:, 0:QK_W])
    q_ref[0] = (zq * (HEAD_DIM ** -0.5 * LOG2E)).astype(BF16)
    zk = _dot(hb, w_ref[:, QK_W:2 * QK_W])
    kv_buf[slot, 0] = zk
    kb_ref[0] = zk.astype(BF16)
    zv = _dot(hb, w_ref[:, 2 * QK_W:2 * QK_W + ATTN_W])
    kv_buf[slot, 1] = zv
    vb_ref[0] = zv.astype(BF16)

    y = y_ref[...]
    gm = gmat_ref[...]
    yh, yl = _split_bf16(y)
    mu = (_dot(yh, gm) + _dot(yl, gm)) * (1.0 / GROUP_CH)
    d = y - mu
    dh, dl = _split_bf16(d * d)
    var = (_dot(dh, gm) + _dot(dl, gm)) * (1.0 / GROUP_CH)
    yn = d * lax.rsqrt(var + EPS) * gcn_ref[...] + bcn_ref[...]
    co_ref[0] = (yn * _sigmoid(yn)).astype(BF16)

    tail = ext_ref[tm:tm + CONV_HALO, :]
    st_ref[0] = tail
    ext_ref[0:CONV_HALO, :] = tail

    for c in kv_copies(slot):
        c.start()

    @pl.when(step == last)
    def _():
        for c in kv_copies(slot):
            c.wait()

    @pl.when(jnp.logical_and(step == last, step >= 1))
    def _():
        for c in kv_copies(1 - slot):
            c.wait()


def _premix(x, mod, g_pre_mix, w_in_bf, buf, w_dw, b_dw, gmat, g_cn, b_cn, tm):
    B, L, _ = x.shape
    assert L % tm == 0 and tm % CONV_ROWS == 0
    row = lambda b, s: (b, s, 0)
    const2 = lambda b, s: (0, 0)
    bf_tile = jax.ShapeDtypeStruct((B, L, QK_W), BF16)
    f_heads = jax.ShapeDtypeStruct((B, L, N_HEADS, V_DIM), F32)
    return pl.pallas_call(
        functools.partial(_premix_kernel, tm=tm),
        grid=(B, L // tm),
        in_specs=[pl.BlockSpec((1, tm, D_MODEL), row),
                  pl.BlockSpec((1, N_MOD, D_MODEL), lambda b, s: (b, 0, 0)),
                  pl.BlockSpec((1, D_MODEL), const2),
                  pl.BlockSpec((D_MODEL, IN_W), const2),
                  pl.BlockSpec((1, CONV_HALO, CONV_CH), lambda b, s: (b, 0, 0)),
                  pl.BlockSpec((CONV_HALO, CONV_CH), const2),
                  pl.BlockSpec((1, CONV_CH), const2),
                  pl.BlockSpec((CONV_CH, CONV_CH), const2),
                  pl.BlockSpec((1, CONV_CH), const2),
                  pl.BlockSpec((1, CONV_CH), const2)],
        out_specs=[pl.BlockSpec((1, tm, QK_W), row)] * 3
                  + [pl.BlockSpec(memory_space=pl.ANY)] * 2
                  + [pl.BlockSpec((1, tm, QK_W), row),
                     pl.BlockSpec((1, CONV_HALO, CONV_CH), lambda b, s: (b, 0, 0))],
        out_shape=[bf_tile, bf_tile, bf_tile, f_heads, f_heads, bf_tile,
                   jax.ShapeDtypeStruct((B, CONV_HALO, CONV_CH), F32)],
        scratch_shapes=[pltpu.VMEM((CONV_HALO + tm, CONV_CH), F32),
                        pltpu.VMEM((SUBLANES - 1, CONV_HALO + tm, CONV_CH), F32),
                        pltpu.VMEM((tm, CONV_CH), F32),
                        pltpu.VMEM((2, 2, tm, QK_W), F32),
                        pltpu.SemaphoreType.DMA((2, 2, N_HEADS))],
        compiler_params=pltpu.CompilerParams(dimension_semantics=("arbitrary", "arbitrary"),
                                             vmem_limit_bytes=VMEM_LIMIT),
        name="premix",
    )(x, mod, g_pre_mix.reshape(1, D_MODEL), w_in_bf, buf, w_dw, b_dw.reshape(1, CONV_CH), gmat,
      g_cn.reshape(1, CONV_CH), b_cn.reshape(1, CONV_CH))


def _attn_kernel(slopes_ref, q_ref, k_ref, v_ref, kb_ref, mt_ref, lam_ref, gs_ref, o_ref,
                 ke1_ref, ke2_ref, ve_ref, m1_ref, a1_ref, m2_ref, a2_ref,
                 *, tq, tk, rows, q_offset, causal, lam_init):
    slope = slopes_ref[pl.program_id(1)] * LOG2E
    lq = q_ref.shape[1]
    lk = k_ref.shape[1]
    nq = lq // tq
    lane = lax.broadcasted_iota(I32, (1, V_DIM), 1)

    low = jnp.broadcast_to(lane < HEAD_DIM, (lk, V_DIM))
    k = k_ref[0]
    zero = jnp.zeros((lk, V_DIM), BF16)
    ke1_ref[...] = jnp.where(low, k, zero)
    ke2_ref[...] = jnp.where(low, zero, k)
    ve_ref[:, 0:V_DIM] = v_ref[0]
    ve_ref[:, V_DIM:2 * V_DIM] = jnp.broadcast_to(jnp.where(lane == 0, 1.0, 0.0), (lk, V_DIM)).astype(BF16)

    nt = (((1,), (1,)), ((), ()))
    maps = ((ke1_ref, m1_ref, a1_ref), (ke2_ref, m2_ref, a2_ref))

    def lanes(x, n):
        if n % LANES == 0:
            return jnp.concatenate([x] * (n // LANES), axis=1)
        return jnp.broadcast_to(x[:, 0:1], (rows, n))

    def tile(q0, k0, masked):
        ve = ve_ref[pl.ds(k0, tk), :]
        q_first = (lax.broadcasted_iota(I32, (1, 1), 0) + (q0 + q_offset)).astype(F32)
        bias = kb_ref[0, :, pl.ds(k0, tk)] - slope * q_first
        for ke_ref, m_ref, a_ref in maps:
            ke = ke_ref[pl.ds(k0, tk), :]
            for r0 in range(0, tq, rows):
                s = lax.dot_general(q_ref[0, pl.ds(q0 + r0, rows), :], ke, nt, preferred_element_type=F32)
                s = s + bias
                if masked:
                    s = s + mt_ref[0, r0:r0 + rows, :]
                m_old = m_ref[r0:r0 + rows, :]
                m_new = jnp.maximum(m_old, jnp.max(s, axis=-1, keepdims=True))
                p = jnp.exp2(s - lanes(m_new, tk))
                alpha = jnp.exp2(m_old - m_new)
                a_ref[r0:r0 + rows, :] = (lanes(alpha, 2 * V_DIM) * a_ref[r0:r0 + rows, :]
                                          + _dot(p.astype(BF16), ve))
                m_ref[r0:r0 + rows, :] = m_new

    def reset():
        for _, m_ref, a_ref in maps:
            m_ref[...] = jnp.full(m_ref.shape, NEG, F32)
            a_ref[...] = jnp.zeros(a_ref.shape, F32)

    lv = lam_ref[...]
    lam = (jnp.exp(jnp.sum(lv[0:1, :] * lv[1:2, :], axis=-1, keepdims=True))
           - jnp.exp(jnp.sum(lv[2:3, :] * lv[3:4, :], axis=-1, keepdims=True)) + lam_init)

    def finish(q0):
        a1 = a1_ref[...]
        a2 = a2_ref[...]
        o = a1[:, 0:V_DIM] / a1[:, V_DIM:V_DIM + 1] - lam * (a2[:, 0:V_DIM] / a2[:, V_DIM:V_DIM + 1])
        o_ref[0, pl.ds(q0, tq), :] = (_rms(o) * gs_ref[...] * (1.0 - lam_init)).astype(BF16)

    if not causal:
        reset()
        tile(0, 0, True)
        finish(0)
        return

    def q_tile(qi, carry):
        q0 = pl.multiple_of(qi * tq, tq)
        reset()

        def pair(i, c):
            tile(q0, pl.multiple_of(2 * i * tk, tk), False)
            tile(q0, pl.multiple_of((2 * i + 1) * tk, tk), False)
            return c

        lax.fori_loop(0, qi // 2, pair, 0)
        odd = qi % 2 == 1

        @pl.when(odd)
        def _():
            tile(q0, pl.multiple_of((qi - 1) * tk, tk), False)
            tile(q0, pl.multiple_of(qi * tk, tk), True)

        @pl.when(jnp.logical_not(odd))
        def _():
            tile(q0, pl.multiple_of(qi * tk, tk), True)

        finish(q0)
        return carry

    lax.fori_loop(0, nq, q_tile, 0)


def _mask_table(slopes, tq, tk, q_offset):
    qpos = q_offset + jnp.arange(tq, dtype=I32)[:, None]
    kpos = jnp.arange(tk, dtype=I32)[None, :]
    visible = jnp.right_shift(kpos, CHUNK_SHIFT) <= jnp.right_shift(qpos, CHUNK_SHIFT)
    fix = jnp.where(kpos > qpos, 2 * (qpos - kpos), 0).astype(F32)
    return jnp.where(visible[None], (slopes * LOG2E)[:, None, None] * fix[None], NEG)


def _attention(q, k, v, lam_rows, g_subln, lam_init, tq, tk, q_offset, causal):
    B, Lq, _ = q.shape
    Lk = k.shape[1]
    assert Lq % tq == 0 and Lk % tk == 0 and (not causal or (tq == tk and tq % CHUNK == 0 and q_offset == 0))
    assert causal or Lk == tk
    rows = min(tq, ATTN_ROWS)
    slopes = jnp.asarray([2.0 ** (-8.0 * (h + 1) / N_HEADS) for h in range(N_HEADS)], F32)
    table = _mask_table(slopes, tq, tk, q_offset)
    key_bias = ((slopes * LOG2E)[:, None] * jnp.arange(Lk, dtype=F32)[None, :]).reshape(N_HEADS, 1, Lk)
    key_ext = pltpu.VMEM((Lk, V_DIM), BF16)
    stat = pltpu.VMEM((tq, LANES), F32)
    acc = pltpu.VMEM((tq, 2 * V_DIM), F32)
    head = lambda b, h, sl: (b, 0, h)
    return pl.pallas_call(
        functools.partial(_attn_kernel, tq=tq, tk=tk, rows=rows, q_offset=q_offset, causal=causal,
                          lam_init=lam_init),
        grid_spec=pltpu.PrefetchScalarGridSpec(
            num_scalar_prefetch=1,
            grid=(B, N_HEADS),
            in_specs=[pl.BlockSpec((1, Lq, V_DIM), head),
                      pl.BlockSpec((1, Lk, V_DIM), head),
                      pl.BlockSpec((1, Lk, V_DIM), head),
                      pl.BlockSpec((1, 1, Lk), lambda b, h, sl: (h, 0, 0)),
                      pl.BlockSpec((1, tq, tk), lambda b, h, sl: (h, 0, 0)),
                      pl.BlockSpec((8, LANES), lambda b, h, sl: (0, 0)),
                      pl.BlockSpec((1, V_DIM), lambda b, h, sl: (0, 0))],
            out_specs=pl.BlockSpec((1, Lq, V_DIM), head),
            scratch_shapes=[key_ext, key_ext, pltpu.VMEM((Lk, 2 * V_DIM), BF16), stat, acc, stat, acc]),
        out_shape=jax.ShapeDtypeStruct((B, Lq, ATTN_W), BF16),
        compiler_params=pltpu.CompilerParams(dimension_semantics=("arbitrary",) * 2,
                                             vmem_limit_bytes=VMEM_LIMIT),
        name="attn",
    )(slopes, q, k, v, key_bias, table, lam_rows, g_subln.reshape(1, V_DIM))


def _postmix_kernel(a_ref, c_ref, x_ref, mod_ref, wo_ref, gpm_ref, gpf_ref, wr_ref, br_ref, cin_ref,
                    x1_ref, h2_ref, rt_ref, cnt_ref, carry_ref, *, tm):
    first = jnp.logical_and(pl.program_id(0) == 0, pl.program_id(1) == 0)

    @pl.when(first)
    def _():
        carry_ref[...] = cin_ref[...]

    mix = _dot(a_ref[0], wo_ref[0:ATTN_W, :]) + _dot(c_ref[0], wo_ref[ATTN_W:ATTN_W + CONV_CH, :])
    gate_a = mod_ref[0, 2:3, :]
    shift_f = mod_ref[0, 3:4, :]
    scale_f = mod_ref[0, 4:5, :]
    x1 = x_ref[0] + gate_a * (_rms(mix) * gpm_ref[...])
    x1_ref[0] = x1
    h2 = _rms(x1) * gpf_ref[...] * (1.0 + scale_f) + shift_f
    h2_ref[...] = _pack_halves(h2)

    logits = _dot3(h2, wr_ref[...]) + br_ref[...]
    lane = lax.broadcasted_iota(I32, (tm, LANES), 1).astype(F32)
    vals, idxs = [], []
    for _ in range(TOP_K):
        m = jnp.max(logits, axis=-1, keepdims=True)
        idx = jnp.min(jnp.where(logits == m, lane, float(LANES)), axis=-1, keepdims=True)
        vals.append(m)
        idxs.append(idx)
        logits = jnp.where(lane == idx, 2.0 * NEG, logits)
    es = [jnp.exp(v - vals[0]) for v in vals]
    denom = es[0] + es[1] + es[2] + es[3]

    onehot = jnp.zeros((tm, LANES), F32)
    for idx in idxs:
        onehot = jnp.where(lane == idx, 1.0, onehot)
    r_i = lax.broadcasted_iota(I32, (tm, tm), 0)
    c_i = lax.broadcasted_iota(I32, (tm, tm), 1)
    tri = jnp.where(c_i < r_i, 1.0, 0.0).astype(BF16)
    before = _dot(tri, onehot.astype(BF16)) + carry_ref[0:1, :]

    rt = jnp.zeros((tm, LANES), F32)
    for k in range(TOP_K):
        rank = jnp.sum(jnp.where(lane == idxs[k], before, 0.0), axis=-1, keepdims=True)
        rt = jnp.where(lane == k, idxs[k], rt)
        rt = jnp.where(lane == TOP_K + k, es[k] / denom, rt)
        rt = jnp.where(lane == 2 * TOP_K + k, rank, rt)
    rt_ref[0] = rt

    carry_ref[...] = carry_ref[...] + jnp.sum(onehot, axis=0, keepdims=True)
    cnt_ref[0] = carry_ref[...]


def _postmix_into_kernel(pool_ref, *refs, tm):
    del pool_ref
    _postmix_kernel(*refs, tm=tm)


def _postmix(attn, conv, x, mod, w_out_bf, g_post_mix, g_pre_ffn, w_router_pad, b_router_pad, counts_in, tm,
             pool, pool_rows, pool_first):
    B, L, _ = x.shape
    assert pool_first % tm == 0
    row = lambda b, s: (b, s, 0)
    const2 = lambda b, s: (0, 0)
    body, aliases, lead_specs, lead_args = _postmix_kernel, {}, [], ()
    if pool is not None:
        body, aliases = _postmix_into_kernel, {0: 1}
        lead_specs, lead_args = [pl.BlockSpec(memory_space=pl.ANY)], (pool,)
    return pl.pallas_call(
        functools.partial(body, tm=tm),
        grid=(B, L // tm),
        input_output_aliases=aliases,
        in_specs=lead_specs + [
                  pl.BlockSpec((1, tm, ATTN_W), row),
                  pl.BlockSpec((1, tm, CONV_CH), row),
                  pl.BlockSpec((1, tm, D_MODEL), row),
                  pl.BlockSpec((1, N_MOD, D_MODEL), lambda b, s: (b, 0, 0)),
                  pl.BlockSpec((D_MODEL, D_MODEL), const2),
                  pl.BlockSpec((1, D_MODEL), const2),
                  pl.BlockSpec((1, D_MODEL), const2),
                  pl.BlockSpec((D_MODEL, LANES), const2),
                  pl.BlockSpec((1, LANES), const2),
                  pl.BlockSpec((8, LANES), const2)],
        out_specs=[pl.BlockSpec((1, tm, D_MODEL), row),
                   pl.BlockSpec((tm, HALF), lambda b, s: (pool_first // tm + b * (L // tm) + s, 0)),
                   pl.BlockSpec((1, tm, LANES), row),
                   pl.BlockSpec((1, SUBLANES, LANES), lambda b, s: (b, 0, 0))],
        out_shape=[jax.ShapeDtypeStruct((B, L, D_MODEL), F32),
                   jax.ShapeDtypeStruct((pool_rows, HALF), I32),
                   jax.ShapeDtypeStruct((B, L, LANES), F32),
                   jax.ShapeDtypeStruct((B, SUBLANES, LANES), F32)],
        scratch_shapes=[pltpu.VMEM((8, LANES), F32)],
        compiler_params=pltpu.CompilerParams(dimension_semantics=("arbitrary", "arbitrary"),
                                             vmem_limit_bytes=VMEM_LIMIT),
        name="postmix",
    )(*lead_args, attn, conv, x, mod, w_out_bf, g_post_mix.reshape(1, D_MODEL), g_pre_ffn.reshape(1, D_MODEL),
      w_router_pad, b_router_pad, counts_in)


def _expert_kernel(be_ref, nv_ref, x_ref, wgu_ref, bgu_ref, wd_ref, bd_ref, y_ref, wgu_bf, wd_bf):
    i = pl.program_id(0)
    ff_half = D_FF // 2

    @pl.when(jnp.logical_or(i == 0, be_ref[i] != be_ref[jnp.maximum(i - 1, 0)]))
    def _():
        wgu_bf[...] = wgu_ref[0].astype(BF16)
        wd_bf[...] = wd_ref[0].astype(BF16)

    @pl.when(nv_ref[i] > 0)
    def _():
        x_lo, x_hi = _unpack_halves(x_ref[...])
        x_lo = x_lo.astype(BF16)
        x_hi = x_hi.astype(BF16)
        acc = None
        for c in range(2):
            lo, hi = c * ff_half, (c + 1) * ff_half
            g = (_dot(x_lo, wgu_bf[0:HALF, lo:hi]) + _dot(x_hi, wgu_bf[HALF:D_MODEL, lo:hi])
                 + bgu_ref[0, :, lo:hi])
            lin = (_dot(x_lo, wgu_bf[0:HALF, D_FF + lo:D_FF + hi]) + _dot(x_hi, wgu_bf[HALF:D_MODEL, D_FF + lo:D_FF + hi])
                   + bgu_ref[0, :, D_FF + lo:D_FF + hi])
            g = jnp.minimum(g, SWIGLU_LIMIT)
            lin = jnp.clip(lin, -SWIGLU_LIMIT, SWIGLU_LIMIT)
            act = g * _sigmoid(SWIGLU_ALPHA * g) * (lin + 1.0)
            part = _dot(act.astype(BF16), wd_bf[lo:hi, :])
            acc = part if acc is None else acc + part
        y_ref[...] = _pack_halves(acc + bd_ref[0])


def _experts(x_pad, block_e, n_valid, w_gu, b_gu, w_down, b_down, blk):
    n_blocks = block_e.shape[0]
    ex = lambda i, be, nv: (be[i], 0, 0)
    rows = lambda i, be, nv: (i, 0)
    return pl.pallas_call(
        _expert_kernel,
        grid_spec=pltpu.PrefetchScalarGridSpec(
            num_scalar_prefetch=2,
            grid=(n_blocks,),
            in_specs=[pl.BlockSpec((blk, HALF), rows),
                      pl.BlockSpec((1, D_MODEL, 2 * D_FF), ex), pl.BlockSpec((1, 1, 2 * D_FF), ex),
                      pl.BlockSpec((1, D_FF, D_MODEL), ex), pl.BlockSpec((1, 1, D_MODEL), ex)],
            out_specs=pl.BlockSpec((blk, HALF), rows),
            scratch_shapes=[pltpu.VMEM((D_MODEL, 2 * D_FF), BF16), pltpu.VMEM((D_FF, D_MODEL), BF16)]),
        out_shape=jax.ShapeDtypeStruct((n_blocks * blk, HALF), I32),
        compiler_params=pltpu.CompilerParams(dimension_semantics=("arbitrary",),
                                             vmem_limit_bytes=EXPERT_VMEM_LIMIT),
        name="experts",
    )(block_e, n_valid, x_pad, w_gu, b_gu.reshape(N_EXPERTS, 1, 2 * D_FF), w_down,
      b_down.reshape(N_EXPERTS, 1, D_MODEL))


def _combine_kernel(y0_ref, y1_ref, y2_ref, y3_ref, rt_ref, x1_ref, mod_ref, g_ref, o_ref):
    rt = rt_ref[0]
    f_lo = f_hi = None
    for k, y_ref in enumerate((y0_ref, y1_ref, y2_ref, y3_ref)):
        lo, hi = _unpack_halves(y_ref[...])
        w = rt[:, TOP_K + k:TOP_K + k + 1]
        f_lo = w * lo if f_lo is None else f_lo + w * lo
        f_hi = w * hi if f_hi is None else f_hi + w * hi
    ms = (jnp.sum(f_lo * f_lo, axis=-1, keepdims=True) + jnp.sum(f_hi * f_hi, axis=-1, keepdims=True)) / D_MODEL
    r = lax.rsqrt(ms + EPS)
    gate_f = mod_ref[0, 5:6, :]
    g = g_ref[...]
    o_ref[0, :, 0:HALF] = x1_ref[0, :, 0:HALF] + gate_f[:, 0:HALF] * (f_lo * r * g[:, 0:HALF])
    o_ref[0, :, HALF:D_MODEL] = x1_ref[0, :, HALF:D_MODEL] + gate_f[:, HALF:D_MODEL] * (f_hi * r * g[:, HALF:D_MODEL])


def _combine_into_kernel(prev_ref, *refs):
    del prev_ref
    _combine_kernel(*refs)


def _combine(y, route, x1, mod, g_post_ffn, tm, t_all, t_first, b_first=0, n_batch=None, out_prev=None):
    B, L, _ = x1.shape
    n_batch = B if n_batch is None else n_batch
    assert t_all % tm == 0 and t_first % tm == 0
    row = lambda b, s: (b + b_first, s, 0)
    y_specs = [pl.BlockSpec((tm, HALF), lambda b, s, k=k: ((k * t_all + t_first) // tm + b * (L // tm) + s, 0))
               for k in range(TOP_K)]
    in_specs = y_specs + [pl.BlockSpec((1, tm, LANES), row),
                          pl.BlockSpec((1, tm, D_MODEL), row),
                          pl.BlockSpec((1, N_MOD, D_MODEL), lambda b, s: (b + b_first, 0, 0)),
                          pl.BlockSpec((1, D_MODEL), lambda b, s: (0, 0))]
    args = (y, y, y, y, route, x1, mod, g_post_ffn.reshape(1, D_MODEL))
    body, aliases = _combine_kernel, {}
    if out_prev is not None:
        body, aliases = _combine_into_kernel, {0: 0}
        in_specs = [pl.BlockSpec(memory_space=pl.ANY)] + in_specs
        args = (out_prev,) + args
    return pl.pallas_call(
        body,
        grid=(n_batch, L // tm),
        in_specs=in_specs,
        out_specs=pl.BlockSpec((1, tm, D_MODEL), row),
        out_shape=jax.ShapeDtypeStruct((B, L, D_MODEL), F32),
        input_output_aliases=aliases,
        compiler_params=pltpu.CompilerParams(dimension_semantics=("arbitrary", "arbitrary"),
                                             vmem_limit_bytes=VMEM_LIMIT),
        name="combine",
    )(*args)


SC_GATHER_BYTES = 128 * 1024


def _scatter_quantum(width):
    rows = SC_GATHER_BYTES // (width * 4)
    sc = plsc.get_sparse_core_info()
    return rows, sc.num_cores * sc.num_subcores * rows * 2


def _scatter_rows(table, row_first, dest, n_slots):
    n, width = dest.shape[0], table.shape[1]
    rows, quantum = _scatter_quantum(width)
    sc = plsc.get_sparse_core_info()
    workers = sc.num_cores * sc.num_subcores
    n_pad = -(-n // quantum) * quantum
    extra = n_pad - n
    assert row_first % SUBLANES == 0 and n % rows == 0 and n >= rows and row_first + n <= table.shape[0]
    last_window = row_first + n - rows
    spare = n_slots + jnp.arange(extra * TOP_K, dtype=I32).reshape(extra, TOP_K)
    per_w = n_pad // workers
    wins = per_w // rows
    wins_tile = -(-wins // SUBLANES) * SUBLANES
    idx = jnp.concatenate([dest, spare], axis=0).T.reshape(TOP_K, workers, wins, rows)
    idx = jnp.pad(idx, ((0, 0), (0, 0), (0, wins_tile - wins), (0, 0))).reshape(TOP_K, workers * wins_tile, rows)
    mesh = plsc.VectorSubcoreMesh(core_axis_name="c", subcore_axis_name="s")
    buf = pltpu.VMEM((rows, width), table.dtype)

    @functools.partial(
        pl.kernel, mesh=mesh,
        out_type=jax.ShapeDtypeStruct((n_slots + extra * TOP_K, width), table.dtype),
        scratch_types=[pltpu.VMEM((TOP_K, wins_tile, rows), I32), buf, buf,
                       pltpu.SemaphoreType.DMA, pltpu.SemaphoreType.DMA],
        name="scatter_rows",
    )
    def scatter_kernel(table_hbm, idx_hbm, out_hbm, idx_v, buf0, buf1, sem0, sem1):
        w = lax.axis_index("s") * sc.num_cores + lax.axis_index("c")
        first = pl.multiple_of(w * wins_tile, SUBLANES)
        for k in range(TOP_K):
            pltpu.sync_copy(idx_hbm.at[k, pl.ds(first, wins_tile)], idx_v.at[k])
        base = row_first + w * per_w

        def window(i, buf, sem):
            start = pl.multiple_of(jnp.minimum(base + i * rows, last_window), SUBLANES)
            pltpu.sync_copy(table_hbm.at[pl.ds(start, rows)], buf)
            copies = [pltpu.async_copy(buf, out_hbm.at[idx_v.at[k, i]], sem) for k in range(TOP_K)]
            return copies

        @pl.loop(0, wins, step=2)
        def _(i):
            c0 = window(i, buf0, sem0)
            c1 = window(i + 1, buf1, sem1)
            for c in c0 + c1:
                c.wait()

    return scatter_kernel(table, idx)


def _gather_rows(table, idx):
    n = idx.shape[0]
    width = table.shape[1]
    rows = SC_GATHER_BYTES // (width * 4)
    sc = plsc.get_sparse_core_info()
    workers = sc.num_cores * sc.num_subcores
    quantum = workers * rows * 2
    n_pad = -(-n // quantum) * quantum
    filler = jnp.arange(n_pad - n, dtype=I32) % table.shape[0]
    per_w = n_pad // workers
    wins = per_w // rows
    wins_tile = -(-wins // SUBLANES) * SUBLANES
    idx = jnp.concatenate([idx, filler]).reshape(workers, wins, rows)
    idx = jnp.pad(idx, ((0, 0), (0, wins_tile - wins), (0, 0))).reshape(workers * wins_tile, rows)
    mesh = plsc.VectorSubcoreMesh(core_axis_name="c", subcore_axis_name="s")
    buf = pltpu.VMEM((rows, width), table.dtype)

    @functools.partial(
        pl.kernel, mesh=mesh,
        out_type=jax.ShapeDtypeStruct((n_pad, width), table.dtype),
        scratch_types=[pltpu.VMEM((wins_tile, rows), I32), buf, buf,
                       pltpu.SemaphoreType.DMA, pltpu.SemaphoreType.DMA],
        name="gather_rows",
    )
    def gather_kernel(table_hbm, idx_hbm, out_hbm, idx_v, buf0, buf1, sem0, sem1):
        w = lax.axis_index("s") * sc.num_cores + lax.axis_index("c")
        pltpu.sync_copy(idx_hbm.at[pl.ds(pl.multiple_of(w * wins_tile, SUBLANES), wins_tile)], idx_v)
        base = pl.multiple_of(w * per_w, rows)

        @pl.loop(0, wins, step=2)
        def _(i):
            g0 = pltpu.async_copy(table_hbm.at[idx_v.at[i]], buf0, sem0)
            g1 = pltpu.async_copy(table_hbm.at[idx_v.at[i + 1]], buf1, sem1)
            g0.wait()
            pltpu.sync_copy(buf0, out_hbm.at[pl.ds(base + i * rows, rows)])
            g1.wait()
            pltpu.sync_copy(buf1, out_hbm.at[pl.ds(base + (i + 1) * rows, rows)])

    return gather_kernel(table, idx)


def _routing_tables(route, counts, counts_before, blk):
    T = route.shape[0]
    TK = T * TOP_K
    idx = route[:, 0:TOP_K].astype(I32)
    rank = route[:, 2 * TOP_K:3 * TOP_K].astype(I32) - counts_before[idx]
    padded = (counts + blk - 1) // blk * blk
    pad_end = jnp.cumsum(padded)
    pad_start = pad_end - padded
    dest = pad_start[idx] + rank
    n_blocks = -(-TK // blk) + N_EXPERTS
    blk_start = jnp.arange(n_blocks, dtype=I32) * blk
    block_e = jnp.sum((blk_start[:, None] >= pad_end[None, :]).astype(I32), axis=1)
    block_e = jnp.minimum(block_e, N_EXPERTS - 1)
    last = (pad_start + counts)[block_e]
    n_valid = jnp.clip(last - blk_start, 0, blk)
    return block_e, n_valid, dest


def _mix(x, mod, buf, past_k, past_v, p, lam_init, counts_in, tm, tq, pool, pool_rows, pool_first):
    B, L, _ = x.shape
    q, kb, vb, kf, vf, conv, state = _premix(x, mod, p["g_pre_mix"], p["w_in"], buf, p["w_dw"], p["b_dw"],
                                             p["gmat"], p["g_cn"], p["b_cn"], tm)
    if past_k is None:
        attn = _attention(q, kb, vb, p["lam_rows"], p["g_subln"], lam_init, tq, tq, 0, True)
    else:
        P = past_k.shape[1]
        keys = jnp.concatenate([past_k.reshape(B, P, QK_W).astype(BF16), kb], axis=1)
        vals = jnp.concatenate([past_v.reshape(B, P, ATTN_W).astype(BF16), vb], axis=1)
        attn = _attention(q, keys, vals, p["lam_rows"], p["g_subln"], lam_init, L, P + L, P, False)
    x1, h2, route, cnt = _postmix(attn, conv, x, mod, p["w_out"], p["g_post_mix"], p["g_pre_ffn"],
                                  p["w_router"], p["b_router"], counts_in, tm, pool, pool_rows, pool_first)
    return x1, h2, route, cnt, kf, vf, state[:, CONV_HALO - (CONV_W - 1):, :]


def _moe_rows(pool, row_first, route_groups, counts, counts_before, p, blk):
    route = jnp.concatenate([r.reshape(-1, LANES) for r in route_groups], axis=0)
    T = route.shape[0]
    block_e, n_valid, dest = _routing_tables(route, counts, counts_before, blk)
    x_pad = _scatter_rows(pool, row_first, dest, block_e.shape[0] * blk)
    y_pad = _experts(x_pad, block_e, n_valid, p["w_gu"], p["b_gu"], p["w_down"], p["b_down"], blk)
    return _gather_rows(y_pad, dest.T.reshape(T * TOP_K))


def _prepare_params(l, w_ada, b_ada, g_pre_mix, g_post_mix, w_in, lambda_q1, lambda_k1, lambda_q2, lambda_k2,
                    g_subln, w_dw, b_dw, g_cnorm, b_cnorm, w_out, g_pre_ffn, g_post_ffn,
                    w_router, b_router, w_gu, b_gu, w_down, b_down):
    lam_rows = jnp.zeros((8, LANES), F32)
    for r, vec in enumerate((lambda_q1[l], lambda_k1[l], lambda_q2[l], lambda_k2[l])):
        lam_rows = lam_rows.at[r, :HEAD_DIM].set(vec)
    ch = jnp.arange(CONV_CH, dtype=I32) // GROUP_CH
    gmat = (ch[:, None] == ch[None, :]).astype(BF16)
    w_dw_pad = jnp.zeros((CONV_HALO, CONV_CH), F32).at[:CONV_W].set(w_dw[l])
    w_router_pad = jnp.zeros((D_MODEL, LANES), F32).at[:, :N_EXPERTS].set(w_router[l])
    b_router_pad = jnp.full((1, LANES), NEG, F32).at[0, :N_EXPERTS].set(b_router[l])
    return dict(w_ada=w_ada[l], b_ada=b_ada[l], g_pre_mix=g_pre_mix[l], g_post_mix=g_post_mix[l],
                w_in=w_in[l].astype(BF16), lam_rows=lam_rows, g_subln=g_subln[l], w_dw=w_dw_pad, b_dw=b_dw[l],
                gmat=gmat, g_cn=g_cnorm[l], b_cn=b_cnorm[l], w_out=w_out[l].astype(BF16),
                g_pre_ffn=g_pre_ffn[l], g_post_ffn=g_post_ffn[l], w_router=w_router_pad, b_router=b_router_pad,
                w_gu=w_gu[l], b_gu=b_gu[l], w_down=w_down[l], b_down=b_down[l])


def kernel(x_prompt, x_sample, cache_k, cache_v, state_conv, c_prompt, c_sample, w_ada, b_ada, g_pre_mix, g_post_mix, w_in, lambda_q1, lambda_k1, lambda_q2, lambda_k2, g_subln, w_dw, b_dw, g_cnorm, b_cnorm, w_out, g_pre_ffn, g_post_ffn, w_router, b_router, w_gu, b_gu, w_down, b_down):
    depth = w_ada.shape[0]
    Bp, Lp, _ = x_prompt.shape
    Bs, Ls, _ = x_sample.shape
    yp, ys = x_prompt, x_sample
    outs = [[] for _ in range(6)]
    for l in range(depth):
        p = _prepare_params(l, w_ada, b_ada, g_pre_mix, g_post_mix, w_in, lambda_q1, lambda_k1, lambda_q2,
                            lambda_k2, g_subln, w_dw, b_dw, g_cnorm, b_cnorm, w_out, g_pre_ffn, g_post_ffn,
                            w_router, b_router, w_gu, b_gu, w_down, b_down)
        lam_init = 0.8 - 0.6 * math.exp(-0.3 * l)
        mod = _ada(jnp.concatenate([c_prompt, c_sample], axis=0), p["w_ada"], p["b_ada"])
        mod = mod.reshape(Bp + Bs, N_MOD, D_MODEL)
        buf_p = jnp.zeros((Bp, CONV_HALO, CONV_CH), F32)
        buf_s = jnp.pad(state_conv[l], ((0, 0), (CONV_HALO - (CONV_W - 1), 0), (0, 0)))
        tm_p = min(Lp, 512)
        assert Bp % 2 == 0
        hb = Bp // 2
        t_a, t_b = hb * Lp, hb * Lp + Bs * Ls
        pool_rows = Bp * Lp + Bs * Ls
        zero_counts = jnp.zeros((SUBLANES, LANES), F32)
        x1p, pool, rtp, cnt_p, kp, vp, cp = _mix(yp, mod[:Bp], buf_p, None, None, p, lam_init, zero_counts,
                                                 tm_p, tm_p, None, pool_rows, 0)
        x1s, pool, rts, cnt_s, ks, vs, cs = _mix(ys, mod[Bp:], buf_s, cache_k[l], cache_v[l], p, lam_init,
                                                 cnt_p[Bp - 1], Ls, Ls, pool, pool_rows, Bp * Lp)
        n_a = cnt_p[hb - 1, 0, :N_EXPERTS].astype(I32)
        n_b = cnt_s[Bs - 1, 0, :N_EXPERTS].astype(I32) - n_a
        y_a = _moe_rows(pool, 0, (rtp[:hb],), n_a, jnp.zeros_like(n_a), p, EXPERT_ROWS)
        y_b = _moe_rows(pool, t_a, (rtp[hb:], rts), n_b, n_a, p, EXPERT_ROWS)
        tm_c = min(tm_p, 256)
        yp = _combine(y_a, rtp, x1p, mod[:Bp], p["g_post_ffn"], tm_c, t_a, 0, 0, hb)
        yp = _combine(y_b, rtp, x1p, mod[:Bp], p["g_post_ffn"], tm_c, t_b, 0, hb, hb, out_prev=yp)
        ys = _combine(y_b, rts, x1s, mod[Bp:], p["g_post_ffn"], Ls, t_b, hb * Lp)
        for lst, val in zip(outs, (kp, vp, cp, ks, vs, cs)):
            lst.append(val)
    return (yp, ys) + tuple(jnp.stack(o) for o in outs)
```

```python
import functools
import math

import jax
import jax.numpy as jnp
from jax import lax
from jax.experimental import pallas as pl
from jax.experimental.pallas import tpu as pltpu
from jax.experimental.pallas import tpu_sc as plsc

F32 = jnp.float32
BF16 = jnp.bfloat16
I32 = jnp.int32

D_MODEL = 1024
CHUNK = 64
CHUNK_SHIFT = 6
N_HEADS = 4
V_DIM = 128
HEAD_DIM = 64
QK_W = 512
ATTN_W = 512
CONV_CH = 512
CONV_W = 31
CONV_GROUPS = 8
GROUP_CH = CONV_CH // CONV_GROUPS
IN_W = 2 * QK_W + ATTN_W + 2 * CONV_CH
N_MOD = 6
N_EXPERTS = 32
TOP_K = 4
D_FF = 1024
SWIGLU_ALPHA = 1.702
SWIGLU_LIMIT = 7.0
EPS = 1e-6

LANES = 128
SUBLANES = 8
CONV_HALO = 32
CONV_ROWS = 32
NEG = -1e30
LOG2E = math.log2(math.e)
ATTN_ROWS = 128
ATTN_GROUP = 4
EXPERT_ROWS = 512
VMEM_LIMIT = 48 * 1024 * 1024
EXPERT_VMEM_LIMIT = 56 * 1024 * 1024


def _sigmoid(x):
    return 1.0 / (1.0 + jnp.exp(-x))


def _split_bf16(x):
    hi = x.astype(BF16)
    lo = (x - hi.astype(F32)).astype(BF16)
    return hi, lo


def _dot(a, b):
    return jnp.dot(a, b, preferred_element_type=F32)


def _dot3(a, b):
    ah, al = _split_bf16(a)
    bh, bl = _split_bf16(b)
    return _dot(ah, bh) + _dot(ah, bl) + _dot(al, bh)


def _rms(x):
    return x * lax.rsqrt(jnp.mean(x * x, axis=-1, keepdims=True) + EPS)


HALF = D_MODEL // 2
HIGH16 = -65536


def _pack_halves(x):
    lo = lax.bitcast_convert_type(x[:, :HALF].astype(BF16).astype(F32), I32)
    hi = lax.bitcast_convert_type(x[:, HALF:].astype(BF16).astype(F32), I32)
    return jnp.bitwise_or(jnp.bitwise_and(jnp.right_shift(lo, 16), 0xFFFF), jnp.bitwise_and(hi, HIGH16))


def _unpack_halves(w):
    lo = lax.bitcast_convert_type(jnp.left_shift(w, 16), F32)
    hi = lax.bitcast_convert_type(jnp.bitwise_and(w, HIGH16), F32)
    return lo, hi


def _ada_kernel(c_ref, w_ref, b_ref, o_ref):
    c = c_ref[...]
    o_ref[...] = _dot3(c * _sigmoid(c), w_ref[...]) + b_ref[...]


def _ada(c, w_ada, b_ada):
    n = c.shape[0]
    return pl.pallas_call(
        _ada_kernel,
        grid=(N_MOD,),
        in_specs=[pl.BlockSpec((n, D_MODEL), lambda j: (0, 0)),
                  pl.BlockSpec((D_MODEL, D_MODEL), lambda j: (0, j)),
                  pl.BlockSpec((1, D_MODEL), lambda j: (0, j))],
        out_specs=pl.BlockSpec((n, D_MODEL), lambda j: (0, j)),
        out_shape=jax.ShapeDtypeStruct((n, N_MOD * D_MODEL), F32),
        compiler_params=pltpu.CompilerParams(dimension_semantics=("arbitrary",), vmem_limit_bytes=VMEM_LIMIT),
        name="ada",
    )(c, w_ada, b_ada.reshape(1, N_MOD * D_MODEL))


def _premix_kernel(x_ref, mod_ref, g_ref, w_ref, buf_ref, wdw_ref, bdw_ref, gmat_ref, gcn_ref, bcn_ref,
                   q_ref, kb_ref, vb_ref, kf_hbm, vf_hbm, co_ref, st_ref, ext_ref, sh_ref, y_ref, kv_buf, kv_sem,
                   *, tm):
    batch = pl.program_id(0)
    s = pl.program_id(1)
    step = batch * pl.num_programs(1) + s
    last = pl.num_programs(0) * pl.num_programs(1) - 1
    slot = step % 2

    def kv_copies(slot):
        return [pltpu.make_async_copy(kv_buf.at[slot, i, :, pl.ds(h * V_DIM, V_DIM)],
                                      out.at[batch, pl.ds(s * tm, tm), h, :], kv_sem.at[slot, i, h])
                for i, out in enumerate((kf_hbm, vf_hbm)) for h in range(N_HEADS)]

    @pl.when(step >= 2)
    def _():
        for c in kv_copies(slot):
            c.wait()

    @pl.when(s == 0)
    def _():
        ext_ref[0:CONV_HALO, :] = buf_ref[0]

    x = x_ref[0]
    shift = mod_ref[0, 0:1, :]
    scale = mod_ref[0, 1:2, :]
    h = _rms(x) * g_ref[...] * (1.0 + scale) + shift
    hb = h.astype(BF16)

    u0 = 2 * QK_W + ATTN_W
    val = _dot(hb, w_ref[:, u0:u0 + CONV_CH])
    gate = _dot(hb, w_ref[:, u0 + CONV_CH:u0 + 2 * CONV_CH])
    ext_ref[CONV_HALO:CONV_HALO + tm, :] = val * _sigmoid(gate)

    off = CONV_HALO - (CONV_W - 1)
    span = tm + CONV_HALO - SUBLANES
    for b in range(1, SUBLANES):
        sh_ref[b - 1, 0:span, :] = ext_ref[b:b + span, :]
    for c in range(tm // CONV_ROWS):
        r0 = c * CONV_ROWS
        acc = jnp.zeros((CONV_ROWS, CONV_CH), F32)
        for j in range(CONV_W):
            b = (j + off) % SUBLANES
            a = r0 + j + off - b
            rows_j = ext_ref[a:a + CONV_ROWS, :] if b == 0 else sh_ref[b - 1, a:a + CONV_ROWS, :]
            acc = acc + wdw_ref[j:j + 1, :] * rows_j
        y_ref[r0:r0 + CONV_ROWS, :] = acc + bdw_ref[...]

    zq = _dot(hb, w_ref[:, 0:QK_W])
    q_ref[0] = (zq * (HEAD_DIM ** -0.5 * LOG2E)).astype(BF16)
    zk = _dot(hb, w_ref[:, QK_W:2 * QK_W])
    kv_buf[slot, 0] = zk
    kb_ref[0] = zk.astype(BF16)
    zv = _dot(hb, w_ref[:, 2 * QK_W:2 * QK_W + ATTN_W])
    kv_buf[slot, 1] = zv
    vb_ref[0] = zv.astype(BF16)

    y = y_ref[...]
    gm = gmat_ref[...]
    yh, yl = _split_bf16(y)
    mu = (_dot(yh, gm) + _dot(yl, gm)) * (1.0 / GROUP_CH)
    d = y - mu
    dh, dl = _split_bf16(d * d)
    var = (_dot(dh, gm) + _dot(dl, gm)) * (1.0 / GROUP_CH)
    yn = d * lax.rsqrt(var + EPS) * gcn_ref[...] + bcn_ref[...]
    co_ref[0] = (yn * _sigmoid(yn)).astype(BF16)

    tail = ext_ref[tm:tm + CONV_HALO, :]
    st_ref[0] = tail
    ext_ref[0:CONV_HALO, :] = tail

    for c in kv_copies(slot):
        c.start()

    @pl.when(step == last)
    def _():
        for c in kv_copies(slot):
            c.wait()

    @pl.when(jnp.logical_and(step == last, step >= 1))
    def _():
        for c in kv_copies(1 - slot):
            c.wait()


def _premix(x, mod, g_pre_mix, w_in_bf, buf, w_dw, b_dw, gmat, g_cn, b_cn, tm):
    B, L, _ = x.shape
    assert L % tm == 0 and tm % CONV_ROWS == 0
    row = lambda b, s: (b, s, 0)
    const2 = lambda b, s: (0, 0)
    bf_tile = jax.ShapeDtypeStruct((B, L, QK_W), BF16)
    f_heads = jax.ShapeDtypeStruct((B, L, N_HEADS, V_DIM), F32)
    return pl.pallas_call(
        functools.partial(_premix_kernel, tm=tm),
        grid=(B, L // tm),
        in_specs=[pl.BlockSpec((1, tm, D_MODEL), row),
                  pl.BlockSpec((1, N_MOD, D_MODEL), lambda b, s: (b, 0, 0)),
                  pl.BlockSpec((1, D_MODEL), const2),
                  pl.BlockSpec((D_MODEL, IN_W), const2),
                  pl.BlockSpec((1, CONV_HALO, CONV_CH), lambda b, s: (b, 0, 0)),
                  pl.BlockSpec((CONV_HALO, CONV_CH), const2),
                  pl.BlockSpec((1, CONV_CH), const2),
                  pl.BlockSpec((CONV_CH, CONV_CH), const2),
                  pl.BlockSpec((1, CONV_CH), const2),
                  pl.BlockSpec((1, CONV_CH), const2)],
        out_specs=[pl.BlockSpec((1, tm, QK_W), row)] * 3
                  + [pl.BlockSpec(memory_space=pl.ANY)] * 2
                  + [pl.BlockSpec((1, tm, QK_W), row),
                     pl.BlockSpec((1, CONV_HALO, CONV_CH), lambda b, s: (b, 0, 0))],
        out_shape=[bf_tile, bf_tile, bf_tile, f_heads, f_heads, bf_tile,
                   jax.ShapeDtypeStruct((B, CONV_HALO, CONV_CH), F32)],
        scratch_shapes=[pltpu.VMEM((CONV_HALO + tm, CONV_CH), F32),
                        pltpu.VMEM((SUBLANES - 1, CONV_HALO + tm, CONV_CH), F32),
                        pltpu.VMEM((tm, CONV_CH), F32),
                        pltpu.VMEM((2, 2, tm, QK_W), F32),
                        pltpu.SemaphoreType.DMA((2, 2, N_HEADS))],
        compiler_params=pltpu.CompilerParams(dimension_semantics=("arbitrary", "arbitrary"),
                                             vmem_limit_bytes=VMEM_LIMIT),
        name="premix",
    )(x, mod, g_pre_mix.reshape(1, D_MODEL), w_in_bf, buf, w_dw, b_dw.reshape(1, CONV_CH), gmat,
      g_cn.reshape(1, CONV_CH), b_cn.reshape(1, CONV_CH))


def _attn_kernel(slopes_ref, q_ref, k_ref, v_ref, kb_ref, mt_ref, lam_ref, gs_ref, o_ref,
                 ke1_ref, ke2_ref, ve_ref, m1_ref, a1_ref, m2_ref, a2_ref,
                 *, tq, tk, rows, q_offset, causal, lam_init):
    slope = slopes_ref[pl.program_id(1)] * LOG2E
    lq = q_ref.shape[1]
    lk = k_ref.shape[1]
    nq = lq // tq
    lane = lax.broadcasted_iota(I32, (1, V_DIM), 1)

    low = jnp.broadcast_to(lane < HEAD_DIM, (lk, V_DIM))
    k = k_ref[0]
    zero = jnp.zeros((lk, V_DIM), BF16)
    ke1_ref[...] = jnp.where(low, k, zero)
    ke2_ref[...] = jnp.where(low, zero, k)
    ve_ref[:, 0:V_DIM] = v_ref[0]
    ve_ref[:, V_DIM:2 * V_DIM] = jnp.broadcast_to(jnp.where(lane == 0, 1.0, 0.0), (lk, V_DIM)).astype(BF16)

    nt = (((1,), (1,)), ((), ()))
    maps = ((ke1_ref, m1_ref, a1_ref), (ke2_ref, m2_ref, a2_ref))

    def lanes(x, n):
        if n % LANES == 0:
            return jnp.concatenate([x] * (n // LANES), axis=1)
        return jnp.broadcast_to(x[:, 0:1], (rows, n))

    def tile(q0, k0, masked):
        ve = ve_ref[pl.ds(k0, tk), :]
        q_first = (lax.broadcasted_iota(I32, (1, 1), 0) + (q0 + q_offset)).astype(F32)
        bias = kb_ref[0, :, pl.ds(k0, tk)] - slope * q_first
        for ke_ref, m_ref, a_ref in maps:
            ke = ke_ref[pl.ds(k0, tk), :]
            for r0 in range(0, tq, rows):
                s = lax.dot_general(q_ref[0, pl.ds(q0 + r0, rows), :], ke, nt, preferred_element_type=F32)
                s = s + bias
                if masked:
                    s = s + mt_ref[0, r0:r0 + rows, :]
                m_old = m_ref[r0:r0 + rows, :]
                m_new = jnp.maximum(m_old, jnp.max(s, axis=-1, keepdims=True))
                p = jnp.exp2(s - lanes(m_new, tk))
                alpha = jnp.exp2(m_old - m_new)
                a_ref[r0:r0 + rows, :] = (lanes(alpha, 2 * V_DIM) * a_ref[r0:r0 + rows, :]
                                          + _dot(p.astype(BF16), ve))
                m_ref[r0:r0 + rows, :] = m_new

    def reset():
        for _, m_ref, a_ref in maps:
            m_ref[...] = jnp.full(m_ref.shape, NEG, F32)
            a_ref[...] = jnp.zeros(a_ref.shape, F32)

    lv = lam_ref[...]
    lam = (jnp.exp(jnp.sum(lv[0:1, :] * lv[1:2, :], axis=-1, keepdims=True))
           - jnp.exp(jnp.sum(lv[2:3, :] * lv[3:4, :], axis=-1, keepdims=True)) + lam_init)

    def finish(q0):
        a1 = a1_ref[...]
        a2 = a2_ref[...]
        o = a1[:, 0:V_DIM] / a1[:, V_DIM:V_DIM + 1] - lam * (a2[:, 0:V_DIM] / a2[:, V_DIM:V_DIM + 1])
        o_ref[0, pl.ds(q0, tq), :] = (_rms(o) * gs_ref[...] * (1.0 - lam_init)).astype(BF16)

    if not causal:
        reset()
        tile(0, 0, True)
        finish(0)
        return

    def q_tile(qi, carry):
        q0 = pl.multiple_of(qi * tq, tq)
        reset()

        def group(i, c):
            for j in range(ATTN_GROUP):
                tile(q0, pl.multiple_of((ATTN_GROUP * i + j) * tk, tk), False)
            return c

        lax.fori_loop(0, qi // ATTN_GROUP, group, 0)
        rest = qi % ATTN_GROUP

        for r in range(ATTN_GROUP):
            @pl.when(rest == r)
            def _(r=r):
                for j in range(r, 0, -1):
                    tile(q0, pl.multiple_of((qi - j) * tk, tk), False)
                tile(q0, pl.multiple_of(qi * tk, tk), True)

        finish(q0)
        return carry

    lax.fori_loop(0, nq, q_tile, 0)


def _mask_table(slopes, tq, tk, q_offset):
    qpos = q_offset + jnp.arange(tq, dtype=I32)[:, None]
    kpos = jnp.arange(tk, dtype=I32)[None, :]
    visible = jnp.right_shift(kpos, CHUNK_SHIFT) <= jnp.right_shift(qpos, CHUNK_SHIFT)
    fix = jnp.where(kpos > qpos, 2 * (qpos - kpos), 0).astype(F32)
    return jnp.where(visible[None], (slopes * LOG2E)[:, None, None] * fix[None], NEG)


def _attention(q, k, v, lam_rows, g_subln, lam_init, tq, tk, q_offset, causal):
    B, Lq, _ = q.shape
    Lk = k.shape[1]
    assert Lq % tq == 0 and Lk % tk == 0 and (not causal or (tq == tk and tq % CHUNK == 0 and q_offset == 0))
    assert causal or Lk == tk
    rows = min(tq, ATTN_ROWS)
    slopes = jnp.asarray([2.0 ** (-8.0 * (h + 1) / N_HEADS) for h in range(N_HEADS)], F32)
    table = _mask_table(slopes, tq, tk, q_offset)
    key_bias = ((slopes * LOG2E)[:, None] * jnp.arange(Lk, dtype=F32)[None, :]).reshape(N_HEADS, 1, Lk)
    key_ext = pltpu.VMEM((Lk, V_DIM), BF16)
    stat = pltpu.VMEM((tq, LANES), F32)
    acc = pltpu.VMEM((tq, 2 * V_DIM), F32)
    head = lambda b, h, sl: (b, 0, h)
    return pl.pallas_call(
        functools.partial(_attn_kernel, tq=tq, tk=tk, rows=rows, q_offset=q_offset, causal=causal,
                          lam_init=lam_init),
        grid_spec=pltpu.PrefetchScalarGridSpec(
            num_scalar_prefetch=1,
            grid=(B, N_HEADS),
            in_specs=[pl.BlockSpec((1, Lq, V_DIM), head),
                      pl.BlockSpec((1, Lk, V_DIM), head),
                      pl.BlockSpec((1, Lk, V_DIM), head),
                      pl.BlockSpec((1, 1, Lk), lambda b, h, sl: (h, 0, 0)),
                      pl.BlockSpec((1, tq, tk), lambda b, h, sl: (h, 0, 0)),
                      pl.BlockSpec((8, LANES), lambda b, h, sl: (0, 0)),
                      pl.BlockSpec((1, V_DIM), lambda b, h, sl: (0, 0))],
            out_specs=pl.BlockSpec((1, Lq, V_DIM), head),
            scratch_shapes=[key_ext, key_ext, pltpu.VMEM((Lk, 2 * V_DIM), BF16), stat, acc, stat, acc]),
        out_shape=jax.ShapeDtypeStruct((B, Lq, ATTN_W), BF16),
        compiler_params=pltpu.CompilerParams(dimension_semantics=("arbitrary",) * 2,
                                             vmem_limit_bytes=VMEM_LIMIT),
        name="attn",
    )(slopes, q, k, v, key_bias, table, lam_rows, g_subln.reshape(1, V_DIM))


def _postmix_kernel(a_ref, c_ref, x_ref, mod_ref, wo_ref, gpm_ref, gpf_ref, wr_ref, br_ref, cin_ref,
                    x1_ref, h2_ref, rt_ref, cnt_ref, carry_ref, *, tm):
    first = jnp.logical_and(pl.program_id(0) == 0, pl.program_id(1) == 0)

    @pl.when(first)
    def _():
        carry_ref[...] = cin_ref[...]

    mix = _dot(a_ref[0], wo_ref[0:ATTN_W, :]) + _dot(c_ref[0], wo_ref[ATTN_W:ATTN_W + CONV_CH, :])
    gate_a = mod_ref[0, 2:3, :]
    shift_f = mod_ref[0, 3:4, :]
    scale_f = mod_ref[0, 4:5, :]
    x1 = x_ref[0] + gate_a * (_rms(mix) * gpm_ref[...])
    x1_ref[0] = x1
    h2 = _rms(x1) * gpf_ref[...] * (1.0 + scale_f) + shift_f
    h2_ref[...] = _pack_halves(h2)

    logits = _dot3(h2, wr_ref[...]) + br_ref[...]
    lane = lax.broadcasted_iota(I32, (tm, LANES), 1).astype(F32)
    vals, idxs = [], []
    for _ in range(TOP_K):
        m = jnp.max(logits, axis=-1, keepdims=True)
        idx = jnp.min(jnp.where(logits == m, lane, float(LANES)), axis=-1, keepdims=True)
        vals.append(m)
        idxs.append(idx)
        logits = jnp.where(lane == idx, 2.0 * NEG, logits)
    es = [jnp.exp(v - vals[0]) for v in vals]
    denom = es[0] + es[1] + es[2] + es[3]

    onehot = jnp.zeros((tm, LANES), F32)
    for idx in idxs:
        onehot = jnp.where(lane == idx, 1.0, onehot)
    r_i = lax.broadcasted_iota(I32, (tm, tm), 0)
    c_i = lax.broadcasted_iota(I32, (tm, tm), 1)
    tri = jnp.where(c_i < r_i, 1.0, 0.0).astype(BF16)
    before = _dot(tri, onehot.astype(BF16)) + carry_ref[0:1, :]

    rt = jnp.zeros((tm, LANES), F32)
    for k in range(TOP_K):
        rank = jnp.sum(jnp.where(lane == idxs[k], before, 0.0), axis=-1, keepdims=True)
        rt = jnp.where(lane == k, idxs[k], rt)
        rt = jnp.where(lane == TOP_K + k, es[k] / denom, rt)
        rt = jnp.where(lane == 2 * TOP_K + k, rank, rt)
    rt_ref[0] = rt

    carry_ref[...] = carry_ref[...] + jnp.sum(onehot, axis=0, keepdims=True)
    cnt_ref[0] = carry_ref[...]


def _postmix_into_kernel(pool_ref, *refs, tm):
    del pool_ref
    _postmix_kernel(*refs, tm=tm)


def _postmix(attn, conv, x, mod, w_out_bf, g_post_mix, g_pre_ffn, w_router_pad, b_router_pad, counts_in, tm,
             pool, pool_rows, pool_first):
    B, L, _ = x.shape
    assert pool_first % tm == 0
    row = lambda b, s: (b, s, 0)
    const2 = lambda b, s: (0, 0)
    body, aliases, lead_specs, lead_args = _postmix_kernel, {}, [], ()
    if pool is not None:
        body, aliases = _postmix_into_kernel, {0: 1}
        lead_specs, lead_args = [pl.BlockSpec(memory_space=pl.ANY)], (pool,)
    return pl.pallas_call(
        functools.partial(body, tm=tm),
        grid=(B, L // tm),
        input_output_aliases=aliases,
        in_specs=lead_specs + [
                  pl.BlockSpec((1, tm, ATTN_W), row),
                  pl.BlockSpec((1, tm, CONV_CH), row),
                  pl.BlockSpec((1, tm, D_MODEL), row),
                  pl.BlockSpec((1, N_MOD, D_MODEL), lambda b, s: (b, 0, 0)),
                  pl.BlockSpec((D_MODEL, D_MODEL), const2),
                  pl.BlockSpec((1, D_MODEL), const2),
                  pl.BlockSpec((1, D_MODEL), const2),
                  pl.BlockSpec((D_MODEL, LANES), const2),
                  pl.BlockSpec((1, LANES), const2),
                  pl.BlockSpec((8, LANES), const2)],
        out_specs=[pl.BlockSpec((1, tm, D_MODEL), row),
                   pl.BlockSpec((tm, HALF), lambda b, s: (pool_first // tm + b * (L // tm) + s, 0)),
                   pl.BlockSpec((1, tm, LANES), row),
                   pl.BlockSpec((1, SUBLANES, LANES), lambda b, s: (b, 0, 0))],
        out_shape=[jax.ShapeDtypeStruct((B, L, D_MODEL), F32),
                   jax.ShapeDtypeStruct((pool_rows, HALF), I32),
                   jax.ShapeDtypeStruct((B, L, LANES), F32),
                   jax.ShapeDtypeStruct((B, SUBLANES, LANES), F32)],
        scratch_shapes=[pltpu.VMEM((8, LANES), F32)],
        compiler_params=pltpu.CompilerParams(dimension_semantics=("arbitrary", "arbitrary"),
                                             vmem_limit_bytes=VMEM_LIMIT),
        name="postmix",
    )(*lead_args, attn, conv, x, mod, w_out_bf, g_post_mix.reshape(1, D_MODEL), g_pre_ffn.reshape(1, D_MODEL),
      w_router_pad, b_router_pad, counts_in)


def _expert_kernel(be_ref, nv_ref, x_ref, wgu_ref, bgu_ref, wd_ref, bd_ref, y_ref, wgu_bf, wd_bf):
    i = pl.program_id(0)
    ff_half = D_FF // 2

    @pl.when(jnp.logical_or(i == 0, be_ref[i] != be_ref[jnp.maximum(i - 1, 0)]))
    def _():
        wgu_bf[...] = wgu_ref[0].astype(BF16)
        wd_bf[...] = wd_ref[0].astype(BF16)

    @pl.when(nv_ref[i] > 0)
    def _():
        x_lo, x_hi = _unpack_halves(x_ref[...])
        x_lo = x_lo.astype(BF16)
        x_hi = x_hi.astype(BF16)
        acc = None
        for c in range(2):
            lo, hi = c * ff_half, (c + 1) * ff_half
            g = (_dot(x_lo, wgu_bf[0:HALF, lo:hi]) + _dot(x_hi, wgu_bf[HALF:D_MODEL, lo:hi])
                 + bgu_ref[0, :, lo:hi])
            lin = (_dot(x_lo, wgu_bf[0:HALF, D_FF + lo:D_FF + hi]) + _dot(x_hi, wgu_bf[HALF:D_MODEL, D_FF + lo:D_FF + hi])
                   + bgu_ref[0, :, D_FF + lo:D_FF + hi])
            g = jnp.minimum(g, SWIGLU_LIMIT)
            lin = jnp.clip(lin, -SWIGLU_LIMIT, SWIGLU_LIMIT)
            act = g * _sigmoid(SWIGLU_ALPHA * g) * (lin + 1.0)
            part = _dot(act.astype(BF16), wd_bf[lo:hi, :])
            acc = part if acc is None else acc + part
        y_ref[...] = _pack_halves(acc + bd_ref[0])


def _experts(x_pad, block_e, n_valid, w_gu, b_gu, w_down, b_down, blk):
    n_blocks = block_e.shape[0]
    ex = lambda i, be, nv: (be[i], 0, 0)
    rows = lambda i, be, nv: (i, 0)
    return pl.pallas_call(
        _expert_kernel,
        grid_spec=pltpu.PrefetchScalarGridSpec(
            num_scalar_prefetch=2,
            grid=(n_blocks,),
            in_specs=[pl.BlockSpec((blk, HALF), rows),
                      pl.BlockSpec((1, D_MODEL, 2 * D_FF), ex), pl.BlockSpec((1, 1, 2 * D_FF), ex),
                      pl.BlockSpec((1, D_FF, D_MODEL), ex), pl.BlockSpec((1, 1, D_MODEL), ex)],
            out_specs=pl.BlockSpec((blk, HALF), rows),
            scratch_shapes=[pltpu.VMEM((D_MODEL, 2 * D_FF), BF16), pltpu.VMEM((D_FF, D_MODEL), BF16)]),
        out_shape=jax.ShapeDtypeStruct((n_blocks * blk, HALF), I32),
        compiler_params=pltpu.CompilerParams(dimension_semantics=("arbitrary",),
                                             vmem_limit_bytes=EXPERT_VMEM_LIMIT),
        name="experts",
    )(block_e, n_valid, x_pad, w_gu, b_gu.reshape(N_EXPERTS, 1, 2 * D_FF), w_down,
      b_down.reshape(N_EXPERTS, 1, D_MODEL))


def _combine_kernel(y0_ref, y1_ref, y2_ref, y3_ref, rt_ref, x1_ref, mod_ref, g_ref, o_ref):
    rt = rt_ref[0]
    f_lo = f_hi = None
    for k, y_ref in enumerate((y0_ref, y1_ref, y2_ref, y3_ref)):
        lo, hi = _unpack_halves(y_ref[...])
        w = rt[:, TOP_K + k:TOP_K + k + 1]
        f_lo = w * lo if f_lo is None else f_lo + w * lo
        f_hi = w * hi if f_hi is None else f_hi + w * hi
    ms = (jnp.sum(f_lo * f_lo, axis=-1, keepdims=True) + jnp.sum(f_hi * f_hi, axis=-1, keepdims=True)) / D_MODEL
    r = lax.rsqrt(ms + EPS)
    gate_f = mod_ref[0, 5:6, :]
    g = g_ref[...]
    o_ref[0, :, 0:HALF] = x1_ref[0, :, 0:HALF] + gate_f[:, 0:HALF] * (f_lo * r * g[:, 0:HALF])
    o_ref[0, :, HALF:D_MODEL] = x1_ref[0, :, HALF:D_MODEL] + gate_f[:, HALF:D_MODEL] * (f_hi * r * g[:, HALF:D_MODEL])


def _combine_into_kernel(prev_ref, *refs):
    del prev_ref
    _combine_kernel(*refs)


def _combine(y, route, x1, mod, g_post_ffn, tm, t_all, t_first, b_first=0, n_batch=None, out_prev=None):
    B, L, _ = x1.shape
    n_batch = B if n_batch is None else n_batch
    assert t_all % tm == 0 and t_first % tm == 0
    row = lambda b, s: (b + b_first, s, 0)
    y_specs = [pl.BlockSpec((tm, HALF), lambda b, s, k=k: ((k * t_all + t_first) // tm + b * (L // tm) + s, 0))
               for k in range(TOP_K)]
    in_specs = y_specs + [pl.BlockSpec((1, tm, LANES), row),
                          pl.BlockSpec((1, tm, D_MODEL), row),
                          pl.BlockSpec((1, N_MOD, D_MODEL), lambda b, s: (b + b_first, 0, 0)),
                          pl.BlockSpec((1, D_MODEL), lambda b, s: (0, 0))]
    args = (y, y, y, y, route, x1, mod, g_post_ffn.reshape(1, D_MODEL))
    body, aliases = _combine_kernel, {}
    if out_prev is not None:
        body, aliases = _combine_into_kernel, {0: 0}
        in_specs = [pl.BlockSpec(memory_space=pl.ANY)] + in_specs
        args = (out_prev,) + args
    return pl.pallas_call(
        body,
        grid=(n_batch, L // tm),
        in_specs=in_specs,
        out_specs=pl.BlockSpec((1, tm, D_MODEL), row),
        out_shape=jax.ShapeDtypeStruct((B, L, D_MODEL), F32),
        input_output_aliases=aliases,
        compiler_params=pltpu.CompilerParams(dimension_semantics=("arbitrary", "arbitrary"),
                                             vmem_limit_bytes=VMEM_LIMIT),
        name="combine",
    )(*args)


SC_GATHER_BYTES = 128 * 1024


def _scatter_quantum(width):
    rows = SC_GATHER_BYTES // (width * 4)
    sc = plsc.get_sparse_core_info()
    return rows, sc.num_cores * sc.num_subcores * rows * 2


def _scatter_rows(table, row_first, dest, n_slots):
    n, width = dest.shape[0], table.shape[1]
    rows, quantum = _scatter_quantum(width)
    sc = plsc.get_sparse_core_info()
    workers = sc.num_cores * sc.num_subcores
    n_pad = -(-n // quantum) * quantum
    extra = n_pad - n
    assert row_first % SUBLANES == 0 and n % rows == 0 and n >= rows and row_first + n <= table.shape[0]
    last_window = row_first + n - rows
    spare = n_slots + jnp.arange(extra * TOP_K, dtype=I32).reshape(extra, TOP_K)
    per_w = n_pad // workers
    wins = per_w // rows
    wins_tile = -(-wins // SUBLANES) * SUBLANES
    idx = jnp.concatenate([dest, spare], axis=0).T.reshape(TOP_K, workers, wins, rows)
    idx = jnp.pad(idx, ((0, 0), (0, 0), (0, wins_tile - wins), (0, 0))).reshape(TOP_K, workers * wins_tile, rows)
    mesh = plsc.VectorSubcoreMesh(core_axis_name="c", subcore_axis_name="s")
    buf = pltpu.VMEM((rows, width), table.dtype)

    @functools.partial(
        pl.kernel, mesh=mesh,
        out_type=jax.ShapeDtypeStruct((n_slots + extra * TOP_K, width), table.dtype),
        scratch_types=[pltpu.VMEM((TOP_K, wins_tile, rows), I32), buf, buf,
                       pltpu.SemaphoreType.DMA, pltpu.SemaphoreType.DMA],
        name="scatter_rows",
    )
    def scatter_kernel(table_hbm, idx_hbm, out_hbm, idx_v, buf0, buf1, sem0, sem1):
        w = lax.axis_index("s") * sc.num_cores + lax.axis_index("c")
        first = pl.multiple_of(w * wins_tile, SUBLANES)
        for k in range(TOP_K):
            pltpu.sync_copy(idx_hbm.at[k, pl.ds(first, wins_tile)], idx_v.at[k])
        base = row_first + w * per_w

        def window(i, buf, sem):
            start = pl.multiple_of(jnp.minimum(base + i * rows, last_window), SUBLANES)
            pltpu.sync_copy(table_hbm.at[pl.ds(start, rows)], buf)
            copies = [pltpu.async_copy(buf, out_hbm.at[idx_v.at[k, i]], sem) for k in range(TOP_K)]
            return copies

        @pl.loop(0, wins, step=2)
        def _(i):
            c0 = window(i, buf0, sem0)
            c1 = window(i + 1, buf1, sem1)
            for c in c0 + c1:
                c.wait()

    return scatter_kernel(table, idx)


def _gather_rows(table, idx):
    n = idx.shape[0]
    width = table.shape[1]
    rows = SC_GATHER_BYTES // (width * 4)
    sc = plsc.get_sparse_core_info()
    workers = sc.num_cores * sc.num_subcores
    quantum = workers * rows * 2
    n_pad = -(-n // quantum) * quantum
    filler = jnp.arange(n_pad - n, dtype=I32) % table.shape[0]
    per_w = n_pad // workers
    wins = per_w // rows
    wins_tile = -(-wins // SUBLANES) * SUBLANES
    idx = jnp.concatenate([idx, filler]).reshape(workers, wins, rows)
    idx = jnp.pad(idx, ((0, 0), (0, wins_tile - wins), (0, 0))).reshape(workers * wins_tile, rows)
    mesh = plsc.VectorSubcoreMesh(core_axis_name="c", subcore_axis_name="s")
    buf = pltpu.VMEM((rows, width), table.dtype)

    @functools.partial(
        pl.kernel, mesh=mesh,
        out_type=jax.ShapeDtypeStruct((n_pad, width), table.dtype),
        scratch_types=[pltpu.VMEM((wins_tile, rows), I32), buf, buf,
                       pltpu.SemaphoreType.DMA, pltpu.SemaphoreType.DMA],
        name="gather_rows",
    )
    def gather_kernel(table_hbm, idx_hbm, out_hbm, idx_v, buf0, buf1, sem0, sem1):
        w = lax.axis_index("s") * sc.num_cores + lax.axis_index("c")
        pltpu.sync_copy(idx_hbm.at[pl.ds(pl.multiple_of(w * wins_tile, SUBLANES), wins_tile)], idx_v)
        base = pl.multiple_of(w * per_w, rows)

        @pl.loop(0, wins, step=2)
        def _(i):
            g0 = pltpu.async_copy(table_hbm.at[idx_v.at[i]], buf0, sem0)
            g1 = pltpu.async_copy(table_hbm.at[idx_v.at[i + 1]], buf1, sem1)
            g0.wait()
            pltpu.sync_copy(buf0, out_hbm.at[pl.ds(base + i * rows, rows)])
            g1.wait()
            pltpu.sync_copy(buf1, out_hbm.at[pl.ds(base + (i + 1) * rows, rows)])

    return gather_kernel(table, idx)


def _routing_tables(route, counts, counts_before, blk):
    T = route.shape[0]
    TK = T * TOP_K
    idx = route[:, 0:TOP_K].astype(I32)
    rank = route[:, 2 * TOP_K:3 * TOP_K].astype(I32) - counts_before[idx]
    padded = (counts + blk - 1) // blk * blk
    pad_end = jnp.cumsum(padded)
    pad_start = pad_end - padded
    dest = pad_start[idx] + rank
    n_blocks = -(-TK // blk) + N_EXPERTS
    blk_start = jnp.arange(n_blocks, dtype=I32) * blk
    block_e = jnp.sum((blk_start[:, None] >= pad_end[None, :]).astype(I32), axis=1)
    block_e = jnp.minimum(block_e, N_EXPERTS - 1)
    last = (pad_start + counts)[block_e]
    n_valid = jnp.clip(last - blk_start, 0, blk)
    return block_e, n_valid, dest


def _mix(x, mod, buf, past_k, past_v, p, lam_init, counts_in, tm, tq, pool, pool_rows, pool_first):
    B, L, _ = x.shape
    q, kb, vb, kf, vf, conv, state = _premix(x, mod, p["g_pre_mix"], p["w_in"], buf, p["w_dw"], p["b_dw"],
                                             p["gmat"], p["g_cn"], p["b_cn"], tm)
    if past_k is None:
        attn = _attention(q, kb, vb, p["lam_rows"], p["g_subln"], lam_init, tq, tq, 0, True)
    else:
        P = past_k.shape[1]
        keys = jnp.concatenate([past_k.reshape(B, P, QK_W).astype(BF16), kb], axis=1)
        vals = jnp.concatenate([past_v.reshape(B, P, ATTN_W).astype(BF16), vb], axis=1)
        attn = _attention(q, keys, vals, p["lam_rows"], p["g_subln"], lam_init, L, P + L, P, False)
    x1, h2, route, cnt = _postmix(attn, conv, x, mod, p["w_out"], p["g_post_mix"], p["g_pre_ffn"],
                                  p["w_router"], p["b_router"], counts_in, tm, pool, pool_rows, pool_first)
    return x1, h2, route, cnt, kf, vf, state[:, CONV_HALO - (CONV_W - 1):, :]


def _moe_rows(pool, row_first, route_groups, counts, counts_before, p, blk):
    route = jnp.concatenate([r.reshape(-1, LANES) for r in route_groups], axis=0)
    T = route.shape[0]
    block_e, n_valid, dest = _routing_tables(route, counts, counts_before, blk)
    x_pad = _scatter_rows(pool, row_first, dest, block_e.shape[0] * blk)
    y_pad = _experts(x_pad, block_e, n_valid, p["w_gu"], p["b_gu"], p["w_down"], p["b_down"], blk)
    return _gather_rows(y_pad, dest.T.reshape(T * TOP_K))


def _prepare_params(l, w_ada, b_ada, g_pre_mix, g_post_mix, w_in, lambda_q1, lambda_k1, lambda_q2, lambda_k2,
                    g_subln, w_dw, b_dw, g_cnorm, b_cnorm, w_out, g_pre_ffn, g_post_ffn,
                    w_router, b_router, w_gu, b_gu, w_down, b_down):
    lam_rows = jnp.zeros((8, LANES), F32)
    for r, vec in enumerate((lambda_q1[l], lambda_k1[l], lambda_q2[l], lambda_k2[l])):
        lam_rows = lam_rows.at[r, :HEAD_DIM].set(vec)
    ch = jnp.arange(CONV_CH, dtype=I32) // GROUP_CH
    gmat = (ch[:, None] == ch[None, :]).astype(BF16)
    w_dw_pad = jnp.zeros((CONV_HALO, CONV_CH), F32).at[:CONV_W].set(w_dw[l])
    w_router_pad = jnp.zeros((D_MODEL, LANES), F32).at[:, :N_EXPERTS].set(w_router[l])
    b_router_pad = jnp.full((1, LANES), NEG, F32).at[0, :N_EXPERTS].set(b_router[l])
    return dict(w_ada=w_ada[l], b_ada=b_ada[l], g_pre_mix=g_pre_mix[l], g_post_mix=g_post_mix[l],
                w_in=w_in[l].astype(BF16), lam_rows=lam_rows, g_subln=g_subln[l], w_dw=w_dw_pad, b_dw=b_dw[l],
                gmat=gmat, g_cn=g_cnorm[l], b_cn=b_cnorm[l], w_out=w_out[l].astype(BF16),
                g_pre_ffn=g_pre_ffn[l], g_post_ffn=g_post_ffn[l], w_router=w_router_pad, b_router=b_router_pad,
                w_gu=w_gu[l], b_gu=b_gu[l], w_down=w_down[l], b_down=b_down[l])


def kernel(x_prompt, x_sample, cache_k, cache_v, state_conv, c_prompt, c_sample, w_ada, b_ada, g_pre_mix, g_post_mix, w_in, lambda_q1, lambda_k1, lambda_q2, lambda_k2, g_subln, w_dw, b_dw, g_cnorm, b_cnorm, w_out, g_pre_ffn, g_post_ffn, w_router, b_router, w_gu, b_gu, w_down, b_down):
    depth = w_ada.shape[0]
    Bp, Lp, _ = x_prompt.shape
    Bs, Ls, _ = x_sample.shape
    yp, ys = x_prompt, x_sample
    outs = [[] for _ in range(6)]
    for l in range(depth):
        p = _prepare_params(l, w_ada, b_ada, g_pre_mix, g_post_mix, w_in, lambda_q1, lambda_k1, lambda_q2,
                            lambda_k2, g_subln, w_dw, b_dw, g_cnorm, b_cnorm, w_out, g_pre_ffn, g_post_ffn,
                            w_router, b_router, w_gu, b_gu, w_down, b_down)
        lam_init = 0.8 - 0.6 * math.exp(-0.3 * l)
        mod = _ada(jnp.concatenate([c_prompt, c_sample], axis=0), p["w_ada"], p["b_ada"])
        mod = mod.reshape(Bp + Bs, N_MOD, D_MODEL)
        buf_p = jnp.zeros((Bp, CONV_HALO, CONV_CH), F32)
        buf_s = jnp.pad(state_conv[l], ((0, 0), (CONV_HALO - (CONV_W - 1), 0), (0, 0)))
        tm_p = min(Lp, 512)
        assert Bp % 2 == 0
        hb = Bp // 2
        t_a, t_b = hb * Lp, hb * Lp + Bs * Ls
        pool_rows = Bp * Lp + Bs * Ls
        zero_counts = jnp.zeros((SUBLANES, LANES), F32)
        x1p, pool, rtp, cnt_p, kp, vp, cp = _mix(yp, mod[:Bp], buf_p, None, None, p, lam_init, zero_counts,
                                                 tm_p, tm_p, None, pool_rows, 0)
        x1s, pool, rts, cnt_s, ks, vs, cs = _mix(ys, mod[Bp:], buf_s, cache_k[l], cache_v[l], p, lam_init,
                                                 cnt_p[Bp - 1], Ls, Ls, pool, pool_rows, Bp * Lp)
        n_a = cnt_p[hb - 1, 0, :N_EXPERTS].astype(I32)
        n_b = cnt_s[Bs - 1, 0, :N_EXPERTS].astype(I32) - n_a
        y_a = _moe_rows(pool, 0, (rtp[:hb],), n_a, jnp.zeros_like(n_a), p, EXPERT_ROWS)
        y_b = _moe_rows(pool, t_a, (rtp[hb:], rts), n_b, n_a, p, EXPERT_ROWS)
        tm_c = min(tm_p, 256)
        yp = _combine(y_a, rtp, x1p, mod[:Bp], p["g_post_ffn"], tm_c, t_a, 0, 0, hb)
        yp = _combine(y_b, rtp, x1p, mod[:Bp], p["g_post_ffn"], tm_c, t_b, 0, hb, hb, out_prev=yp)
        ys = _combine(y_b, rts, x1s, mod[Bp:], p["g_post_ffn"], Ls, t_b, hb * Lp)
        for lst, val in zip(outs, (kp, vp, cp, ks, vs, cs)):
            lst.append(val)
    return (yp, ys) + tuple(jnp.stack(o) for o in outs)
```

```python
import functools
import math

import jax
import jax.numpy as jnp
from jax import lax
from jax.experimental import pallas as pl
from jax.experimental.pallas import tpu as pltpu
from jax.experimental.pallas import tpu_sc as plsc

F32 = jnp.float32
BF16 = jnp.bfloat16
I32 = jnp.int32

D_MODEL = 1024
CHUNK = 64
CHUNK_SHIFT = 6
N_HEADS = 4
V_DIM = 128
HEAD_DIM = 64
QK_W = 512
ATTN_W = 512
CONV_CH = 512
CONV_W = 31
CONV_GROUPS = 8
GROUP_CH = CONV_CH // CONV_GROUPS
IN_W = 2 * QK_W + ATTN_W + 2 * CONV_CH
N_MOD = 6
N_EXPERTS = 32
TOP_K = 4
D_FF = 1024
SWIGLU_ALPHA = 1.702
SWIGLU_LIMIT = 7.0
EPS = 1e-6

LANES = 128
SUBLANES = 8
CONV_HALO = 32
CONV_ROWS = 32
NEG = -1e30
LOG2E = math.log2(math.e)
ATTN_ROWS = 128
EXPERT_ROWS = 512
VMEM_LIMIT = 48 * 1024 * 1024
EXPERT_VMEM_LIMIT = 56 * 1024 * 1024


def _sigmoid(x):
    return 1.0 / (1.0 + jnp.exp(-x))


def _split_bf16(x):
    hi = x.astype(BF16)
    lo = (x - hi.astype(F32)).astype(BF16)
    return hi, lo


def _dot(a, b):
    return jnp.dot(a, b, preferred_element_type=F32)


def _dot3(a, b):
    ah, al = _split_bf16(a)
    bh, bl = _split_bf16(b)
    return _dot(ah, bh) + _dot(ah, bl) + _dot(al, bh)


def _rms(x):
    return x * lax.rsqrt(jnp.mean(x * x, axis=-1, keepdims=True) + EPS)


HALF = D_MODEL // 2
HIGH16 = -65536


def _pack_halves(x):
    lo = lax.bitcast_convert_type(x[:, :HALF].astype(BF16).astype(F32), I32)
    hi = lax.bitcast_convert_type(x[:, HALF:].astype(BF16).astype(F32), I32)
    return jnp.bitwise_or(jnp.bitwise_and(jnp.right_shift(lo, 16), 0xFFFF), jnp.bitwise_and(hi, HIGH16))


def _unpack_halves(w):
    lo = lax.bitcast_convert_type(jnp.left_shift(w, 16), F32)
    hi = lax.bitcast_convert_type(jnp.bitwise_and(w, HIGH16), F32)
    return lo, hi


def _ada_kernel(c_ref, w_ref, b_ref, o_ref):
    c = c_ref[...]
    o_ref[...] = _dot3(c * _sigmoid(c), w_ref[...]) + b_ref[...]


def _ada(c, w_ada, b_ada):
    n = c.shape[0]
    return pl.pallas_call(
        _ada_kernel,
        grid=(N_MOD,),
        in_specs=[pl.BlockSpec((n, D_MODEL), lambda j: (0, 0)),
                  pl.BlockSpec((D_MODEL, D_MODEL), lambda j: (0, j)),
                  pl.BlockSpec((1, D_MODEL), lambda j: (0, j))],
        out_specs=pl.BlockSpec((n, D_MODEL), lambda j: (0, j)),
        out_shape=jax.ShapeDtypeStruct((n, N_MOD * D_MODEL), F32),
        compiler_params=pltpu.CompilerParams(dimension_semantics=("arbitrary",), vmem_limit_bytes=VMEM_LIMIT),
        name="ada",
    )(c, w_ada, b_ada.reshape(1, N_MOD * D_MODEL))


def _premix_kernel(x_ref, mod_ref, g_ref, w_ref, buf_ref, wdw_ref, bdw_ref, gmat_ref, gcn_ref, bcn_ref,
                   q_ref, kb_ref, vb_ref, kf_hbm, vf_hbm, co_ref, st_ref, ext_ref, sh_ref, y_ref, kv_buf, kv_sem,
                   *, tm):
    batch = pl.program_id(0)
    s = pl.program_id(1)
    step = batch * pl.num_programs(1) + s
    last = pl.num_programs(0) * pl.num_programs(1) - 1
    slot = step % 2

    def kv_copies(slot):
        return [pltpu.make_async_copy(kv_buf.at[slot, i, :, pl.ds(h * V_DIM, V_DIM)],
                                      out.at[batch, pl.ds(s * tm, tm), h, :], kv_sem.at[slot, i, h])
                for i, out in enumerate((kf_hbm, vf_hbm)) for h in range(N_HEADS)]

    @pl.when(step >= 2)
    def _():
        for c in kv_copies(slot):
            c.wait()

    @pl.when(s == 0)
    def _():
        ext_ref[0:CONV_HALO, :] = buf_ref[0]

    x = x_ref[0]
    shift = mod_ref[0, 0:1, :]
    scale = mod_ref[0, 1:2, :]
    h = _rms(x) * g_ref[...] * (1.0 + scale) + shift
    hb = h.astype(BF16)

    u0 = 2 * QK_W + ATTN_W
    val = _dot(hb, w_ref[:, u0:u0 + CONV_CH])
    gate = _dot(hb, w_ref[:, u0 + CONV_CH:u0 + 2 * CONV_CH])
    ext_ref[CONV_HALO:CONV_HALO + tm, :] = val * _sigmoid(gate)

    off = CONV_HALO - (CONV_W - 1)
    span = tm + CONV_HALO - SUBLANES
    for b in range(1, SUBLANES):
        sh_ref[b - 1, 0:span, :] = ext_ref[b:b + span, :]
    for c in range(tm // CONV_ROWS):
        r0 = c * CONV_ROWS
        acc = jnp.zeros((CONV_ROWS, CONV_CH), F32)
        for j in range(CONV_W):
            b = (j + off) % SUBLANES
            a = r0 + j + off - b
            rows_j = ext_ref[a:a + CONV_ROWS, :] if b == 0 else sh_ref[b - 1, a:a + CONV_ROWS, :]
            acc = acc + wdw_ref[j:j + 1, :] * rows_j
        y_ref[r0:r0 + CONV_ROWS, :] = acc + bdw_ref[...]

    zq = _dot(hb, w_ref[:, 0:QK_W])
    q_ref[0] = (zq * (HEAD_DIM ** -0.5 * LOG2E)).astype(BF16)
    zk = _dot(hb, w_ref[:, QK_W:2 * QK_W])
    kv_buf[slot, 0] = zk
    kb_ref[0] = zk.astype(BF16)
    zv = _dot(hb, w_ref[:, 2 * QK_W:2 * QK_W + ATTN_W])
    kv_buf[slot, 1] = zv
    vb_ref[0] = zv.astype(BF16)

    y = y_ref[...]
    gm = gmat_ref[...]
    yh, yl = _split_bf16(y)
    mu = (_dot(yh, gm) + _dot(yl, gm)) * (1.0 / GROUP_CH)
    d = y - mu
    dh, dl = _split_bf16(d * d)
    var = (_dot(dh, gm) + _dot(dl, gm)) * (1.0 / GROUP_CH)
    yn = d * lax.rsqrt(var + EPS) * gcn_ref[...] + bcn_ref[...]
    co_ref[0] = (yn * _sigmoid(yn)).astype(BF16)

    tail = ext_ref[tm:tm + CONV_HALO, :]
    st_ref[0] = tail
    ext_ref[0:CONV_HALO, :] = tail

    for c in kv_copies(slot):
        c.start()

    @pl.when(step == last)
    def _():
        for c in kv_copies(slot):
            c.wait()

    @pl.when(jnp.logical_and(step == last, step >= 1))
    def _():
        for c in kv_copies(1 - slot):
            c.wait()


def _premix(x, mod, g_pre_mix, w_in_bf, buf, w_dw, b_dw, gmat, g_cn, b_cn, tm):
    B, L, _ = x.shape
    assert L % tm == 0 and tm % CONV_ROWS == 0
    row = lambda b, s: (b, s, 0)
    const2 = lambda b, s: (0, 0)
    bf_tile = jax.ShapeDtypeStruct((B, L, QK_W), BF16)
    f_heads = jax.ShapeDtypeStruct((B, L, N_HEADS, V_DIM), F32)
    return pl.pallas_call(
        functools.partial(_premix_kernel, tm=tm),
        grid=(B, L // tm),
        in_specs=[pl.BlockSpec((1, tm, D_MODEL), row),
                  pl.BlockSpec((1, N_MOD, D_MODEL), lambda b, s: (b, 0, 0)),
                  pl.BlockSpec((1, D_MODEL), const2),
                  pl.BlockSpec((D_MODEL, IN_W), const2),
                  pl.BlockSpec((1, CONV_HALO, CONV_CH), lambda b, s: (b, 0, 0)),
                  pl.BlockSpec((CONV_HALO, CONV_CH), const2),
                  pl.BlockSpec((1, CONV_CH), const2),
                  pl.BlockSpec((CONV_CH, CONV_CH), const2),
                  pl.BlockSpec((1, CONV_CH), const2),
                  pl.BlockSpec((1, CONV_CH), const2)],
        out_specs=[pl.BlockSpec((1, tm, QK_W), row)] * 3
                  + [pl.BlockSpec(memory_space=pl.ANY)] * 2
                  + [pl.BlockSpec((1, tm, QK_W), row),
                     pl.BlockSpec((1, CONV_HALO, CONV_CH), lambda b, s: (b, 0, 0))],
        out_shape=[bf_tile, bf_tile, bf_tile, f_heads, f_heads, bf_tile,
                   jax.ShapeDtypeStruct((B, CONV_HALO, CONV_CH), F32)],
        scratch_shapes=[pltpu.VMEM((CONV_HALO + tm, CONV_CH), F32),
                        pltpu.VMEM((SUBLANES - 1, CONV_HALO + tm, CONV_CH), F32),
                        pltpu.VMEM((tm, CONV_CH), F32),
                        pltpu.VMEM((2, 2, tm, QK_W), F32),
                        pltpu.SemaphoreType.DMA((2, 2, N_HEADS))],
        compiler_params=pltpu.CompilerParams(dimension_semantics=("arbitrary", "arbitrary"),
                                             vmem_limit_bytes=VMEM_LIMIT),
        name="premix",
    )(x, mod, g_pre_mix.reshape(1, D_MODEL), w_in_bf, buf, w_dw, b_dw.reshape(1, CONV_CH), gmat,
      g_cn.reshape(1, CONV_CH), b_cn.reshape(1, CONV_CH))


def _attn_kernel(slopes_ref, q_ref, k_ref, v_ref, kb_ref, mt_ref, lam_ref, gs_ref, o_ref,
                 ke1_ref, ke2_ref, ve_ref, m1_ref, a1_ref, m2_ref, a2_ref,
                 *, tq, tk, rows, q_offset, causal, lam_init):
    slope = slopes_ref[pl.program_id(1)] * LOG2E
    lq = q_ref.shape[1]
    lk = k_ref.shape[1]
    nq = lq // tq
    lane = lax.broadcasted_iota(I32, (1, V_DIM), 1)

    low = jnp.broadcast_to(lane < HEAD_DIM, (lk, V_DIM))
    k = k_ref[0]
    zero = jnp.zeros((lk, V_DIM), BF16)
    ke1_ref[...] = jnp.where(low, k, zero)
    ke2_ref[...] = jnp.where(low, zero, k)
    ve_ref[:, 0:V_DIM] = v_ref[0]
    ve_ref[:, V_DIM:2 * V_DIM] = jnp.broadcast_to(jnp.where(lane == 0, 1.0, 0.0), (lk, V_DIM)).astype(BF16)

    nt = (((1,), (1,)), ((), ()))
    maps = ((ke1_ref, m1_ref, a1_ref), (ke2_ref, m2_ref, a2_ref))

    def lanes(x, n):
        if n % LANES == 0:
            return jnp.concatenate([x] * (n // LANES), axis=1)
        return jnp.broadcast_to(x[:, 0:1], (rows, n))

    def tile(q0, k0, masked):
        ve = ve_ref[pl.ds(k0, tk), :]
        q_first = (lax.broadcasted_iota(I32, (1, 1), 0) + (q0 + q_offset)).astype(F32)
        bias = kb_ref[0, :, pl.ds(k0, tk)] - slope * q_first
        for ke_ref, m_ref, a_ref in maps:
            ke = ke_ref[pl.ds(k0, tk), :]
            for r0 in range(0, tq, rows):
                s = lax.dot_general(q_ref[0, pl.ds(q0 + r0, rows), :], ke, nt, preferred_element_type=F32)
                s = s + bias
                if masked:
                    s = s + mt_ref[0, r0:r0 + rows, :]
                m_old = m_ref[r0:r0 + rows, :]
                m_new = jnp.maximum(m_old, jnp.max(s, axis=-1, keepdims=True))
                p = jnp.exp2(s - lanes(m_new, tk))
                alpha = jnp.exp2(m_old - m_new)
                a_ref[r0:r0 + rows, :] = (lanes(alpha, 2 * V_DIM) * a_ref[r0:r0 + rows, :]
                                          + _dot(p.astype(BF16), ve))
                m_ref[r0:r0 + rows, :] = m_new

    def reset():
        for _, m_ref, a_ref in maps:
            m_ref[...] = jnp.full(m_ref.shape, NEG, F32)
            a_ref[...] = jnp.zeros(a_ref.shape, F32)

    lv = lam_ref[...]
    lam = (jnp.exp(jnp.sum(lv[0:1, :] * lv[1:2, :], axis=-1, keepdims=True))
           - jnp.exp(jnp.sum(lv[2:3, :] * lv[3:4, :], axis=-1, keepdims=True)) + lam_init)

    def finish(q0):
        a1 = a1_ref[...]
        a2 = a2_ref[...]
        o = a1[:, 0:V_DIM] / a1[:, V_DIM:V_DIM + 1] - lam * (a2[:, 0:V_DIM] / a2[:, V_DIM:V_DIM + 1])
        o_ref[0, pl.ds(q0, tq), :] = (_rms(o) * gs_ref[...] * (1.0 - lam_init)).astype(BF16)

    if not causal:
        reset()
        tile(0, 0, True)
        finish(0)
        return

    for qi in range(nq):
        q0 = qi * tq
        reset()
        for ki in range(qi):
            tile(q0, ki * tk, False)
        tile(q0, qi * tk, True)
        finish(q0)


def _mask_table(slopes, tq, tk, q_offset):
    qpos = q_offset + jnp.arange(tq, dtype=I32)[:, None]
    kpos = jnp.arange(tk, dtype=I32)[None, :]
    visible = jnp.right_shift(kpos, CHUNK_SHIFT) <= jnp.right_shift(qpos, CHUNK_SHIFT)
    fix = jnp.where(kpos > qpos, 2 * (qpos - kpos), 0).astype(F32)
    return jnp.where(visible[None], (slopes * LOG2E)[:, None, None] * fix[None], NEG)


def _attention(q, k, v, lam_rows, g_subln, lam_init, tq, tk, q_offset, causal):
    B, Lq, _ = q.shape
    Lk = k.shape[1]
    assert Lq % tq == 0 and Lk % tk == 0 and (not causal or (tq == tk and tq % CHUNK == 0 and q_offset == 0))
    assert causal or Lk == tk
    rows = min(tq, ATTN_ROWS)
    slopes = jnp.asarray([2.0 ** (-8.0 * (h + 1) / N_HEADS) for h in range(N_HEADS)], F32)
    table = _mask_table(slopes, tq, tk, q_offset)
    key_bias = ((slopes * LOG2E)[:, None] * jnp.arange(Lk, dtype=F32)[None, :]).reshape(N_HEADS, 1, Lk)
    key_ext = pltpu.VMEM((Lk, V_DIM), BF16)
    stat = pltpu.VMEM((tq, LANES), F32)
    acc = pltpu.VMEM((tq, 2 * V_DIM), F32)
    head = lambda b, h, sl: (b, 0, h)
    return pl.pallas_call(
        functools.partial(_attn_kernel, tq=tq, tk=tk, rows=rows, q_offset=q_offset, causal=causal,
                          lam_init=lam_init),
        grid_spec=pltpu.PrefetchScalarGridSpec(
            num_scalar_prefetch=1,
            grid=(B, N_HEADS),
            in_specs=[pl.BlockSpec((1, Lq, V_DIM), head),
                      pl.BlockSpec((1, Lk, V_DIM), head),
                      pl.BlockSpec((1, Lk, V_DIM), head),
                      pl.BlockSpec((1, 1, Lk), lambda b, h, sl: (h, 0, 0)),
                      pl.BlockSpec((1, tq, tk), lambda b, h, sl: (h, 0, 0)),
                      pl.BlockSpec((8, LANES), lambda b, h, sl: (0, 0)),
                      pl.BlockSpec((1, V_DIM), lambda b, h, sl: (0, 0))],
            out_specs=pl.BlockSpec((1, Lq, V_DIM), head),
            scratch_shapes=[key_ext, key_ext, pltpu.VMEM((Lk, 2 * V_DIM), BF16), stat, acc, stat, acc]),
        out_shape=jax.ShapeDtypeStruct((B, Lq, ATTN_W), BF16),
        compiler_params=pltpu.CompilerParams(dimension_semantics=("arbitrary",) * 2,
                                             vmem_limit_bytes=VMEM_LIMIT),
        name="attn",
    )(slopes, q, k, v, key_bias, table, lam_rows, g_subln.reshape(1, V_DIM))


def _postmix_kernel(a_ref, c_ref, x_ref, mod_ref, wo_ref, gpm_ref, gpf_ref, wr_ref, br_ref, cin_ref,
                    x1_ref, h2_ref, rt_ref, cnt_ref, carry_ref, *, tm):
    first = jnp.logical_and(pl.program_id(0) == 0, pl.program_id(1) == 0)

    @pl.when(first)
    def _():
        carry_ref[...] = cin_ref[...]

    mix = _dot(a_ref[0], wo_ref[0:ATTN_W, :]) + _dot(c_ref[0], wo_ref[ATTN_W:ATTN_W + CONV_CH, :])
    gate_a = mod_ref[0, 2:3, :]
    shift_f = mod_ref[0, 3:4, :]
    scale_f = mod_ref[0, 4:5, :]
    x1 = x_ref[0] + gate_a * (_rms(mix) * gpm_ref[...])
    x1_ref[0] = x1
    h2 = _rms(x1) * gpf_ref[...] * (1.0 + scale_f) + shift_f
    h2_ref[...] = _pack_halves(h2)

    logits = _dot3(h2, wr_ref[...]) + br_ref[...]
    lane = lax.broadcasted_iota(I32, (tm, LANES), 1).astype(F32)
    vals, idxs = [], []
    for _ in range(TOP_K):
        m = jnp.max(logits, axis=-1, keepdims=True)
        idx = jnp.min(jnp.where(logits == m, lane, float(LANES)), axis=-1, keepdims=True)
        vals.append(m)
        idxs.append(idx)
        logits = jnp.where(lane == idx, 2.0 * NEG, logits)
    es = [jnp.exp(v - vals[0]) for v in vals]
    denom = es[0] + es[1] + es[2] + es[3]

    onehot = jnp.zeros((tm, LANES), F32)
    for idx in idxs:
        onehot = jnp.where(lane == idx, 1.0, onehot)
    r_i = lax.broadcasted_iota(I32, (tm, tm), 0)
    c_i = lax.broadcasted_iota(I32, (tm, tm), 1)
    tri = jnp.where(c_i < r_i, 1.0, 0.0).astype(BF16)
    before = _dot(tri, onehot.astype(BF16)) + carry_ref[0:1, :]

    rt = jnp.zeros((tm, LANES), F32)
    for k in range(TOP_K):
        rank = jnp.sum(jnp.where(lane == idxs[k], before, 0.0), axis=-1, keepdims=True)
        rt = jnp.where(lane == k, idxs[k], rt)
        rt = jnp.where(lane == TOP_K + k, es[k] / denom, rt)
        rt = jnp.where(lane == 2 * TOP_K + k, rank, rt)
    rt_ref[0] = rt

    carry_ref[...] = carry_ref[...] + jnp.sum(onehot, axis=0, keepdims=True)
    cnt_ref[0] = carry_ref[...]


def _postmix_into_kernel(pool_ref, *refs, tm):
    del pool_ref
    _postmix_kernel(*refs, tm=tm)


def _postmix(attn, conv, x, mod, w_out_bf, g_post_mix, g_pre_ffn, w_router_pad, b_router_pad, counts_in, tm,
             pool, pool_rows, pool_first):
    B, L, _ = x.shape
    assert pool_first % tm == 0
    row = lambda b, s: (b, s, 0)
    const2 = lambda b, s: (0, 0)
    body, aliases, lead_specs, lead_args = _postmix_kernel, {}, [], ()
    if pool is not None:
        body, aliases = _postmix_into_kernel, {0: 1}
        lead_specs, lead_args = [pl.BlockSpec(memory_space=pl.ANY)], (pool,)
    return pl.pallas_call(
        functools.partial(body, tm=tm),
        grid=(B, L // tm),
        input_output_aliases=aliases,
        in_specs=lead_specs + [
                  pl.BlockSpec((1, tm, ATTN_W), row),
                  pl.BlockSpec((1, tm, CONV_CH), row),
                  pl.BlockSpec((1, tm, D_MODEL), row),
                  pl.BlockSpec((1, N_MOD, D_MODEL), lambda b, s: (b, 0, 0)),
                  pl.BlockSpec((D_MODEL, D_MODEL), const2),
                  pl.BlockSpec((1, D_MODEL), const2),
                  pl.BlockSpec((1, D_MODEL), const2),
                  pl.BlockSpec((D_MODEL, LANES), const2),
                  pl.BlockSpec((1, LANES), const2),
                  pl.BlockSpec((8, LANES), const2)],
        out_specs=[pl.BlockSpec((1, tm, D_MODEL), row),
                   pl.BlockSpec((tm, HALF), lambda b, s: (pool_first // tm + b * (L // tm) + s, 0)),
                   pl.BlockSpec((1, tm, LANES), row),
                   pl.BlockSpec((1, SUBLANES, LANES), lambda b, s: (b, 0, 0))],
        out_shape=[jax.ShapeDtypeStruct((B, L, D_MODEL), F32),
                   jax.ShapeDtypeStruct((pool_rows, HALF), I32),
                   jax.ShapeDtypeStruct((B, L, LANES), F32),
                   jax.ShapeDtypeStruct((B, SUBLANES, LANES), F32)],
        scratch_shapes=[pltpu.VMEM((8, LANES), F32)],
        compiler_params=pltpu.CompilerParams(dimension_semantics=("arbitrary", "arbitrary"),
                                             vmem_limit_bytes=VMEM_LIMIT),
        name="postmix",
    )(*lead_args, attn, conv, x, mod, w_out_bf, g_post_mix.reshape(1, D_MODEL), g_pre_ffn.reshape(1, D_MODEL),
      w_router_pad, b_router_pad, counts_in)


def _expert_kernel(be_ref, nv_ref, x_ref, wgu_ref, bgu_ref, wd_ref, bd_ref, y_ref, wgu_bf, wd_bf):
    i = pl.program_id(0)
    ff_half = D_FF // 2

    @pl.when(jnp.logical_or(i == 0, be_ref[i] != be_ref[jnp.maximum(i - 1, 0)]))
    def _():
        wgu_bf[...] = wgu_ref[0].astype(BF16)
        wd_bf[...] = wd_ref[0].astype(BF16)

    @pl.when(nv_ref[i] > 0)
    def _():
        x_lo, x_hi = _unpack_halves(x_ref[...])
        x_lo = x_lo.astype(BF16)
        x_hi = x_hi.astype(BF16)
        acc = None
        for c in range(2):
            lo, hi = c * ff_half, (c + 1) * ff_half
            g = (_dot(x_lo, wgu_bf[0:HALF, lo:hi]) + _dot(x_hi, wgu_bf[HALF:D_MODEL, lo:hi])
                 + bgu_ref[0, :, lo:hi])
            lin = (_dot(x_lo, wgu_bf[0:HALF, D_FF + lo:D_FF + hi]) + _dot(x_hi, wgu_bf[HALF:D_MODEL, D_FF + lo:D_FF + hi])
                   + bgu_ref[0, :, D_FF + lo:D_FF + hi])
            g = jnp.minimum(g, SWIGLU_LIMIT)
            lin = jnp.clip(lin, -SWIGLU_LIMIT, SWIGLU_LIMIT)
            act = g * _sigmoid(SWIGLU_ALPHA * g) * (lin + 1.0)
            part = _dot(act.astype(BF16), wd_bf[lo:hi, :])
            acc = part if acc is None else acc + part
        y_ref[...] = _pack_halves(acc + bd_ref[0])


def _experts(x_pad, block_e, n_valid, w_gu, b_gu, w_down, b_down, blk):
    n_blocks = block_e.shape[0]
    ex = lambda i, be, nv: (be[i], 0, 0)
    rows = lambda i, be, nv: (i, 0)
    return pl.pallas_call(
        _expert_kernel,
        grid_spec=pltpu.PrefetchScalarGridSpec(
            num_scalar_prefetch=2,
            grid=(n_blocks,),
            in_specs=[pl.BlockSpec((blk, HALF), rows),
                      pl.BlockSpec((1, D_MODEL, 2 * D_FF), ex), pl.BlockSpec((1, 1, 2 * D_FF), ex),
                      pl.BlockSpec((1, D_FF, D_MODEL), ex), pl.BlockSpec((1, 1, D_MODEL), ex)],
            out_specs=pl.BlockSpec((blk, HALF), rows),
            scratch_shapes=[pltpu.VMEM((D_MODEL, 2 * D_FF), BF16), pltpu.VMEM((D_FF, D_MODEL), BF16)]),
        out_shape=jax.ShapeDtypeStruct((n_blocks * blk, HALF), I32),
        compiler_params=pltpu.CompilerParams(dimension_semantics=("arbitrary",),
                                             vmem_limit_bytes=EXPERT_VMEM_LIMIT),
        name="experts",
    )(block_e, n_valid, x_pad, w_gu, b_gu.reshape(N_EXPERTS, 1, 2 * D_FF), w_down,
      b_down.reshape(N_EXPERTS, 1, D_MODEL))


def _combine_kernel(y0_ref, y1_ref, y2_ref, y3_ref, rt_ref, x1_ref, mod_ref, g_ref, o_ref):
    rt = rt_ref[0]
    f_lo = f_hi = None
    for k, y_ref in enumerate((y0_ref, y1_ref, y2_ref, y3_ref)):
        lo, hi = _unpack_halves(y_ref[...])
        w = rt[:, TOP_K + k:TOP_K + k + 1]
        f_lo = w * lo if f_lo is None else f_lo + w * lo
        f_hi = w * hi if f_hi is None else f_hi + w * hi
    ms = (jnp.sum(f_lo * f_lo, axis=-1, keepdims=True) + jnp.sum(f_hi * f_hi, axis=-1, keepdims=True)) / D_MODEL
    r = lax.rsqrt(ms + EPS)
    gate_f = mod_ref[0, 5:6, :]
    g = g_ref[...]
    o_ref[0, :, 0:HALF] = x1_ref[0, :, 0:HALF] + gate_f[:, 0:HALF] * (f_lo * r * g[:, 0:HALF])
    o_ref[0, :, HALF:D_MODEL] = x1_ref[0, :, HALF:D_MODEL] + gate_f[:, HALF:D_MODEL] * (f_hi * r * g[:, HALF:D_MODEL])


def _combine_into_kernel(prev_ref, *refs):
    del prev_ref
    _combine_kernel(*refs)


def _combine(y, route, x1, mod, g_post_ffn, tm, t_all, t_first, b_first=0, n_batch=None, out_prev=None):
    B, L, _ = x1.shape
    n_batch = B if n_batch is None else n_batch
    assert t_all % tm == 0 and t_first % tm == 0
    row = lambda b, s: (b + b_first, s, 0)
    y_specs = [pl.BlockSpec((tm, HALF), lambda b, s, k=k: ((k * t_all + t_first) // tm + b * (L // tm) + s, 0))
               for k in range(TOP_K)]
    in_specs = y_specs + [pl.BlockSpec((1, tm, LANES), row),
                          pl.BlockSpec((1, tm, D_MODEL), row),
                          pl.BlockSpec((1, N_MOD, D_MODEL), lambda b, s: (b + b_first, 0, 0)),
                          pl.BlockSpec((1, D_MODEL), lambda b, s: (0, 0))]
    args = (y, y, y, y, route, x1, mod, g_post_ffn.reshape(1, D_MODEL))
    body, aliases = _combine_kernel, {}
    if out_prev is not None:
        body, aliases = _combine_into_kernel, {0: 0}
        in_specs = [pl.BlockSpec(memory_space=pl.ANY)] + in_specs
        args = (out_prev,) + args
    return pl.pallas_call(
        body,
        grid=(n_batch, L // tm),
        in_specs=in_specs,
        out_specs=pl.BlockSpec((1, tm, D_MODEL), row),
        out_shape=jax.ShapeDtypeStruct((B, L, D_MODEL), F32),
        input_output_aliases=aliases,
        compiler_params=pltpu.CompilerParams(dimension_semantics=("arbitrary", "arbitrary"),
                                             vmem_limit_bytes=VMEM_LIMIT),
        name="combine",
    )(*args)


SC_GATHER_BYTES = 128 * 1024


def _scatter_quantum(width):
    rows = SC_GATHER_BYTES // (width * 4)
    sc = plsc.get_sparse_core_info()
    return rows, sc.num_cores * sc.num_subcores * rows * 2


def _scatter_rows(table, row_first, dest, n_slots):
    n, width = dest.shape[0], table.shape[1]
    rows, quantum = _scatter_quantum(width)
    sc = plsc.get_sparse_core_info()
    workers = sc.num_cores * sc.num_subcores
    n_pad = -(-n // quantum) * quantum
    extra = n_pad - n
    assert row_first % SUBLANES == 0 and n % rows == 0 and n >= rows and row_first + n <= table.shape[0]
    last_window = row_first + n - rows
    spare = n_slots + jnp.arange(extra * TOP_K, dtype=I32).reshape(extra, TOP_K)
    per_w = n_pad // workers
    wins = per_w // rows
    wins_tile = -(-wins // SUBLANES) * SUBLANES
    idx = jnp.concatenate([dest, spare], axis=0).T.reshape(TOP_K, workers, wins, rows)
    idx = jnp.pad(idx, ((0, 0), (0, 0), (0, wins_tile - wins), (0, 0))).reshape(TOP_K, workers * wins_tile, rows)
    mesh = plsc.VectorSubcoreMesh(core_axis_name="c", subcore_axis_name="s")
    buf = pltpu.VMEM((rows, width), table.dtype)

    @functools.partial(
        pl.kernel, mesh=mesh,
        out_type=jax.ShapeDtypeStruct((n_slots + extra * TOP_K, width), table.dtype),
        scratch_types=[pltpu.VMEM((TOP_K, wins_tile, rows), I32), buf, buf,
                       pltpu.SemaphoreType.DMA, pltpu.SemaphoreType.DMA],
        name="scatter_rows",
    )
    def scatter_kernel(table_hbm, idx_hbm, out_hbm, idx_v, buf0, buf1, sem0, sem1):
        w = lax.axis_index("s") * sc.num_cores + lax.axis_index("c")
        first = pl.multiple_of(w * wins_tile, SUBLANES)
        for k in range(TOP_K):
            pltpu.sync_copy(idx_hbm.at[k, pl.ds(first, wins_tile)], idx_v.at[k])
        base = row_first + w * per_w

        def window(i, buf, sem):
            start = pl.multiple_of(jnp.minimum(base + i * rows, last_window), SUBLANES)
            pltpu.sync_copy(table_hbm.at[pl.ds(start, rows)], buf)
            copies = [pltpu.async_copy(buf, out_hbm.at[idx_v.at[k, i]], sem) for k in range(TOP_K)]
            return copies

        @pl.loop(0, wins, step=2)
        def _(i):
            c0 = window(i, buf0, sem0)
            c1 = window(i + 1, buf1, sem1)
            for c in c0 + c1:
                c.wait()

    return scatter_kernel(table, idx)


def _gather_rows(table, idx):
    n = idx.shape[0]
    width = table.shape[1]
    rows = SC_GATHER_BYTES // (width * 4)
    sc = plsc.get_sparse_core_info()
    workers = sc.num_cores * sc.num_subcores
    quantum = workers * rows * 2
    n_pad = -(-n // quantum) * quantum
    filler = jnp.arange(n_pad - n, dtype=I32) % table.shape[0]
    per_w = n_pad // workers
    wins = per_w // rows
    wins_tile = -(-wins // SUBLANES) * SUBLANES
    idx = jnp.concatenate([idx, filler]).reshape(workers, wins, rows)
    idx = jnp.pad(idx, ((0, 0), (0, wins_tile - wins), (0, 0))).reshape(workers * wins_tile, rows)
    mesh = plsc.VectorSubcoreMesh(core_axis_name="c", subcore_axis_name="s")
    buf = pltpu.VMEM((rows, width), table.dtype)

    @functools.partial(
        pl.kernel, mesh=mesh,
        out_type=jax.ShapeDtypeStruct((n_pad, width), table.dtype),
        scratch_types=[pltpu.VMEM((wins_tile, rows), I32), buf, buf,
                       pltpu.SemaphoreType.DMA, pltpu.SemaphoreType.DMA],
        name="gather_rows",
    )
    def gather_kernel(table_hbm, idx_hbm, out_hbm, idx_v, buf0, buf1, sem0, sem1):
        w = lax.axis_index("s") * sc.num_cores + lax.axis_index("c")
        pltpu.sync_copy(idx_hbm.at[pl.ds(pl.multiple_of(w * wins_tile, SUBLANES), wins_tile)], idx_v)
        base = pl.multiple_of(w * per_w, rows)

        @pl.loop(0, wins, step=2)
        def _(i):
            g0 = pltpu.async_copy(table_hbm.at[idx_v.at[i]], buf0, sem0)
            g1 = pltpu.async_copy(table_hbm.at[idx_v.at[i + 1]], buf1, sem1)
            g0.wait()
            pltpu.sync_copy(buf0, out_hbm.at[pl.ds(base + i * rows, rows)])
            g1.wait()
            pltpu.sync_copy(buf1, out_hbm.at[pl.ds(base + (i + 1) * rows, rows)])

    return gather_kernel(table, idx)


def _routing_tables(route, counts, counts_before, blk):
    T = route.shape[0]
    TK = T * TOP_K
    idx = route[:, 0:TOP_K].astype(I32)
    rank = route[:, 2 * TOP_K:3 * TOP_K].astype(I32) - counts_before[idx]
    padded = (counts + blk - 1) // blk * blk
    pad_end = jnp.cumsum(padded)
    pad_start = pad_end - padded
    dest = pad_start[idx] + rank
    n_blocks = -(-TK // blk) + N_EXPERTS
    blk_start = jnp.arange(n_blocks, dtype=I32) * blk
    block_e = jnp.sum((blk_start[:, None] >= pad_end[None, :]).astype(I32), axis=1)
    block_e = jnp.minimum(block_e, N_EXPERTS - 1)
    last = (pad_start + counts)[block_e]
    n_valid = jnp.clip(last - blk_start, 0, blk)
    return block_e, n_valid, dest


def _mix(x, mod, buf, past_k, past_v, p, lam_init, counts_in, tm, tq, pool, pool_rows, pool_first):
    B, L, _ = x.shape
    q, kb, vb, kf, vf, conv, state = _premix(x, mod, p["g_pre_mix"], p["w_in"], buf, p["w_dw"], p["b_dw"],
                                             p["gmat"], p["g_cn"], p["b_cn"], tm)
    if past_k is None:
        attn = _attention(q, kb, vb, p["lam_rows"], p["g_subln"], lam_init, tq, tq, 0, True)
    else:
        P = past_k.shape[1]
        keys = jnp.concatenate([past_k.reshape(B, P, QK_W).astype(BF16), kb], axis=1)
        vals = jnp.concatenate([past_v.reshape(B, P, ATTN_W).astype(BF16), vb], axis=1)
        attn = _attention(q, keys, vals, p["lam_rows"], p["g_subln"], lam_init, L, P + L, P, False)
    x1, h2, route, cnt = _postmix(attn, conv, x, mod, p["w_out"], p["g_post_mix"], p["g_pre_ffn"],
                                  p["w_router"], p["b_router"], counts_in, tm, pool, pool_rows, pool_first)
    return x1, h2, route, cnt, kf, vf, state[:, CONV_HALO - (CONV_W - 1):, :]


def _moe_rows(pool, row_first, route_groups, counts, counts_before, p, blk):
    route = jnp.concatenate([r.reshape(-1, LANES) for r in route_groups], axis=0)
    T = route.shape[0]
    block_e, n_valid, dest = _routing_tables(route, counts, counts_before, blk)
    x_pad = _scatter_rows(pool, row_first, dest, block_e.shape[0] * blk)
    y_pad = _experts(x_pad, block_e, n_valid, p["w_gu"], p["b_gu"], p["w_down"], p["b_down"], blk)
    return _gather_rows(y_pad, dest.T.reshape(T * TOP_K))


def _prepare_params(l, w_ada, b_ada, g_pre_mix, g_post_mix, w_in, lambda_q1, lambda_k1, lambda_q2, lambda_k2,
                    g_subln, w_dw, b_dw, g_cnorm, b_cnorm, w_out, g_pre_ffn, g_post_ffn,
                    w_router, b_router, w_gu, b_gu, w_down, b_down):
    lam_rows = jnp.zeros((8, LANES), F32)
    for r, vec in enumerate((lambda_q1[l], lambda_k1[l], lambda_q2[l], lambda_k2[l])):
        lam_rows = lam_rows.at[r, :HEAD_DIM].set(vec)
    ch = jnp.arange(CONV_CH, dtype=I32) // GROUP_CH
    gmat = (ch[:, None] == ch[None, :]).astype(BF16)
    w_dw_pad = jnp.zeros((CONV_HALO, CONV_CH), F32).at[:CONV_W].set(w_dw[l])
    w_router_pad = jnp.zeros((D_MODEL, LANES), F32).at[:, :N_EXPERTS].set(w_router[l])
    b_router_pad = jnp.full((1, LANES), NEG, F32).at[0, :N_EXPERTS].set(b_router[l])
    return dict(w_ada=w_ada[l], b_ada=b_ada[l], g_pre_mix=g_pre_mix[l], g_post_mix=g_post_mix[l],
                w_in=w_in[l].astype(BF16), lam_rows=lam_rows, g_subln=g_subln[l], w_dw=w_dw_pad, b_dw=b_dw[l],
                gmat=gmat, g_cn=g_cnorm[l], b_cn=b_cnorm[l], w_out=w_out[l].astype(BF16),
                g_pre_ffn=g_pre_ffn[l], g_post_ffn=g_post_ffn[l], w_router=w_router_pad, b_router=b_router_pad,
                w_gu=w_gu[l], b_gu=b_gu[l], w_down=w_down[l], b_down=b_down[l])


def kernel(x_prompt, x_sample, cache_k, cache_v, state_conv, c_prompt, c_sample, w_ada, b_ada, g_pre_mix, g_post_mix, w_in, lambda_q1, lambda_k1, lambda_q2, lambda_k2, g_subln, w_dw, b_dw, g_cnorm, b_cnorm, w_out, g_pre_ffn, g_post_ffn, w_router, b_router, w_gu, b_gu, w_down, b_down):
    depth = w_ada.shape[0]
    Bp, Lp, _ = x_prompt.shape
    Bs, Ls, _ = x_sample.shape
    yp, ys = x_prompt, x_sample
    outs = [[] for _ in range(6)]
    for l in range(depth):
        p = _prepare_params(l, w_ada, b_ada, g_pre_mix, g_post_mix, w_in, lambda_q1, lambda_k1, lambda_q2,
                            lambda_k2, g_subln, w_dw, b_dw, g_cnorm, b_cnorm, w_out, g_pre_ffn, g_post_ffn,
                            w_router, b_router, w_gu, b_gu, w_down, b_down)
        lam_init = 0.8 - 0.6 * math.exp(-0.3 * l)
        mod = _ada(jnp.concatenate([c_prompt, c_sample], axis=0), p["w_ada"], p["b_ada"])
        mod = mod.reshape(Bp + Bs, N_MOD, D_MODEL)
        buf_p = jnp.zeros((Bp, CONV_HALO, CONV_CH), F32)
        buf_s = jnp.pad(state_conv[l], ((0, 0), (CONV_HALO - (CONV_W - 1), 0), (0, 0)))
        tm_p = min(Lp, 512)
        assert Bp % 2 == 0
        hb = Bp // 2
        t_a, t_b = hb * Lp, hb * Lp + Bs * Ls
        pool_rows = Bp * Lp + Bs * Ls
        zero_counts = jnp.zeros((SUBLANES, LANES), F32)
        x1p, pool, rtp, cnt_p, kp, vp, cp = _mix(yp, mod[:Bp], buf_p, None, None, p, lam_init, zero_counts,
                                                 tm_p, tm_p, None, pool_rows, 0)
        x1s, pool, rts, cnt_s, ks, vs, cs = _mix(ys, mod[Bp:], buf_s, cache_k[l], cache_v[l], p, lam_init,
                                                 cnt_p[Bp - 1], Ls, Ls, pool, pool_rows, Bp * Lp)
        n_a = cnt_p[hb - 1, 0, :N_EXPERTS].astype(I32)
        n_b = cnt_s[Bs - 1, 0, :N_EXPERTS].astype(I32) - n_a
        y_a = _moe_rows(pool, 0, (rtp[:hb],), n_a, jnp.zeros_like(n_a), p, EXPERT_ROWS)
        y_b = _moe_rows(pool, t_a, (rtp[hb:], rts), n_b, n_a, p, EXPERT_ROWS)
        tm_c = min(tm_p, 256)
        yp = _combine(y_a, rtp, x1p, mod[:Bp], p["g_post_ffn"], tm_c, t_a, 0, 0, hb)
        yp = _combine(y_b, rtp, x1p, mod[:Bp], p["g_post_ffn"], tm_c, t_b, 0, hb, hb, out_prev=yp)
        ys = _combine(y_b, rts, x1s, mod[Bp:], p["g_post_ffn"], Ls, t_b, hb * Lp)
        for lst, val in zip(outs, (kp, vp, cp, ks, vs, cs)):
            lst.append(val)
    return (yp, ys) + tuple(jnp.stack(o) for o in outs)
```

```python
import functools
import math

import jax
import jax.numpy as jnp
from jax import lax
from jax.experimental import pallas as pl
from jax.experimental.pallas import tpu as pltpu
from jax.experimental.pallas import tpu_sc as plsc

F32 = jnp.float32
BF16 = jnp.bfloat16
I32 = jnp.int32

D_MODEL = 1024
CHUNK = 64
CHUNK_SHIFT = 6
N_HEADS = 4
V_DIM = 128
HEAD_DIM = 64
QK_W = 512
ATTN_W = 512
CONV_CH = 512
CONV_W = 31
CONV_GROUPS = 8
GROUP_CH = CONV_CH // CONV_GROUPS
IN_W = 2 * QK_W + ATTN_W + 2 * CONV_CH
N_MOD = 6
N_EXPERTS = 32
TOP_K = 4
D_FF = 1024
SWIGLU_ALPHA = 1.702
SWIGLU_LIMIT = 7.0
EPS = 1e-6

LANES = 128
SUBLANES = 8
CONV_HALO = 32
CONV_ROWS = 32
NEG = -1e30
LOG2E = math.log2(math.e)
ATTN_ROWS = 128
EXPERT_ROWS = 512
VMEM_LIMIT = 48 * 1024 * 1024
EXPERT_VMEM_LIMIT = 56 * 1024 * 1024


def _sigmoid(x):
    return 1.0 / (1.0 + jnp.exp(-x))


def _split_bf16(x):
    hi = x.astype(BF16)
    lo = (x - hi.astype(F32)).astype(BF16)
    return hi, lo


def _dot(a, b):
    return jnp.dot(a, b, preferred_element_type=F32)


def _dot3(a, b):
    ah, al = _split_bf16(a)
    bh, bl = _split_bf16(b)
    return _dot(ah, bh) + _dot(ah, bl) + _dot(al, bh)


def _rms(x):
    return x * lax.rsqrt(jnp.mean(x * x, axis=-1, keepdims=True) + EPS)


HALF = D_MODEL // 2
HIGH16 = -65536


def _pack_halves(x):
    lo = lax.bitcast_convert_type(x[:, :HALF].astype(BF16).astype(F32), I32)
    hi = lax.bitcast_convert_type(x[:, HALF:].astype(BF16).astype(F32), I32)
    return jnp.bitwise_or(jnp.bitwise_and(jnp.right_shift(lo, 16), 0xFFFF), jnp.bitwise_and(hi, HIGH16))


def _unpack_halves(w):
    lo = lax.bitcast_convert_type(jnp.left_shift(w, 16), F32)
    hi = lax.bitcast_convert_type(jnp.bitwise_and(w, HIGH16), F32)
    return lo, hi


def _ada_kernel(c_ref, w_ref, b_ref, o_ref):
    c = c_ref[...]
    o_ref[...] = _dot3(c * _sigmoid(c), w_ref[...]) + b_ref[...]


def _ada(c, w_ada, b_ada):
    n = c.shape[0]
    return pl.pallas_call(
        _ada_kernel,
        grid=(N_MOD,),
        in_specs=[pl.BlockSpec((n, D_MODEL), lambda j: (0, 0)),
                  pl.BlockSpec((D_MODEL, D_MODEL), lambda j: (0, j)),
                  pl.BlockSpec((1, D_MODEL), lambda j: (0, j))],
        out_specs=pl.BlockSpec((n, D_MODEL), lambda j: (0, j)),
        out_shape=jax.ShapeDtypeStruct((n, N_MOD * D_MODEL), F32),
        compiler_params=pltpu.CompilerParams(dimension_semantics=("arbitrary",), vmem_limit_bytes=VMEM_LIMIT),
        name="ada",
    )(c, w_ada, b_ada.reshape(1, N_MOD * D_MODEL))


def _premix_kernel(x_ref, mod_ref, g_ref, w_ref, buf_ref, wdw_ref, bdw_ref, gmat_ref, gcn_ref, bcn_ref,
                   q_ref, kb_ref, vb_ref, kf_hbm, vf_hbm, co_ref, st_ref, ext_ref, sh_ref, y_ref, kv_buf, kv_sem,
                   *, tm):
    batch = pl.program_id(0)
    s = pl.program_id(1)
    step = batch * pl.num_programs(1) + s
    last = pl.num_programs(0) * pl.num_programs(1) - 1
    slot = step % 2

    def kv_copies(slot):
        return [pltpu.make_async_copy(kv_buf.at[slot, i, :, pl.ds(h * V_DIM, V_DIM)],
                                      out.at[batch, pl.ds(s * tm, tm), h, :], kv_sem.at[slot, i, h])
                for i, out in enumerate((kf_hbm, vf_hbm)) for h in range(N_HEADS)]

    @pl.when(step >= 2)
    def _():
        for c in kv_copies(slot):
            c.wait()

    @pl.when(s == 0)
    def _():
        ext_ref[0:CONV_HALO, :] = buf_ref[0]

    x = x_ref[0]
    shift = mod_ref[0, 0:1, :]
    scale = mod_ref[0, 1:2, :]
    h = _rms(x) * g_ref[...] * (1.0 + scale) + shift
    hb = h.astype(BF16)

    u0 = 2 * QK_W + ATTN_W
    val = _dot(hb, w_ref[:, u0:u0 + CONV_CH])
    gate = _dot(hb, w_ref[:, u0 + CONV_CH:u0 + 2 * CONV_CH])
    ext_ref[CONV_HALO:CONV_HALO + tm, :] = val * _sigmoid(gate)

    off = CONV_HALO - (CONV_W - 1)
    span = tm + CONV_HALO - SUBLANES
    for b in range(1, SUBLANES):
        sh_ref[b - 1, 0:span, :] = ext_ref[b:b + span, :]
    for c in range(tm // CONV_ROWS):
        r0 = c * CONV_ROWS
        acc = jnp.zeros((CONV_ROWS, CONV_CH), F32)
        for j in range(CONV_W):
            b = (j + off) % SUBLANES
            a = r0 + j + off - b
            rows_j = ext_ref[a:a + CONV_ROWS, :] if b == 0 else sh_ref[b - 1, a:a + CONV_ROWS, :]
            acc = acc + wdw_ref[j:j + 1, :] * rows_j
        y_ref[r0:r0 + CONV_ROWS, :] = acc + bdw_ref[...]

    zq = _dot(hb, w_ref[:, 0:QK_W])
    q_ref[0] = (zq * (HEAD_DIM ** -0.5 * LOG2E)).astype(BF16)
    zk = _dot(hb, w_ref[:, QK_W:2 * QK_W])
    kv_buf[slot, 0] = zk
    kb_ref[0] = zk.astype(BF16)
    zv = _dot(hb, w_ref[:, 2 * QK_W:2 * QK_W + ATTN_W])
    kv_buf[slot, 1] = zv
    vb_ref[0] = zv.astype(BF16)

    y = y_ref[...]
    gm = gmat_ref[...]
    yh, yl = _split_bf16(y)
    mu = (_dot(yh, gm) + _dot(yl, gm)) * (1.0 / GROUP_CH)
    d = y - mu
    dh, dl = _split_bf16(d * d)
    var = (_dot(dh, gm) + _dot(dl, gm)) * (1.0 / GROUP_CH)
    yn = d * lax.rsqrt(var + EPS) * gcn_ref[...] + bcn_ref[...]
    co_ref[0] = (yn * _sigmoid(yn)).astype(BF16)

    tail = ext_ref[tm:tm + CONV_HALO, :]
    st_ref[0] = tail
    ext_ref[0:CONV_HALO, :] = tail

    for c in kv_copies(slot):
        c.start()

    @pl.when(step == last)
    def _():
        for c in kv_copies(slot):
            c.wait()

    @pl.when(jnp.logical_and(step == last, step >= 1))
    def _():
        for c in kv_copies(1 - slot):
            c.wait()


def _premix(x, mod, g_pre_mix, w_in_bf, buf, w_dw, b_dw, gmat, g_cn, b_cn, tm):
    B, L, _ = x.shape
    assert L % tm == 0 and tm % CONV_ROWS == 0
    row = lambda b, s: (b, s, 0)
    const2 = lambda b, s: (0, 0)
    bf_tile = jax.ShapeDtypeStruct((B, L, QK_W), BF16)
    f_heads = jax.ShapeDtypeStruct((B, L, N_HEADS, V_DIM), F32)
    return pl.pallas_call(
        functools.partial(_premix_kernel, tm=tm),
        grid=(B, L // tm),
        in_specs=[pl.BlockSpec((1, tm, D_MODEL), row),
                  pl.BlockSpec((1, N_MOD, D_MODEL), lambda b, s: (b, 0, 0)),
                  pl.BlockSpec((1, D_MODEL), const2),
                  pl.BlockSpec((D_MODEL, IN_W), const2),
                  pl.BlockSpec((1, CONV_HALO, CONV_CH), lambda b, s: (b, 0, 0)),
                  pl.BlockSpec((CONV_HALO, CONV_CH), const2),
                  pl.BlockSpec((1, CONV_CH), const2),
                  pl.BlockSpec((CONV_CH, CONV_CH), const2),
                  pl.BlockSpec((1, CONV_CH), const2),
                  pl.BlockSpec((1, CONV_CH), const2)],
        out_specs=[pl.BlockSpec((1, tm, QK_W), row)] * 3
                  + [pl.BlockSpec(memory_space=pl.ANY)] * 2
                  + [pl.BlockSpec((1, tm, QK_W), row),
                     pl.BlockSpec((1, CONV_HALO, CONV_CH), lambda b, s: (b, 0, 0))],
        out_shape=[bf_tile, bf_tile, bf_tile, f_heads, f_heads, bf_tile,
                   jax.ShapeDtypeStruct((B, CONV_HALO, CONV_CH), F32)],
        scratch_shapes=[pltpu.VMEM((CONV_HALO + tm, CONV_CH), F32),
                        pltpu.VMEM((SUBLANES - 1, CONV_HALO + tm, CONV_CH), F32),
                        pltpu.VMEM((tm, CONV_CH), F32),
                        pltpu.VMEM((2, 2, tm, QK_W), F32),
                        pltpu.SemaphoreType.DMA((2, 2, N_HEADS))],
        compiler_params=pltpu.CompilerParams(dimension_semantics=("arbitrary", "arbitrary"),
                                             vmem_limit_bytes=VMEM_LIMIT),
        name="premix",
    )(x, mod, g_pre_mix.reshape(1, D_MODEL), w_in_bf, buf, w_dw, b_dw.reshape(1, CONV_CH), gmat,
      g_cn.reshape(1, CONV_CH), b_cn.reshape(1, CONV_CH))


def _attn_kernel(slopes_ref, q_ref, k_ref, v_ref, kb_ref, mt_ref, lam_ref, gs_ref, o_ref,
                 ke1_ref, ke2_ref, ve_ref, m1_ref, a1_ref, m2_ref, a2_ref,
                 *, tq, tk, rows, q_offset, causal, lam_init):
    slope = slopes_ref[pl.program_id(1)] * LOG2E
    lq = q_ref.shape[1]
    lk = k_ref.shape[1]
    nq = lq // tq
    lane = lax.broadcasted_iota(I32, (1, V_DIM), 1)

    low = jnp.broadcast_to(lane < HEAD_DIM, (lk, V_DIM))
    k = k_ref[0]
    zero = jnp.zeros((lk, V_DIM), BF16)
    ke1_ref[...] = jnp.where(low, k, zero)
    ke2_ref[...] = jnp.where(low, zero, k)
    ve_ref[:, 0:V_DIM] = v_ref[0]
    ve_ref[:, V_DIM:2 * V_DIM] = jnp.broadcast_to(jnp.where(lane == 0, 1.0, 0.0), (lk, V_DIM)).astype(BF16)

    nt = (((1,), (1,)), ((), ()))
    maps = ((ke1_ref, m1_ref, a1_ref), (ke2_ref, m2_ref, a2_ref))

    def lanes(x, n):
        if n % LANES == 0:
            return jnp.concatenate([x] * (n // LANES), axis=1)
        return jnp.broadcast_to(x[:, 0:1], (rows, n))

    def tile(q0, k0, masked, first):
        ve = ve_ref[pl.ds(k0, tk), :]
        q_first = (lax.broadcasted_iota(I32, (1, 1), 0) + (q0 + q_offset)).astype(F32)
        bias = kb_ref[0, :, pl.ds(k0, tk)] - slope * q_first
        for ke_ref, m_ref, a_ref in maps:
            ke = ke_ref[pl.ds(k0, tk), :]
            for r0 in range(0, tq, rows):
                s = lax.dot_general(q_ref[0, pl.ds(q0 + r0, rows), :], ke, nt, preferred_element_type=F32)
                s = s + bias
                if masked:
                    s = s + mt_ref[0, r0:r0 + rows, :]
                m_tile = jnp.max(s, axis=-1, keepdims=True)
                if first:
                    m_new = jnp.broadcast_to(m_tile, (rows, LANES))
                    a_ref[r0:r0 + rows, :] = _dot(jnp.exp2(s - lanes(m_new, tk)).astype(BF16), ve)
                else:
                    m_old = m_ref[r0:r0 + rows, :]
                    m_new = jnp.maximum(m_old, m_tile)
                    p = jnp.exp2(s - lanes(m_new, tk))
                    alpha = jnp.exp2(m_old - m_new)
                    a_ref[r0:r0 + rows, :] = (lanes(alpha, 2 * V_DIM) * a_ref[r0:r0 + rows, :]
                                              + _dot(p.astype(BF16), ve))
                m_ref[r0:r0 + rows, :] = m_new

    lv = lam_ref[...]
    lam = (jnp.exp(jnp.sum(lv[0:1, :] * lv[1:2, :], axis=-1, keepdims=True))
           - jnp.exp(jnp.sum(lv[2:3, :] * lv[3:4, :], axis=-1, keepdims=True)) + lam_init)

    def finish(q0):
        a1 = a1_ref[...]
        a2 = a2_ref[...]
        o = a1[:, 0:V_DIM] / a1[:, V_DIM:V_DIM + 1] - lam * (a2[:, 0:V_DIM] / a2[:, V_DIM:V_DIM + 1])
        o_ref[0, pl.ds(q0, tq), :] = (_rms(o) * gs_ref[...] * (1.0 - lam_init)).astype(BF16)

    if not causal:
        tile(0, 0, True, True)
        finish(0)
        return

    for qi in range(nq):
        q0 = qi * tq
        for ki in range(qi):
            tile(q0, ki * tk, False, ki == 0)
        tile(q0, qi * tk, True, qi == 0)
        finish(q0)


def _mask_table(slopes, tq, tk, q_offset):
    qpos = q_offset + jnp.arange(tq, dtype=I32)[:, None]
    kpos = jnp.arange(tk, dtype=I32)[None, :]
    visible = jnp.right_shift(kpos, CHUNK_SHIFT) <= jnp.right_shift(qpos, CHUNK_SHIFT)
    fix = jnp.where(kpos > qpos, 2 * (qpos - kpos), 0).astype(F32)
    return jnp.where(visible[None], (slopes * LOG2E)[:, None, None] * fix[None], NEG)


def _attention(q, k, v, lam_rows, g_subln, lam_init, tq, tk, q_offset, causal):
    B, Lq, _ = q.shape
    Lk = k.shape[1]
    assert Lq % tq == 0 and Lk % tk == 0 and (not causal or (tq == tk and tq % CHUNK == 0 and q_offset == 0))
    assert causal or Lk == tk
    rows = min(tq, ATTN_ROWS)
    slopes = jnp.asarray([2.0 ** (-8.0 * (h + 1) / N_HEADS) for h in range(N_HEADS)], F32)
    table = _mask_table(slopes, tq, tk, q_offset)
    key_bias = ((slopes * LOG2E)[:, None] * jnp.arange(Lk, dtype=F32)[None, :]).reshape(N_HEADS, 1, Lk)
    key_ext = pltpu.VMEM((Lk, V_DIM), BF16)
    stat = pltpu.VMEM((tq, LANES), F32)
    acc = pltpu.VMEM((tq, 2 * V_DIM), F32)
    head = lambda b, h, sl: (b, 0, h)
    return pl.pallas_call(
        functools.partial(_attn_kernel, tq=tq, tk=tk, rows=rows, q_offset=q_offset, causal=causal,
                          lam_init=lam_init),
        grid_spec=pltpu.PrefetchScalarGridSpec(
            num_scalar_prefetch=1,
            grid=(B, N_HEADS),
            in_specs=[pl.BlockSpec((1, Lq, V_DIM), head),
                      pl.BlockSpec((1, Lk, V_DIM), head),
                      pl.BlockSpec((1, Lk, V_DIM), head),
                      pl.BlockSpec((1, 1, Lk), lambda b, h, sl: (h, 0, 0)),
                      pl.BlockSpec((1, tq, tk), lambda b, h, sl: (h, 0, 0)),
                      pl.BlockSpec((8, LANES), lambda b, h, sl: (0, 0)),
                      pl.BlockSpec((1, V_DIM), lambda b, h, sl: (0, 0))],
            out_specs=pl.BlockSpec((1, Lq, V_DIM), head),
            scratch_shapes=[key_ext, key_ext, pltpu.VMEM((Lk, 2 * V_DIM), BF16), stat, acc, stat, acc]),
        out_shape=jax.ShapeDtypeStruct((B, Lq, ATTN_W), BF16),
        compiler_params=pltpu.CompilerParams(dimension_semantics=("arbitrary",) * 2,
                                             vmem_limit_bytes=VMEM_LIMIT),
        name="attn",
    )(slopes, q, k, v, key_bias, table, lam_rows, g_subln.reshape(1, V_DIM))


def _postmix_kernel(a_ref, c_ref, x_ref, mod_ref, wo_ref, gpm_ref, gpf_ref, wr_ref, br_ref, cin_ref,
                    x1_ref, h2_ref, rt_ref, cnt_ref, carry_ref, *, tm):
    first = jnp.logical_and(pl.program_id(0) == 0, pl.program_id(1) == 0)

    @pl.when(first)
    def _():
        carry_ref[...] = cin_ref[...]

    mix = _dot(a_ref[0], wo_ref[0:ATTN_W, :]) + _dot(c_ref[0], wo_ref[ATTN_W:ATTN_W + CONV_CH, :])
    gate_a = mod_ref[0, 2:3, :]
    shift_f = mod_ref[0, 3:4, :]
    scale_f = mod_ref[0, 4:5, :]
    x1 = x_ref[0] + gate_a * (_rms(mix) * gpm_ref[...])
    x1_ref[0] = x1
    h2 = _rms(x1) * gpf_ref[...] * (1.0 + scale_f) + shift_f
    h2_ref[...] = _pack_halves(h2)

    logits = _dot3(h2, wr_ref[...]) + br_ref[...]
    lane = lax.broadcasted_iota(I32, (tm, LANES), 1).astype(F32)
    vals, idxs = [], []
    for _ in range(TOP_K):
        m = jnp.max(logits, axis=-1, keepdims=True)
        idx = jnp.min(jnp.where(logits == m, lane, float(LANES)), axis=-1, keepdims=True)
        vals.append(m)
        idxs.append(idx)
        logits = jnp.where(lane == idx, 2.0 * NEG, logits)
    es = [jnp.exp(v - vals[0]) for v in vals]
    denom = es[0] + es[1] + es[2] + es[3]

    onehot = jnp.zeros((tm, LANES), F32)
    for idx in idxs:
        onehot = jnp.where(lane == idx, 1.0, onehot)
    r_i = lax.broadcasted_iota(I32, (tm, tm), 0)
    c_i = lax.broadcasted_iota(I32, (tm, tm), 1)
    tri = jnp.where(c_i < r_i, 1.0, 0.0).astype(BF16)
    before = _dot(tri, onehot.astype(BF16)) + carry_ref[0:1, :]

    rt = jnp.zeros((tm, LANES), F32)
    for k in range(TOP_K):
        rank = jnp.sum(jnp.where(lane == idxs[k], before, 0.0), axis=-1, keepdims=True)
        rt = jnp.where(lane == k, idxs[k], rt)
        rt = jnp.where(lane == TOP_K + k, es[k] / denom, rt)
        rt = jnp.where(lane == 2 * TOP_K + k, rank, rt)
    rt_ref[0] = rt

    carry_ref[...] = carry_ref[...] + jnp.sum(onehot, axis=0, keepdims=True)
    cnt_ref[0] = carry_ref[...]


def _postmix_into_kernel(pool_ref, *refs, tm):
    del pool_ref
    _postmix_kernel(*refs, tm=tm)


def _postmix(attn, conv, x, mod, w_out_bf, g_post_mix, g_pre_ffn, w_router_pad, b_router_pad, counts_in, tm,
             pool, pool_rows, pool_first):
    B, L, _ = x.shape
    assert pool_first % tm == 0
    row = lambda b, s: (b, s, 0)
    const2 = lambda b, s: (0, 0)
    body, aliases, lead_specs, lead_args = _postmix_kernel, {}, [], ()
    if pool is not None:
        body, aliases = _postmix_into_kernel, {0: 1}
        lead_specs, lead_args = [pl.BlockSpec(memory_space=pl.ANY)], (pool,)
    return pl.pallas_call(
        functools.partial(body, tm=tm),
        grid=(B, L // tm),
        input_output_aliases=aliases,
        in_specs=lead_specs + [
                  pl.BlockSpec((1, tm, ATTN_W), row),
                  pl.BlockSpec((1, tm, CONV_CH), row),
                  pl.BlockSpec((1, tm, D_MODEL), row),
                  pl.BlockSpec((1, N_MOD, D_MODEL), lambda b, s: (b, 0, 0)),
                  pl.BlockSpec((D_MODEL, D_MODEL), const2),
                  pl.BlockSpec((1, D_MODEL), const2),
                  pl.BlockSpec((1, D_MODEL), const2),
                  pl.BlockSpec((D_MODEL, LANES), const2),
                  pl.BlockSpec((1, LANES), const2),
                  pl.BlockSpec((8, LANES), const2)],
        out_specs=[pl.BlockSpec((1, tm, D_MODEL), row),
                   pl.BlockSpec((tm, HALF), lambda b, s: (pool_first // tm + b * (L // tm) + s, 0)),
                   pl.BlockSpec((1, tm, LANES), row),
                   pl.BlockSpec((1, SUBLANES, LANES), lambda b, s: (b, 0, 0))],
        out_shape=[jax.ShapeDtypeStruct((B, L, D_MODEL), F32),
                   jax.ShapeDtypeStruct((pool_rows, HALF), I32),
                   jax.ShapeDtypeStruct((B, L, LANES), F32),
                   jax.ShapeDtypeStruct((B, SUBLANES, LANES), F32)],
        scratch_shapes=[pltpu.VMEM((8, LANES), F32)],
        compiler_params=pltpu.CompilerParams(dimension_semantics=("arbitrary", "arbitrary"),
                                             vmem_limit_bytes=VMEM_LIMIT),
        name="postmix",
    )(*lead_args, attn, conv, x, mod, w_out_bf, g_post_mix.reshape(1, D_MODEL), g_pre_ffn.reshape(1, D_MODEL),
      w_router_pad, b_router_pad, counts_in)


def _expert_kernel(be_ref, nv_ref, x_ref, wgu_ref, bgu_ref, wd_ref, bd_ref, y_ref, wgu_bf, wd_bf):
    i = pl.program_id(0)
    ff_half = D_FF // 2

    @pl.when(jnp.logical_or(i == 0, be_ref[i] != be_ref[jnp.maximum(i - 1, 0)]))
    def _():
        wgu_bf[...] = wgu_ref[0].astype(BF16)
        wd_bf[...] = wd_ref[0].astype(BF16)

    @pl.when(nv_ref[i] > 0)
    def _():
        x_lo, x_hi = _unpack_halves(x_ref[...])
        x_lo = x_lo.astype(BF16)
        x_hi = x_hi.astype(BF16)
        acc = None
        for c in range(2):
            lo, hi = c * ff_half, (c + 1) * ff_half
            g = (_dot(x_lo, wgu_bf[0:HALF, lo:hi]) + _dot(x_hi, wgu_bf[HALF:D_MODEL, lo:hi])
                 + bgu_ref[0, :, lo:hi])
            lin = (_dot(x_lo, wgu_bf[0:HALF, D_FF + lo:D_FF + hi]) + _dot(x_hi, wgu_bf[HALF:D_MODEL, D_FF + lo:D_FF + hi])
                   + bgu_ref[0, :, D_FF + lo:D_FF + hi])
            g = jnp.minimum(g, SWIGLU_LIMIT)
            lin = jnp.clip(lin, -SWIGLU_LIMIT, SWIGLU_LIMIT)
            act = g * _sigmoid(SWIGLU_ALPHA * g) * (lin + 1.0)
            part = _dot(act.astype(BF16), wd_bf[lo:hi, :])
            acc = part if acc is None else acc + part
        y_ref[...] = _pack_halves(acc + bd_ref[0])


def _experts(x_pad, block_e, n_valid, w_gu, b_gu, w_down, b_down, blk):
    n_blocks = block_e.shape[0]
    ex = lambda i, be, nv: (be[i], 0, 0)
    rows = lambda i, be, nv: (i, 0)
    return pl.pallas_call(
        _expert_kernel,
        grid_spec=pltpu.PrefetchScalarGridSpec(
            num_scalar_prefetch=2,
            grid=(n_blocks,),
            in_specs=[pl.BlockSpec((blk, HALF), rows),
                      pl.BlockSpec((1, D_MODEL, 2 * D_FF), ex), pl.BlockSpec((1, 1, 2 * D_FF), ex),
                      pl.BlockSpec((1, D_FF, D_MODEL), ex), pl.BlockSpec((1, 1, D_MODEL), ex)],
            out_specs=pl.BlockSpec((blk, HALF), rows),
            scratch_shapes=[pltpu.VMEM((D_MODEL, 2 * D_FF), BF16), pltpu.VMEM((D_FF, D_MODEL), BF16)]),
        out_shape=jax.ShapeDtypeStruct((n_blocks * blk, HALF), I32),
        compiler_params=pltpu.CompilerParams(dimension_semantics=("arbitrary",),
                                             vmem_limit_bytes=EXPERT_VMEM_LIMIT),
        name="experts",
    )(block_e, n_valid, x_pad, w_gu, b_gu.reshape(N_EXPERTS, 1, 2 * D_FF), w_down,
      b_down.reshape(N_EXPERTS, 1, D_MODEL))


def _combine_kernel(y0_ref, y1_ref, y2_ref, y3_ref, rt_ref, x1_ref, mod_ref, g_ref, o_ref):
    rt = rt_ref[0]
    f_lo = f_hi = None
    for k, y_ref in enumerate((y0_ref, y1_ref, y2_ref, y3_ref)):
        lo, hi = _unpack_halves(y_ref[...])
        w = rt[:, TOP_K + k:TOP_K + k + 1]
        f_lo = w * lo if f_lo is None else f_lo + w * lo
        f_hi = w * hi if f_hi is None else f_hi + w * hi
    ms = (jnp.sum(f_lo * f_lo, axis=-1, keepdims=True) + jnp.sum(f_hi * f_hi, axis=-1, keepdims=True)) / D_MODEL
    r = lax.rsqrt(ms + EPS)
    gate_f = mod_ref[0, 5:6, :]
    g = g_ref[...]
    o_ref[0, :, 0:HALF] = x1_ref[0, :, 0:HALF] + gate_f[:, 0:HALF] * (f_lo * r * g[:, 0:HALF])
    o_ref[0, :, HALF:D_MODEL] = x1_ref[0, :, HALF:D_MODEL] + gate_f[:, HALF:D_MODEL] * (f_hi * r * g[:, HALF:D_MODEL])


def _combine_into_kernel(prev_ref, *refs):
    del prev_ref
    _combine_kernel(*refs)


def _combine(y, route, x1, mod, g_post_ffn, tm, t_all, t_first, b_first=0, n_batch=None, out_prev=None):
    B, L, _ = x1.shape
    n_batch = B if n_batch is None else n_batch
    assert t_all % tm == 0 and t_first % tm == 0
    row = lambda b, s: (b + b_first, s, 0)
    y_specs = [pl.BlockSpec((tm, HALF), lambda b, s, k=k: ((k * t_all + t_first) // tm + b * (L // tm) + s, 0))
               for k in range(TOP_K)]
    in_specs = y_specs + [pl.BlockSpec((1, tm, LANES), row),
                          pl.BlockSpec((1, tm, D_MODEL), row),
                          pl.BlockSpec((1, N_MOD, D_MODEL), lambda b, s: (b + b_first, 0, 0)),
                          pl.BlockSpec((1, D_MODEL), lambda b, s: (0, 0))]
    args = (y, y, y, y, route, x1, mod, g_post_ffn.reshape(1, D_MODEL))
    body, aliases = _combine_kernel, {}
    if out_prev is not None:
        body, aliases = _combine_into_kernel, {0: 0}
        in_specs = [pl.BlockSpec(memory_space=pl.ANY)] + in_specs
        args = (out_prev,) + args
    return pl.pallas_call(
        body,
        grid=(n_batch, L // tm),
        in_specs=in_specs,
        out_specs=pl.BlockSpec((1, tm, D_MODEL), row),
        out_shape=jax.ShapeDtypeStruct((B, L, D_MODEL), F32),
        input_output_aliases=aliases,
        compiler_params=pltpu.CompilerParams(dimension_semantics=("arbitrary", "arbitrary"),
                                             vmem_limit_bytes=VMEM_LIMIT),
        name="combine",
    )(*args)


SC_GATHER_BYTES = 128 * 1024


def _scatter_quantum(width):
    rows = SC_GATHER_BYTES // (width * 4)
    sc = plsc.get_sparse_core_info()
    return rows, sc.num_cores * sc.num_subcores * rows * 2


def _scatter_rows(table, row_first, dest, n_slots):
    n, width = dest.shape[0], table.shape[1]
    rows, quantum = _scatter_quantum(width)
    sc = plsc.get_sparse_core_info()
    workers = sc.num_cores * sc.num_subcores
    n_pad = -(-n // quantum) * quantum
    extra = n_pad - n
    assert row_first % SUBLANES == 0 and n % rows == 0 and n >= rows and row_first + n <= table.shape[0]
    last_window = row_first + n - rows
    spare = n_slots + jnp.arange(extra * TOP_K, dtype=I32).reshape(extra, TOP_K)
    per_w = n_pad // workers
    wins = per_w // rows
    wins_tile = -(-wins // SUBLANES) * SUBLANES
    idx = jnp.concatenate([dest, spare], axis=0).T.reshape(TOP_K, workers, wins, rows)
    idx = jnp.pad(idx, ((0, 0), (0, 0), (0, wins_tile - wins), (0, 0))).reshape(TOP_K, workers * wins_tile, rows)
    mesh = plsc.VectorSubcoreMesh(core_axis_name="c", subcore_axis_name="s")
    buf = pltpu.VMEM((rows, width), table.dtype)

    @functools.partial(
        pl.kernel, mesh=mesh,
        out_type=jax.ShapeDtypeStruct((n_slots + extra * TOP_K, width), table.dtype),
        scratch_types=[pltpu.VMEM((TOP_K, wins_tile, rows), I32), buf, buf,
                       pltpu.SemaphoreType.DMA, pltpu.SemaphoreType.DMA],
        name="scatter_rows",
    )
    def scatter_kernel(table_hbm, idx_hbm, out_hbm, idx_v, buf0, buf1, sem0, sem1):
        w = lax.axis_index("s") * sc.num_cores + lax.axis_index("c")
        first = pl.multiple_of(w * wins_tile, SUBLANES)
        for k in range(TOP_K):
            pltpu.sync_copy(idx_hbm.at[k, pl.ds(first, wins_tile)], idx_v.at[k])
        base = row_first + w * per_w

        def window(i, buf, sem):
            start = pl.multiple_of(jnp.minimum(base + i * rows, last_window), SUBLANES)
            pltpu.sync_copy(table_hbm.at[pl.ds(start, rows)], buf)
            copies = [pltpu.async_copy(buf, out_hbm.at[idx_v.at[k, i]], sem) for k in range(TOP_K)]
            return copies

        @pl.loop(0, wins, step=2)
        def _(i):
            c0 = window(i, buf0, sem0)
            c1 = window(i + 1, buf1, sem1)
            for c in c0 + c1:
                c.wait()

    return scatter_kernel(table, idx)


def _gather_rows(table, idx):
    n = idx.shape[0]
    width = table.shape[1]
    rows = SC_GATHER_BYTES // (width * 4)
    sc = plsc.get_sparse_core_info()
    workers = sc.num_cores * sc.num_subcores
    quantum = workers * rows * 2
    n_pad = -(-n // quantum) * quantum
    filler = jnp.arange(n_pad - n, dtype=I32) % table.shape[0]
    per_w = n_pad // workers
    wins = per_w // rows
    wins_tile = -(-wins // SUBLANES) * SUBLANES
    idx = jnp.concatenate([idx, filler]).reshape(workers, wins, rows)
    idx = jnp.pad(idx, ((0, 0), (0, wins_tile - wins), (0, 0))).reshape(workers * wins_tile, rows)
    mesh = plsc.VectorSubcoreMesh(core_axis_name="c", subcore_axis_name="s")
    buf = pltpu.VMEM((rows, width), table.dtype)

    @functools.partial(
        pl.kernel, mesh=mesh,
        out_type=jax.ShapeDtypeStruct((n_pad, width), table.dtype),
        scratch_types=[pltpu.VMEM((wins_tile, rows), I32), buf, buf,
                       pltpu.SemaphoreType.DMA, pltpu.SemaphoreType.DMA],
        name="gather_rows",
    )
    def gather_kernel(table_hbm, idx_hbm, out_hbm, idx_v, buf0, buf1, sem0, sem1):
        w = lax.axis_index("s") * sc.num_cores + lax.axis_index("c")
        pltpu.sync_copy(idx_hbm.at[pl.ds(pl.multiple_of(w * wins_tile, SUBLANES), wins_tile)], idx_v)
        base = pl.multiple_of(w * per_w, rows)

        @pl.loop(0, wins, step=2)
        def _(i):
            g0 = pltpu.async_copy(table_hbm.at[idx_v.at[i]], buf0, sem0)
            g1 = pltpu.async_copy(table_hbm.at[idx_v.at[i + 1]], buf1, sem1)
            g0.wait()
            pltpu.sync_copy(buf0, out_hbm.at[pl.ds(base + i * rows, rows)])
            g1.wait()
            pltpu.sync_copy(buf1, out_hbm.at[pl.ds(base + (i + 1) * rows, rows)])

    return gather_kernel(table, idx)


def _routing_tables(route, counts, counts_before, blk):
    T = route.shape[0]
    TK = T * TOP_K
    idx = route[:, 0:TOP_K].astype(I32)
    rank = route[:, 2 * TOP_K:3 * TOP_K].astype(I32) - counts_before[idx]
    padded = (counts + blk - 1) // blk * blk
    pad_end = jnp.cumsum(padded)
    pad_start = pad_end - padded
    dest = pad_start[idx] + rank
    n_blocks = -(-TK // blk) + N_EXPERTS
    blk_start = jnp.arange(n_blocks, dtype=I32) * blk
    block_e = jnp.sum((blk_start[:, None] >= pad_end[None, :]).astype(I32), axis=1)
    block_e = jnp.minimum(block_e, N_EXPERTS - 1)
    last = (pad_start + counts)[block_e]
    n_valid = jnp.clip(last - blk_start, 0, blk)
    return block_e, n_valid, dest


def _mix(x, mod, buf, past_k, past_v, p, lam_init, counts_in, tm, tq, pool, pool_rows, pool_first):
    B, L, _ = x.shape
    q, kb, vb, kf, vf, conv, state = _premix(x, mod, p["g_pre_mix"], p["w_in"], buf, p["w_dw"], p["b_dw"],
                                             p["gmat"], p["g_cn"], p["b_cn"], tm)
    if past_k is None:
        attn = _attention(q, kb, vb, p["lam_rows"], p["g_subln"], lam_init, tq, tq, 0, True)
    else:
        P = past_k.shape[1]
        keys = jnp.concatenate([past_k.reshape(B, P, QK_W).astype(BF16), kb], axis=1)
        vals = jnp.concatenate([past_v.reshape(B, P, ATTN_W).astype(BF16), vb], axis=1)
        attn = _attention(q, keys, vals, p["lam_rows"], p["g_subln"], lam_init, L, P + L, P, False)
    x1, h2, route, cnt = _postmix(attn, conv, x, mod, p["w_out"], p["g_post_mix"], p["g_pre_ffn"],
                                  p["w_router"], p["b_router"], counts_in, tm, pool, pool_rows, pool_first)
    return x1, h2, route, cnt, kf, vf, state[:, CONV_HALO - (CONV_W - 1):, :]


def _moe_rows(pool, row_first, route_groups, counts, counts_before, p, blk):
    route = jnp.concatenate([r.reshape(-1, LANES) for r in route_groups], axis=0)
    T = route.shape[0]
    block_e, n_valid, dest = _routing_tables(route, counts, counts_before, blk)
    x_pad = _scatter_rows(pool, row_first, dest, block_e.shape[0] * blk)
    y_pad = _experts(x_pad, block_e, n_valid, p["w_gu"], p["b_gu"], p["w_down"], p["b_down"], blk)
    return _gather_rows(y_pad, dest.T.reshape(T * TOP_K))


def _prepare_params(l, w_ada, b_ada, g_pre_mix, g_post_mix, w_in, lambda_q1, lambda_k1, lambda_q2, lambda_k2,
                    g_subln, w_dw, b_dw, g_cnorm, b_cnorm, w_out, g_pre_ffn, g_post_ffn,
                    w_router, b_router, w_gu, b_gu, w_down, b_down):
    lam_rows = jnp.zeros((8, LANES), F32)
    for r, vec in enumerate((lambda_q1[l], lambda_k1[l], lambda_q2[l], lambda_k2[l])):
        lam_rows = lam_rows.at[r, :HEAD_DIM].set(vec)
    ch = jnp.arange(CONV_CH, dtype=I32) // GROUP_CH
    gmat = (ch[:, None] == ch[None, :]).astype(BF16)
    w_dw_pad = jnp.zeros((CONV_HALO, CONV_CH), F32).at[:CONV_W].set(w_dw[l])
    w_router_pad = jnp.zeros((D_MODEL, LANES), F32).at[:, :N_EXPERTS].set(w_router[l])
    b_router_pad = jnp.full((1, LANES), NEG, F32).at[0, :N_EXPERTS].set(b_router[l])
    return dict(w_ada=w_ada[l], b_ada=b_ada[l], g_pre_mix=g_pre_mix[l], g_post_mix=g_post_mix[l],
                w_in=w_in[l].astype(BF16), lam_rows=lam_rows, g_subln=g_subln[l], w_dw=w_dw_pad, b_dw=b_dw[l],
                gmat=gmat, g_cn=g_cnorm[l], b_cn=b_cnorm[l], w_out=w_out[l].astype(BF16),
                g_pre_ffn=g_pre_ffn[l], g_post_ffn=g_post_ffn[l], w_router=w_router_pad, b_router=b_router_pad,
                w_gu=w_gu[l], b_gu=b_gu[l], w_down=w_down[l], b_down=b_down[l])


def kernel(x_prompt, x_sample, cache_k, cache_v, state_conv, c_prompt, c_sample, w_ada, b_ada, g_pre_mix, g_post_mix, w_in, lambda_q1, lambda_k1, lambda_q2, lambda_k2, g_subln, w_dw, b_dw, g_cnorm, b_cnorm, w_out, g_pre_ffn, g_post_ffn, w_router, b_router, w_gu, b_gu, w_down, b_down):
    depth = w_ada.shape[0]
    Bp, Lp, _ = x_prompt.shape
    Bs, Ls, _ = x_sample.shape
    yp, ys = x_prompt, x_sample
    outs = [[] for _ in range(6)]
    for l in range(depth):
        p = _prepare_params(l, w_ada, b_ada, g_pre_mix, g_post_mix, w_in, lambda_q1, lambda_k1, lambda_q2,
                            lambda_k2, g_subln, w_dw, b_dw, g_cnorm, b_cnorm, w_out, g_pre_ffn, g_post_ffn,
                            w_router, b_router, w_gu, b_gu, w_down, b_down)
        lam_init = 0.8 - 0.6 * math.exp(-0.3 * l)
        mod = _ada(jnp.concatenate([c_prompt, c_sample], axis=0), p["w_ada"], p["b_ada"])
        mod = mod.reshape(Bp + Bs, N_MOD, D_MODEL)
        buf_p = jnp.zeros((Bp, CONV_HALO, CONV_CH), F32)
        buf_s = jnp.pad(state_conv[l], ((0, 0), (CONV_HALO - (CONV_W - 1), 0), (0, 0)))
        tm_p = min(Lp, 512)
        assert Bp % 2 == 0
        hb = Bp // 2
        t_a, t_b = hb * Lp, hb * Lp + Bs * Ls
        pool_rows = Bp * Lp + Bs * Ls
        zero_counts = jnp.zeros((SUBLANES, LANES), F32)
        x1p, pool, rtp, cnt_p, kp, vp, cp = _mix(yp, mod[:Bp], buf_p, None, None, p, lam_init, zero_counts,
                                                 tm_p, tm_p, None, pool_rows, 0)
        x1s, pool, rts, cnt_s, ks, vs, cs = _mix(ys, mod[Bp:], buf_s, cache_k[l], cache_v[l], p, lam_init,
                                                 cnt_p[Bp - 1], Ls, Ls, pool, pool_rows, Bp * Lp)
        n_a = cnt_p[hb - 1, 0, :N_EXPERTS].astype(I32)
        n_b = cnt_s[Bs - 1, 0, :N_EXPERTS].astype(I32) - n_a
        y_a = _moe_rows(pool, 0, (rtp[:hb],), n_a, jnp.zeros_like(n_a), p, EXPERT_ROWS)
        y_b = _moe_rows(pool, t_a, (rtp[hb:], rts), n_b, n_a, p, EXPERT_ROWS)
        tm_c = min(tm_p, 256)
        yp = _combine(y_a, rtp, x1p, mod[:Bp], p["g_post_ffn"], tm_c, t_a, 0, 0, hb)
        yp = _combine(y_b, rtp, x1p, mod[:Bp], p["g_post_ffn"], tm_c, t_b, 0, hb, hb, out_prev=yp)
        ys = _combine(y_b, rts, x1s, mod[Bp:], p["g_post_ffn"], Ls, t_b, hb * Lp)
        for lst, val in zip(outs, (kp, vp, cp, ks, vs, cs)):
            lst.append(val)
    return (yp, ys) + tuple(jnp.stack(o) for o in outs)
```

```python
import functools
import math

import jax
import jax.numpy as jnp
from jax import lax
from jax.experimental import pallas as pl
from jax.experimental.pallas import tpu as pltpu
from jax.experimental.pallas import tpu_sc as plsc

F32 = jnp.float32
BF16 = jnp.bfloat16
I32 = jnp.int32

D_MODEL = 1024
CHUNK = 64
CHUNK_SHIFT = 6
N_HEADS = 4
V_DIM = 128
HEAD_DIM = 64
QK_W = 512
ATTN_W = 512
CONV_CH = 512
CONV_W = 31
CONV_GROUPS = 8
GROUP_CH = CONV_CH // CONV_GROUPS
IN_W = 2 * QK_W + ATTN_W + 2 * CONV_CH
N_MOD = 6
N_EXPERTS = 32
TOP_K = 4
D_FF = 1024
SWIGLU_ALPHA = 1.702
SWIGLU_LIMIT = 7.0
EPS = 1e-6

LANES = 128
SUBLANES = 8
CONV_HALO = 32
CONV_ROWS = 32
NEG = -1e30
LOG2E = math.log2(math.e)
ATTN_ROWS = 128
EXPERT_ROWS = 512
VMEM_LIMIT = 48 * 1024 * 1024
EXPERT_VMEM_LIMIT = 56 * 1024 * 1024


def _sigmoid(x):
    return 1.0 / (1.0 + jnp.exp(-x))


def _split_bf16(x):
    hi = x.astype(BF16)
    lo = (x - hi.astype(F32)).astype(BF16)
    return hi, lo


def _dot(a, b):
    return jnp.dot(a, b, preferred_element_type=F32)


def _dot3(a, b):
    ah, al = _split_bf16(a)
    bh, bl = _split_bf16(b)
    return _dot(ah, bh) + _dot(ah, bl) + _dot(al, bh)


def _rms(x):
    return x * lax.rsqrt(jnp.mean(x * x, axis=-1, keepdims=True) + EPS)


HALF = D_MODEL // 2
HIGH16 = -65536


def _pack_halves(x):
    lo = lax.bitcast_convert_type(x[:, :HALF].astype(BF16).astype(F32), I32)
    hi = lax.bitcast_convert_type(x[:, HALF:].astype(BF16).astype(F32), I32)
    return jnp.bitwise_or(jnp.bitwise_and(jnp.right_shift(lo, 16), 0xFFFF), jnp.bitwise_and(hi, HIGH16))


def _unpack_halves(w):
    lo = lax.bitcast_convert_type(jnp.left_shift(w, 16), F32)
    hi = lax.bitcast_convert_type(jnp.bitwise_and(w, HIGH16), F32)
    return lo, hi


def _ada_kernel(c_ref, w_ref, b_ref, o_ref):
    c = c_ref[...]
    o_ref[...] = _dot3(c * _sigmoid(c), w_ref[...]) + b_ref[...]


def _ada(c, w_ada, b_ada):
    n = c.shape[0]
    return pl.pallas_call(
        _ada_kernel,
        grid=(N_MOD,),
        in_specs=[pl.BlockSpec((n, D_MODEL), lambda j: (0, 0)),
                  pl.BlockSpec((D_MODEL, D_MODEL), lambda j: (0, j)),
                  pl.BlockSpec((1, D_MODEL), lambda j: (0, j))],
        out_specs=pl.BlockSpec((n, D_MODEL), lambda j: (0, j)),
        out_shape=jax.ShapeDtypeStruct((n, N_MOD * D_MODEL), F32),
        compiler_params=pltpu.CompilerParams(dimension_semantics=("arbitrary",), vmem_limit_bytes=VMEM_LIMIT),
        name="ada",
    )(c, w_ada, b_ada.reshape(1, N_MOD * D_MODEL))


def _premix_kernel(x_ref, mod_ref, g_ref, w_ref, buf_ref, wdw_ref, bdw_ref, gmat_ref, gcn_ref, bcn_ref,
                   q_ref, kb_ref, vb_ref, kf_hbm, vf_hbm, co_ref, st_ref, ext_ref, sh_ref, y_ref, kv_buf, kv_sem,
                   *, tm):
    batch = pl.program_id(0)
    s = pl.program_id(1)
    step = batch * pl.num_programs(1) + s
    last = pl.num_programs(0) * pl.num_programs(1) - 1
    slot = step % 2

    def kv_copies(slot):
        return [pltpu.make_async_copy(kv_buf.at[slot, i, :, pl.ds(h * V_DIM, V_DIM)],
                                      out.at[batch, pl.ds(s * tm, tm), h, :], kv_sem.at[slot, i, h])
                for i, out in enumerate((kf_hbm, vf_hbm)) for h in range(N_HEADS)]

    @pl.when(step >= 2)
    def _():
        for c in kv_copies(slot):
            c.wait()

    @pl.when(s == 0)
    def _():
        ext_ref[0:CONV_HALO, :] = buf_ref[0]

    x = x_ref[0]
    shift = mod_ref[0, 0:1, :]
    scale = mod_ref[0, 1:2, :]
    h = _rms(x) * g_ref[...] * (1.0 + scale) + shift
    hb = h.astype(BF16)

    u0 = 2 * QK_W + ATTN_W
    val = _dot(hb, w_ref[:, u0:u0 + CONV_CH])
    gate = _dot(hb, w_ref[:, u0 + CONV_CH:u0 + 2 * CONV_CH])
    ext_ref[CONV_HALO:CONV_HALO + tm, :] = val * _sigmoid(gate)

    off = CONV_HALO - (CONV_W - 1)
    span = tm + CONV_HALO - SUBLANES
    for b in range(1, SUBLANES):
        sh_ref[b - 1, 0:span, :] = ext_ref[b:b + span, :]
    for c in range(tm // CONV_ROWS):
        r0 = c * CONV_ROWS
        acc = jnp.zeros((CONV_ROWS, CONV_CH), F32)
        for j in range(CONV_W):
            b = (j + off) % SUBLANES
            a = r0 + j + off - b
            rows_j = ext_ref[a:a + CONV_ROWS, :] if b == 0 else sh_ref[b - 1, a:a + CONV_ROWS, :]
            acc = acc + wdw_ref[j:j + 1, :] * rows_j
        y_ref[r0:r0 + CONV_ROWS, :] = acc + bdw_ref[...]

    zq = _dot(hb, w_ref[:, 0:QK_W])
    q_ref[0] = (zq * (HEAD_DIM ** -0.5 * LOG2E)).astype(BF16)
    zk = _dot(hb, w_ref[:, QK_W:2 * QK_W])
    kv_buf[slot, 0] = zk
    kb_ref[0] = zk.astype(BF16)
    zv = _dot(hb, w_ref[:, 2 * QK_W:2 * QK_W + ATTN_W])
    kv_buf[slot, 1] = zv
    vb_ref[0] = zv.astype(BF16)

    y = y_ref[...]
    gm = gmat_ref[...]
    yh, yl = _split_bf16(y)
    mu = (_dot(yh, gm) + _dot(yl, gm)) * (1.0 / GROUP_CH)
    d = y - mu
    dh, dl = _split_bf16(d * d)
    var = (_dot(dh, gm) + _dot(dl, gm)) * (1.0 / GROUP_CH)
    yn = d * lax.rsqrt(var + EPS) * gcn_ref[...] + bcn_ref[...]
    co_ref[0] = (yn * _sigmoid(yn)).astype(BF16)

    tail = ext_ref[tm:tm + CONV_HALO, :]
    st_ref[0] = tail
    ext_ref[0:CONV_HALO, :] = tail

    for c in kv_copies(slot):
        c.start()

    @pl.when(step == last)
    def _():
        for c in kv_copies(slot):
            c.wait()

    @pl.when(jnp.logical_and(step == last, step >= 1))
    def _():
        for c in kv_copies(1 - slot):
            c.wait()


def _premix(x, mod, g_pre_mix, w_in_bf, buf, w_dw, b_dw, gmat, g_cn, b_cn, tm):
    B, L, _ = x.shape
    assert L % tm == 0 and tm % CONV_ROWS == 0
    row = lambda b, s: (b, s, 0)
    const2 = lambda b, s: (0, 0)
    bf_tile = jax.ShapeDtypeStruct((B, L, QK_W), BF16)
    f_heads = jax.ShapeDtypeStruct((B, L, N_HEADS, V_DIM), F32)
    return pl.pallas_call(
        functools.partial(_premix_kernel, tm=tm),
        grid=(B, L // tm),
        in_specs=[pl.BlockSpec((1, tm, D_MODEL), row),
                  pl.BlockSpec((1, N_MOD, D_MODEL), lambda b, s: (b, 0, 0)),
                  pl.BlockSpec((1, D_MODEL), const2),
                  pl.BlockSpec((D_MODEL, IN_W), const2),
                  pl.BlockSpec((1, CONV_HALO, CONV_CH), lambda b, s: (b, 0, 0)),
                  pl.BlockSpec((CONV_HALO, CONV_CH), const2),
                  pl.BlockSpec((1, CONV_CH), const2),
                  pl.BlockSpec((CONV_CH, CONV_CH), const2),
                  pl.BlockSpec((1, CONV_CH), const2),
                  pl.BlockSpec((1, CONV_CH), const2)],
        out_specs=[pl.BlockSpec((1, tm, QK_W), row)] * 3
                  + [pl.BlockSpec(memory_space=pl.ANY)] * 2
                  + [pl.BlockSpec((1, tm, QK_W), row),
                     pl.BlockSpec((1, CONV_HALO, CONV_CH), lambda b, s: (b, 0, 0))],
        out_shape=[bf_tile, bf_tile, bf_tile, f_heads, f_heads, bf_tile,
                   jax.ShapeDtypeStruct((B, CONV_HALO, CONV_CH), F32)],
        scratch_shapes=[pltpu.VMEM((CONV_HALO + tm, CONV_CH), F32),
                        pltpu.VMEM((SUBLANES - 1, CONV_HALO + tm, CONV_CH), F32),
                        pltpu.VMEM((tm, CONV_CH), F32),
                        pltpu.VMEM((2, 2, tm, QK_W), F32),
                        pltpu.SemaphoreType.DMA((2, 2, N_HEADS))],
        compiler_params=pltpu.CompilerParams(dimension_semantics=("arbitrary", "arbitrary"),
                                             vmem_limit_bytes=VMEM_LIMIT),
        name="premix",
    )(x, mod, g_pre_mix.reshape(1, D_MODEL), w_in_bf, buf, w_dw, b_dw.reshape(1, CONV_CH), gmat,
      g_cn.reshape(1, CONV_CH), b_cn.reshape(1, CONV_CH))


def _attn_kernel(slopes_ref, q_ref, k_ref, v_ref, kb_ref, mt_ref, lam_ref, gs_ref, o_ref,
                 ke1_ref, ke2_ref, ve_ref, m1_ref, a1_ref, m2_ref, a2_ref,
                 *, tq, tk, rows, q_offset, causal, lam_init):
    slope = slopes_ref[pl.program_id(1)] * LOG2E
    lq = q_ref.shape[1]
    lk = k_ref.shape[1]
    nq = lq // tq
    lane = lax.broadcasted_iota(I32, (1, V_DIM), 1)

    low = jnp.broadcast_to(lane < HEAD_DIM, (lk, V_DIM))
    k = k_ref[0]
    zero = jnp.zeros((lk, V_DIM), BF16)
    ke1_ref[...] = jnp.where(low, k, zero)
    ke2_ref[...] = jnp.where(low, zero, k)
    ve_ref[:, 0:V_DIM] = v_ref[0]
    ve_ref[:, V_DIM:2 * V_DIM] = jnp.broadcast_to(jnp.where(lane == 0, 1.0, 0.0), (lk, V_DIM)).astype(BF16)

    nt = (((1,), (1,)), ((), ()))
    maps = ((ke1_ref, m1_ref, a1_ref), (ke2_ref, m2_ref, a2_ref))

    def lanes(x, n):
        if n % LANES == 0:
            return jnp.concatenate([x] * (n // LANES), axis=1)
        return jnp.broadcast_to(x[:, 0:1], (rows, n))

    def tile(q0, k0, masked, first, tk=tk):
        ve = ve_ref[pl.ds(k0, tk), :]
        q_first = (lax.broadcasted_iota(I32, (1, 1), 0) + (q0 + q_offset)).astype(F32)
        bias = kb_ref[0, :, pl.ds(k0, tk)] - slope * q_first
        for ke_ref, m_ref, a_ref in maps:
            ke = ke_ref[pl.ds(k0, tk), :]
            for r0 in range(0, tq, rows):
                s = lax.dot_general(q_ref[0, pl.ds(q0 + r0, rows), :], ke, nt, preferred_element_type=F32)
                s = s + bias
                if masked:
                    s = s + mt_ref[0, r0:r0 + rows, :]
                m_tile = jnp.max(s, axis=-1, keepdims=True)
                if first:
                    m_new = jnp.broadcast_to(m_tile, (rows, LANES))
                    a_ref[r0:r0 + rows, :] = _dot(jnp.exp2(s - lanes(m_new, tk)).astype(BF16), ve)
                else:
                    m_old = m_ref[r0:r0 + rows, :]
                    m_new = jnp.maximum(m_old, m_tile)
                    p = jnp.exp2(s - lanes(m_new, tk))
                    alpha = jnp.exp2(m_old - m_new)
                    a_ref[r0:r0 + rows, :] = (lanes(alpha, 2 * V_DIM) * a_ref[r0:r0 + rows, :]
                                              + _dot(p.astype(BF16), ve))
                m_ref[r0:r0 + rows, :] = m_new

    lv = lam_ref[...]
    lam = (jnp.exp(jnp.sum(lv[0:1, :] * lv[1:2, :], axis=-1, keepdims=True))
           - jnp.exp(jnp.sum(lv[2:3, :] * lv[3:4, :], axis=-1, keepdims=True)) + lam_init)

    def finish(q0):
        a1 = a1_ref[...]
        a2 = a2_ref[...]
        o = a1[:, 0:V_DIM] / a1[:, V_DIM:V_DIM + 1] - lam * (a2[:, 0:V_DIM] / a2[:, V_DIM:V_DIM + 1])
        o_ref[0, pl.ds(q0, tq), :] = (_rms(o) * gs_ref[...] * (1.0 - lam_init)).astype(BF16)

    if not causal:
        tile(0, 0, True, True)
        finish(0)
        return

    for qi in range(nq):
        q0 = qi * tq
        for k0 in range(0, qi * tk, 2 * tk):
            tile(q0, k0, False, k0 == 0, min(2 * tk, qi * tk - k0))
        tile(q0, qi * tk, True, qi == 0)
        finish(q0)


def _mask_table(slopes, tq, tk, q_offset):
    qpos = q_offset + jnp.arange(tq, dtype=I32)[:, None]
    kpos = jnp.arange(tk, dtype=I32)[None, :]
    visible = jnp.right_shift(kpos, CHUNK_SHIFT) <= jnp.right_shift(qpos, CHUNK_SHIFT)
    fix = jnp.where(kpos > qpos, 2 * (qpos - kpos), 0).astype(F32)
    return jnp.where(visible[None], (slopes * LOG2E)[:, None, None] * fix[None], NEG)


def _attention(q, k, v, lam_rows, g_subln, lam_init, tq, tk, q_offset, causal):
    B, Lq, _ = q.shape
    Lk = k.shape[1]
    assert Lq % tq == 0 and Lk % tk == 0 and (not causal or (tq == tk and tq % CHUNK == 0 and q_offset == 0))
    assert causal or Lk == tk
    rows = min(tq, ATTN_ROWS)
    slopes = jnp.asarray([2.0 ** (-8.0 * (h + 1) / N_HEADS) for h in range(N_HEADS)], F32)
    table = _mask_table(slopes, tq, tk, q_offset)
    key_bias = ((slopes * LOG2E)[:, None] * jnp.arange(Lk, dtype=F32)[None, :]).reshape(N_HEADS, 1, Lk)
    key_ext = pltpu.VMEM((Lk, V_DIM), BF16)
    stat = pltpu.VMEM((tq, LANES), F32)
    acc = pltpu.VMEM((tq, 2 * V_DIM), F32)
    head = lambda b, h, sl: (b, 0, h)
    return pl.pallas_call(
        functools.partial(_attn_kernel, tq=tq, tk=tk, rows=rows, q_offset=q_offset, causal=causal,
                          lam_init=lam_init),
        grid_spec=pltpu.PrefetchScalarGridSpec(
            num_scalar_prefetch=1,
            grid=(B, N_HEADS),
            in_specs=[pl.BlockSpec((1, Lq, V_DIM), head),
                      pl.BlockSpec((1, Lk, V_DIM), head),
                      pl.BlockSpec((1, Lk, V_DIM), head),
                      pl.BlockSpec((1, 1, Lk), lambda b, h, sl: (h, 0, 0)),
                      pl.BlockSpec((1, tq, tk), lambda b, h, sl: (h, 0, 0)),
                      pl.BlockSpec((8, LANES), lambda b, h, sl: (0, 0)),
                      pl.BlockSpec((1, V_DIM), lambda b, h, sl: (0, 0))],
            out_specs=pl.BlockSpec((1, Lq, V_DIM), head),
            scratch_shapes=[key_ext, key_ext, pltpu.VMEM((Lk, 2 * V_DIM), BF16), stat, acc, stat, acc]),
        out_shape=jax.ShapeDtypeStruct((B, Lq, ATTN_W), BF16),
        compiler_params=pltpu.CompilerParams(dimension_semantics=("arbitrary",) * 2,
                                             vmem_limit_bytes=VMEM_LIMIT),
        name="attn",
    )(slopes, q, k, v, key_bias, table, lam_rows, g_subln.reshape(1, V_DIM))


def _postmix_kernel(a_ref, c_ref, x_ref, mod_ref, wo_ref, gpm_ref, gpf_ref, wr_ref, br_ref, cin_ref,
                    x1_ref, h2_ref, rt_ref, cnt_ref, carry_ref, *, tm):
    first = jnp.logical_and(pl.program_id(0) == 0, pl.program_id(1) == 0)

    @pl.when(first)
    def _():
        carry_ref[...] = cin_ref[...]

    mix = _dot(a_ref[0], wo_ref[0:ATTN_W, :]) + _dot(c_ref[0], wo_ref[ATTN_W:ATTN_W + CONV_CH, :])
    gate_a = mod_ref[0, 2:3, :]
    shift_f = mod_ref[0, 3:4, :]
    scale_f = mod_ref[0, 4:5, :]
    x1 = x_ref[0] + gate_a * (_rms(mix) * gpm_ref[...])
    x1_ref[0] = x1
    h2 = _rms(x1) * gpf_ref[...] * (1.0 + scale_f) + shift_f
    h2_ref[...] = _pack_halves(h2)

    logits = _dot3(h2, wr_ref[...]) + br_ref[...]
    lane = lax.broadcasted_iota(I32, (tm, LANES), 1).astype(F32)
    vals, idxs = [], []
    for _ in range(TOP_K):
        m = jnp.max(logits, axis=-1, keepdims=True)
        idx = jnp.min(jnp.where(logits == m, lane, float(LANES)), axis=-1, keepdims=True)
        vals.append(m)
        idxs.append(idx)
        logits = jnp.where(lane == idx, 2.0 * NEG, logits)
    es = [jnp.exp(v - vals[0]) for v in vals]
    denom = es[0] + es[1] + es[2] + es[3]

    onehot = jnp.zeros((tm, LANES), F32)
    for idx in idxs:
        onehot = jnp.where(lane == idx, 1.0, onehot)
    r_i = lax.broadcasted_iota(I32, (tm, tm), 0)
    c_i = lax.broadcasted_iota(I32, (tm, tm), 1)
    tri = jnp.where(c_i < r_i, 1.0, 0.0).astype(BF16)
    before = _dot(tri, onehot.astype(BF16)) + carry_ref[0:1, :]

    rt = jnp.zeros((tm, LANES), F32)
    for k in range(TOP_K):
        rank = jnp.sum(jnp.where(lane == idxs[k], before, 0.0), axis=-1, keepdims=True)
        rt = jnp.where(lane == k, idxs[k], rt)
        rt = jnp.where(lane == TOP_K + k, es[k] / denom, rt)
        rt = jnp.where(lane == 2 * TOP_K + k, rank, rt)
    rt_ref[0] = rt

    carry_ref[...] = carry_ref[...] + jnp.sum(onehot, axis=0, keepdims=True)
    cnt_ref[0] = carry_ref[...]


def _postmix_into_kernel(pool_ref, *refs, tm):
    del pool_ref
    _postmix_kernel(*refs, tm=tm)


def _postmix(attn, conv, x, mod, w_out_bf, g_post_mix, g_pre_ffn, w_router_pad, b_router_pad, counts_in, tm,
             pool, pool_rows, pool_first):
    B, L, _ = x.shape
    assert pool_first % tm == 0
    row = lambda b, s: (b, s, 0)
    const2 = lambda b, s: (0, 0)
    body, aliases, lead_specs, lead_args = _postmix_kernel, {}, [], ()
    if pool is not None:
        body, aliases = _postmix_into_kernel, {0: 1}
        lead_specs, lead_args = [pl.BlockSpec(memory_space=pl.ANY)], (pool,)
    return pl.pallas_call(
        functools.partial(body, tm=tm),
        grid=(B, L // tm),
        input_output_aliases=aliases,
        in_specs=lead_specs + [
                  pl.BlockSpec((1, tm, ATTN_W), row),
                  pl.BlockSpec((1, tm, CONV_CH), row),
                  pl.BlockSpec((1, tm, D_MODEL), row),
                  pl.BlockSpec((1, N_MOD, D_MODEL), lambda b, s: (b, 0, 0)),
                  pl.BlockSpec((D_MODEL, D_MODEL), const2),
                  pl.BlockSpec((1, D_MODEL), const2),
                  pl.BlockSpec((1, D_MODEL), const2),
                  pl.BlockSpec((D_MODEL, LANES), const2),
                  pl.BlockSpec((1, LANES), const2),
                  pl.BlockSpec((8, LANES), const2)],
        out_specs=[pl.BlockSpec((1, tm, D_MODEL), row),
                   pl.BlockSpec((tm, HALF), lambda b, s: (pool_first // tm + b * (L // tm) + s, 0)),
                   pl.BlockSpec((1, tm, LANES), row),
                   pl.BlockSpec((1, SUBLANES, LANES), lambda b, s: (b, 0, 0))],
        out_shape=[jax.ShapeDtypeStruct((B, L, D_MODEL), F32),
                   jax.ShapeDtypeStruct((pool_rows, HALF), I32),
                   jax.ShapeDtypeStruct((B, L, LANES), F32),
                   jax.ShapeDtypeStruct((B, SUBLANES, LANES), F32)],
        scratch_shapes=[pltpu.VMEM((8, LANES), F32)],
        compiler_params=pltpu.CompilerParams(dimension_semantics=("arbitrary", "arbitrary"),
                                             vmem_limit_bytes=VMEM_LIMIT),
        name="postmix",
    )(*lead_args, attn, conv, x, mod, w_out_bf, g_post_mix.reshape(1, D_MODEL), g_pre_ffn.reshape(1, D_MODEL),
      w_router_pad, b_router_pad, counts_in)


def _expert_kernel(be_ref, nv_ref, x_ref, wgu_ref, bgu_ref, wd_ref, bd_ref, y_ref, wgu_bf, wd_bf):
    i = pl.program_id(0)
    ff_half = D_FF // 2

    @pl.when(jnp.logical_or(i == 0, be_ref[i] != be_ref[jnp.maximum(i - 1, 0)]))
    def _():
        wgu_bf[...] = wgu_ref[0].astype(BF16)
        wd_bf[...] = wd_ref[0].astype(BF16)

    @pl.when(nv_ref[i] > 0)
    def _():
        x_lo, x_hi = _unpack_halves(x_ref[...])
        x_lo = x_lo.astype(BF16)
        x_hi = x_hi.astype(BF16)
        acc = None
        for c in range(2):
            lo, hi = c * ff_half, (c + 1) * ff_half
            g = (_dot(x_lo, wgu_bf[0:HALF, lo:hi]) + _dot(x_hi, wgu_bf[HALF:D_MODEL, lo:hi])
                 + bgu_ref[0, :, lo:hi])
            lin = (_dot(x_lo, wgu_bf[0:HALF, D_FF + lo:D_FF + hi]) + _dot(x_hi, wgu_bf[HALF:D_MODEL, D_FF + lo:D_FF + hi])
                   + bgu_ref[0, :, D_FF + lo:D_FF + hi])
            g = jnp.minimum(g, SWIGLU_LIMIT)
            lin = jnp.clip(lin, -SWIGLU_LIMIT, SWIGLU_LIMIT)
            act = g * _sigmoid(SWIGLU_ALPHA * g) * (lin + 1.0)
            part = _dot(act.astype(BF16), wd_bf[lo:hi, :])
            acc = part if acc is None else acc + part
        y_ref[...] = _pack_halves(acc + bd_ref[0])


def _experts(x_pad, block_e, n_valid, w_gu, b_gu, w_down, b_down, blk):
    n_blocks = block_e.shape[0]
    ex = lambda i, be, nv: (be[i], 0, 0)
    rows = lambda i, be, nv: (i, 0)
    return pl.pallas_call(
        _expert_kernel,
        grid_spec=pltpu.PrefetchScalarGridSpec(
            num_scalar_prefetch=2,
            grid=(n_blocks,),
            in_specs=[pl.BlockSpec((blk, HALF), rows),
                      pl.BlockSpec((1, D_MODEL, 2 * D_FF), ex), pl.BlockSpec((1, 1, 2 * D_FF), ex),
                      pl.BlockSpec((1, D_FF, D_MODEL), ex), pl.BlockSpec((1, 1, D_MODEL), ex)],
            out_specs=pl.BlockSpec((blk, HALF), rows),
            scratch_shapes=[pltpu.VMEM((D_MODEL, 2 * D_FF), BF16), pltpu.VMEM((D_FF, D_MODEL), BF16)]),
        out_shape=jax.ShapeDtypeStruct((n_blocks * blk, HALF), I32),
        compiler_params=pltpu.CompilerParams(dimension_semantics=("arbitrary",),
                                             vmem_limit_bytes=EXPERT_VMEM_LIMIT),
        name="experts",
    )(block_e, n_valid, x_pad, w_gu, b_gu.reshape(N_EXPERTS, 1, 2 * D_FF), w_down,
      b_down.reshape(N_EXPERTS, 1, D_MODEL))


def _combine_kernel(y0_ref, y1_ref, y2_ref, y3_ref, rt_ref, x1_ref, mod_ref, g_ref, o_ref):
    rt = rt_ref[0]
    f_lo = f_hi = None
    for k, y_ref in enumerate((y0_ref, y1_ref, y2_ref, y3_ref)):
        lo, hi = _unpack_halves(y_ref[...])
        w = rt[:, TOP_K + k:TOP_K + k + 1]
        f_lo = w * lo if f_lo is None else f_lo + w * lo
        f_hi = w * hi if f_hi is None else f_hi + w * hi
    ms = (jnp.sum(f_lo * f_lo, axis=-1, keepdims=True) + jnp.sum(f_hi * f_hi, axis=-1, keepdims=True)) / D_MODEL
    r = lax.rsqrt(ms + EPS)
    gate_f = mod_ref[0, 5:6, :]
    g = g_ref[...]
    o_ref[0, :, 0:HALF] = x1_ref[0, :, 0:HALF] + gate_f[:, 0:HALF] * (f_lo * r * g[:, 0:HALF])
    o_ref[0, :, HALF:D_MODEL] = x1_ref[0, :, HALF:D_MODEL] + gate_f[:, HALF:D_MODEL] * (f_hi * r * g[:, HALF:D_MODEL])


def _combine_into_kernel(prev_ref, *refs):
    del prev_ref
    _combine_kernel(*refs)


def _combine(y, route, x1, mod, g_post_ffn, tm, t_all, t_first, b_first=0, n_batch=None, out_prev=None):
    B, L, _ = x1.shape
    n_batch = B if n_batch is None else n_batch
    assert t_all % tm == 0 and t_first % tm == 0
    row = lambda b, s: (b + b_first, s, 0)
    y_specs = [pl.BlockSpec((tm, HALF), lambda b, s, k=k: ((k * t_all + t_first) // tm + b * (L // tm) + s, 0))
               for k in range(TOP_K)]
    in_specs = y_specs + [pl.BlockSpec((1, tm, LANES), row),
                          pl.BlockSpec((1, tm, D_MODEL), row),
                          pl.BlockSpec((1, N_MOD, D_MODEL), lambda b, s: (b + b_first, 0, 0)),
                          pl.BlockSpec((1, D_MODEL), lambda b, s: (0, 0))]
    args = (y, y, y, y, route, x1, mod, g_post_ffn.reshape(1, D_MODEL))
    body, aliases = _combine_kernel, {}
    if out_prev is not None:
        body, aliases = _combine_into_kernel, {0: 0}
        in_specs = [pl.BlockSpec(memory_space=pl.ANY)] + in_specs
        args = (out_prev,) + args
    return pl.pallas_call(
        body,
        grid=(n_batch, L // tm),
        in_specs=in_specs,
        out_specs=pl.BlockSpec((1, tm, D_MODEL), row),
        out_shape=jax.ShapeDtypeStruct((B, L, D_MODEL), F32),
        input_output_aliases=aliases,
        compiler_params=pltpu.CompilerParams(dimension_semantics=("arbitrary", "arbitrary"),
                                             vmem_limit_bytes=VMEM_LIMIT),
        name="combine",
    )(*args)


SC_GATHER_BYTES = 128 * 1024


def _scatter_quantum(width):
    rows = SC_GATHER_BYTES // (width * 4)
    sc = plsc.get_sparse_core_info()
    return rows, sc.num_cores * sc.num_subcores * rows * 2


def _scatter_rows(table, row_first, dest, n_slots):
    n, width = dest.shape[0], table.shape[1]
    rows, quantum = _scatter_quantum(width)
    sc = plsc.get_sparse_core_info()
    workers = sc.num_cores * sc.num_subcores
    n_pad = -(-n // quantum) * quantum
    extra = n_pad - n
    assert row_first % SUBLANES == 0 and n % rows == 0 and n >= rows and row_first + n <= table.shape[0]
    last_window = row_first + n - rows
    spare = n_slots + jnp.arange(extra * TOP_K, dtype=I32).reshape(extra, TOP_K)
    per_w = n_pad // workers
    wins = per_w // rows
    wins_tile = -(-wins // SUBLANES) * SUBLANES
    idx = jnp.concatenate([dest, spare], axis=0).T.reshape(TOP_K, workers, wins, rows)
    idx = jnp.pad(idx, ((0, 0), (0, 0), (0, wins_tile - wins), (0, 0))).reshape(TOP_K, workers * wins_tile, rows)
    mesh = plsc.VectorSubcoreMesh(core_axis_name="c", subcore_axis_name="s")
    buf = pltpu.VMEM((rows, width), table.dtype)

    @functools.partial(
        pl.kernel, mesh=mesh,
        out_type=jax.ShapeDtypeStruct((n_slots + extra * TOP_K, width), table.dtype),
        scratch_types=[pltpu.VMEM((TOP_K, wins_tile, rows), I32), buf, buf,
                       pltpu.SemaphoreType.DMA, pltpu.SemaphoreType.DMA],
        name="scatter_rows",
    )
    def scatter_kernel(table_hbm, idx_hbm, out_hbm, idx_v, buf0, buf1, sem0, sem1):
        w = lax.axis_index("s") * sc.num_cores + lax.axis_index("c")
        first = pl.multiple_of(w * wins_tile, SUBLANES)
        for k in range(TOP_K):
            pltpu.sync_copy(idx_hbm.at[k, pl.ds(first, wins_tile)], idx_v.at[k])
        base = row_first + w * per_w

        def window(i, buf, sem):
            start = pl.multiple_of(jnp.minimum(base + i * rows, last_window), SUBLANES)
            pltpu.sync_copy(table_hbm.at[pl.ds(start, rows)], buf)
            copies = [pltpu.async_copy(buf, out_hbm.at[idx_v.at[k, i]], sem) for k in range(TOP_K)]
            return copies

        @pl.loop(0, wins, step=2)
        def _(i):
            c0 = window(i, buf0, sem0)
            c1 = window(i + 1, buf1, sem1)
            for c in c0 + c1:
                c.wait()

    return scatter_kernel(table, idx)


def _gather_rows(table, idx):
    n = idx.shape[0]
    width = table.shape[1]
    rows = SC_GATHER_BYTES // (width * 4)
    sc = plsc.get_sparse_core_info()
    workers = sc.num_cores * sc.num_subcores
    quantum = workers * rows * 2
    n_pad = -(-n // quantum) * quantum
    filler = jnp.arange(n_pad - n, dtype=I32) % table.shape[0]
    per_w = n_pad // workers
    wins = per_w // rows
    wins_tile = -(-wins // SUBLANES) * SUBLANES
    idx = jnp.concatenate([idx, filler]).reshape(workers, wins, rows)
    idx = jnp.pad(idx, ((0, 0), (0, wins_tile - wins), (0, 0))).reshape(workers * wins_tile, rows)
    mesh = plsc.VectorSubcoreMesh(core_axis_name="c", subcore_axis_name="s")
    buf = pltpu.VMEM((rows, width), table.dtype)

    @functools.partial(
        pl.kernel, mesh=mesh,
        out_type=jax.ShapeDtypeStruct((n_pad, width), table.dtype),
        scratch_types=[pltpu.VMEM((wins_tile, rows), I32), buf, buf,
                       pltpu.SemaphoreType.DMA, pltpu.SemaphoreType.DMA],
        name="gather_rows",
    )
    def gather_kernel(table_hbm, idx_hbm, out_hbm, idx_v, buf0, buf1, sem0, sem1):
        w = lax.axis_index("s") * sc.num_cores + lax.axis_index("c")
        pltpu.sync_copy(idx_hbm.at[pl.ds(pl.multiple_of(w * wins_tile, SUBLANES), wins_tile)], idx_v)
        base = pl.multiple_of(w * per_w, rows)

        @pl.loop(0, wins, step=2)
        def _(i):
            g0 = pltpu.async_copy(table_hbm.at[idx_v.at[i]], buf0, sem0)
            g1 = pltpu.async_copy(table_hbm.at[idx_v.at[i + 1]], buf1, sem1)
            g0.wait()
            pltpu.sync_copy(buf0, out_hbm.at[pl.ds(base + i * rows, rows)])
            g1.wait()
            pltpu.sync_copy(buf1, out_hbm.at[pl.ds(base + (i + 1) * rows, rows)])

    return gather_kernel(table, idx)


def _routing_tables(route, counts, counts_before, blk):
    T = route.shape[0]
    TK = T * TOP_K
    idx = route[:, 0:TOP_K].astype(I32)
    rank = route[:, 2 * TOP_K:3 * TOP_K].astype(I32) - counts_before[idx]
    padded = (counts + blk - 1) // blk * blk
    pad_end = jnp.cumsum(padded)
    pad_start = pad_end - padded
    dest = pad_start[idx] + rank
    n_blocks = -(-TK // blk) + N_EXPERTS
    blk_start = jnp.arange(n_blocks, dtype=I32) * blk
    block_e = jnp.sum((blk_start[:, None] >= pad_end[None, :]).astype(I32), axis=1)
    block_e = jnp.minimum(block_e, N_EXPERTS - 1)
    last = (pad_start + counts)[block_e]
    n_valid = jnp.clip(last - blk_start, 0, blk)
    return block_e, n_valid, dest


def _mix(x, mod, buf, past_k, past_v, p, lam_init, counts_in, tm, tq, pool, pool_rows, pool_first):
    B, L, _ = x.shape
    q, kb, vb, kf, vf, conv, state = _premix(x, mod, p["g_pre_mix"], p["w_in"], buf, p["w_dw"], p["b_dw"],
                                             p["gmat"], p["g_cn"], p["b_cn"], tm)
    if past_k is None:
        attn = _attention(q, kb, vb, p["lam_rows"], p["g_subln"], lam_init, tq, tq, 0, True)
    else:
        P = past_k.shape[1]
        keys = jnp.concatenate([past_k.reshape(B, P, QK_W).astype(BF16), kb], axis=1)
        vals = jnp.concatenate([past_v.reshape(B, P, ATTN_W).astype(BF16), vb], axis=1)
        attn = _attention(q, keys, vals, p["lam_rows"], p["g_subln"], lam_init, L, P + L, P, False)
    x1, h2, route, cnt = _postmix(attn, conv, x, mod, p["w_out"], p["g_post_mix"], p["g_pre_ffn"],
                                  p["w_router"], p["b_router"], counts_in, tm, pool, pool_rows, pool_first)
    return x1, h2, route, cnt, kf, vf, state[:, CONV_HALO - (CONV_W - 1):, :]


def _moe_rows(pool, row_first, route_groups, counts, counts_before, p, blk):
    route = jnp.concatenate([r.reshape(-1, LANES) for r in route_groups], axis=0)
    T = route.shape[0]
    block_e, n_valid, dest = _routing_tables(route, counts, counts_before, blk)
    x_pad = _scatter_rows(pool, row_first, dest, block_e.shape[0] * blk)
    y_pad = _experts(x_pad, block_e, n_valid, p["w_gu"], p["b_gu"], p["w_down"], p["b_down"], blk)
    return _gather_rows(y_pad, dest.T.reshape(T * TOP_K))


def _prepare_params(l, w_ada, b_ada, g_pre_mix, g_post_mix, w_in, lambda_q1, lambda_k1, lambda_q2, lambda_k2,
                    g_subln, w_dw, b_dw, g_cnorm, b_cnorm, w_out, g_pre_ffn, g_post_ffn,
                    w_router, b_router, w_gu, b_gu, w_down, b_down):
    lam_rows = jnp.zeros((8, LANES), F32)
    for r, vec in enumerate((lambda_q1[l], lambda_k1[l], lambda_q2[l], lambda_k2[l])):
        lam_rows = lam_rows.at[r, :HEAD_DIM].set(vec)
    ch = jnp.arange(CONV_CH, dtype=I32) // GROUP_CH
    gmat = (ch[:, None] == ch[None, :]).astype(BF16)
    w_dw_pad = jnp.zeros((CONV_HALO, CONV_CH), F32).at[:CONV_W].set(w_dw[l])
    w_router_pad = jnp.zeros((D_MODEL, LANES), F32).at[:, :N_EXPERTS].set(w_router[l])
    b_router_pad = jnp.full((1, LANES), NEG, F32).at[0, :N_EXPERTS].set(b_router[l])
    return dict(w_ada=w_ada[l], b_ada=b_ada[l], g_pre_mix=g_pre_mix[l], g_post_mix=g_post_mix[l],
                w_in=w_in[l].astype(BF16), lam_rows=lam_rows, g_subln=g_subln[l], w_dw=w_dw_pad, b_dw=b_dw[l],
                gmat=gmat, g_cn=g_cnorm[l], b_cn=b_cnorm[l], w_out=w_out[l].astype(BF16),
                g_pre_ffn=g_pre_ffn[l], g_post_ffn=g_post_ffn[l], w_router=w_router_pad, b_router=b_router_pad,
                w_gu=w_gu[l], b_gu=b_gu[l], w_down=w_down[l], b_down=b_down[l])


def kernel(x_prompt, x_sample, cache_k, cache_v, state_conv, c_prompt, c_sample, w_ada, b_ada, g_pre_mix, g_post_mix, w_in, lambda_q1, lambda_k1, lambda_q2, lambda_k2, g_subln, w_dw, b_dw, g_cnorm, b_cnorm, w_out, g_pre_ffn, g_post_ffn, w_router, b_router, w_gu, b_gu, w_down, b_down):
    depth = w_ada.shape[0]
    Bp, Lp, _ = x_prompt.shape
    Bs, Ls, _ = x_sample.shape
    yp, ys = x_prompt, x_sample
    outs = [[] for _ in range(6)]
    for l in range(depth):
        p = _prepare_params(l, w_ada, b_ada, g_pre_mix, g_post_mix, w_in, lambda_q1, lambda_k1, lambda_q2,
                            lambda_k2, g_subln, w_dw, b_dw, g_cnorm, b_cnorm, w_out, g_pre_ffn, g_post_ffn,
                            w_router, b_router, w_gu, b_gu, w_down, b_down)
        lam_init = 0.8 - 0.6 * math.exp(-0.3 * l)
        mod = _ada(jnp.concatenate([c_prompt, c_sample], axis=0), p["w_ada"], p["b_ada"])
        mod = mod.reshape(Bp + Bs, N_MOD, D_MODEL)
        buf_p = jnp.zeros((Bp, CONV_HALO, CONV_CH), F32)
        buf_s = jnp.pad(state_conv[l], ((0, 0), (CONV_HALO - (CONV_W - 1), 0), (0, 0)))
        tm_p = min(Lp, 512)
        assert Bp % 2 == 0
        hb = Bp // 2
        t_a, t_b = hb * Lp, hb * Lp + Bs * Ls
        pool_rows = Bp * Lp + Bs * Ls
        zero_counts = jnp.zeros((SUBLANES, LANES), F32)
        x1p, pool, rtp, cnt_p, kp, vp, cp = _mix(yp, mod[:Bp], buf_p, None, None, p, lam_init, zero_counts,
                                                 tm_p, tm_p, None, pool_rows, 0)
        x1s, pool, rts, cnt_s, ks, vs, cs = _mix(ys, mod[Bp:], buf_s, cache_k[l], cache_v[l], p, lam_init,
                                                 cnt_p[Bp - 1], Ls, Ls, pool, pool_rows, Bp * Lp)
        n_a = cnt_p[hb - 1, 0, :N_EXPERTS].astype(I32)
        n_b = cnt_s[Bs - 1, 0, :N_EXPERTS].astype(I32) - n_a
        y_a = _moe_rows(pool, 0, (rtp[:hb],), n_a, jnp.zeros_like(n_a), p, EXPERT_ROWS)
        y_b = _moe_rows(pool, t_a, (rtp[hb:], rts), n_b, n_a, p, EXPERT_ROWS)
        tm_c = min(tm_p, 256)
        yp = _combine(y_a, rtp, x1p, mod[:Bp], p["g_post_ffn"], tm_c, t_a, 0, 0, hb)
        yp = _combine(y_b, rtp, x1p, mod[:Bp], p["g_post_ffn"], tm_c, t_b, 0, hb, hb, out_prev=yp)
        ys = _combine(y_b, rts, x1s, mod[Bp:], p["g_post_ffn"], Ls, t_b, hb * Lp)
        for lst, val in zip(outs, (kp, vp, cp, ks, vs, cs)):
            lst.append(val)
    return (yp, ys) + tuple(jnp.stack(o) for o in outs)
```

```python
import functools
import math

import jax
import jax.numpy as jnp
from jax import lax
from jax.experimental import pallas as pl
from jax.experimental.pallas import tpu as pltpu
from jax.experimental.pallas import tpu_sc as plsc

F32 = jnp.float32
BF16 = jnp.bfloat16
I32 = jnp.int32

D_MODEL = 1024
CHUNK = 64
CHUNK_SHIFT = 6
N_HEADS = 4
V_DIM = 128
HEAD_DIM = 64
QK_W = 512
ATTN_W = 512
CONV_CH = 512
CONV_W = 31
CONV_GROUPS = 8
GROUP_CH = CONV_CH // CONV_GROUPS
IN_W = 2 * QK_W + ATTN_W + 2 * CONV_CH
N_MOD = 6
N_EXPERTS = 32
TOP_K = 4
D_FF = 1024
SWIGLU_ALPHA = 1.702
SWIGLU_LIMIT = 7.0
EPS = 1e-6

LANES = 128
SUBLANES = 8
CONV_HALO = 32
CONV_ROWS = 32
NEG = -1e30
LOG2E = math.log2(math.e)
ATTN_ROWS = 128
EXPERT_ROWS = 512
VMEM_LIMIT = 48 * 1024 * 1024
EXPERT_VMEM_LIMIT = 56 * 1024 * 1024


def _sigmoid(x):
    return 1.0 / (1.0 + jnp.exp(-x))


def _split_bf16(x):
    hi = x.astype(BF16)
    lo = (x - hi.astype(F32)).astype(BF16)
    return hi, lo


def _dot(a, b):
    return jnp.dot(a, b, preferred_element_type=F32)


def _dot3(a, b):
    ah, al = _split_bf16(a)
    bh, bl = _split_bf16(b)
    return _dot(ah, bh) + _dot(ah, bl) + _dot(al, bh)


def _rms(x):
    return x * lax.rsqrt(jnp.mean(x * x, axis=-1, keepdims=True) + EPS)


HALF = D_MODEL // 2
HIGH16 = -65536


def _pack_halves(x):
    lo = lax.bitcast_convert_type(x[:, :HALF].astype(BF16).astype(F32), I32)
    hi = lax.bitcast_convert_type(x[:, HALF:].astype(BF16).astype(F32), I32)
    return jnp.bitwise_or(jnp.bitwise_and(jnp.right_shift(lo, 16), 0xFFFF), jnp.bitwise_and(hi, HIGH16))


def _unpack_halves(w):
    lo = lax.bitcast_convert_type(jnp.left_shift(w, 16), F32)
    hi = lax.bitcast_convert_type(jnp.bitwise_and(w, HIGH16), F32)
    return lo, hi


def _ada_kernel(c_ref, w_ref, b_ref, o_ref):
    c = c_ref[...]
    o_ref[...] = _dot3(c * _sigmoid(c), w_ref[...]) + b_ref[...]


def _ada(c, w_ada, b_ada):
    n = c.shape[0]
    return pl.pallas_call(
        _ada_kernel,
        grid=(N_MOD,),
        in_specs=[pl.BlockSpec((n, D_MODEL), lambda j: (0, 0)),
                  pl.BlockSpec((D_MODEL, D_MODEL), lambda j: (0, j)),
                  pl.BlockSpec((1, D_MODEL), lambda j: (0, j))],
        out_specs=pl.BlockSpec((n, D_MODEL), lambda j: (0, j)),
        out_shape=jax.ShapeDtypeStruct((n, N_MOD * D_MODEL), F32),
        compiler_params=pltpu.CompilerParams(dimension_semantics=("arbitrary",), vmem_limit_bytes=VMEM_LIMIT),
        name="ada",
    )(c, w_ada, b_ada.reshape(1, N_MOD * D_MODEL))


def _premix_kernel(x_ref, mod_ref, g_ref, w_ref, buf_ref, wdw_ref, bdw_ref, gmat_ref, gcn_ref, bcn_ref,
                   q_ref, kb_ref, vb_ref, kf_hbm, vf_hbm, co_ref, st_ref, ext_ref, sh_ref, y_ref, kv_buf, kv_sem,
                   *, tm):
    batch = pl.program_id(0)
    s = pl.program_id(1)
    step = batch * pl.num_programs(1) + s
    last = pl.num_programs(0) * pl.num_programs(1) - 1
    slot = step % 2

    def kv_copies(slot):
        return [pltpu.make_async_copy(kv_buf.at[slot, i, :, pl.ds(h * V_DIM, V_DIM)],
                                      out.at[batch, pl.ds(s * tm, tm), h, :], kv_sem.at[slot, i, h])
                for i, out in enumerate((kf_hbm, vf_hbm)) for h in range(N_HEADS)]

    @pl.when(step >= 2)
    def _():
        for c in kv_copies(slot):
            c.wait()

    @pl.when(s == 0)
    def _():
        ext_ref[0:CONV_HALO, :] = buf_ref[0]

    x = x_ref[0]
    shift = mod_ref[0, 0:1, :]
    scale = mod_ref[0, 1:2, :]
    h = _rms(x) * g_ref[...] * (1.0 + scale) + shift
    hb = h.astype(BF16)

    u0 = 2 * QK_W + ATTN_W
    val = _dot(hb, w_ref[:, u0:u0 + CONV_CH])
    gate = _dot(hb, w_ref[:, u0 + CONV_CH:u0 + 2 * CONV_CH])
    ext_ref[CONV_HALO:CONV_HALO + tm, :] = val * _sigmoid(gate)

    off = CONV_HALO - (CONV_W - 1)
    span = tm + CONV_HALO - SUBLANES
    for b in range(1, SUBLANES):
        sh_ref[b - 1, 0:span, :] = ext_ref[b:b + span, :]
    for c in range(tm // CONV_ROWS):
        r0 = c * CONV_ROWS
        acc = jnp.zeros((CONV_ROWS, CONV_CH), F32)
        for j in range(CONV_W):
            b = (j + off) % SUBLANES
            a = r0 + j + off - b
            rows_j = ext_ref[a:a + CONV_ROWS, :] if b == 0 else sh_ref[b - 1, a:a + CONV_ROWS, :]
            acc = acc + wdw_ref[j:j + 1, :] * rows_j
        y_ref[r0:r0 + CONV_ROWS, :] = acc + bdw_ref[...]

    zq = _dot(hb, w_ref[:, 0:QK_W])
    q_ref[0] = (zq * (HEAD_DIM ** -0.5 * LOG2E)).astype(BF16)
    zk = _dot(hb, w_ref[:, QK_W:2 * QK_W])
    kv_buf[slot, 0] = zk
    kb_ref[0] = zk.astype(BF16)
    zv = _dot(hb, w_ref[:, 2 * QK_W:2 * QK_W + ATTN_W])
    kv_buf[slot, 1] = zv
    vb_ref[0] = zv.astype(BF16)

    y = y_ref[...]
    gm = gmat_ref[...]
    yh, yl = _split_bf16(y)
    mu = (_dot(yh, gm) + _dot(yl, gm)) * (1.0 / GROUP_CH)
    d = y - mu
    dh, dl = _split_bf16(d * d)
    var = (_dot(dh, gm) + _dot(dl, gm)) * (1.0 / GROUP_CH)
    yn = d * lax.rsqrt(var + EPS) * gcn_ref[...] + bcn_ref[...]
    co_ref[0] = (yn * _sigmoid(yn)).astype(BF16)

    tail = ext_ref[tm:tm + CONV_HALO, :]
    st_ref[0] = tail
    ext_ref[0:CONV_HALO, :] = tail

    for c in kv_copies(slot):
        c.start()

    @pl.when(step == last)
    def _():
        for c in kv_copies(slot):
            c.wait()

    @pl.when(jnp.logical_and(step == last, step >= 1))
    def _():
        for c in kv_copies(1 - slot):
            c.wait()


def _premix(x, mod, g_pre_mix, w_in_bf, buf, w_dw, b_dw, gmat, g_cn, b_cn, tm):
    B, L, _ = x.shape
    assert L % tm == 0 and tm % CONV_ROWS == 0
    row = lambda b, s: (b, s, 0)
    const2 = lambda b, s: (0, 0)
    bf_tile = jax.ShapeDtypeStruct((B, L, QK_W), BF16)
    f_heads = jax.ShapeDtypeStruct((B, L, N_HEADS, V_DIM), F32)
    return pl.pallas_call(
        functools.partial(_premix_kernel, tm=tm),
        grid=(B, L // tm),
        in_specs=[pl.BlockSpec((1, tm, D_MODEL), row),
                  pl.BlockSpec((1, N_MOD, D_MODEL), lambda b, s: (b, 0, 0)),
                  pl.BlockSpec((1, D_MODEL), const2),
                  pl.BlockSpec((D_MODEL, IN_W), const2),
                  pl.BlockSpec((1, CONV_HALO, CONV_CH), lambda b, s: (b, 0, 0)),
                  pl.BlockSpec((CONV_HALO, CONV_CH), const2),
                  pl.BlockSpec((1, CONV_CH), const2),
                  pl.BlockSpec((CONV_CH, CONV_CH), const2),
                  pl.BlockSpec((1, CONV_CH), const2),
                  pl.BlockSpec((1, CONV_CH), const2)],
        out_specs=[pl.BlockSpec((1, tm, QK_W), row)] * 3
                  + [pl.BlockSpec(memory_space=pl.ANY)] * 2
                  + [pl.BlockSpec((1, tm, QK_W), row),
                     pl.BlockSpec((1, CONV_HALO, CONV_CH), lambda b, s: (b, 0, 0))],
        out_shape=[bf_tile, bf_tile, bf_tile, f_heads, f_heads, bf_tile,
                   jax.ShapeDtypeStruct((B, CONV_HALO, CONV_CH), F32)],
        scratch_shapes=[pltpu.VMEM((CONV_HALO + tm, CONV_CH), F32),
                        pltpu.VMEM((SUBLANES - 1, CONV_HALO + tm, CONV_CH), F32),
                        pltpu.VMEM((tm, CONV_CH), F32),
                        pltpu.VMEM((2, 2, tm, QK_W), F32),
                        pltpu.SemaphoreType.DMA((2, 2, N_HEADS))],
        compiler_params=pltpu.CompilerParams(dimension_semantics=("arbitrary", "arbitrary"),
                                             vmem_limit_bytes=VMEM_LIMIT),
        name="premix",
    )(x, mod, g_pre_mix.reshape(1, D_MODEL), w_in_bf, buf, w_dw, b_dw.reshape(1, CONV_CH), gmat,
      g_cn.reshape(1, CONV_CH), b_cn.reshape(1, CONV_CH))


def _attn_kernel(slopes_ref, q_ref, k_ref, v_ref, kb_ref, mt_ref, lam_ref, gs_ref, o_ref,
                 ke1_ref, ke2_ref, ve_ref, m1_ref, a1_ref, m2_ref, a2_ref,
                 *, tq, tk, rows, q_offset, causal, lam_init):
    slope = slopes_ref[pl.program_id(1)] * LOG2E
    lq = q_ref.shape[1]
    lk = k_ref.shape[1]
    nq = lq // tq
    lane = lax.broadcasted_iota(I32, (1, V_DIM), 1)

    low = jnp.broadcast_to(lane < HEAD_DIM, (lk, V_DIM))
    k = k_ref[0]
    zero = jnp.zeros((lk, V_DIM), BF16)
    ke1_ref[...] = jnp.where(low, k, zero)
    ke2_ref[...] = jnp.where(low, zero, k)
    ve_ref[:, 0:V_DIM] = v_ref[0]
    ve_ref[:, V_DIM:2 * V_DIM] = jnp.broadcast_to(jnp.where(lane == 0, 1.0, 0.0), (lk, V_DIM)).astype(BF16)

    nt = (((1,), (1,)), ((), ()))
    maps = ((ke1_ref, m1_ref, a1_ref), (ke2_ref, m2_ref, a2_ref))

    def lanes(x, n):
        if n % LANES == 0:
            return jnp.concatenate([x] * (n // LANES), axis=1)
        return jnp.broadcast_to(x[:, 0:1], (rows, n))

    def tile(q0, k0, masked, first, tk=tk):
        ve = ve_ref[pl.ds(k0, tk), :]
        q_first = (lax.broadcasted_iota(I32, (1, 1), 0) + (q0 + q_offset)).astype(F32)
        bias = kb_ref[0, :, pl.ds(k0, tk)] - slope * q_first
        for ke_ref, m_ref, a_ref in maps:
            ke = ke_ref[pl.ds(k0, tk), :]
            for r0 in range(0, tq, rows):
                s = lax.dot_general(q_ref[0, pl.ds(q0 + r0, rows), :], ke, nt, preferred_element_type=F32)
                s = s + bias
                if masked:
                    s = s + mt_ref[0, r0:r0 + rows, :]
                m_tile = jnp.max(s, axis=-1, keepdims=True)
                if first:
                    m_new = jnp.broadcast_to(m_tile, (rows, LANES))
                    a_ref[r0:r0 + rows, :] = _dot(jnp.exp2(s - lanes(m_new, tk)).astype(BF16), ve)
                else:
                    m_old = m_ref[r0:r0 + rows, :]
                    m_new = jnp.maximum(m_old, m_tile)
                    p = jnp.exp2(s - lanes(m_new, tk))
                    alpha = jnp.exp2(m_old - m_new)
                    a_ref[r0:r0 + rows, :] = (lanes(alpha, 2 * V_DIM) * a_ref[r0:r0 + rows, :]
                                              + _dot(p.astype(BF16), ve))
                m_ref[r0:r0 + rows, :] = m_new

    lv = lam_ref[...]
    lam = (jnp.exp(jnp.sum(lv[0:1, :] * lv[1:2, :], axis=-1, keepdims=True))
           - jnp.exp(jnp.sum(lv[2:3, :] * lv[3:4, :], axis=-1, keepdims=True)) + lam_init)

    def finish(q0):
        a1 = a1_ref[...]
        a2 = a2_ref[...]
        o = a1[:, 0:V_DIM] / a1[:, V_DIM:V_DIM + 1] - lam * (a2[:, 0:V_DIM] / a2[:, V_DIM:V_DIM + 1])
        o_ref[0, pl.ds(q0, tq), :] = (_rms(o) * gs_ref[...] * (1.0 - lam_init)).astype(BF16)

    if not causal:
        tile(0, 0, True, True)
        finish(0)
        return

    for qi in range(nq):
        q0 = qi * tq
        for k0 in range(0, qi * tk, tk // 2):
            tile(q0, k0, False, k0 == 0, tk // 2)
        tile(q0, qi * tk, True, qi == 0)
        finish(q0)


def _mask_table(slopes, tq, tk, q_offset):
    qpos = q_offset + jnp.arange(tq, dtype=I32)[:, None]
    kpos = jnp.arange(tk, dtype=I32)[None, :]
    visible = jnp.right_shift(kpos, CHUNK_SHIFT) <= jnp.right_shift(qpos, CHUNK_SHIFT)
    fix = jnp.where(kpos > qpos, 2 * (qpos - kpos), 0).astype(F32)
    return jnp.where(visible[None], (slopes * LOG2E)[:, None, None] * fix[None], NEG)


def _attention(q, k, v, lam_rows, g_subln, lam_init, tq, tk, q_offset, causal):
    B, Lq, _ = q.shape
    Lk = k.shape[1]
    assert Lq % tq == 0 and Lk % tk == 0 and (not causal or (tq == tk and tq % CHUNK == 0 and q_offset == 0))
    assert causal or Lk == tk
    rows = min(tq, ATTN_ROWS)
    slopes = jnp.asarray([2.0 ** (-8.0 * (h + 1) / N_HEADS) for h in range(N_HEADS)], F32)
    table = _mask_table(slopes, tq, tk, q_offset)
    key_bias = ((slopes * LOG2E)[:, None] * jnp.arange(Lk, dtype=F32)[None, :]).reshape(N_HEADS, 1, Lk)
    key_ext = pltpu.VMEM((Lk, V_DIM), BF16)
    stat = pltpu.VMEM((tq, LANES), F32)
    acc = pltpu.VMEM((tq, 2 * V_DIM), F32)
    head = lambda b, h, sl: (b, 0, h)
    return pl.pallas_call(
        functools.partial(_attn_kernel, tq=tq, tk=tk, rows=rows, q_offset=q_offset, causal=causal,
                          lam_init=lam_init),
        grid_spec=pltpu.PrefetchScalarGridSpec(
            num_scalar_prefetch=1,
            grid=(B, N_HEADS),
            in_specs=[pl.BlockSpec((1, Lq, V_DIM), head),
                      pl.BlockSpec((1, Lk, V_DIM), head),
                      pl.BlockSpec((1, Lk, V_DIM), head),
                      pl.BlockSpec((1, 1, Lk), lambda b, h, sl: (h, 0, 0)),
                      pl.BlockSpec((1, tq, tk), lambda b, h, sl: (h, 0, 0)),
                      pl.BlockSpec((8, LANES), lambda b, h, sl: (0, 0)),
                      pl.BlockSpec((1, V_DIM), lambda b, h, sl: (0, 0))],
            out_specs=pl.BlockSpec((1, Lq, V_DIM), head),
            scratch_shapes=[key_ext, key_ext, pltpu.VMEM((Lk, 2 * V_DIM), BF16), stat, acc, stat, acc]),
        out_shape=jax.ShapeDtypeStruct((B, Lq, ATTN_W), BF16),
        compiler_params=pltpu.CompilerParams(dimension_semantics=("arbitrary",) * 2,
                                             vmem_limit_bytes=VMEM_LIMIT),
        name="attn",
    )(slopes, q, k, v, key_bias, table, lam_rows, g_subln.reshape(1, V_DIM))


def _postmix_kernel(a_ref, c_ref, x_ref, mod_ref, wo_ref, gpm_ref, gpf_ref, wr_ref, br_ref, cin_ref,
                    x1_ref, h2_ref, rt_ref, cnt_ref, carry_ref, *, tm):
    first = jnp.logical_and(pl.program_id(0) == 0, pl.program_id(1) == 0)

    @pl.when(first)
    def _():
        carry_ref[...] = cin_ref[...]

    mix = _dot(a_ref[0], wo_ref[0:ATTN_W, :]) + _dot(c_ref[0], wo_ref[ATTN_W:ATTN_W + CONV_CH, :])
    gate_a = mod_ref[0, 2:3, :]
    shift_f = mod_ref[0, 3:4, :]
    scale_f = mod_ref[0, 4:5, :]
    x1 = x_ref[0] + gate_a * (_rms(mix) * gpm_ref[...])
    x1_ref[0] = x1
    h2 = _rms(x1) * gpf_ref[...] * (1.0 + scale_f) + shift_f
    h2_ref[...] = _pack_halves(h2)

    logits = _dot3(h2, wr_ref[...]) + br_ref[...]
    lane = lax.broadcasted_iota(I32, (tm, LANES), 1).astype(F32)
    vals, idxs = [], []
    for _ in range(TOP_K):
        m = jnp.max(logits, axis=-1, keepdims=True)
        idx = jnp.min(jnp.where(logits == m, lane, float(LANES)), axis=-1, keepdims=True)
        vals.append(m)
        idxs.append(idx)
        logits = jnp.where(lane == idx, 2.0 * NEG, logits)
    es = [jnp.exp(v - vals[0]) for v in vals]
    denom = es[0] + es[1] + es[2] + es[3]

    onehot = jnp.zeros((tm, LANES), F32)
    for idx in idxs:
        onehot = jnp.where(lane == idx, 1.0, onehot)
    r_i = lax.broadcasted_iota(I32, (tm, tm), 0)
    c_i = lax.broadcasted_iota(I32, (tm, tm), 1)
    tri = jnp.where(c_i < r_i, 1.0, 0.0).astype(BF16)
    before = _dot(tri, onehot.astype(BF16)) + carry_ref[0:1, :]

    rt = jnp.zeros((tm, LANES), F32)
    for k in range(TOP_K):
        rank = jnp.sum(jnp.where(lane == idxs[k], before, 0.0), axis=-1, keepdims=True)
        rt = jnp.where(lane == k, idxs[k], rt)
        rt = jnp.where(lane == TOP_K + k, es[k] / denom, rt)
        rt = jnp.where(lane == 2 * TOP_K + k, rank, rt)
    rt_ref[0] = rt

    carry_ref[...] = carry_ref[...] + jnp.sum(onehot, axis=0, keepdims=True)
    cnt_ref[0] = carry_ref[...]


def _postmix_into_kernel(pool_ref, *refs, tm):
    del pool_ref
    _postmix_kernel(*refs, tm=tm)


def _postmix(attn, conv, x, mod, w_out_bf, g_post_mix, g_pre_ffn, w_router_pad, b_router_pad, counts_in, tm,
             pool, pool_rows, pool_first):
    B, L, _ = x.shape
    assert pool_first % tm == 0
    row = lambda b, s: (b, s, 0)
    const2 = lambda b, s: (0, 0)
    body, aliases, lead_specs, lead_args = _postmix_kernel, {}, [], ()
    if pool is not None:
        body, aliases = _postmix_into_kernel, {0: 1}
        lead_specs, lead_args = [pl.BlockSpec(memory_space=pl.ANY)], (pool,)
    return pl.pallas_call(
        functools.partial(body, tm=tm),
        grid=(B, L // tm),
        input_output_aliases=aliases,
        in_specs=lead_specs + [
                  pl.BlockSpec((1, tm, ATTN_W), row),
                  pl.BlockSpec((1, tm, CONV_CH), row),
                  pl.BlockSpec((1, tm, D_MODEL), row),
                  pl.BlockSpec((1, N_MOD, D_MODEL), lambda b, s: (b, 0, 0)),
                  pl.BlockSpec((D_MODEL, D_MODEL), const2),
                  pl.BlockSpec((1, D_MODEL), const2),
                  pl.BlockSpec((1, D_MODEL), const2),
                  pl.BlockSpec((D_MODEL, LANES), const2),
                  pl.BlockSpec((1, LANES), const2),
                  pl.BlockSpec((8, LANES), const2)],
        out_specs=[pl.BlockSpec((1, tm, D_MODEL), row),
                   pl.BlockSpec((tm, HALF), lambda b, s: (pool_first // tm + b * (L // tm) + s, 0)),
                   pl.BlockSpec((1, tm, LANES), row),
                   pl.BlockSpec((1, SUBLANES, LANES), lambda b, s: (b, 0, 0))],
        out_shape=[jax.ShapeDtypeStruct((B, L, D_MODEL), F32),
                   jax.ShapeDtypeStruct((pool_rows, HALF), I32),
                   jax.ShapeDtypeStruct((B, L, LANES), F32),
                   jax.ShapeDtypeStruct((B, SUBLANES, LANES), F32)],
        scratch_shapes=[pltpu.VMEM((8, LANES), F32)],
        compiler_params=pltpu.CompilerParams(dimension_semantics=("arbitrary", "arbitrary"),
                                             vmem_limit_bytes=VMEM_LIMIT),
        name="postmix",
    )(*lead_args, attn, conv, x, mod, w_out_bf, g_post_mix.reshape(1, D_MODEL), g_pre_ffn.reshape(1, D_MODEL),
      w_router_pad, b_router_pad, counts_in)


def _expert_kernel(be_ref, nv_ref, x_ref, wgu_ref, bgu_ref, wd_ref, bd_ref, y_ref, wgu_bf, wd_bf):
    i = pl.program_id(0)
    ff_half = D_FF // 2

    @pl.when(jnp.logical_or(i == 0, be_ref[i] != be_ref[jnp.maximum(i - 1, 0)]))
    def _():
        wgu_bf[...] = wgu_ref[0].astype(BF16)
        wd_bf[...] = wd_ref[0].astype(BF16)

    @pl.when(nv_ref[i] > 0)
    def _():
        x_lo, x_hi = _unpack_halves(x_ref[...])
        x_lo = x_lo.astype(BF16)
        x_hi = x_hi.astype(BF16)
        acc = None
        for c in range(2):
            lo, hi = c * ff_half, (c + 1) * ff_half
            g = (_dot(x_lo, wgu_bf[0:HALF, lo:hi]) + _dot(x_hi, wgu_bf[HALF:D_MODEL, lo:hi])
                 + bgu_ref[0, :, lo:hi])
            lin = (_dot(x_lo, wgu_bf[0:HALF, D_FF + lo:D_FF + hi]) + _dot(x_hi, wgu_bf[HALF:D_MODEL, D_FF + lo:D_FF + hi])
                   + bgu_ref[0, :, D_FF + lo:D_FF + hi])
            g = jnp.minimum(g, SWIGLU_LIMIT)
            lin = jnp.clip(lin, -SWIGLU_LIMIT, SWIGLU_LIMIT)
            act = g * _sigmoid(SWIGLU_ALPHA * g) * (lin + 1.0)
            part = _dot(act.astype(BF16), wd_bf[lo:hi, :])
            acc = part if acc is None else acc + part
        y_ref[...] = _pack_halves(acc + bd_ref[0])


def _experts(x_pad, block_e, n_valid, w_gu, b_gu, w_down, b_down, blk):
    n_blocks = block_e.shape[0]
    ex = lambda i, be, nv: (be[i], 0, 0)
    rows = lambda i, be, nv: (i, 0)
    return pl.pallas_call(
        _expert_kernel,
        grid_spec=pltpu.PrefetchScalarGridSpec(
            num_scalar_prefetch=2,
            grid=(n_blocks,),
            in_specs=[pl.BlockSpec((blk, HALF), rows),
                      pl.BlockSpec((1, D_MODEL, 2 * D_FF), ex), pl.BlockSpec((1, 1, 2 * D_FF), ex),
                      pl.BlockSpec((1, D_FF, D_MODEL), ex), pl.BlockSpec((1, 1, D_MODEL), ex)],
            out_specs=pl.BlockSpec((blk, HALF), rows),
            scratch_shapes=[pltpu.VMEM((D_MODEL, 2 * D_FF), BF16), pltpu.VMEM((D_FF, D_MODEL), BF16)]),
        out_shape=jax.ShapeDtypeStruct((n_blocks * blk, HALF), I32),
        compiler_params=pltpu.CompilerParams(dimension_semantics=("arbitrary",),
                                             vmem_limit_bytes=EXPERT_VMEM_LIMIT),
        name="experts",
    )(block_e, n_valid, x_pad, w_gu, b_gu.reshape(N_EXPERTS, 1, 2 * D_FF), w_down,
      b_down.reshape(N_EXPERTS, 1, D_MODEL))


def _combine_kernel(y0_ref, y1_ref, y2_ref, y3_ref, rt_ref, x1_ref, mod_ref, g_ref, o_ref):
    rt = rt_ref[0]
    f_lo = f_hi = None
    for k, y_ref in enumerate((y0_ref, y1_ref, y2_ref, y3_ref)):
        lo, hi = _unpack_halves(y_ref[...])
        w = rt[:, TOP_K + k:TOP_K + k + 1]
        f_lo = w * lo if f_lo is None else f_lo + w * lo
        f_hi = w * hi if f_hi is None else f_hi + w * hi
    ms = (jnp.sum(f_lo * f_lo, axis=-1, keepdims=True) + jnp.sum(f_hi * f_hi, axis=-1, keepdims=True)) / D_MODEL
    r = lax.rsqrt(ms + EPS)
    gate_f = mod_ref[0, 5:6, :]
    g = g_ref[...]
    o_ref[0, :, 0:HALF] = x1_ref[0, :, 0:HALF] + gate_f[:, 0:HALF] * (f_lo * r * g[:, 0:HALF])
    o_ref[0, :, HALF:D_MODEL] = x1_ref[0, :, HALF:D_MODEL] + gate_f[:, HALF:D_MODEL] * (f_hi * r * g[:, HALF:D_MODEL])


def _combine_into_kernel(prev_ref, *refs):
    del prev_ref
    _combine_kernel(*refs)


def _combine(y, route, x1, mod, g_post_ffn, tm, t_all, t_first, b_first=0, n_batch=None, out_prev=None):
    B, L, _ = x1.shape
    n_batch = B if n_batch is None else n_batch
    assert t_all % tm == 0 and t_first % tm == 0
    row = lambda b, s: (b + b_first, s, 0)
    y_specs = [pl.BlockSpec((tm, HALF), lambda b, s, k=k: ((k * t_all + t_first) // tm + b * (L // tm) + s, 0))
               for k in range(TOP_K)]
    in_specs = y_specs + [pl.BlockSpec((1, tm, LANES), row),
                          pl.BlockSpec((1, tm, D_MODEL), row),
                          pl.BlockSpec((1, N_MOD, D_MODEL), lambda b, s: (b + b_first, 0, 0)),
                          pl.BlockSpec((1, D_MODEL), lambda b, s: (0, 0))]
    args = (y, y, y, y, route, x1, mod, g_post_ffn.reshape(1, D_MODEL))
    body, aliases = _combine_kernel, {}
    if out_prev is not None:
        body, aliases = _combine_into_kernel, {0: 0}
        in_specs = [pl.BlockSpec(memory_space=pl.ANY)] + in_specs
        args = (out_prev,) + args
    return pl.pallas_call(
        body,
        grid=(n_batch, L // tm),
        in_specs=in_specs,
        out_specs=pl.BlockSpec((1, tm, D_MODEL), row),
        out_shape=jax.ShapeDtypeStruct((B, L, D_MODEL), F32),
        input_output_aliases=aliases,
        compiler_params=pltpu.CompilerParams(dimension_semantics=("arbitrary", "arbitrary"),
                                             vmem_limit_bytes=VMEM_LIMIT),
        name="combine",
    )(*args)


SC_GATHER_BYTES = 128 * 1024


def _scatter_quantum(width):
    rows = SC_GATHER_BYTES // (width * 4)
    sc = plsc.get_sparse_core_info()
    return rows, sc.num_cores * sc.num_subcores * rows * 2


def _scatter_rows(table, row_first, dest, n_slots):
    n, width = dest.shape[0], table.shape[1]
    rows, quantum = _scatter_quantum(width)
    sc = plsc.get_sparse_core_info()
    workers = sc.num_cores * sc.num_subcores
    n_pad = -(-n // quantum) * quantum
    extra = n_pad - n
    assert row_first % SUBLANES == 0 and n % rows == 0 and n >= rows and row_first + n <= table.shape[0]
    last_window = row_first + n - rows
    spare = n_slots + jnp.arange(extra * TOP_K, dtype=I32).reshape(extra, TOP_K)
    per_w = n_pad // workers
    wins = per_w // rows
    wins_tile = -(-wins // SUBLANES) * SUBLANES
    idx = jnp.concatenate([dest, spare], axis=0).T.reshape(TOP_K, workers, wins, rows)
    idx = jnp.pad(idx, ((0, 0), (0, 0), (0, wins_tile - wins), (0, 0))).reshape(TOP_K, workers * wins_tile, rows)
    mesh = plsc.VectorSubcoreMesh(core_axis_name="c", subcore_axis_name="s")
    buf = pltpu.VMEM((rows, width), table.dtype)

    @functools.partial(
        pl.kernel, mesh=mesh,
        out_type=jax.ShapeDtypeStruct((n_slots + extra * TOP_K, width), table.dtype),
        scratch_types=[pltpu.VMEM((TOP_K, wins_tile, rows), I32), buf, buf,
                       pltpu.SemaphoreType.DMA, pltpu.SemaphoreType.DMA],
        name="scatter_rows",
    )
    def scatter_kernel(table_hbm, idx_hbm, out_hbm, idx_v, buf0, buf1, sem0, sem1):
        w = lax.axis_index("s") * sc.num_cores + lax.axis_index("c")
        first = pl.multiple_of(w * wins_tile, SUBLANES)
        for k in range(TOP_K):
            pltpu.sync_copy(idx_hbm.at[k, pl.ds(first, wins_tile)], idx_v.at[k])
        base = row_first + w * per_w

        def window(i, buf, sem):
            start = pl.multiple_of(jnp.minimum(base + i * rows, last_window), SUBLANES)
            pltpu.sync_copy(table_hbm.at[pl.ds(start, rows)], buf)
            copies = [pltpu.async_copy(buf, out_hbm.at[idx_v.at[k, i]], sem) for k in range(TOP_K)]
            return copies

        @pl.loop(0, wins, step=2)
        def _(i):
            c0 = window(i, buf0, sem0)
            c1 = window(i + 1, buf1, sem1)
            for c in c0 + c1:
                c.wait()

    return scatter_kernel(table, idx)


def _gather_rows(table, idx):
    n = idx.shape[0]
    width = table.shape[1]
    rows = SC_GATHER_BYTES // (width * 4)
    sc = plsc.get_sparse_core_info()
    workers = sc.num_cores * sc.num_subcores
    quantum = workers * rows * 2
    n_pad = -(-n // quantum) * quantum
    filler = jnp.arange(n_pad - n, dtype=I32) % table.shape[0]
    per_w = n_pad // workers
    wins = per_w // rows
    wins_tile = -(-wins // SUBLANES) * SUBLANES
    idx = jnp.concatenate([idx, filler]).reshape(workers, wins, rows)
    idx = jnp.pad(idx, ((0, 0), (0, wins_tile - wins), (0, 0))).reshape(workers * wins_tile, rows)
    mesh = plsc.VectorSubcoreMesh(core_axis_name="c", subcore_axis_name="s")
    buf = pltpu.VMEM((rows, width), table.dtype)

    @functools.partial(
        pl.kernel, mesh=mesh,
        out_type=jax.ShapeDtypeStruct((n_pad, width), table.dtype),
        scratch_types=[pltpu.VMEM((wins_tile, rows), I32), buf, buf,
                       pltpu.SemaphoreType.DMA, pltpu.SemaphoreType.DMA],
        name="gather_rows",
    )
    def gather_kernel(table_hbm, idx_hbm, out_hbm, idx_v, buf0, buf1, sem0, sem1):
        w = lax.axis_index("s") * sc.num_cores + lax.axis_index("c")
        pltpu.sync_copy(idx_hbm.at[pl.ds(pl.multiple_of(w * wins_tile, SUBLANES), wins_tile)], idx_v)
        base = pl.multiple_of(w * per_w, rows)

        @pl.loop(0, wins, step=2)
        def _(i):
            g0 = pltpu.async_copy(table_hbm.at[idx_v.at[i]], buf0, sem0)
            g1 = pltpu.async_copy(table_hbm.at[idx_v.at[i + 1]], buf1, sem1)
            g0.wait()
            pltpu.sync_copy(buf0, out_hbm.at[pl.ds(base + i * rows, rows)])
            g1.wait()
            pltpu.sync_copy(buf1, out_hbm.at[pl.ds(base + (i + 1) * rows, rows)])

    return gather_kernel(table, idx)


def _routing_tables(route, counts, counts_before, blk):
    T = route.shape[0]
    TK = T * TOP_K
    idx = route[:, 0:TOP_K].astype(I32)
    rank = route[:, 2 * TOP_K:3 * TOP_K].astype(I32) - counts_before[idx]
    padded = (counts + blk - 1) // blk * blk
    pad_end = jnp.cumsum(padded)
    pad_start = pad_end - padded
    dest = pad_start[idx] + rank
    n_blocks = -(-TK // blk) + N_EXPERTS
    blk_start = jnp.arange(n_blocks, dtype=I32) * blk
    block_e = jnp.sum((blk_start[:, None] >= pad_end[None, :]).astype(I32), axis=1)
    block_e = jnp.minimum(block_e, N_EXPERTS - 1)
    last = (pad_start + counts)[block_e]
    n_valid = jnp.clip(last - blk_start, 0, blk)
    return block_e, n_valid, dest


def _mix(x, mod, buf, past_k, past_v, p, lam_init, counts_in, tm, tq, pool, pool_rows, pool_first):
    B, L, _ = x.shape
    q, kb, vb, kf, vf, conv, state = _premix(x, mod, p["g_pre_mix"], p["w_in"], buf, p["w_dw"], p["b_dw"],
                                             p["gmat"], p["g_cn"], p["b_cn"], tm)
    if past_k is None:
        attn = _attention(q, kb, vb, p["lam_rows"], p["g_subln"], lam_init, tq, tq, 0, True)
    else:
        P = past_k.shape[1]
        keys = jnp.concatenate([past_k.reshape(B, P, QK_W).astype(BF16), kb], axis=1)
        vals = jnp.concatenate([past_v.reshape(B, P, ATTN_W).astype(BF16), vb], axis=1)
        attn = _attention(q, keys, vals, p["lam_rows"], p["g_subln"], lam_init, L, P + L, P, False)
    x1, h2, route, cnt = _postmix(attn, conv, x, mod, p["w_out"], p["g_post_mix"], p["g_pre_ffn"],
                                  p["w_router"], p["b_router"], counts_in, tm, pool, pool_rows, pool_first)
    return x1, h2, route, cnt, kf, vf, state[:, CONV_HALO - (CONV_W - 1):, :]


def _moe_rows(pool, row_first, route_groups, counts, counts_before, p, blk):
    route = jnp.concatenate([r.reshape(-1, LANES) for r in route_groups], axis=0)
    T = route.shape[0]
    block_e, n_valid, dest = _routing_tables(route, counts, counts_before, blk)
    x_pad = _scatter_rows(pool, row_first, dest, block_e.shape[0] * blk)
    y_pad = _experts(x_pad, block_e, n_valid, p["w_gu"], p["b_gu"], p["w_down"], p["b_down"], blk)
    return _gather_rows(y_pad, dest.T.reshape(T * TOP_K))


def _prepare_params(l, w_ada, b_ada, g_pre_mix, g_post_mix, w_in, lambda_q1, lambda_k1, lambda_q2, lambda_k2,
                    g_subln, w_dw, b_dw, g_cnorm, b_cnorm, w_out, g_pre_ffn, g_post_ffn,
                    w_router, b_router, w_gu, b_gu, w_down, b_down):
    lam_rows = jnp.zeros((8, LANES), F32)
    for r, vec in enumerate((lambda_q1[l], lambda_k1[l], lambda_q2[l], lambda_k2[l])):
        lam_rows = lam_rows.at[r, :HEAD_DIM].set(vec)
    ch = jnp.arange(CONV_CH, dtype=I32) // GROUP_CH
    gmat = (ch[:, None] == ch[None, :]).astype(BF16)
    w_dw_pad = jnp.zeros((CONV_HALO, CONV_CH), F32).at[:CONV_W].set(w_dw[l])
    w_router_pad = jnp.zeros((D_MODEL, LANES), F32).at[:, :N_EXPERTS].set(w_router[l])
    b_router_pad = jnp.full((1, LANES), NEG, F32).at[0, :N_EXPERTS].set(b_router[l])
    return dict(w_ada=w_ada[l], b_ada=b_ada[l], g_pre_mix=g_pre_mix[l], g_post_mix=g_post_mix[l],
                w_in=w_in[l].astype(BF16), lam_rows=lam_rows, g_subln=g_subln[l], w_dw=w_dw_pad, b_dw=b_dw[l],
                gmat=gmat, g_cn=g_cnorm[l], b_cn=b_cnorm[l], w_out=w_out[l].astype(BF16),
                g_pre_ffn=g_pre_ffn[l], g_post_ffn=g_post_ffn[l], w_router=w_router_pad, b_router=b_router_pad,
                w_gu=w_gu[l], b_gu=b_gu[l], w_down=w_down[l], b_down=b_down[l])


def kernel(x_prompt, x_sample, cache_k, cache_v, state_conv, c_prompt, c_sample, w_ada, b_ada, g_pre_mix, g_post_mix, w_in, lambda_q1, lambda_k1, lambda_q2, lambda_k2, g_subln, w_dw, b_dw, g_cnorm, b_cnorm, w_out, g_pre_ffn, g_post_ffn, w_router, b_router, w_gu, b_gu, w_down, b_down):
    depth = w_ada.shape[0]
    Bp, Lp, _ = x_prompt.shape
    Bs, Ls, _ = x_sample.shape
    yp, ys = x_prompt, x_sample
    outs = [[] for _ in range(6)]
    for l in range(depth):
        p = _prepare_params(l, w_ada, b_ada, g_pre_mix, g_post_mix, w_in, lambda_q1, lambda_k1, lambda_q2,
                            lambda_k2, g_subln, w_dw, b_dw, g_cnorm, b_cnorm, w_out, g_pre_ffn, g_post_ffn,
                            w_router, b_router, w_gu, b_gu, w_down, b_down)
        lam_init = 0.8 - 0.6 * math.exp(-0.3 * l)
        mod = _ada(jnp.concatenate([c_prompt, c_sample], axis=0), p["w_ada"], p["b_ada"])
        mod = mod.reshape(Bp + Bs, N_MOD, D_MODEL)
        buf_p = jnp.zeros((Bp, CONV_HALO, CONV_CH), F32)
        buf_s = jnp.pad(state_conv[l], ((0, 0), (CONV_HALO - (CONV_W - 1), 0), (0, 0)))
        tm_p = min(Lp, 512)
        assert Bp % 2 == 0
        hb = Bp // 2
        t_a, t_b = hb * Lp, hb * Lp + Bs * Ls
        pool_rows = Bp * Lp + Bs * Ls
        zero_counts = jnp.zeros((SUBLANES, LANES), F32)
        x1p, pool, rtp, cnt_p, kp, vp, cp = _mix(yp, mod[:Bp], buf_p, None, None, p, lam_init, zero_counts,
                                                 tm_p, tm_p, None, pool_rows, 0)
        x1s, pool, rts, cnt_s, ks, vs, cs = _mix(ys, mod[Bp:], buf_s, cache_k[l], cache_v[l], p, lam_init,
                                                 cnt_p[Bp - 1], Ls, Ls, pool, pool_rows, Bp * Lp)
        n_a = cnt_p[hb - 1, 0, :N_EXPERTS].astype(I32)
        n_b = cnt_s[Bs - 1, 0, :N_EXPERTS].astype(I32) - n_a
        y_a = _moe_rows(pool, 0, (rtp[:hb],), n_a, jnp.zeros_like(n_a), p, EXPERT_ROWS)
        y_b = _moe_rows(pool, t_a, (rtp[hb:], rts), n_b, n_a, p, EXPERT_ROWS)
        tm_c = min(tm_p, 256)
        yp = _combine(y_a, rtp, x1p, mod[:Bp], p["g_post_ffn"], tm_c, t_a, 0, 0, hb)
        yp = _combine(y_b, rtp, x1p, mod[:Bp], p["g_post_ffn"], tm_c, t_b, 0, hb, hb, out_prev=yp)
        ys = _combine(y_b, rts, x1s, mod[Bp:], p["g_post_ffn"], Ls, t_b, hb * Lp)
        for lst, val in zip(outs, (kp, vp, cp, ks, vs, cs)):
            lst.append(val)
    return (yp, ys) + tuple(jnp.stack(o) for o in outs)
```

```python
import functools
import math

import jax
import jax.numpy as jnp
from jax import lax
from jax.experimental import pallas as pl
from jax.experimental.pallas import tpu as pltpu
from jax.experimental.pallas import tpu_sc as plsc

F32 = jnp.float32
BF16 = jnp.bfloat16
I32 = jnp.int32

D_MODEL = 1024
CHUNK = 64
CHUNK_SHIFT = 6
N_HEADS = 4
V_DIM = 128
HEAD_DIM = 64
QK_W = 512
ATTN_W = 512
CONV_CH = 512
CONV_W = 31
CONV_GROUPS = 8
GROUP_CH = CONV_CH // CONV_GROUPS
IN_W = 2 * QK_W + ATTN_W + 2 * CONV_CH
N_MOD = 6
N_EXPERTS = 32
TOP_K = 4
D_FF = 1024
SWIGLU_ALPHA = 1.702
SWIGLU_LIMIT = 7.0
EPS = 1e-6

LANES = 128
SUBLANES = 8
CONV_HALO = 32
CONV_ROWS = 32
NEG = -1e30
LOG2E = math.log2(math.e)
ATTN_ROWS = 128
EXPERT_ROWS = 512
VMEM_LIMIT = 48 * 1024 * 1024
EXPERT_VMEM_LIMIT = 56 * 1024 * 1024


def _sigmoid(x):
    return 1.0 / (1.0 + jnp.exp(-x))


def _split_bf16(x):
    hi = x.astype(BF16)
    lo = (x - hi.astype(F32)).astype(BF16)
    return hi, lo


def _dot(a, b):
    return jnp.dot(a, b, preferred_element_type=F32)


def _dot3(a, b):
    ah, al = _split_bf16(a)
    bh, bl = _split_bf16(b)
    return _dot(ah, bh) + _dot(ah, bl) + _dot(al, bh)


def _rms(x):
    return x * lax.rsqrt(jnp.mean(x * x, axis=-1, keepdims=True) + EPS)


HALF = D_MODEL // 2
HIGH16 = -65536


def _pack_halves(x):
    lo = lax.bitcast_convert_type(x[:, :HALF].astype(BF16).astype(F32), I32)
    hi = lax.bitcast_convert_type(x[:, HALF:].astype(BF16).astype(F32), I32)
    return jnp.bitwise_or(jnp.bitwise_and(jnp.right_shift(lo, 16), 0xFFFF), jnp.bitwise_and(hi, HIGH16))


def _unpack_halves(w):
    lo = lax.bitcast_convert_type(jnp.left_shift(w, 16), F32)
    hi = lax.bitcast_convert_type(jnp.bitwise_and(w, HIGH16), F32)
    return lo, hi


def _ada_kernel(c_ref, w_ref, b_ref, o_ref):
    c = c_ref[...]
    o_ref[...] = _dot3(c * _sigmoid(c), w_ref[...]) + b_ref[...]


def _ada(c, w_ada, b_ada):
    n = c.shape[0]
    return pl.pallas_call(
        _ada_kernel,
        grid=(N_MOD,),
        in_specs=[pl.BlockSpec((n, D_MODEL), lambda j: (0, 0)),
                  pl.BlockSpec((D_MODEL, D_MODEL), lambda j: (0, j)),
                  pl.BlockSpec((1, D_MODEL), lambda j: (0, j))],
        out_specs=pl.BlockSpec((n, D_MODEL), lambda j: (0, j)),
        out_shape=jax.ShapeDtypeStruct((n, N_MOD * D_MODEL), F32),
        compiler_params=pltpu.CompilerParams(dimension_semantics=("arbitrary",), vmem_limit_bytes=VMEM_LIMIT),
        name="ada",
    )(c, w_ada, b_ada.reshape(1, N_MOD * D_MODEL))


def _premix_kernel(x_ref, mod_ref, g_ref, w_ref, buf_ref, wdw_ref, bdw_ref, gmat_ref, gcn_ref, bcn_ref,
                   q_ref, kb_ref, vb_ref, kf_hbm, vf_hbm, co_ref, st_ref, ext_ref, sh_ref, y_ref, kv_buf, kv_sem,
                   *, tm):
    batch = pl.program_id(0)
    s = pl.program_id(1)
    step = batch * pl.num_programs(1) + s
    last = pl.num_programs(0) * pl.num_programs(1) - 1
    slot = step % 2

    def kv_copies(slot):
        return [pltpu.make_async_copy(kv_buf.at[slot, i, :, pl.ds(h * V_DIM, V_DIM)],
                                      out.at[batch, pl.ds(s * tm, tm), h, :], kv_sem.at[slot, i, h])
                for i, out in enumerate((kf_hbm, vf_hbm)) for h in range(N_HEADS)]

    @pl.when(step >= 2)
    def _():
        for c in kv_copies(slot):
            c.wait()

    @pl.when(s == 0)
    def _():
        ext_ref[0:CONV_HALO, :] = buf_ref[0]

    x = x_ref[0]
    shift = mod_ref[0, 0:1, :]
    scale = mod_ref[0, 1:2, :]
    h = _rms(x) * g_ref[...] * (1.0 + scale) + shift
    hb = h.astype(BF16)

    u0 = 2 * QK_W + ATTN_W
    val = _dot(hb, w_ref[:, u0:u0 + CONV_CH])
    gate = _dot(hb, w_ref[:, u0 + CONV_CH:u0 + 2 * CONV_CH])
    ext_ref[CONV_HALO:CONV_HALO + tm, :] = val * _sigmoid(gate)

    off = CONV_HALO - (CONV_W - 1)
    span = tm + CONV_HALO - SUBLANES
    for b in range(1, SUBLANES):
        sh_ref[b - 1, 0:span, :] = ext_ref[b:b + span, :]
    for c in range(tm // CONV_ROWS):
        r0 = c * CONV_ROWS
        acc = jnp.zeros((CONV_ROWS, CONV_CH), F32)
        for j in range(CONV_W):
            b = (j + off) % SUBLANES
            a = r0 + j + off - b
            rows_j = ext_ref[a:a + CONV_ROWS, :] if b == 0 else sh_ref[b - 1, a:a + CONV_ROWS, :]
            acc = acc + wdw_ref[j:j + 1, :] * rows_j
        y_ref[r0:r0 + CONV_ROWS, :] = acc + bdw_ref[...]

    zq = _dot(hb, w_ref[:, 0:QK_W])
    q_ref[0] = (zq * (HEAD_DIM ** -0.5 * LOG2E)).astype(BF16)
    zk = _dot(hb, w_ref[:, QK_W:2 * QK_W])
    kv_buf[slot, 0] = zk
    kb_ref[0] = zk.astype(BF16)
    zv = _dot(hb, w_ref[:, 2 * QK_W:2 * QK_W + ATTN_W])
    kv_buf[slot, 1] = zv
    vb_ref[0] = zv.astype(BF16)

    y = y_ref[...]
    gm = gmat_ref[...]
    mu = _dot(y.astype(BF16), gm) * (1.0 / GROUP_CH)
    d = y - mu
    var = _dot((d * d).astype(BF16), gm) * (1.0 / GROUP_CH)
    yn = d * lax.rsqrt(var + EPS) * gcn_ref[...] + bcn_ref[...]
    co_ref[0] = (yn * _sigmoid(yn)).astype(BF16)

    tail = ext_ref[tm:tm + CONV_HALO, :]
    st_ref[0] = tail
    ext_ref[0:CONV_HALO, :] = tail

    for c in kv_copies(slot):
        c.start()

    @pl.when(step == last)
    def _():
        for c in kv_copies(slot):
            c.wait()

    @pl.when(jnp.logical_and(step == last, step >= 1))
    def _():
        for c in kv_copies(1 - slot):
            c.wait()


def _premix(x, mod, g_pre_mix, w_in_bf, buf, w_dw, b_dw, gmat, g_cn, b_cn, tm):
    B, L, _ = x.shape
    assert L % tm == 0 and tm % CONV_ROWS == 0
    row = lambda b, s: (b, s, 0)
    const2 = lambda b, s: (0, 0)
    bf_tile = jax.ShapeDtypeStruct((B, L, QK_W), BF16)
    f_heads = jax.ShapeDtypeStruct((B, L, N_HEADS, V_DIM), F32)
    return pl.pallas_call(
        functools.partial(_premix_kernel, tm=tm),
        grid=(B, L // tm),
        in_specs=[pl.BlockSpec((1, tm, D_MODEL), row),
                  pl.BlockSpec((1, N_MOD, D_MODEL), lambda b, s: (b, 0, 0)),
                  pl.BlockSpec((1, D_MODEL), const2),
                  pl.BlockSpec((D_MODEL, IN_W), const2),
                  pl.BlockSpec((1, CONV_HALO, CONV_CH), lambda b, s: (b, 0, 0)),
                  pl.BlockSpec((CONV_HALO, CONV_CH), const2),
                  pl.BlockSpec((1, CONV_CH), const2),
                  pl.BlockSpec((CONV_CH, CONV_CH), const2),
                  pl.BlockSpec((1, CONV_CH), const2),
                  pl.BlockSpec((1, CONV_CH), const2)],
        out_specs=[pl.BlockSpec((1, tm, QK_W), row)] * 3
                  + [pl.BlockSpec(memory_space=pl.ANY)] * 2
                  + [pl.BlockSpec((1, tm, QK_W), row),
                     pl.BlockSpec((1, CONV_HALO, CONV_CH), lambda b, s: (b, 0, 0))],
        out_shape=[bf_tile, bf_tile, bf_tile, f_heads, f_heads, bf_tile,
                   jax.ShapeDtypeStruct((B, CONV_HALO, CONV_CH), F32)],
        scratch_shapes=[pltpu.VMEM((CONV_HALO + tm, CONV_CH), F32),
                        pltpu.VMEM((SUBLANES - 1, CONV_HALO + tm, CONV_CH), F32),
                        pltpu.VMEM((tm, CONV_CH), F32),
                        pltpu.VMEM((2, 2, tm, QK_W), F32),
                        pltpu.SemaphoreType.DMA((2, 2, N_HEADS))],
        compiler_params=pltpu.CompilerParams(dimension_semantics=("arbitrary", "arbitrary"),
                                             vmem_limit_bytes=VMEM_LIMIT),
        name="premix",
    )(x, mod, g_pre_mix.reshape(1, D_MODEL), w_in_bf, buf, w_dw, b_dw.reshape(1, CONV_CH), gmat,
      g_cn.reshape(1, CONV_CH), b_cn.reshape(1, CONV_CH))


def _attn_kernel(slopes_ref, q_ref, k_ref, v_ref, kb_ref, mt_ref, lam_ref, gs_ref, o_ref,
                 ke1_ref, ke2_ref, ve_ref, m1_ref, a1_ref, m2_ref, a2_ref,
                 *, tq, tk, rows, q_offset, causal, lam_init):
    slope = slopes_ref[pl.program_id(1)] * LOG2E
    lq = q_ref.shape[1]
    lk = k_ref.shape[1]
    nq = lq // tq
    lane = lax.broadcasted_iota(I32, (1, V_DIM), 1)

    low = jnp.broadcast_to(lane < HEAD_DIM, (lk, V_DIM))
    k = k_ref[0]
    zero = jnp.zeros((lk, V_DIM), BF16)
    ke1_ref[...] = jnp.where(low, k, zero)
    ke2_ref[...] = jnp.where(low, zero, k)
    ve_ref[:, 0:V_DIM] = v_ref[0]
    ve_ref[:, V_DIM:2 * V_DIM] = jnp.broadcast_to(jnp.where(lane == 0, 1.0, 0.0), (lk, V_DIM)).astype(BF16)

    nt = (((1,), (1,)), ((), ()))
    maps = ((ke1_ref, m1_ref, a1_ref), (ke2_ref, m2_ref, a2_ref))

    def lanes(x, n):
        if n % LANES == 0:
            return jnp.concatenate([x] * (n // LANES), axis=1)
        return jnp.broadcast_to(x[:, 0:1], (rows, n))

    def tile(q0, k0, masked, first):
        ve = ve_ref[pl.ds(k0, tk), :]
        q_first = (lax.broadcasted_iota(I32, (1, 1), 0) + (q0 + q_offset)).astype(F32)
        bias = kb_ref[0, :, pl.ds(k0, tk)] - slope * q_first
        for ke_ref, m_ref, a_ref in maps:
            ke = ke_ref[pl.ds(k0, tk), :]
            for r0 in range(0, tq, rows):
                s = lax.dot_general(q_ref[0, pl.ds(q0 + r0, rows), :], ke, nt, preferred_element_type=F32)
                s = s + bias
                if masked:
                    s = s + mt_ref[0, r0:r0 + rows, :]
                m_tile = jnp.max(s, axis=-1, keepdims=True)
                if first:
                    m_new = jnp.broadcast_to(m_tile, (rows, LANES))
                    a_ref[r0:r0 + rows, :] = _dot(jnp.exp2(s - lanes(m_new, tk)).astype(BF16), ve)
                else:
                    m_old = m_ref[r0:r0 + rows, :]
                    m_new = jnp.maximum(m_old, m_tile)
                    p = jnp.exp2(s - lanes(m_new, tk))
                    alpha = jnp.exp2(m_old - m_new)
                    a_ref[r0:r0 + rows, :] = (lanes(alpha, 2 * V_DIM) * a_ref[r0:r0 + rows, :]
                                              + _dot(p.astype(BF16), ve))
                m_ref[r0:r0 + rows, :] = m_new

    lv = lam_ref[...]
    lam = (jnp.exp(jnp.sum(lv[0:1, :] * lv[1:2, :], axis=-1, keepdims=True))
           - jnp.exp(jnp.sum(lv[2:3, :] * lv[3:4, :], axis=-1, keepdims=True)) + lam_init)

    def finish(q0):
        a1 = a1_ref[...]
        a2 = a2_ref[...]
        o = a1[:, 0:V_DIM] / a1[:, V_DIM:V_DIM + 1] - lam * (a2[:, 0:V_DIM] / a2[:, V_DIM:V_DIM + 1])
        o_ref[0, pl.ds(q0, tq), :] = (_rms(o) * gs_ref[...] * (1.0 - lam_init)).astype(BF16)

    if not causal:
        tile(0, 0, True, True)
        finish(0)
        return

    for qi in range(nq):
        q0 = qi * tq
        for ki in range(qi):
            tile(q0, ki * tk, False, ki == 0)
        tile(q0, qi * tk, True, qi == 0)
        finish(q0)


def _mask_table(slopes, tq, tk, q_offset):
    qpos = q_offset + jnp.arange(tq, dtype=I32)[:, None]
    kpos = jnp.arange(tk, dtype=I32)[None, :]
    visible = jnp.right_shift(kpos, CHUNK_SHIFT) <= jnp.right_shift(qpos, CHUNK_SHIFT)
    fix = jnp.where(kpos > qpos, 2 * (qpos - kpos), 0).astype(F32)
    return jnp.where(visible[None], (slopes * LOG2E)[:, None, None] * fix[None], NEG)


def _attention(q, k, v, lam_rows, g_subln, lam_init, tq, tk, q_offset, causal):
    B, Lq, _ = q.shape
    Lk = k.shape[1]
    assert Lq % tq == 0 and Lk % tk == 0 and (not causal or (tq == tk and tq % CHUNK == 0 and q_offset == 0))
    assert causal or Lk == tk
    rows = min(tq, ATTN_ROWS)
    slopes = jnp.asarray([2.0 ** (-8.0 * (h + 1) / N_HEADS) for h in range(N_HEADS)], F32)
    table = _mask_table(slopes, tq, tk, q_offset)
    key_bias = ((slopes * LOG2E)[:, None] * jnp.arange(Lk, dtype=F32)[None, :]).reshape(N_HEADS, 1, Lk)
    key_ext = pltpu.VMEM((Lk, V_DIM), BF16)
    stat = pltpu.VMEM((tq, LANES), F32)
    acc = pltpu.VMEM((tq, 2 * V_DIM), F32)
    head = lambda b, h, sl: (b, 0, h)
    return pl.pallas_call(
        functools.partial(_attn_kernel, tq=tq, tk=tk, rows=rows, q_offset=q_offset, causal=causal,
                          lam_init=lam_init),
        grid_spec=pltpu.PrefetchScalarGridSpec(
            num_scalar_prefetch=1,
            grid=(B, N_HEADS),
            in_specs=[pl.BlockSpec((1, Lq, V_DIM), head),
                      pl.BlockSpec((1, Lk, V_DIM), head),
                      pl.BlockSpec((1, Lk, V_DIM), head),
                      pl.BlockSpec((1, 1, Lk), lambda b, h, sl: (h, 0, 0)),
                      pl.BlockSpec((1, tq, tk), lambda b, h, sl: (h, 0, 0)),
                      pl.BlockSpec((8, LANES), lambda b, h, sl: (0, 0)),
                      pl.BlockSpec((1, V_DIM), lambda b, h, sl: (0, 0))],
            out_specs=pl.BlockSpec((1, Lq, V_DIM), head),
            scratch_shapes=[key_ext, key_ext, pltpu.VMEM((Lk, 2 * V_DIM), BF16), stat, acc, stat, acc]),
        out_shape=jax.ShapeDtypeStruct((B, Lq, ATTN_W), BF16),
        compiler_params=pltpu.CompilerParams(dimension_semantics=("arbitrary",) * 2,
                                             vmem_limit_bytes=VMEM_LIMIT),
        name="attn",
    )(slopes, q, k, v, key_bias, table, lam_rows, g_subln.reshape(1, V_DIM))


def _postmix_kernel(a_ref, c_ref, x_ref, mod_ref, wo_ref, gpm_ref, gpf_ref, wr_ref, br_ref, cin_ref,
                    x1_ref, h2_ref, rt_ref, cnt_ref, carry_ref, *, tm):
    first = jnp.logical_and(pl.program_id(0) == 0, pl.program_id(1) == 0)

    @pl.when(first)
    def _():
        carry_ref[...] = cin_ref[...]

    mix = _dot(a_ref[0], wo_ref[0:ATTN_W, :]) + _dot(c_ref[0], wo_ref[ATTN_W:ATTN_W + CONV_CH, :])
    gate_a = mod_ref[0, 2:3, :]
    shift_f = mod_ref[0, 3:4, :]
    scale_f = mod_ref[0, 4:5, :]
    x1 = x_ref[0] + gate_a * (_rms(mix) * gpm_ref[...])
    x1_ref[0] = x1
    h2 = _rms(x1) * gpf_ref[...] * (1.0 + scale_f) + shift_f
    h2_ref[...] = _pack_halves(h2)

    logits = _dot3(h2, wr_ref[...]) + br_ref[...]
    lane = lax.broadcasted_iota(I32, (tm, LANES), 1).astype(F32)
    vals, idxs = [], []
    for _ in range(TOP_K):
        m = jnp.max(logits, axis=-1, keepdims=True)
        idx = jnp.min(jnp.where(logits == m, lane, float(LANES)), axis=-1, keepdims=True)
        vals.append(m)
        idxs.append(idx)
        logits = jnp.where(lane == idx, 2.0 * NEG, logits)
    es = [jnp.exp(v - vals[0]) for v in vals]
    denom = es[0] + es[1] + es[2] + es[3]

    onehot = jnp.zeros((tm, LANES), F32)
    for idx in idxs:
        onehot = jnp.where(lane == idx, 1.0, onehot)
    r_i = lax.broadcasted_iota(I32, (tm, tm), 0)
    c_i = lax.broadcasted_iota(I32, (tm, tm), 1)
    tri = jnp.where(c_i < r_i, 1.0, 0.0).astype(BF16)
    before = _dot(tri, onehot.astype(BF16)) + carry_ref[0:1, :]

    rt = jnp.zeros((tm, LANES), F32)
    for k in range(TOP_K):
        rank = jnp.sum(jnp.where(lane == idxs[k], before, 0.0), axis=-1, keepdims=True)
        rt = jnp.where(lane == k, idxs[k], rt)
        rt = jnp.where(lane == TOP_K + k, es[k] / denom, rt)
        rt = jnp.where(lane == 2 * TOP_K + k, rank, rt)
    rt_ref[0] = rt

    carry_ref[...] = carry_ref[...] + jnp.sum(onehot, axis=0, keepdims=True)
    cnt_ref[0] = carry_ref[...]


def _postmix_into_kernel(pool_ref, *refs, tm):
    del pool_ref
    _postmix_kernel(*refs, tm=tm)


def _postmix(attn, conv, x, mod, w_out_bf, g_post_mix, g_pre_ffn, w_router_pad, b_router_pad, counts_in, tm,
             pool, pool_rows, pool_first):
    B, L, _ = x.shape
    assert pool_first % tm == 0
    row = lambda b, s: (b, s, 0)
    const2 = lambda b, s: (0, 0)
    body, aliases, lead_specs, lead_args = _postmix_kernel, {}, [], ()
    if pool is not None:
        body, aliases = _postmix_into_kernel, {0: 1}
        lead_specs, lead_args = [pl.BlockSpec(memory_space=pl.ANY)], (pool,)
    return pl.pallas_call(
        functools.partial(body, tm=tm),
        grid=(B, L // tm),
        input_output_aliases=aliases,
        in_specs=lead_specs + [
                  pl.BlockSpec((1, tm, ATTN_W), row),
                  pl.BlockSpec((1, tm, CONV_CH), row),
                  pl.BlockSpec((1, tm, D_MODEL), row),
                  pl.BlockSpec((1, N_MOD, D_MODEL), lambda b, s: (b, 0, 0)),
                  pl.BlockSpec((D_MODEL, D_MODEL), const2),
                  pl.BlockSpec((1, D_MODEL), const2),
                  pl.BlockSpec((1, D_MODEL), const2),
                  pl.BlockSpec((D_MODEL, LANES), const2),
                  pl.BlockSpec((1, LANES), const2),
                  pl.BlockSpec((8, LANES), const2)],
        out_specs=[pl.BlockSpec((1, tm, D_MODEL), row),
                   pl.BlockSpec((tm, HALF), lambda b, s: (pool_first // tm + b * (L // tm) + s, 0)),
                   pl.BlockSpec((1, tm, LANES), row),
                   pl.BlockSpec((1, SUBLANES, LANES), lambda b, s: (b, 0, 0))],
        out_shape=[jax.ShapeDtypeStruct((B, L, D_MODEL), F32),
                   jax.ShapeDtypeStruct((pool_rows, HALF), I32),
                   jax.ShapeDtypeStruct((B, L, LANES), F32),
                   jax.ShapeDtypeStruct((B, SUBLANES, LANES), F32)],
        scratch_shapes=[pltpu.VMEM((8, LANES), F32)],
        compiler_params=pltpu.CompilerParams(dimension_semantics=("arbitrary", "arbitrary"),
                                             vmem_limit_bytes=VMEM_LIMIT),
        name="postmix",
    )(*lead_args, attn, conv, x, mod, w_out_bf, g_post_mix.reshape(1, D_MODEL), g_pre_ffn.reshape(1, D_MODEL),
      w_router_pad, b_router_pad, counts_in)


def _expert_kernel(be_ref, nv_ref, x_ref, wgu_ref, bgu_ref, wd_ref, bd_ref, y_ref, wgu_bf, wd_bf):
    i = pl.program_id(0)
    ff_half = D_FF // 2

    @pl.when(jnp.logical_or(i == 0, be_ref[i] != be_ref[jnp.maximum(i - 1, 0)]))
    def _():
        wgu_bf[...] = wgu_ref[0].astype(BF16)
        wd_bf[...] = wd_ref[0].astype(BF16)

    @pl.when(nv_ref[i] > 0)
    def _():
        x_lo, x_hi = _unpack_halves(x_ref[...])
        x_lo = x_lo.astype(BF16)
        x_hi = x_hi.astype(BF16)
        acc = None
        for c in range(2):
            lo, hi = c * ff_half, (c + 1) * ff_half
            g = (_dot(x_lo, wgu_bf[0:HALF, lo:hi]) + _dot(x_hi, wgu_bf[HALF:D_MODEL, lo:hi])
                 + bgu_ref[0, :, lo:hi])
            lin = (_dot(x_lo, wgu_bf[0:HALF, D_FF + lo:D_FF + hi]) + _dot(x_hi, wgu_bf[HALF:D_MODEL, D_FF + lo:D_FF + hi])
                   + bgu_ref[0, :, D_FF + lo:D_FF + hi])
            g = jnp.minimum(g, SWIGLU_LIMIT)
            lin = jnp.clip(lin, -SWIGLU_LIMIT, SWIGLU_LIMIT)
            act = g * _sigmoid(SWIGLU_ALPHA * g) * (lin + 1.0)
            part = _dot(act.astype(BF16), wd_bf[lo:hi, :])
            acc = part if acc is None else acc + part
        y_ref[...] = _pack_halves(acc + bd_ref[0])


def _experts(x_pad, block_e, n_valid, w_gu, b_gu, w_down, b_down, blk):
    n_blocks = block_e.shape[0]
    ex = lambda i, be, nv: (be[i], 0, 0)
    rows = lambda i, be, nv: (i, 0)
    return pl.pallas_call(
        _expert_kernel,
        grid_spec=pltpu.PrefetchScalarGridSpec(
            num_scalar_prefetch=2,
            grid=(n_blocks,),
            in_specs=[pl.BlockSpec((blk, HALF), rows),
                      pl.BlockSpec((1, D_MODEL, 2 * D_FF), ex), pl.BlockSpec((1, 1, 2 * D_FF), ex),
                      pl.BlockSpec((1, D_FF, D_MODEL), ex), pl.BlockSpec((1, 1, D_MODEL), ex)],
            out_specs=pl.BlockSpec((blk, HALF), rows),
            scratch_shapes=[pltpu.VMEM((D_MODEL, 2 * D_FF), BF16), pltpu.VMEM((D_FF, D_MODEL), BF16)]),
        out_shape=jax.ShapeDtypeStruct((n_blocks * blk, HALF), I32),
        compiler_params=pltpu.CompilerParams(dimension_semantics=("arbitrary",),
                                             vmem_limit_bytes=EXPERT_VMEM_LIMIT),
        name="experts",
    )(block_e, n_valid, x_pad, w_gu, b_gu.reshape(N_EXPERTS, 1, 2 * D_FF), w_down,
      b_down.reshape(N_EXPERTS, 1, D_MODEL))


def _combine_kernel(y0_ref, y1_ref, y2_ref, y3_ref, rt_ref, x1_ref, mod_ref, g_ref, o_ref):
    rt = rt_ref[0]
    f_lo = f_hi = None
    for k, y_ref in enumerate((y0_ref, y1_ref, y2_ref, y3_ref)):
        lo, hi = _unpack_halves(y_ref[...])
        w = rt[:, TOP_K + k:TOP_K + k + 1]
        f_lo = w * lo if f_lo is None else f_lo + w * lo
        f_hi = w * hi if f_hi is None else f_hi + w * hi
    ms = (jnp.sum(f_lo * f_lo, axis=-1, keepdims=True) + jnp.sum(f_hi * f_hi, axis=-1, keepdims=True)) / D_MODEL
    r = lax.rsqrt(ms + EPS)
    gate_f = mod_ref[0, 5:6, :]
    g = g_ref[...]
    o_ref[0, :, 0:HALF] = x1_ref[0, :, 0:HALF] + gate_f[:, 0:HALF] * (f_lo * r * g[:, 0:HALF])
    o_ref[0, :, HALF:D_MODEL] = x1_ref[0, :, HALF:D_MODEL] + gate_f[:, HALF:D_MODEL] * (f_hi * r * g[:, HALF:D_MODEL])


def _combine_into_kernel(prev_ref, *refs):
    del prev_ref
    _combine_kernel(*refs)


def _combine(y, route, x1, mod, g_post_ffn, tm, t_all, t_first, b_first=0, n_batch=None, out_prev=None):
    B, L, _ = x1.shape
    n_batch = B if n_batch is None else n_batch
    assert t_all % tm == 0 and t_first % tm == 0
    row = lambda b, s: (b + b_first, s, 0)
    y_specs = [pl.BlockSpec((tm, HALF), lambda b, s, k=k: ((k * t_all + t_first) // tm + b * (L // tm) + s, 0))
               for k in range(TOP_K)]
    in_specs = y_specs + [pl.BlockSpec((1, tm, LANES), row),
                          pl.BlockSpec((1, tm, D_MODEL), row),
                          pl.BlockSpec((1, N_MOD, D_MODEL), lambda b, s: (b + b_first, 0, 0)),
                          pl.BlockSpec((1, D_MODEL), lambda b, s: (0, 0))]
    args = (y, y, y, y, route, x1, mod, g_post_ffn.reshape(1, D_MODEL))
    body, aliases = _combine_kernel, {}
    if out_prev is not None:
        body, aliases = _combine_into_kernel, {0: 0}
        in_specs = [pl.BlockSpec(memory_space=pl.ANY)] + in_specs
        args = (out_prev,) + args
    return pl.pallas_call(
        body,
        grid=(n_batch, L // tm),
        in_specs=in_specs,
        out_specs=pl.BlockSpec((1, tm, D_MODEL), row),
        out_shape=jax.ShapeDtypeStruct((B, L, D_MODEL), F32),
        input_output_aliases=aliases,
        compiler_params=pltpu.CompilerParams(dimension_semantics=("arbitrary", "arbitrary"),
                                             vmem_limit_bytes=VMEM_LIMIT),
        name="combine",
    )(*args)


SC_GATHER_BYTES = 128 * 1024


def _scatter_quantum(width):
    rows = SC_GATHER_BYTES // (width * 4)
    sc = plsc.get_sparse_core_info()
    return rows, sc.num_cores * sc.num_subcores * rows * 2


def _scatter_rows(table, row_first, dest, n_slots):
    n, width = dest.shape[0], table.shape[1]
    rows, quantum = _scatter_quantum(width)
    sc = plsc.get_sparse_core_info()
    workers = sc.num_cores * sc.num_subcores
    n_pad = -(-n // quantum) * quantum
    extra = n_pad - n
    assert row_first % SUBLANES == 0 and n % rows == 0 and n >= rows and row_first + n <= table.shape[0]
    last_window = row_first + n - rows
    spare = n_slots + jnp.arange(extra * TOP_K, dtype=I32).reshape(extra, TOP_K)
    per_w = n_pad // workers
    wins = per_w // rows
    wins_tile = -(-wins // SUBLANES) * SUBLANES
    idx = jnp.concatenate([dest, spare], axis=0).T.reshape(TOP_K, workers, wins, rows)
    idx = jnp.pad(idx, ((0, 0), (0, 0), (0, wins_tile - wins), (0, 0))).reshape(TOP_K, workers * wins_tile, rows)
    mesh = plsc.VectorSubcoreMesh(core_axis_name="c", subcore_axis_name="s")
    buf = pltpu.VMEM((rows, width), table.dtype)

    @functools.partial(
        pl.kernel, mesh=mesh,
        out_type=jax.ShapeDtypeStruct((n_slots + extra * TOP_K, width), table.dtype),
        scratch_types=[pltpu.VMEM((TOP_K, wins_tile, rows), I32), buf, buf,
                       pltpu.SemaphoreType.DMA, pltpu.SemaphoreType.DMA],
        name="scatter_rows",
    )
    def scatter_kernel(table_hbm, idx_hbm, out_hbm, idx_v, buf0, buf1, sem0, sem1):
        w = lax.axis_index("s") * sc.num_cores + lax.axis_index("c")
        first = pl.multiple_of(w * wins_tile, SUBLANES)
        for k in range(TOP_K):
            pltpu.sync_copy(idx_hbm.at[k, pl.ds(first, wins_tile)], idx_v.at[k])
        base = row_first + w * per_w

        def window(i, buf, sem):
            start = pl.multiple_of(jnp.minimum(base + i * rows, last_window), SUBLANES)
            pltpu.sync_copy(table_hbm.at[pl.ds(start, rows)], buf)
            copies = [pltpu.async_copy(buf, out_hbm.at[idx_v.at[k, i]], sem) for k in range(TOP_K)]
            return copies

        @pl.loop(0, wins, step=2)
        def _(i):
            c0 = window(i, buf0, sem0)
            c1 = window(i + 1, buf1, sem1)
            for c in c0 + c1:
                c.wait()

    return scatter_kernel(table, idx)


def _gather_rows(table, idx):
    n = idx.shape[0]
    width = table.shape[1]
    rows = SC_GATHER_BYTES // (width * 4)
    sc = plsc.get_sparse_core_info()
    workers = sc.num_cores * sc.num_subcores
    quantum = workers * rows * 2
    n_pad = -(-n // quantum) * quantum
    filler = jnp.arange(n_pad - n, dtype=I32) % table.shape[0]
    per_w = n_pad // workers
    wins = per_w // rows
    wins_tile = -(-wins // SUBLANES) * SUBLANES
    idx = jnp.concatenate([idx, filler]).reshape(workers, wins, rows)
    idx = jnp.pad(idx, ((0, 0), (0, wins_tile - wins), (0, 0))).reshape(workers * wins_tile, rows)
    mesh = plsc.VectorSubcoreMesh(core_axis_name="c", subcore_axis_name="s")
    buf = pltpu.VMEM((rows, width), table.dtype)

    @functools.partial(
        pl.kernel, mesh=mesh,
        out_type=jax.ShapeDtypeStruct((n_pad, width), table.dtype),
        scratch_types=[pltpu.VMEM((wins_tile, rows), I32), buf, buf,
                       pltpu.SemaphoreType.DMA, pltpu.SemaphoreType.DMA],
        name="gather_rows",
    )
    def gather_kernel(table_hbm, idx_hbm, out_hbm, idx_v, buf0, buf1, sem0, sem1):
        w = lax.axis_index("s") * sc.num_cores + lax.axis_index("c")
        pltpu.sync_copy(idx_hbm.at[pl.ds(pl.multiple_of(w * wins_tile, SUBLANES), wins_tile)], idx_v)
        base = pl.multiple_of(w * per_w, rows)

        @pl.loop(0, wins, step=2)
        def _(i):
            g0 = pltpu.async_copy(table_hbm.at[idx_v.at[i]], buf0, sem0)
            g1 = pltpu.async_copy(table_hbm.at[idx_v.at[i + 1]], buf1, sem1)
            g0.wait()
            pltpu.sync_copy(buf0, out_hbm.at[pl.ds(base + i * rows, rows)])
            g1.wait()
            pltpu.sync_copy(buf1, out_hbm.at[pl.ds(base + (i + 1) * rows, rows)])

    return gather_kernel(table, idx)


def _routing_tables(route, counts, counts_before, blk):
    T = route.shape[0]
    TK = T * TOP_K
    idx = route[:, 0:TOP_K].astype(I32)
    rank = route[:, 2 * TOP_K:3 * TOP_K].astype(I32) - counts_before[idx]
    padded = (counts + blk - 1) // blk * blk
    pad_end = jnp.cumsum(padded)
    pad_start = pad_end - padded
    dest = pad_start[idx] + rank
    n_blocks = -(-TK // blk) + N_EXPERTS
    blk_start = jnp.arange(n_blocks, dtype=I32) * blk
    block_e = jnp.sum((blk_start[:, None] >= pad_end[None, :]).astype(I32), axis=1)
    block_e = jnp.minimum(block_e, N_EXPERTS - 1)
    last = (pad_start + counts)[block_e]
    n_valid = jnp.clip(last - blk_start, 0, blk)
    return block_e, n_valid, dest


def _mix(x, mod, buf, past_k, past_v, p, lam_init, counts_in, tm, tq, pool, pool_rows, pool_first):
    B, L, _ = x.shape
    q, kb, vb, kf, vf, conv, state = _premix(x, mod, p["g_pre_mix"], p["w_in"], buf, p["w_dw"], p["b_dw"],
                                             p["gmat"], p["g_cn"], p["b_cn"], tm)
    if past_k is None:
        attn = _attention(q, kb, vb, p["lam_rows"], p["g_subln"], lam_init, tq, tq, 0, True)
    else:
        P = past_k.shape[1]
        keys = jnp.concatenate([past_k.reshape(B, P, QK_W).astype(BF16), kb], axis=1)
        vals = jnp.concatenate([past_v.reshape(B, P, ATTN_W).astype(BF16), vb], axis=1)
        attn = _attention(q, keys, vals, p["lam_rows"], p["g_subln"], lam_init, L, P + L, P, False)
    x1, h2, route, cnt = _postmix(attn, conv, x, mod, p["w_out"], p["g_post_mix"], p["g_pre_ffn"],
                                  p["w_router"], p["b_router"], counts_in, tm, pool, pool_rows, pool_first)
    return x1, h2, route, cnt, kf, vf, state[:, CONV_HALO - (CONV_W - 1):, :]


def _moe_rows(pool, row_first, route_groups, counts, counts_before, p, blk):
    route = jnp.concatenate([r.reshape(-1, LANES) for r in route_groups], axis=0)
    T = route.shape[0]
    block_e, n_valid, dest = _routing_tables(route, counts, counts_before, blk)
    x_pad = _scatter_rows(pool, row_first, dest, block_e.shape[0] * blk)
    y_pad = _experts(x_pad, block_e, n_valid, p["w_gu"], p["b_gu"], p["w_down"], p["b_down"], blk)
    return _gather_rows(y_pad, dest.T.reshape(T * TOP_K))


def _prepare_params(l, w_ada, b_ada, g_pre_mix, g_post_mix, w_in, lambda_q1, lambda_k1, lambda_q2, lambda_k2,
                    g_subln, w_dw, b_dw, g_cnorm, b_cnorm, w_out, g_pre_ffn, g_post_ffn,
                    w_router, b_router, w_gu, b_gu, w_down, b_down):
    lam_rows = jnp.zeros((8, LANES), F32)
    for r, vec in enumerate((lambda_q1[l], lambda_k1[l], lambda_q2[l], lambda_k2[l])):
        lam_rows = lam_rows.at[r, :HEAD_DIM].set(vec)
    ch = jnp.arange(CONV_CH, dtype=I32) // GROUP_CH
    gmat = (ch[:, None] == ch[None, :]).astype(BF16)
    w_dw_pad = jnp.zeros((CONV_HALO, CONV_CH), F32).at[:CONV_W].set(w_dw[l])
    w_router_pad = jnp.zeros((D_MODEL, LANES), F32).at[:, :N_EXPERTS].set(w_router[l])
    b_router_pad = jnp.full((1, LANES), NEG, F32).at[0, :N_EXPERTS].set(b_router[l])
    return dict(w_ada=w_ada[l], b_ada=b_ada[l], g_pre_mix=g_pre_mix[l], g_post_mix=g_post_mix[l],
                w_in=w_in[l].astype(BF16), lam_rows=lam_rows, g_subln=g_subln[l], w_dw=w_dw_pad, b_dw=b_dw[l],
                gmat=gmat, g_cn=g_cnorm[l], b_cn=b_cnorm[l], w_out=w_out[l].astype(BF16),
                g_pre_ffn=g_pre_ffn[l], g_post_ffn=g_post_ffn[l], w_router=w_router_pad, b_router=b_router_pad,
                w_gu=w_gu[l], b_gu=b_gu[l], w_down=w_down[l], b_down=b_down[l])


def kernel(x_prompt, x_sample, cache_k, cache_v, state_conv, c_prompt, c_sample, w_ada, b_ada, g_pre_mix, g_post_mix, w_in, lambda_q1, lambda_k1, lambda_q2, lambda_k2, g_subln, w_dw, b_dw, g_cnorm, b_cnorm, w_out, g_pre_ffn, g_post_ffn, w_router, b_router, w_gu, b_gu, w_down, b_down):
    depth = w_ada.shape[0]
    Bp, Lp, _ = x_prompt.shape
    Bs, Ls, _ = x_sample.shape
    yp, ys = x_prompt, x_sample
    outs = [[] for _ in range(6)]
    for l in range(depth):
        p = _prepare_params(l, w_ada, b_ada, g_pre_mix, g_post_mix, w_in, lambda_q1, lambda_k1, lambda_q2,
                            lambda_k2, g_subln, w_dw, b_dw, g_cnorm, b_cnorm, w_out, g_pre_ffn, g_post_ffn,
                            w_router, b_router, w_gu, b_gu, w_down, b_down)
        lam_init = 0.8 - 0.6 * math.exp(-0.3 * l)
        mod = _ada(jnp.concatenate([c_prompt, c_sample], axis=0), p["w_ada"], p["b_ada"])
        mod = mod.reshape(Bp + Bs, N_MOD, D_MODEL)
        buf_p = jnp.zeros((Bp, CONV_HALO, CONV_CH), F32)
        buf_s = jnp.pad(state_conv[l], ((0, 0), (CONV_HALO - (CONV_W - 1), 0), (0, 0)))
        tm_p = min(Lp, 512)
        assert Bp % 2 == 0
        hb = Bp // 2
        t_a, t_b = hb * Lp, hb * Lp + Bs * Ls
        pool_rows = Bp * Lp + Bs * Ls
        zero_counts = jnp.zeros((SUBLANES, LANES), F32)
        x1p, pool, rtp, cnt_p, kp, vp, cp = _mix(yp, mod[:Bp], buf_p, None, None, p, lam_init, zero_counts,
                                                 tm_p, tm_p, None, pool_rows, 0)
        x1s, pool, rts, cnt_s, ks, vs, cs = _mix(ys, mod[Bp:], buf_s, cache_k[l], cache_v[l], p, lam_init,
                                                 cnt_p[Bp - 1], Ls, Ls, pool, pool_rows, Bp * Lp)
        n_a = cnt_p[hb - 1, 0, :N_EXPERTS].astype(I32)
        n_b = cnt_s[Bs - 1, 0, :N_EXPERTS].astype(I32) - n_a
        y_a = _moe_rows(pool, 0, (rtp[:hb],), n_a, jnp.zeros_like(n_a), p, EXPERT_ROWS)
        y_b = _moe_rows(pool, t_a, (rtp[hb:], rts), n_b, n_a, p, EXPERT_ROWS)
        tm_c = min(tm_p, 256)
        yp = _combine(y_a, rtp, x1p, mod[:Bp], p["g_post_ffn"], tm_c, t_a, 0, 0, hb)
        yp = _combine(y_b, rtp, x1p, mod[:Bp], p["g_post_ffn"], tm_c, t_b, 0, hb, hb, out_prev=yp)
        ys = _combine(y_b, rts, x1s, mod[Bp:], p["g_post_ffn"], Ls, t_b, hb * Lp)
        for lst, val in zip(outs, (kp, vp, cp, ks, vs, cs)):
            lst.append(val)
    return (yp, ys) + tuple(jnp.stack(o) for o in outs)
```
